```python
import math
import jax
import jax.numpy as jnp
from jax import lax
import numpy as np

D_MODEL = 1024
BATCH = 4
SEQ = 4096
DEPTH = 2

MEM_LEN = 256
HEAD_DIM = 64
MIX_WIDTH = 2 * D_MODEL
SSD_WIDTH = MIX_WIDTH // 2
RWKV_WIDTH = MIX_WIDTH - SSD_WIDTH
SSD_HEADS = SSD_WIDTH // HEAD_DIM
SSD_GROUPS = 2
SSD_STATE = 128
SSD_CONV = 4
SSD_CHUNK = 128
SSD_XBC = SSD_WIDTH + 2 * SSD_GROUPS * SSD_STATE
SSD_IN = SSD_WIDTH + SSD_XBC + SSD_HEADS
RWKV_HEADS = RWKV_WIDTH // HEAD_DIM
RWKV_DECAY_LORA = 64
RWKV_ICLR_LORA = 64
RWKV_GATE_LORA = 128
RWKV_IN = 3 * RWKV_WIDTH + RWKV_DECAY_LORA + RWKV_ICLR_LORA + RWKV_GATE_LORA
EVEN_IN = SSD_IN + RWKV_IN
MOBA_HEADS = D_MODEL // HEAD_DIM
MOBA_BLOCK = 256
MOBA_TOPK = 3
MOBA_QBLOCK = 128
XATTN_HEADS = 4
XATTN_HEAD_DIM = D_MODEL // XATTN_HEADS
FFN_RAW = -(-8 * D_MODEL // 3)
FFN_HIDDEN = -(-FFN_RAW // 256) * 256
N_EVEN = (DEPTH + 1) // 2
N_ODD = DEPTH // 2
DEEPNORM_ALPHA = (2 * DEPTH) ** 0.25
DEEPNORM_BETA = (8 * DEPTH) ** -0.25
LN_EPS = 1e-5
RMS_EPS = 1e-5
RWKV_LNX_EPS = 64e-5

kernel_name = "hybrid_ssd_rwkv7_moba_deepnorm"

F32 = jnp.float32


def _layer_norm(x, g, b):
    xf = x.astype(F32)
    mu = jnp.mean(xf, axis=-1, keepdims=True)
    var = jnp.mean(jnp.square(xf - mu), axis=-1, keepdims=True)
    return ((xf - mu) * lax.rsqrt(var + LN_EPS) * g + b).astype(x.dtype)


def _token_shift(s):
    return jnp.pad(s, ((0, 0), (1, 0), (0, 0)))[:, :-1]


def _causal_dwconv(u, w, b):
    k = w.shape[0]
    out = lax.conv_general_dilated(
        u, w.astype(u.dtype)[:, None, :], window_strides=(1,), padding=[(k - 1, 0)],
        dimension_numbers=("NWC", "WIO", "NWC"), feature_group_count=u.shape[-1])
    return out + b


def _ssd_chunked(xs, dt, a_neg, bm, cm):
    bsz, l, h, p = xs.shape
    g, n = bm.shape[2], bm.shape[3]
    rep = h // g
    nc = l // SSD_CHUNK
    bh = jnp.repeat(bm, rep, axis=2).reshape(bsz, nc, SSD_CHUNK, h, n)
    ch = jnp.repeat(cm, rep, axis=2).reshape(bsz, nc, SSD_CHUNK, h, n)
    xdt = (xs.astype(F32) * dt[..., None]).reshape(bsz, nc, SSD_CHUNK, h, p)
    a = (dt * a_neg).reshape(bsz, nc, SSD_CHUNK, h).transpose(0, 3, 1, 2)
    a_cum = jnp.cumsum(a, axis=-1)
    causal = jnp.tril(jnp.ones((SSD_CHUNK, SSD_CHUNK), dtype=bool))
    seg = a_cum[..., :, None] - a_cum[..., None, :]
    lmat = jnp.exp(jnp.where(causal, seg, -jnp.inf))
    cb = jnp.einsum("bclhn,bcshn->bhcls", ch, bh) * lmat
    y_diag = jnp.einsum("bhcls,bcshp->bclhp", cb, xdt)
    decay_states = jnp.exp(a_cum[..., -1:] - a_cum)
    states = jnp.einsum("bcqhn,bhcq,bcqhp->bchpn", bh, decay_states, xdt)
    chunk_decay = jnp.exp(a_cum[..., -1])

    def step(carry, inp):
        st, dec = inp
        return carry * dec[..., None, None] + st, carry

    init = jnp.zeros((bsz, h, p, n), F32)
    _, prev = lax.scan(step, init, (states.transpose(1, 0, 2, 3, 4), chunk_decay.transpose(2, 0, 1)))
    prev = prev.transpose(1, 0, 2, 3, 4)
    y_off = jnp.einsum("bclhn,bchpn,bhcl->bclhp", ch, prev, jnp.exp(a_cum))
    return (y_diag + y_off).reshape(bsz, l, h, p)


def _ssd_mixer(z, xbc, dt_raw, conv_w, conv_b, dt_bias, a_log, d_skip, norm_g):
    bsz, l, _ = z.shape
    xbc = jax.nn.silu(_causal_dwconv(xbc, conv_w, conv_b))
    xs, bm, cm = jnp.split(xbc, [SSD_WIDTH, SSD_WIDTH + SSD_GROUPS * SSD_STATE], axis=-1)
    xs = xs.reshape(bsz, l, SSD_HEADS, HEAD_DIM)
    bm = bm.reshape(bsz, l, SSD_GROUPS, SSD_STATE)
    cm = cm.reshape(bsz, l, SSD_GROUPS, SSD_STATE)
    dt = jax.nn.softplus((dt_raw + dt_bias).astype(F32))
    a_neg = -jnp.exp(a_log.astype(F32))
    y = _ssd_chunked(xs, dt, a_neg, bm, cm) + d_skip[:, None] * xs
    y = y.reshape(bsz, l, SSD_WIDTH) * jax.nn.silu(z)
    yg = y.astype(F32).reshape(bsz, l, SSD_GROUPS, SSD_WIDTH // SSD_GROUPS)
    yg = yg * lax.rsqrt(jnp.mean(jnp.square(yg), axis=-1, keepdims=True) + RMS_EPS)
    return (yg.reshape(bsz, l, SSD_WIDTH) * norm_g).astype(z.dtype)


def _rwkv7_scan(r, w, k, v, a, b):
    bsz, l, h, n = r.shape
    seq = tuple(t.transpose(1, 0, 2, 3) for t in (r, w, k, v, a, b))

    def step(s, inp):
        rt, wt, kt, vt, at, bt = inp
        sa = jnp.einsum("bhij,bhj->bhi", s, at)
        s = s * wt[:, :, None, :] + sa[..., None] * bt[:, :, None, :] + vt[..., None] * kt[:, :, None, :]
        return s, jnp.einsum("bhij,bhj->bhi", s, rt)

    _, y = lax.scan(step, jnp.zeros((bsz, h, n, n), F32), seq)
    return y.transpose(1, 0, 2, 3)


def _rwkv7_mixer(s, mu, w0, w2, a0, a2, g2, k_k, k_a, r_k, lnx_g, lnx_b):
    bsz, l, _ = s.shape
    s = s + (_token_shift(s) - s) * mu
    o1, o2, o3 = RWKV_WIDTH, 2 * RWKV_WIDTH, 3 * RWKV_WIDTH
    r, k, v, w_lo, a_lo, g_lo = jnp.split(
        s, [o1, o2, o3, o3 + RWKV_DECAY_LORA, o3 + RWKV_DECAY_LORA + RWKV_ICLR_LORA], axis=-1)
    w = -jax.nn.softplus(-(w0 + jnp.tanh(w_lo) @ w2)) - 0.5
    decay = jnp.exp(-jnp.exp(w.astype(F32)))
    a = jax.nn.sigmoid(a0 + a_lo @ a2)
    g = jax.nn.sigmoid(g_lo) @ g2
    hs = (bsz, l, RWKV_HEADS, HEAD_DIM)
    kk = (k * k_k).astype(F32).reshape(hs)
    kk = kk * lax.rsqrt(jnp.maximum(jnp.sum(kk * kk, axis=-1, keepdims=True), 1e-24))
    k = k * (1 + (a - 1) * k_a)
    rh, kh, vh, ah = [t.astype(F32).reshape(hs) for t in (r, k, v, a)]
    y = _rwkv7_scan(rh, decay.reshape(hs), kh, vh, -kk, kk * ah)
    ym = jnp.mean(y, axis=-1, keepdims=True)
    yv = jnp.mean(jnp.square(y - ym), axis=-1, keepdims=True)
    y = ((y - ym) * lax.rsqrt(yv + RWKV_LNX_EPS)).reshape(bsz, l, RWKV_WIDTH) * lnx_g + lnx_b
    bonus = jnp.sum(rh * kh * r_k, axis=-1, keepdims=True) * vh
    y = (y + bonus.reshape(bsz, l, RWKV_WIDTH)) * g
    return y.astype(s.dtype)


def _moba_attention(q, k, v):
    bsz, s, h, d = q.shape
    nb = -(-s // MOBA_BLOCK)
    pad = nb * MOBA_BLOCK - s
    q = q.transpose(0, 2, 1, 3)
    k = jnp.pad(k.transpose(0, 2, 1, 3), ((0, 0), (0, 0), (0, pad), (0, 0)))
    v = jnp.pad(v.transpose(0, 2, 1, 3), ((0, 0), (0, 0), (0, pad), (0, 0)))
    kblk = k.reshape(bsz, h, nb, MOBA_BLOCK, d)
    vblk = v.reshape(bsz, h, nb, MOBA_BLOCK, d)
    kmean = jnp.mean(kblk.astype(F32), axis=3)
    topk = min(MOBA_TOPK, nb)
    nqb = s // MOBA_QBLOCK
    qb_all = q.reshape(bsz, h, nqb, MOBA_QBLOCK, d).transpose(2, 0, 1, 3, 4)
    scale = d ** -0.5
    bi = jnp.arange(bsz)[:, None, None, None]
    hi = jnp.arange(h)[None, :, None, None]
    blk_ids = jnp.arange(nb)

    def attend(inp):
        c, qb = inp
        q_pos = c * MOBA_QBLOCK + jnp.arange(MOBA_QBLOCK)
        own = (c * MOBA_QBLOCK) // MOBA_BLOCK
        gate = jnp.einsum("bhqd,bhnd->bhqn", qb.astype(F32), kmean)
        gate = jnp.where(blk_ids < own, gate, -jnp.inf)
        _, idx = lax.top_k(gate, topk)
        sel_ok = idx < own
        kg = kblk[bi, hi, idx]
        vg = vblk[bi, hi, idx]
        s_sel = jnp.einsum("bhqd,bhqjtd->bhqjt", qb, kg).astype(F32) * scale
        s_sel = jnp.where(sel_ok[..., None], s_sel, -jnp.inf).reshape(bsz, h, MOBA_QBLOCK, topk * MOBA_BLOCK)
        k_own = lax.dynamic_slice_in_dim(k, own * MOBA_BLOCK, MOBA_BLOCK, axis=2)
        v_own = lax.dynamic_slice_in_dim(v, own * MOBA_BLOCK, MOBA_BLOCK, axis=2)
        k_pos = own * MOBA_BLOCK + jnp.arange(MOBA_BLOCK)
        s_own = jnp.einsum("bhqd,bhtd->bhqt", qb, k_own).astype(F32) * scale
        s_own = jnp.where(k_pos[None, :] <= q_pos[:, None], s_own, -jnp.inf)
        p = jax.nn.softmax(jnp.concatenate([s_sel, s_own], axis=-1), axis=-1)
        p_sel = p[..., :topk * MOBA_BLOCK].reshape(bsz, h, MOBA_QBLOCK, topk, MOBA_BLOCK).astype(v.dtype)
        p_own = p[..., topk * MOBA_BLOCK:].astype(v.dtype)
        return jnp.einsum("bhqjt,bhqjtd->bhqd", p_sel, vg) + jnp.einsum("bhqt,bhtd->bhqd", p_own, v_own)

    out = lax.map(attend, (jnp.arange(nqb), qb_all))
    return out.transpose(1, 0, 3, 2, 4).reshape(bsz, s, h * d)


def _memory_cross_attention(x, mem, wq, wkv, wo):
    bsz, s, _ = x.shape
    m = mem.shape[1]
    q = (x @ wq).reshape(bsz, s, XATTN_HEADS, XATTN_HEAD_DIM)
    k, v = jnp.split(mem @ wkv, 2, axis=-1)
    k = k.reshape(bsz, m, XATTN_HEADS, XATTN_HEAD_DIM)
    v = v.reshape(bsz, m, XATTN_HEADS, XATTN_HEAD_DIM)
    sc = jnp.einsum("bshd,bmhd->bhsm", q, k).astype(F32) * XATTN_HEAD_DIM ** -0.5
    p = jax.nn.softmax(sc, axis=-1).astype(v.dtype)
    o = jnp.einsum("bhsm,bmhd->bshd", p, v).reshape(bsz, s, D_MODEL)
    return o @ wo


def _swiglu(x, w13, w2):
    gate, up = jnp.split(x @ w13, 2, axis=-1)
    return (jax.nn.silu(gate) * up) @ w2


def setup_inputs(seed: int = 0) -> dict:
    key = jax.random.key(seed)
    ks = iter(jax.random.split(key, 40))

    def nrm(shape, scale):
        return scale * jax.random.normal(next(ks), shape, F32)

    def gain(shape):
        return 1.0 + nrm(shape, 0.02)

    x = nrm((BATCH, SEQ, D_MODEL), 1.0)
    mem = nrm((BATCH, MEM_LEN, D_MODEL), 1.0)
    even_w_in = nrm((N_EVEN, D_MODEL, EVEN_IN), D_MODEL ** -0.5)
    ssd_conv_w = nrm((N_EVEN, SSD_CONV, SSD_XBC), SSD_CONV ** -0.5)
    ssd_conv_b = nrm((N_EVEN, SSD_XBC), 0.01)
    dt0 = jnp.exp(jax.random.uniform(next(ks), (N_EVEN, SSD_HEADS), F32, math.log(1e-3), math.log(1e-1)))
    ssd_dt_bias = dt0 + jnp.log(-jnp.expm1(-dt0))
    ssd_a_log = jnp.log(jax.random.uniform(next(ks), (N_EVEN, SSD_HEADS), F32, 1.0, 16.0))
    ssd_d = 1.0 + nrm((N_EVEN, SSD_HEADS), 0.1)
    ssd_norm_g = gain((N_EVEN, SSD_WIDTH))
    rwkv_mu = jax.random.uniform(next(ks), (N_EVEN, RWKV_IN), F32, 0.0, 1.0)
    rwkv_w0 = jnp.linspace(-6.5, -1.5, RWKV_WIDTH, dtype=F32)[None, :] + nrm((N_EVEN, RWKV_WIDTH), 0.1)
    rwkv_w2 = nrm((N_EVEN, RWKV_DECAY_LORA, RWKV_WIDTH), 0.1 * RWKV_DECAY_LORA ** -0.5)
    rwkv_a0 = nrm((N_EVEN, RWKV_WIDTH), 0.1)
    rwkv_a2 = nrm((N_EVEN, RWKV_ICLR_LORA, RWKV_WIDTH), 0.1 * RWKV_ICLR_LORA ** -0.5)
    rwkv_g2 = nrm((N_EVEN, RWKV_GATE_LORA, RWKV_WIDTH), RWKV_GATE_LORA ** -0.5)
    rwkv_k_k = 0.85 + nrm((N_EVEN, RWKV_WIDTH), 0.05)
    rwkv_k_a = 1.0 + nrm((N_EVEN, RWKV_WIDTH), 0.05)
    rwkv_r_k = nrm((N_EVEN, RWKV_HEADS, HEAD_DIM), 0.1)
    rwkv_lnx_g = gain((N_EVEN, RWKV_WIDTH))
    rwkv_lnx_b = nrm((N_EVEN, RWKV_WIDTH), 0.02)
    even_w_out = nrm((N_EVEN, SSD_WIDTH + RWKV_WIDTH, D_MODEL), DEEPNORM_BETA * (SSD_WIDTH + RWKV_WIDTH) ** -0.5)
    odd_w_qkv = nrm((N_ODD, D_MODEL, 3 * D_MODEL), D_MODEL ** -0.5)
    odd_w_out = nrm((N_ODD, D_MODEL, D_MODEL), DEEPNORM_BETA * D_MODEL ** -0.5)
    ln_mix_g = gain((DEPTH, D_MODEL))
    ln_mix_b = nrm((DEPTH, D_MODEL), 0.02)
    xa_wq = nrm((DEPTH, D_MODEL, D_MODEL), D_MODEL ** -0.5)
    xa_wkv = nrm((DEPTH, D_MODEL, 2 * D_MODEL), D_MODEL ** -0.5)
    xa_wo = nrm((DEPTH, D_MODEL, D_MODEL), DEEPNORM_BETA * D_MODEL ** -0.5)
    ln_xa_g = gain((DEPTH, D_MODEL))
    ln_xa_b = nrm((DEPTH, D_MODEL), 0.02)
    ffn_w13 = nrm((DEPTH, D_MODEL, 2 * FFN_HIDDEN), D_MODEL ** -0.5)
    ffn_w2 = nrm((DEPTH, FFN_HIDDEN, D_MODEL), DEEPNORM_BETA * FFN_HIDDEN ** -0.5)
    ln_ffn_g = gain((DEPTH, D_MODEL))
    ln_ffn_b = nrm((DEPTH, D_MODEL), 0.02)
    return {
        "x": x, "mem": mem, "even_w_in": even_w_in, "ssd_conv_w": ssd_conv_w,
        "ssd_conv_b": ssd_conv_b, "ssd_dt_bias": ssd_dt_bias, "ssd_a_log": ssd_a_log,
        "ssd_d": ssd_d, "ssd_norm_g": ssd_norm_g, "rwkv_mu": rwkv_mu, "rwkv_w0": rwkv_w0,
        "rwkv_w2": rwkv_w2, "rwkv_a0": rwkv_a0, "rwkv_a2": rwkv_a2, "rwkv_g2": rwkv_g2,
        "rwkv_k_k": rwkv_k_k, "rwkv_k_a": rwkv_k_a, "rwkv_r_k": rwkv_r_k,
        "rwkv_lnx_g": rwkv_lnx_g, "rwkv_lnx_b": rwkv_lnx_b, "even_w_out": even_w_out,
        "odd_w_qkv": odd_w_qkv, "odd_w_out": odd_w_out, "ln_mix_g": ln_mix_g,
        "ln_mix_b": ln_mix_b, "xa_wq": xa_wq, "xa_wkv": xa_wkv, "xa_wo": xa_wo,
        "ln_xa_g": ln_xa_g, "ln_xa_b": ln_xa_b, "ffn_w13": ffn_w13, "ffn_w2": ffn_w2,
        "ln_ffn_g": ln_ffn_g, "ln_ffn_b": ln_ffn_b,
    }


def reference(x, mem, even_w_in, ssd_conv_w, ssd_conv_b, ssd_dt_bias, ssd_a_log, ssd_d,
              ssd_norm_g, rwkv_mu, rwkv_w0, rwkv_w2, rwkv_a0, rwkv_a2, rwkv_g2, rwkv_k_k,
              rwkv_k_a, rwkv_r_k, rwkv_lnx_g, rwkv_lnx_b, even_w_out, odd_w_qkv, odd_w_out,
              ln_mix_g, ln_mix_b, xa_wq, xa_wkv, xa_wo, ln_xa_g, ln_xa_b, ffn_w13, ffn_w2,
              ln_ffn_g, ln_ffn_b):
    bsz, s, _ = x.shape
    for layer in range(DEPTH):
        j = layer // 2
        if layer % 2 == 0:
            proj = x @ even_w_in[j]
            z, xbc, dt_raw, rw = jnp.split(proj, [SSD_WIDTH, SSD_WIDTH + SSD_XBC, SSD_IN], axis=-1)
            y_ssd = _ssd_mixer(z, xbc, dt_raw, ssd_conv_w[j], ssd_conv_b[j], ssd_dt_bias[j],
                               ssd_a_log[j], ssd_d[j], ssd_norm_g[j])
            y_rwkv = _rwkv7_mixer(rw, rwkv_mu[j], rwkv_w0[j], rwkv_w2[j], rwkv_a0[j], rwkv_a2[j],
                                  rwkv_g2[j], rwkv_k_k[j], rwkv_k_a[j], rwkv_r_k[j],
                                  rwkv_lnx_g[j], rwkv_lnx_b[j])
            mix = jnp.concatenate([y_ssd, y_rwkv], axis=-1) @ even_w_out[j]
        else:
            q, k, v = jnp.split(x @ odd_w_qkv[j], 3, axis=-1)
            hs = (bsz, s, MOBA_HEADS, HEAD_DIM)
            mix = _moba_attention(q.reshape(hs), k.reshape(hs), v.reshape(hs)) @ odd_w_out[j]
        x = _layer_norm(DEEPNORM_ALPHA * x + mix, ln_mix_g[layer], ln_mix_b[layer])
        xa = _memory_cross_attention(x, mem, xa_wq[layer], xa_wkv[layer], xa_wo[layer])
        x = _layer_norm(DEEPNORM_ALPHA * x + xa, ln_xa_g[layer], ln_xa_b[layer])
        ff = _swiglu(x, ffn_w13[layer], ffn_w2[layer])
        x = _layer_norm(DEEPNORM_ALPHA * x + ff, ln_ffn_g[layer], ln_ffn_b[layer])
    return x
```

```python
import functools
import math

import jax
import jax.numpy as jnp
from jax import lax
from jax.experimental import pallas as pl
from jax.experimental.pallas import tpu as pltpu

F32 = jnp.float32
BF16 = jnp.bfloat16

HEAD_DIM = 64
LANES = 128
SSD_GROUPS = 2
SSD_STATE = 128
SSD_CONV = 4
SSD_CHUNK = 128
RWKV_DECAY_LORA = 64
RWKV_ICLR_LORA = 64
RWKV_GATE_LORA = 128
RWKV_CHUNK = 64
MOBA_BLOCK = 256
MOBA_TOPK = 3
XATTN_HEADS = 4
DEPTH = 2
DEEPNORM_ALPHA = (2 * DEPTH) ** 0.25
LN_EPS = 1e-5
RMS_EPS = 1e-5
RWKV_LNX_EPS = 64e-5
NEG_BIG = -1e30
VMEM_LIMIT = 56 * 1024 * 1024


def _params(*sem):
    return pltpu.CompilerParams(dimension_semantics=sem, vmem_limit_bytes=VMEM_LIMIT)


def _dot(a, b):
    return jnp.dot(a.astype(BF16), b.astype(BF16), preferred_element_type=F32)


def _dot_nt(a, b):
    return lax.dot_general(a.astype(BF16), b.astype(BF16), (((1,), (1,)), ((), ())),
                           preferred_element_type=F32)


def _dot_tn(a, b):
    return lax.dot_general(a.astype(BF16), b.astype(BF16), (((0,), (0,)), ((), ())),
                           preferred_element_type=F32)


def _split3(x):
    hi = x.astype(BF16)
    r1 = x - hi.astype(F32)
    mid = r1.astype(BF16)
    lo = (r1 - mid.astype(F32)).astype(BF16)
    return hi, mid, lo


def _dot_exact_lhs(m, x):
    hi, mid, lo = _split3(x)
    m = m.astype(BF16)
    return (jnp.dot(m, hi, preferred_element_type=F32) + jnp.dot(m, mid, preferred_element_type=F32)
            + jnp.dot(m, lo, preferred_element_type=F32))


def _dot_exact_rhs(x, m):
    hi, mid, lo = _split3(x)
    m = m.astype(BF16)
    return (jnp.dot(hi, m, preferred_element_type=F32) + jnp.dot(mid, m, preferred_element_type=F32)
            + jnp.dot(lo, m, preferred_element_type=F32))


def _sigmoid(x):
    return 1.0 / (1.0 + jnp.exp(-x))


def _softplus(x):
    return jnp.maximum(x, 0.0) + jnp.log1p(jnp.exp(-jnp.abs(x)))


def _layer_norm(v, g, b):
    mu = jnp.mean(v, axis=-1, keepdims=True)
    c = v - mu
    var = jnp.mean(c * c, axis=-1, keepdims=True)
    return c * lax.rsqrt(var + LN_EPS) * g + b


def _mm_kernel(x_ref, w_ref, o_ref):
    o_ref[...] = _dot(x_ref[...], w_ref[...]).astype(o_ref.dtype)


def _matmul(x, w, *, tm, tn, out_dtype=F32):
    m, k = x.shape
    n = w.shape[1]
    assert m % tm == 0 and n % tn == 0
    return pl.pallas_call(
        _mm_kernel,
        grid=(m // tm, n // tn),
        in_specs=[pl.BlockSpec((tm, k), lambda i, j: (i, 0)),
                  pl.BlockSpec((k, tn), lambda i, j: (0, j))],
        out_specs=pl.BlockSpec((tm, tn), lambda i, j: (i, j)),
        out_shape=jax.ShapeDtypeStruct((m, n), out_dtype),
        compiler_params=_params("parallel", "arbitrary"),
        name="matmul",
    )(x, w)


def _mm_res_ln_kernel(n_in, *refs):
    hs = refs[:n_in]
    ws = refs[n_in:2 * n_in]
    res_ref, g_ref, b_ref, o_ref = refs[2 * n_in:]
    acc = _dot(hs[0][...], ws[0][...])
    for h_ref, w_ref in zip(hs[1:], ws[1:]):
        acc = acc + _dot(h_ref[...], w_ref[...])
    o_ref[...] = _layer_norm(DEEPNORM_ALPHA * res_ref[...] + acc, g_ref[...], b_ref[...])


def _matmul_residual_ln(hs, ws, res, g, b, *, tm):
    m, d = res.shape
    n_in = len(hs)
    in_specs = ([pl.BlockSpec((tm, h.shape[1]), lambda i: (i, 0)) for h in hs]
                + [pl.BlockSpec(w.shape, lambda i: (0, 0)) for w in ws]
                + [pl.BlockSpec((tm, d), lambda i: (i, 0)),
                   pl.BlockSpec((1, d), lambda i: (0, 0)),
                   pl.BlockSpec((1, d), lambda i: (0, 0))])
    return pl.pallas_call(
        functools.partial(_mm_res_ln_kernel, n_in),
        grid=(m // tm,),
        in_specs=in_specs,
        out_specs=pl.BlockSpec((tm, d), lambda i: (i, 0)),
        out_shape=jax.ShapeDtypeStruct((m, d), F32),
        compiler_params=_params("parallel"),
        name="matmul_residual_ln",
    )(*hs, *ws, res, g.reshape(1, d), b.reshape(1, d))


def _swiglu_kernel(x_ref, w1_ref, w3_ref, o_ref):
    x = x_ref[...].astype(BF16)
    gate = jnp.dot(x, w1_ref[...], preferred_element_type=F32)
    up = jnp.dot(x, w3_ref[...], preferred_element_type=F32)
    o_ref[...] = (gate * _sigmoid(gate) * up).astype(o_ref.dtype)


def _swiglu_up(x, w1, w3, *, tm, tn):
    m, k = x.shape
    n = w1.shape[1]
    return pl.pallas_call(
        _swiglu_kernel,
        grid=(m // tm, n // tn),
        in_specs=[pl.BlockSpec((tm, k), lambda i, j: (i, 0)),
                  pl.BlockSpec((k, tn), lambda i, j: (0, j)),
                  pl.BlockSpec((k, tn), lambda i, j: (0, j))],
        out_specs=pl.BlockSpec((tm, tn), lambda i, j: (i, j)),
        out_shape=jax.ShapeDtypeStruct((m, n), BF16),
        compiler_params=_params("parallel", "arbitrary"),
        name="swiglu_up",
    )(x, w1, w3)


def _xattn_kernel(x_ref, kv_ref, wq_ref, wo_ref, g_ref, b_ref, o_ref):
    x = x_ref[0]
    d = x.shape[-1]
    hd = d // XATTN_HEADS
    q = _dot(x, wq_ref[...])
    kv = kv_ref[0]
    outs = []
    for h in range(XATTN_HEADS):
        qh = q[:, h * hd:(h + 1) * hd]
        kh = kv[:, h * hd:(h + 1) * hd]
        vh = kv[:, d + h * hd:d + (h + 1) * hd]
        s = _dot_nt(qh, kh) * (hd ** -0.5)
        s = s - jnp.max(s, axis=-1, keepdims=True)
        p = jnp.exp(s)
        p = p / jnp.sum(p, axis=-1, keepdims=True)
        outs.append(_dot(p, vh))
    o = jnp.concatenate(outs, axis=-1)
    xa = _dot(o, wo_ref[...])
    o_ref[0] = _layer_norm(DEEPNORM_ALPHA * x + xa, g_ref[...], b_ref[...])


def _cross_attention_ln(x, kv, wq, wo, g, b, *, tm):
    bsz, s, d = x.shape
    m = kv.shape[1]
    return pl.pallas_call(
        _xattn_kernel,
        grid=(bsz, s // tm),
        in_specs=[pl.BlockSpec((1, tm, d), lambda bi, i: (bi, i, 0)),
                  pl.BlockSpec((1, m, 2 * d), lambda bi, i: (bi, 0, 0)),
                  pl.BlockSpec((d, d), lambda bi, i: (0, 0)),
                  pl.BlockSpec((d, d), lambda bi, i: (0, 0)),
                  pl.BlockSpec((1, d), lambda bi, i: (0, 0)),
                  pl.BlockSpec((1, d), lambda bi, i: (0, 0))],
        out_specs=pl.BlockSpec((1, tm, d), lambda bi, i: (bi, i, 0)),
        out_shape=jax.ShapeDtypeStruct((bsz, s, d), F32),
        compiler_params=_params("parallel", "parallel"),
        name="cross_attention_ln",
    )(x, kv, wq, wo, g.reshape(1, d), b.reshape(1, d))


def _ssd_kernel(z_ref, xbc_ref, xbcp_ref, dt_ref, dtt_ref, cw_ref, cb_ref, dtb_ref, dtbt_ref,
                aneg_ref, anegt_ref, dskip_ref, ng_ref, y_ref, state_ref, ext_ref):
    q = SSD_CHUNK
    width = z_ref.shape[-1]
    n_pairs = width // LANES
    c = pl.program_id(1)

    @pl.when(c == 0)
    def _():
        state_ref[...] = jnp.zeros_like(state_ref)

    ext_ref[0:8, :] = jnp.where(c > 0, xbcp_ref[0], 0.0)
    ext_ref[8:8 + q, :] = xbc_ref[0]
    conv = cb_ref[...] + cw_ref[SSD_CONV - 1:SSD_CONV, :] * ext_ref[8:8 + q, :]
    for k in range(SSD_CONV - 1):
        off = 8 - (SSD_CONV - 1) + k
        conv = conv + cw_ref[k:k + 1, :] * ext_ref[off:off + q, :]
    xc = conv * _sigmoid(conv)
    xs = xc[:, :width]
    gn = SSD_GROUPS * SSD_STATE
    bm = xc[:, width:width + gn]
    cm = xc[:, width + gn:width + 2 * gn]

    dt = _softplus(dt_ref[0] + dtb_ref[...])
    a = dt * aneg_ref[...]
    dtt = _softplus(dtt_ref[0] + dtbt_ref[...])
    at = dtt * anegt_ref[...]
    row = lax.broadcasted_iota(jnp.int32, (q, q), 0)
    col = lax.broadcasted_iota(jnp.int32, (q, q), 1)
    causal = row >= col
    tri = jnp.where(causal, 1.0, 0.0)
    a_cum = _dot_exact_lhs(tri, a)
    a_cumt = _dot_exact_rhs(at, jnp.where(row <= col, 1.0, 0.0))

    lane = lax.broadcasted_iota(jnp.int32, (1, LANES), 1)
    lane_lo = lane < HEAD_DIM
    rowp = lax.broadcasted_iota(jnp.int32, (LANES, 1), 0)
    pairs_per_group = n_pairs // SSD_GROUPS
    ys = []
    for p in range(n_pairs):
        g = p // pairs_per_group
        h0, h1 = 2 * p, 2 * p + 1
        bg = bm[:, g * SSD_STATE:(g + 1) * SSD_STATE]
        cg = cm[:, g * SSD_STATE:(g + 1) * SSD_STATE]
        cb = _dot_nt(cg, bg)
        xs_p = xs[:, p * LANES:(p + 1) * LANES]
        dt_p = jnp.where(lane_lo, dt[:, h0:h0 + 1], dt[:, h1:h1 + 1])
        acum_p = jnp.where(lane_lo, a_cum[:, h0:h0 + 1], a_cum[:, h1:h1 + 1])
        xdt = xs_p * dt_p
        ms = []
        for h in (h0, h1):
            seg = a_cum[:, h:h + 1] - a_cumt[h:h + 1, :]
            ms.append(cb * jnp.exp(jnp.where(causal, seg, NEG_BIG)))
        m2 = jnp.concatenate(ms, axis=1)
        x2 = jnp.concatenate([jnp.where(lane_lo, xdt, 0.0), jnp.where(lane_lo, 0.0, xdt)], axis=0)
        y_diag = _dot(m2, x2)
        prev = state_ref[p]
        y_off = _dot_nt(cg, prev) * jnp.exp(acum_p)
        a_last = jnp.where(lane_lo, a_cum[q - 1:q, h0:h0 + 1], a_cum[q - 1:q, h1:h1 + 1])
        xdw = xdt * jnp.exp(a_last - acum_p)
        st = _dot_tn(xdw, bg)
        cd = jnp.where(rowp < HEAD_DIM, jnp.exp(a_cumt[h0:h0 + 1, q - 1:q]),
                       jnp.exp(a_cumt[h1:h1 + 1, q - 1:q]))
        state_ref[p] = prev * cd + st
        d_p = dskip_ref[:, p * LANES:(p + 1) * LANES]
        ys.append(y_diag + y_off + d_p * xs_p)
    y = jnp.concatenate(ys, axis=1)
    z = z_ref[0]
    y = y * (z * _sigmoid(z))
    gw = width // SSD_GROUPS
    outs = []
    for g in range(SSD_GROUPS):
        yg = y[:, g * gw:(g + 1) * gw]
        outs.append(yg * lax.rsqrt(jnp.mean(yg * yg, axis=-1, keepdims=True) + RMS_EPS))
    y_ref[0] = (jnp.concatenate(outs, axis=1) * ng_ref[...]).astype(y_ref.dtype)


def _ssd_mixer(z, xbc, dt_pad, conv_w, conv_b, dt_bias, a_log, d_skip, norm_g):
    bsz, l, width = z.shape
    heads = width // HEAD_DIM
    xw = xbc.shape[-1]
    q = SSD_CHUNK
    nc = l // q
    dtt = jnp.swapaxes(dt_pad[:, :, :heads], 1, 2)
    pad = LANES - heads
    dtb = jnp.pad(dt_bias, (0, pad)).reshape(1, LANES)
    a_neg = -jnp.exp(a_log.astype(F32))
    aneg = jnp.pad(a_neg, (0, pad)).reshape(1, LANES)
    dskip = jnp.repeat(d_skip, HEAD_DIM).reshape(1, width)
    row = lambda n: pl.BlockSpec((1, n), lambda bi, c: (0, 0))
    return pl.pallas_call(
        _ssd_kernel,
        grid=(bsz, nc),
        in_specs=[pl.BlockSpec((1, q, width), lambda bi, c: (bi, c, 0)),
                  pl.BlockSpec((1, q, xw), lambda bi, c: (bi, c, 0)),
                  pl.BlockSpec((1, 8, xw), lambda bi, c: (bi, jnp.maximum(c * (q // 8) - 1, 0), 0)),
                  pl.BlockSpec((1, q, LANES), lambda bi, c: (bi, c, 0)),
                  pl.BlockSpec((1, heads, q), lambda bi, c: (bi, 0, c)),
                  pl.BlockSpec((SSD_CONV, xw), lambda bi, c: (0, 0)),
                  row(xw), row(LANES),
                  pl.BlockSpec((heads, 1), lambda bi, c: (0, 0)),
                  row(LANES),
                  pl.BlockSpec((heads, 1), lambda bi, c: (0, 0)),
                  row(width), row(width)],
        out_specs=pl.BlockSpec((1, q, width), lambda bi, c: (bi, c, 0)),
        out_shape=jax.ShapeDtypeStruct((bsz, l, width), BF16),
        scratch_shapes=[pltpu.VMEM((width // LANES, LANES, SSD_STATE), F32),
                        pltpu.VMEM((q + 8, xw), F32)],
        compiler_params=_params("parallel", "arbitrary"),
        name="ssd_mixer",
    )(z, xbc, xbc, dt_pad, dtt, conv_w, conv_b.reshape(1, xw), dtb, dt_bias.reshape(heads, 1),
      aneg, a_neg.reshape(heads, 1), dskip, norm_g.reshape(1, width))


def _rwkv_prep_kernel(rw_ref, rwp_ref, mu_ref, w0_ref, a0_ref, kk_ref, ka_ref, w2_ref, a2_ref, g2_ref,
                      e_ref, et_ref, r_ref, lw_ref, k_ref, v_ref, an_ref, bn_ref, g_ref, ext_ref):
    tm = rw_ref.shape[1]
    width = r_ref.shape[-1]
    i = pl.program_id(1)
    rw = rw_ref[0]
    ext_ref[0:8, :] = jnp.where(i > 0, rwp_ref[0], 0.0)
    ext_ref[8:8 + tm, :] = rw
    shifted = ext_ref[7:7 + tm, :]
    s = rw + (shifted - rw) * mu_ref[...]
    r = s[:, :width]
    k = s[:, width:2 * width]
    v = s[:, 2 * width:3 * width]
    lo = s[:, 3 * width:3 * width + RWKV_DECAY_LORA + RWKV_ICLR_LORA]
    g_lo = s[:, 3 * width + RWKV_DECAY_LORA + RWKV_ICLR_LORA:]
    wv = w0_ref[...] + _dot(jnp.tanh(lo), w2_ref[...])
    w = -_softplus(-wv) - 0.5
    av = _sigmoid(a0_ref[...] + _dot(lo, a2_ref[...]))
    g = _dot(_sigmoid(g_lo), g2_ref[...])
    kkr = k * kk_ref[...]
    ss = _dot_exact_rhs(kkr * kkr, e_ref[...])
    inv = lax.rsqrt(jnp.maximum(ss, 1e-24))
    kk = kkr * _dot_exact_rhs(inv, et_ref[...])
    r_ref[0] = r
    lw_ref[0] = -jnp.exp(w)
    k_ref[0] = k * (1.0 + (av - 1.0) * ka_ref[...])
    v_ref[0] = v
    an_ref[0] = -kk
    bn_ref[0] = kk * av
    g_ref[0] = g


def _rwkv_scan_kernel(r_ref, lw_ref, k_ref, v_ref, an_ref, bn_ref, g_ref, rk_ref, lg_ref, lb_ref,
                      y_ref, state_ref):
    c = RWKV_CHUNK
    c2 = 2 * c
    width = r_ref.shape[-1]
    n_pairs = width // LANES
    ci = pl.program_id(1)

    @pl.when(ci == 0)
    def _():
        state_ref[...] = jnp.zeros_like(state_ref)

    row = lax.broadcasted_iota(jnp.int32, (c, c), 0)
    col = lax.broadcasted_iota(jnp.int32, (c, c), 1)
    tri = jnp.where(row >= col, 1.0, 0.0)
    row2 = lax.broadcasted_iota(jnp.int32, (c2, c2), 0)
    col2 = lax.broadcasted_iota(jnp.int32, (c2, c2), 1)
    same = (row2 // c) == (col2 // c)
    strict = jnp.where(same & ((row2 % c) > (col2 % c)), 1.0, 0.0)
    incl = jnp.where(same & ((row2 % c) >= (col2 % c)), 1.0, 0.0)
    eye = jnp.where(row2 == col2, 1.0, 0.0)
    head_avg = jnp.where((row2 // HEAD_DIM) == (col2 // HEAD_DIM), 1.0 / HEAD_DIM, 0.0)
    head_sum = jnp.where((row2 // HEAD_DIM) == (col2 // HEAD_DIM), 1.0, 0.0)
    lane = lax.broadcasted_iota(jnp.int32, (1, LANES), 1)
    m0 = jnp.where(lane < HEAD_DIM, 1.0, 0.0)
    m1 = 1.0 - m0

    def stack(x):
        return jnp.concatenate([x * m0, x * m1], axis=0)

    for p in range(n_pairs):
        sl = slice(p * LANES, (p + 1) * LANES)
        r = r_ref[0, :, sl]
        lw = lw_ref[0, :, sl]
        k = k_ref[0, :, sl]
        v = v_ref[0, :, sl]
        a = an_ref[0, :, sl]
        b = bn_ref[0, :, sl]
        cum = _dot_exact_lhs(tri, lw)
        cum_last = cum[c - 1:c, :]
        e_pos = jnp.exp(cum)
        e_neg = jnp.exp(-cum)
        e_tail = jnp.exp(cum_last - cum)
        at = a * jnp.exp(cum - lw)
        rt = r * e_pos
        bt = b * e_neg
        kt = k * e_neg
        lhs4 = jnp.concatenate([stack(at), stack(rt)], axis=0).astype(BF16)
        rhs4 = jnp.concatenate([bt, bt, kt, kt], axis=0).astype(BF16)
        gm = _dot_nt(lhs4, rhs4)
        a_ab = gm[:c2, :c2] * strict
        a_ak = gm[:c2, c2:] * strict
        a_rb = gm[c2:, :c2] * incl
        a_rk = gm[c2:, c2:] * incl
        pw = a_ab
        tinv = eye + pw
        for _ in range(int(math.log2(c)) - 1):
            pw = _dot(pw, pw)
            tinv = tinv + _dot(tinv, pw)
        st = state_ref[p]
        ahrh = _dot_nt(lhs4, st)
        v_stk = stack(v)
        u_stk = _dot(tinv, ahrh[:c2] + _dot(a_ak, v_stk))
        uv = jnp.concatenate([u_stk, v_stk], axis=0)
        y_stk = ahrh[c2:] + _dot(jnp.concatenate([a_rb, a_rk], axis=1), uv)
        y = y_stk[:c] + y_stk[c:]
        bk = jnp.concatenate([stack(b * e_tail), stack(k * e_tail)], axis=0)
        state_ref[p] = st * jnp.exp(cum_last) + _dot_tn(uv, bk)
        ym = _dot_exact_rhs(y, head_avg)
        yc = y - ym
        yv = _dot_exact_rhs(yc * yc, head_avg)
        yn = yc * lax.rsqrt(yv + RWKV_LNX_EPS) * lg_ref[:, sl] + lb_ref[:, sl]
        bonus = _dot_exact_rhs(r * k * rk_ref[:, sl], head_sum) * v
        y_ref[0, :, sl] = ((yn + bonus) * g_ref[0, :, sl]).astype(y_ref.dtype)


def _rwkv7_mixer(rw, mu, w0, w2, a0, a2, g2, k_k, k_a, r_k, lnx_g, lnx_b, *, tm):
    bsz, l, win = rw.shape
    width = w0.shape[0]
    heads = width // HEAD_DIM
    lora = RWKV_DECAY_LORA + RWKV_ICLR_LORA
    w2p = jnp.concatenate([w2, jnp.zeros((RWKV_ICLR_LORA, width), F32)], axis=0).astype(BF16)
    a2p = jnp.concatenate([jnp.zeros((RWKV_DECAY_LORA, width), F32), a2], axis=0).astype(BF16)
    head_of = jnp.arange(width) // HEAD_DIM
    e = (head_of[:, None] == jnp.arange(LANES)[None, :]).astype(BF16)
    et = e.T
    vec = lambda x: x.reshape(1, -1)
    row = lambda n: pl.BlockSpec((1, n), lambda bi, i: (0, 0))
    full = lambda a: pl.BlockSpec(a.shape, lambda bi, i: (0, 0))
    tile = pl.BlockSpec((1, tm, width), lambda bi, i: (bi, i, 0))
    sds = jax.ShapeDtypeStruct((bsz, l, width), F32)
    g2b = g2.astype(BF16)
    r, lw, k, v, an, bn, g = pl.pallas_call(
        _rwkv_prep_kernel,
        grid=(bsz, l // tm),
        in_specs=[pl.BlockSpec((1, tm, win), lambda bi, i: (bi, i, 0)),
                  pl.BlockSpec((1, 8, win), lambda bi, i: (bi, jnp.maximum(i * (tm // 8) - 1, 0), 0)),
                  row(win), row(width), row(width), row(width), row(width),
                  full(w2p), full(a2p), full(g2b), full(e), full(et)],
        out_specs=[tile] * 7,
        out_shape=[sds] * 7,
        scratch_shapes=[pltpu.VMEM((tm + 8, win), F32)],
        compiler_params=_params("parallel", "parallel"),
        name="rwkv_prep",
    )(rw, rw, vec(mu), vec(w0), vec(a0), vec(k_k), vec(k_a), w2p, a2p, g2b, e, et)
    c = RWKV_CHUNK
    ctile = pl.BlockSpec((1, c, width), lambda bi, ci: (bi, ci, 0))
    crow = pl.BlockSpec((1, width), lambda bi, ci: (0, 0))
    return pl.pallas_call(
        _rwkv_scan_kernel,
        grid=(bsz, l // c),
        in_specs=[ctile] * 7 + [crow] * 3,
        out_specs=ctile,
        out_shape=jax.ShapeDtypeStruct((bsz, l, width), BF16),
        scratch_shapes=[pltpu.VMEM((width // LANES, LANES, LANES), F32)],
        compiler_params=_params("parallel", "arbitrary"),
        name="rwkv_scan",
    )(r, lw, k, v, an, bn, g, vec(r_k), vec(lnx_g), vec(lnx_b))


def _moba_kernel(qt_ref, k_ref, vt_ref, o_ref, kmean_ref, sel_ref):
    blk = MOBA_BLOCK
    nb = k_ref.shape[2]
    qi = pl.program_id(2)
    scale = HEAD_DIM ** -0.5

    @pl.when(qi == 0)
    def _():
        kmean_ref[...] = jnp.mean(k_ref[0, 0], axis=1)

    qt = qt_ref[0, 0]
    gate = jnp.dot(kmean_ref[...], qt, preferred_element_type=F32, precision=lax.Precision.HIGHEST)
    rown = lax.broadcasted_iota(jnp.int32, (nb, blk), 0)
    gate = jnp.where(rown < qi, gate, -jnp.inf)
    sel = jnp.zeros((nb, blk), F32)
    for _ in range(MOBA_TOPK):
        mx = jnp.max(gate, axis=0, keepdims=True)
        first = jnp.min(jnp.where(gate == mx, rown, nb), axis=0, keepdims=True)
        pick = (rown == first) & (mx > -jnp.inf)
        sel = jnp.where(pick, 1.0, sel)
        gate = jnp.where(pick, -jnp.inf, gate)
    sel_ref[...] = sel

    qb = qt.astype(BF16)
    krow = lax.broadcasted_iota(jnp.int32, (blk, blk), 0)
    qcol = lax.broadcasted_iota(jnp.int32, (blk, blk), 1)
    s = _dot(k_ref[0, 0, qi], qb) * scale
    s = jnp.where(krow <= qcol, s, NEG_BIG)
    m = jnp.max(s, axis=0, keepdims=True)
    p = jnp.exp(s - m)
    l = jnp.sum(p, axis=0, keepdims=True)
    acc = _dot(vt_ref[0, 0, qi], p)

    def body(n, carry):
        m, l, acc = carry
        s = _dot(k_ref[0, 0, n], qb) * scale
        s = jnp.where(sel_ref[pl.ds(n, 1), :] > 0.5, s, NEG_BIG)
        m_new = jnp.maximum(m, jnp.max(s, axis=0, keepdims=True))
        alpha = jnp.exp(m - m_new)
        p = jnp.exp(s - m_new)
        l = alpha * l + jnp.sum(p, axis=0, keepdims=True)
        acc = alpha * acc + _dot(vt_ref[0, 0, n], p)
        return m_new, l, acc

    m, l, acc = lax.fori_loop(0, qi, body, (m, l, acc))
    o_ref[0, 0] = acc / l


def _moba_attention(qkv, bsz, s, heads):
    d = HEAD_DIM
    blk = MOBA_BLOCK
    assert s % blk == 0
    nb = s // blk
    qkv = qkv.reshape(bsz, s, 3, heads, d)
    qt = qkv[:, :, 0].transpose(0, 2, 3, 1)
    kb = qkv[:, :, 1].transpose(0, 2, 1, 3).reshape(bsz, heads, nb, blk, d)
    vt = qkv[:, :, 2].reshape(bsz, nb, blk, heads, d).transpose(0, 3, 1, 4, 2)
    ot = pl.pallas_call(
        _moba_kernel,
        grid=(bsz, heads, nb),
        in_specs=[pl.BlockSpec((1, 1, d, blk), lambda b, h, i: (b, h, 0, i)),
                  pl.BlockSpec((1, 1, nb, blk, d), lambda b, h, i: (b, h, 0, 0, 0)),
                  pl.BlockSpec((1, 1, nb, d, blk), lambda b, h, i: (b, h, 0, 0, 0))],
        out_specs=pl.BlockSpec((1, 1, d, blk), lambda b, h, i: (b, h, 0, i)),
        out_shape=jax.ShapeDtypeStruct((bsz, heads, d, s), F32),
        scratch_shapes=[pltpu.VMEM((nb, d), F32), pltpu.VMEM((nb, blk), F32)],
        compiler_params=_params("parallel", "parallel", "arbitrary"),
        name="moba_attention",
    )(qt, kb, vt)
    return ot.transpose(0, 3, 1, 2).reshape(bsz * s, heads * d)


def _row_tile(m):
    for t in (512, 256, 128, 64, 32, 16, 8):
        if m % t == 0:
            return t
    raise ValueError(f"row count {m} is not a multiple of 8")


def _col_tile(n, cap=1024):
    best = None
    for t in range(LANES, min(n, cap) + 1, LANES):
        if n % t == 0:
            best = t
    if best is None:
        raise ValueError(f"column count {n} is not a multiple of {LANES}")
    return best


def kernel(x, mem, even_w_in, ssd_conv_w, ssd_conv_b, ssd_dt_bias, ssd_a_log, ssd_d, ssd_norm_g, rwkv_mu, rwkv_w0, rwkv_w2, rwkv_a0, rwkv_a2, rwkv_g2, rwkv_k_k, rwkv_k_a, rwkv_r_k, rwkv_lnx_g, rwkv_lnx_b, even_w_out, odd_w_qkv, odd_w_out, ln_mix_g, ln_mix_b, xa_wq, xa_wkv, xa_wo, ln_xa_g, ln_xa_b, ffn_w13, ffn_w2, ln_ffn_g, ln_ffn_b):
    bsz, s, d = x.shape
    m = bsz * s
    tm = _row_tile(s)
    ssd_width = ssd_norm_g.shape[-1]
    ssd_heads = ssd_dt_bias.shape[-1]
    ssd_xbc = ssd_conv_b.shape[-1]
    ssd_in = ssd_width + ssd_xbc + ssd_heads
    rwkv_width = rwkv_w0.shape[-1]
    ffn_hidden = ffn_w2.shape[1]
    mem2 = mem.reshape(bsz * mem.shape[1], d)
    x2 = x.reshape(m, d)
    for layer in range(DEPTH):
        j = layer // 2
        if layer % 2 == 0:
            w_in = even_w_in[j].astype(BF16)
            w_z = w_in[:, :ssd_width]
            w_xbc = w_in[:, ssd_width:ssd_width + ssd_xbc]
            w_dt = jnp.pad(w_in[:, ssd_width + ssd_xbc:ssd_in], ((0, 0), (0, LANES - ssd_heads)))
            w_rw = w_in[:, ssd_in:]
            z = _matmul(x2, w_z, tm=tm, tn=_col_tile(ssd_width))
            xbc = _matmul(x2, w_xbc, tm=tm, tn=_col_tile(ssd_xbc))
            dt_pad = _matmul(x2, w_dt, tm=tm, tn=LANES)
            rw = _matmul(x2, w_rw, tm=tm, tn=_col_tile(w_rw.shape[1]))
            y_ssd = _ssd_mixer(z.reshape(bsz, s, -1), xbc.reshape(bsz, s, -1), dt_pad.reshape(bsz, s, -1),
                               ssd_conv_w[j], ssd_conv_b[j], ssd_dt_bias[j], ssd_a_log[j], ssd_d[j],
                               ssd_norm_g[j])
            y_rwkv = _rwkv7_mixer(rw.reshape(bsz, s, -1), rwkv_mu[j], rwkv_w0[j], rwkv_w2[j], rwkv_a0[j],
                                  rwkv_a2[j], rwkv_g2[j], rwkv_k_k[j], rwkv_k_a[j], rwkv_r_k[j],
                                  rwkv_lnx_g[j], rwkv_lnx_b[j], tm=min(tm, 256))
            w_out = even_w_out[j].astype(BF16)
            x2 = _matmul_residual_ln([y_ssd.reshape(m, -1), y_rwkv.reshape(m, -1)],
                                     [w_out[:ssd_width], w_out[ssd_width:]], x2,
                                     ln_mix_g[layer], ln_mix_b[layer], tm=tm)
        else:
            heads = d // HEAD_DIM
            qkv = _matmul(x2, odd_w_qkv[j].astype(BF16), tm=tm, tn=_col_tile(3 * d))
            attn = _moba_attention(qkv, bsz, s, heads)
            x2 = _matmul_residual_ln([attn], [odd_w_out[j].astype(BF16)], x2,
                                     ln_mix_g[layer], ln_mix_b[layer], tm=tm)
        kv = _matmul(mem2, xa_wkv[layer].astype(BF16), tm=_row_tile(mem2.shape[0]), tn=_col_tile(2 * d))
        x3 = _cross_attention_ln(x2.reshape(bsz, s, d), kv.reshape(bsz, -1, 2 * d),
                                 xa_wq[layer].astype(BF16), xa_wo[layer].astype(BF16),
                                 ln_xa_g[layer], ln_xa_b[layer], tm=tm)
        x2 = x3.reshape(m, d)
        w13 = ffn_w13[layer].astype(BF16)
        h = _swiglu_up(x2, w13[:, :ffn_hidden], w13[:, ffn_hidden:], tm=tm, tn=_col_tile(ffn_hidden, 1536))
        x2 = _matmul_residual_ln([h], [ffn_w2[layer].astype(BF16)], x2,
                                 ln_ffn_g[layer], ln_ffn_b[layer], tm=tm)
    return x2.reshape(bsz, s, d)
```

```python
import functools
import math

import jax
import jax.numpy as jnp
from jax import lax
from jax.experimental import pallas as pl
from jax.experimental.pallas import tpu as pltpu

F32 = jnp.float32
BF16 = jnp.bfloat16

HEAD_DIM = 64
LANES = 128
SSD_GROUPS = 2
SSD_STATE = 128
SSD_CONV = 4
SSD_CHUNK = 128
RWKV_DECAY_LORA = 64
RWKV_ICLR_LORA = 64
RWKV_GATE_LORA = 128
RWKV_CHUNK = 64
MOBA_BLOCK = 256
MOBA_TOPK = 3
MOBA_GROUP = 4
XATTN_HEADS = 4
DEPTH = 2
DEEPNORM_ALPHA = (2 * DEPTH) ** 0.25
LN_EPS = 1e-5
RMS_EPS = 1e-5
RWKV_LNX_EPS = 64e-5
NEG_BIG = -1e30
VMEM_LIMIT = 56 * 1024 * 1024


def _params(*sem):
    return pltpu.CompilerParams(dimension_semantics=sem, vmem_limit_bytes=VMEM_LIMIT)


def _dot(a, b):
    return jnp.dot(a.astype(BF16), b.astype(BF16), preferred_element_type=F32)


def _dot_nt(a, b):
    return lax.dot_general(a.astype(BF16), b.astype(BF16), (((1,), (1,)), ((), ())),
                           preferred_element_type=F32)


def _dot_tn(a, b):
    return lax.dot_general(a.astype(BF16), b.astype(BF16), (((0,), (0,)), ((), ())),
                           preferred_element_type=F32)


def _split3(x):
    hi = x.astype(BF16)
    r1 = x - hi.astype(F32)
    mid = r1.astype(BF16)
    lo = (r1 - mid.astype(F32)).astype(BF16)
    return hi, mid, lo


def _dot_exact_lhs(m, x):
    hi, mid, lo = _split3(x)
    m = m.astype(BF16)
    return (jnp.dot(m, hi, preferred_element_type=F32) + jnp.dot(m, mid, preferred_element_type=F32)
            + jnp.dot(m, lo, preferred_element_type=F32))


def _dot_exact_rhs(x, m):
    hi, mid, lo = _split3(x)
    m = m.astype(BF16)
    return (jnp.dot(hi, m, preferred_element_type=F32) + jnp.dot(mid, m, preferred_element_type=F32)
            + jnp.dot(lo, m, preferred_element_type=F32))


def _sigmoid(x):
    return 1.0 / (1.0 + jnp.exp(-x))


def _softplus(x):
    return jnp.maximum(x, 0.0) + jnp.log1p(jnp.exp(-jnp.abs(x)))


def _layer_norm(v, g, b):
    mu = jnp.mean(v, axis=-1, keepdims=True)
    c = v - mu
    var = jnp.mean(c * c, axis=-1, keepdims=True)
    return c * lax.rsqrt(var + LN_EPS) * g + b


def _mm_kernel(x_ref, w_ref, o_ref):
    o_ref[...] = _dot(x_ref[...], w_ref[...]).astype(o_ref.dtype)


def _matmul(x, w, *, tm, tn, out_dtype=F32):
    m, k = x.shape
    n = w.shape[1]
    assert m % tm == 0 and n % tn == 0
    return pl.pallas_call(
        _mm_kernel,
        grid=(m // tm, n // tn),
        in_specs=[pl.BlockSpec((tm, k), lambda i, j: (i, 0)),
                  pl.BlockSpec((k, tn), lambda i, j: (0, j))],
        out_specs=pl.BlockSpec((tm, tn), lambda i, j: (i, j)),
        out_shape=jax.ShapeDtypeStruct((m, n), out_dtype),
        compiler_params=_params("parallel", "arbitrary"),
        name="matmul",
    )(x, w)


def _mm_res_ln_kernel(n_in, *refs):
    hs = refs[:n_in]
    ws = refs[n_in:2 * n_in]
    res_ref, g_ref, b_ref, o_ref = refs[2 * n_in:]
    acc = _dot(hs[0][...], ws[0][...])
    for h_ref, w_ref in zip(hs[1:], ws[1:]):
        acc = acc + _dot(h_ref[...], w_ref[...])
    o_ref[...] = _layer_norm(DEEPNORM_ALPHA * res_ref[...] + acc, g_ref[...], b_ref[...])


def _matmul_residual_ln(hs, ws, res, g, b, *, tm):
    m, d = res.shape
    n_in = len(hs)
    in_specs = ([pl.BlockSpec((tm, h.shape[1]), lambda i: (i, 0)) for h in hs]
                + [pl.BlockSpec(w.shape, lambda i: (0, 0)) for w in ws]
                + [pl.BlockSpec((tm, d), lambda i: (i, 0)),
                   pl.BlockSpec((1, d), lambda i: (0, 0)),
                   pl.BlockSpec((1, d), lambda i: (0, 0))])
    return pl.pallas_call(
        functools.partial(_mm_res_ln_kernel, n_in),
        grid=(m // tm,),
        in_specs=in_specs,
        out_specs=pl.BlockSpec((tm, d), lambda i: (i, 0)),
        out_shape=jax.ShapeDtypeStruct((m, d), F32),
        compiler_params=_params("parallel"),
        name="matmul_residual_ln",
    )(*hs, *ws, res, g.reshape(1, d), b.reshape(1, d))


def _swiglu_kernel(x_ref, w1_ref, w3_ref, o_ref):
    x = x_ref[...].astype(BF16)
    gate = jnp.dot(x, w1_ref[...], preferred_element_type=F32)
    up = jnp.dot(x, w3_ref[...], preferred_element_type=F32)
    o_ref[...] = (gate * _sigmoid(gate) * up).astype(o_ref.dtype)


def _swiglu_up(x, w1, w3, *, tm, tn):
    m, k = x.shape
    n = w1.shape[1]
    return pl.pallas_call(
        _swiglu_kernel,
        grid=(m // tm, n // tn),
        in_specs=[pl.BlockSpec((tm, k), lambda i, j: (i, 0)),
                  pl.BlockSpec((k, tn), lambda i, j: (0, j)),
                  pl.BlockSpec((k, tn), lambda i, j: (0, j))],
        out_specs=pl.BlockSpec((tm, tn), lambda i, j: (i, j)),
        out_shape=jax.ShapeDtypeStruct((m, n), BF16),
        compiler_params=_params("parallel", "arbitrary"),
        name="swiglu_up",
    )(x, w1, w3)


def _xattn_kernel(x_ref, kv_ref, wq_ref, wo_ref, g_ref, b_ref, o_ref):
    x = x_ref[0]
    d = x.shape[-1]
    hd = d // XATTN_HEADS
    q = _dot(x, wq_ref[...])
    kv = kv_ref[0]
    outs = []
    for h in range(XATTN_HEADS):
        qh = q[:, h * hd:(h + 1) * hd]
        kh = kv[:, h * hd:(h + 1) * hd]
        vh = kv[:, d + h * hd:d + (h + 1) * hd]
        s = _dot_nt(qh, kh) * (hd ** -0.5)
        s = s - jnp.max(s, axis=-1, keepdims=True)
        p = jnp.exp(s)
        p = p / jnp.sum(p, axis=-1, keepdims=True)
        outs.append(_dot(p, vh))
    o = jnp.concatenate(outs, axis=-1)
    xa = _dot(o, wo_ref[...])
    o_ref[0] = _layer_norm(DEEPNORM_ALPHA * x + xa, g_ref[...], b_ref[...])


def _cross_attention_ln(x, kv, wq, wo, g, b, *, tm):
    bsz, s, d = x.shape
    m = kv.shape[1]
    return pl.pallas_call(
        _xattn_kernel,
        grid=(bsz, s // tm),
        in_specs=[pl.BlockSpec((1, tm, d), lambda bi, i: (bi, i, 0)),
                  pl.BlockSpec((1, m, 2 * d), lambda bi, i: (bi, 0, 0)),
                  pl.BlockSpec((d, d), lambda bi, i: (0, 0)),
                  pl.BlockSpec((d, d), lambda bi, i: (0, 0)),
                  pl.BlockSpec((1, d), lambda bi, i: (0, 0)),
                  pl.BlockSpec((1, d), lambda bi, i: (0, 0))],
        out_specs=pl.BlockSpec((1, tm, d), lambda bi, i: (bi, i, 0)),
        out_shape=jax.ShapeDtypeStruct((bsz, s, d), F32),
        compiler_params=_params("parallel", "parallel"),
        name="cross_attention_ln",
    )(x, kv, wq, wo, g.reshape(1, d), b.reshape(1, d))


def _ssd_kernel(z_ref, xbc_ref, xbcp_ref, dt_ref, dtt_ref, cw_ref, cb_ref, dtb_ref, dtbt_ref,
                aneg_ref, anegt_ref, dskip_ref, ng_ref, y_ref, state_ref, ext_ref):
    q = SSD_CHUNK
    width = z_ref.shape[-1]
    n_pairs = width // LANES
    c = pl.program_id(1)

    @pl.when(c == 0)
    def _():
        state_ref[...] = jnp.zeros_like(state_ref)

    ext_ref[0:8, :] = jnp.where(c > 0, xbcp_ref[0], 0.0)
    ext_ref[8:8 + q, :] = xbc_ref[0]
    conv = cb_ref[...] + cw_ref[SSD_CONV - 1:SSD_CONV, :] * ext_ref[8:8 + q, :]
    for k in range(SSD_CONV - 1):
        off = 8 - (SSD_CONV - 1) + k
        conv = conv + cw_ref[k:k + 1, :] * ext_ref[off:off + q, :]
    xc = conv * _sigmoid(conv)
    xs = xc[:, :width]
    gn = SSD_GROUPS * SSD_STATE
    bm = xc[:, width:width + gn]
    cm = xc[:, width + gn:width + 2 * gn]

    dt = _softplus(dt_ref[0] + dtb_ref[...])
    a = dt * aneg_ref[...]
    dtt = _softplus(dtt_ref[0] + dtbt_ref[...])
    at = dtt * anegt_ref[...]
    row = lax.broadcasted_iota(jnp.int32, (q, q), 0)
    col = lax.broadcasted_iota(jnp.int32, (q, q), 1)
    causal = row >= col
    tri = jnp.where(causal, 1.0, 0.0)
    a_cum = _dot_exact_lhs(tri, a)
    a_cumt = _dot_exact_rhs(at, jnp.where(row <= col, 1.0, 0.0))

    lane = lax.broadcasted_iota(jnp.int32, (1, LANES), 1)
    lane_lo = lane < HEAD_DIM
    rowp = lax.broadcasted_iota(jnp.int32, (LANES, 1), 0)
    pairs_per_group = n_pairs // SSD_GROUPS
    ys = []
    for p in range(n_pairs):
        g = p // pairs_per_group
        h0, h1 = 2 * p, 2 * p + 1
        bg = bm[:, g * SSD_STATE:(g + 1) * SSD_STATE]
        cg = cm[:, g * SSD_STATE:(g + 1) * SSD_STATE]
        cb = _dot_nt(cg, bg)
        xs_p = xs[:, p * LANES:(p + 1) * LANES]
        dt_p = jnp.where(lane_lo, dt[:, h0:h0 + 1], dt[:, h1:h1 + 1])
        acum_p = jnp.where(lane_lo, a_cum[:, h0:h0 + 1], a_cum[:, h1:h1 + 1])
        xdt = xs_p * dt_p
        ms = []
        for h in (h0, h1):
            seg = a_cum[:, h:h + 1] - a_cumt[h:h + 1, :]
            ms.append(cb * jnp.exp(jnp.where(causal, seg, NEG_BIG)))
        m2 = jnp.concatenate(ms, axis=1)
        x2 = jnp.concatenate([jnp.where(lane_lo, xdt, 0.0), jnp.where(lane_lo, 0.0, xdt)], axis=0)
        y_diag = _dot(m2, x2)
        prev = state_ref[p]
        y_off = _dot_nt(cg, prev) * jnp.exp(acum_p)
        a_last = jnp.where(lane_lo, a_cum[q - 1:q, h0:h0 + 1], a_cum[q - 1:q, h1:h1 + 1])
        xdw = xdt * jnp.exp(a_last - acum_p)
        st = _dot_tn(xdw, bg)
        cd = jnp.where(rowp < HEAD_DIM, jnp.exp(a_cumt[h0:h0 + 1, q - 1:q]),
                       jnp.exp(a_cumt[h1:h1 + 1, q - 1:q]))
        state_ref[p] = prev * cd + st
        d_p = dskip_ref[:, p * LANES:(p + 1) * LANES]
        ys.append(y_diag + y_off + d_p * xs_p)
    y = jnp.concatenate(ys, axis=1)
    z = z_ref[0]
    y = y * (z * _sigmoid(z))
    gw = width // SSD_GROUPS
    outs = []
    for g in range(SSD_GROUPS):
        yg = y[:, g * gw:(g + 1) * gw]
        outs.append(yg * lax.rsqrt(jnp.mean(yg * yg, axis=-1, keepdims=True) + RMS_EPS))
    y_ref[0] = (jnp.concatenate(outs, axis=1) * ng_ref[...]).astype(y_ref.dtype)


def _ssd_mixer(z, xbc, dt_pad, conv_w, conv_b, dt_bias, a_log, d_skip, norm_g):
    bsz, l, width = z.shape
    heads = width // HEAD_DIM
    xw = xbc.shape[-1]
    q = SSD_CHUNK
    nc = l // q
    dtt = jnp.swapaxes(dt_pad[:, :, :heads], 1, 2)
    pad = LANES - heads
    dtb = jnp.pad(dt_bias, (0, pad)).reshape(1, LANES)
    a_neg = -jnp.exp(a_log.astype(F32))
    aneg = jnp.pad(a_neg, (0, pad)).reshape(1, LANES)
    dskip = jnp.repeat(d_skip, HEAD_DIM).reshape(1, width)
    row = lambda n: pl.BlockSpec((1, n), lambda bi, c: (0, 0))
    return pl.pallas_call(
        _ssd_kernel,
        grid=(bsz, nc),
        in_specs=[pl.BlockSpec((1, q, width), lambda bi, c: (bi, c, 0)),
                  pl.BlockSpec((1, q, xw), lambda bi, c: (bi, c, 0)),
                  pl.BlockSpec((1, 8, xw), lambda bi, c: (bi, jnp.maximum(c * (q // 8) - 1, 0), 0)),
                  pl.BlockSpec((1, q, LANES), lambda bi, c: (bi, c, 0)),
                  pl.BlockSpec((1, heads, q), lambda bi, c: (bi, 0, c)),
                  pl.BlockSpec((SSD_CONV, xw), lambda bi, c: (0, 0)),
                  row(xw), row(LANES),
                  pl.BlockSpec((heads, 1), lambda bi, c: (0, 0)),
                  row(LANES),
                  pl.BlockSpec((heads, 1), lambda bi, c: (0, 0)),
                  row(width), row(width)],
        out_specs=pl.BlockSpec((1, q, width), lambda bi, c: (bi, c, 0)),
        out_shape=jax.ShapeDtypeStruct((bsz, l, width), BF16),
        scratch_shapes=[pltpu.VMEM((width // LANES, LANES, SSD_STATE), F32),
                        pltpu.VMEM((q + 8, xw), F32)],
        compiler_params=_params("parallel", "arbitrary"),
        name="ssd_mixer",
    )(z, xbc, xbc, dt_pad, dtt, conv_w, conv_b.reshape(1, xw), dtb, dt_bias.reshape(heads, 1),
      aneg, a_neg.reshape(heads, 1), dskip, norm_g.reshape(1, width))


def _rwkv_prep_kernel(rw_ref, rwp_ref, mu_ref, w0_ref, a0_ref, kk_ref, ka_ref, w2_ref, a2_ref, g2_ref,
                      e_ref, et_ref, r_ref, lw_ref, k_ref, v_ref, an_ref, bn_ref, g_ref, ext_ref):
    tm = rw_ref.shape[1]
    width = r_ref.shape[-1]
    i = pl.program_id(1)
    rw = rw_ref[0]
    ext_ref[0:8, :] = jnp.where(i > 0, rwp_ref[0], 0.0)
    ext_ref[8:8 + tm, :] = rw
    shifted = ext_ref[7:7 + tm, :]
    s = rw + (shifted - rw) * mu_ref[...]
    r = s[:, :width]
    k = s[:, width:2 * width]
    v = s[:, 2 * width:3 * width]
    lo = s[:, 3 * width:3 * width + RWKV_DECAY_LORA + RWKV_ICLR_LORA]
    g_lo = s[:, 3 * width + RWKV_DECAY_LORA + RWKV_ICLR_LORA:]
    wv = w0_ref[...] + _dot(jnp.tanh(lo), w2_ref[...])
    w = -_softplus(-wv) - 0.5
    av = _sigmoid(a0_ref[...] + _dot(lo, a2_ref[...]))
    g = _dot(_sigmoid(g_lo), g2_ref[...])
    kkr = k * kk_ref[...]
    ss = _dot_exact_rhs(kkr * kkr, e_ref[...])
    inv = lax.rsqrt(jnp.maximum(ss, 1e-24))
    kk = kkr * _dot_exact_rhs(inv, et_ref[...])
    r_ref[0] = r
    lw_ref[0] = -jnp.exp(w)
    k_ref[0] = k * (1.0 + (av - 1.0) * ka_ref[...])
    v_ref[0] = v
    an_ref[0] = -kk
    bn_ref[0] = kk * av
    g_ref[0] = g


def _rwkv_scan_kernel(r_ref, lw_ref, k_ref, v_ref, an_ref, bn_ref, g_ref, rk_ref, lg_ref, lb_ref,
                      y_ref, state_ref):
    c = RWKV_CHUNK
    c2 = 2 * c
    width = r_ref.shape[-1]
    n_pairs = width // LANES
    ci = pl.program_id(1)

    @pl.when(ci == 0)
    def _():
        state_ref[...] = jnp.zeros_like(state_ref)

    row = lax.broadcasted_iota(jnp.int32, (c, c), 0)
    col = lax.broadcasted_iota(jnp.int32, (c, c), 1)
    tri = jnp.where(row >= col, 1.0, 0.0)
    row2 = lax.broadcasted_iota(jnp.int32, (c2, c2), 0)
    col2 = lax.broadcasted_iota(jnp.int32, (c2, c2), 1)
    same = (row2 // c) == (col2 // c)
    strict = jnp.where(same & ((row2 % c) > (col2 % c)), 1.0, 0.0)
    incl = jnp.where(same & ((row2 % c) >= (col2 % c)), 1.0, 0.0)
    eye = jnp.where(row2 == col2, 1.0, 0.0)
    head_avg = jnp.where((row2 // HEAD_DIM) == (col2 // HEAD_DIM), 1.0 / HEAD_DIM, 0.0)
    head_sum = jnp.where((row2 // HEAD_DIM) == (col2 // HEAD_DIM), 1.0, 0.0)
    lane = lax.broadcasted_iota(jnp.int32, (1, LANES), 1)
    m0 = jnp.where(lane < HEAD_DIM, 1.0, 0.0)
    m1 = 1.0 - m0

    def stack(x):
        return jnp.concatenate([x * m0, x * m1], axis=0)

    sls = [slice(p * LANES, (p + 1) * LANES) for p in range(n_pairs)]
    lhs4s, rhs4s, v_stks, bks, g_lasts = [], [], [], [], []
    for sl in sls:
        r = r_ref[0, :, sl]
        lw = lw_ref[0, :, sl]
        k = k_ref[0, :, sl]
        b = bn_ref[0, :, sl]
        cum = _dot_exact_lhs(tri, lw)
        cum_last = cum[c - 1:c, :]
        e_neg = jnp.exp(-cum)
        e_tail = jnp.exp(cum_last - cum)
        at = an_ref[0, :, sl] * jnp.exp(cum - lw)
        rt = r * jnp.exp(cum)
        bt = (b * e_neg).astype(BF16)
        kt = (k * e_neg).astype(BF16)
        lhs4s.append(jnp.concatenate([stack(at), stack(rt)], axis=0).astype(BF16))
        rhs4s.append(jnp.concatenate([bt, bt, kt, kt], axis=0))
        v_stks.append(stack(v_ref[0, :, sl]).astype(BF16))
        bks.append(jnp.concatenate([stack(b * e_tail), stack(k * e_tail)], axis=0).astype(BF16))
        g_lasts.append(jnp.exp(cum_last))
    gms = [_dot_nt(l, rh) for l, rh in zip(lhs4s, rhs4s)]
    pws = [gm[:c2, :c2] * strict for gm in gms]
    a_aks = [(gm[:c2, c2:] * strict).astype(BF16) for gm in gms]
    a_rbks = [jnp.concatenate([gm[c2:, :c2] * incl, gm[c2:, c2:] * incl], axis=1).astype(BF16) for gm in gms]
    akvs = [_dot(a_ak, v_stk) for a_ak, v_stk in zip(a_aks, v_stks)]
    tinvs = [eye + pw for pw in pws]
    for _ in range(int(math.log2(c)) - 1):
        pws = [_dot(pw, pw) for pw in pws]
        tinvs = [tinv + _dot(tinv, pw) for tinv, pw in zip(tinvs, pws)]
    sts = [state_ref[p] for p in range(n_pairs)]
    ahrhs = [_dot_nt(l, st) for l, st in zip(lhs4s, sts)]
    u_stks = [_dot(tinv, ahrh[:c2] + akv) for tinv, ahrh, akv in zip(tinvs, ahrhs, akvs)]
    uvs = [jnp.concatenate([u.astype(BF16), v_stk], axis=0) for u, v_stk in zip(u_stks, v_stks)]
    for p in range(n_pairs):
        state_ref[p] = sts[p] * g_lasts[p] + _dot_tn(uvs[p], bks[p])
    y_stks = [ahrh[c2:] + _dot(a_rbk, uv) for ahrh, a_rbk, uv in zip(ahrhs, a_rbks, uvs)]
    for p, sl in enumerate(sls):
        y = y_stks[p][:c] + y_stks[p][c:]
        ym = _dot_exact_rhs(y, head_avg)
        yc = y - ym
        yv = _dot_exact_rhs(yc * yc, head_avg)
        yn = yc * lax.rsqrt(yv + RWKV_LNX_EPS) * lg_ref[:, sl] + lb_ref[:, sl]
        bonus = _dot_exact_rhs(r_ref[0, :, sl] * k_ref[0, :, sl] * rk_ref[:, sl], head_sum) * v_ref[0, :, sl]
        y_ref[0, :, sl] = ((yn + bonus) * g_ref[0, :, sl]).astype(y_ref.dtype)


def _rwkv7_mixer(rw, mu, w0, w2, a0, a2, g2, k_k, k_a, r_k, lnx_g, lnx_b, *, tm):
    bsz, l, win = rw.shape
    width = w0.shape[0]
    heads = width // HEAD_DIM
    lora = RWKV_DECAY_LORA + RWKV_ICLR_LORA
    w2p = jnp.concatenate([w2, jnp.zeros((RWKV_ICLR_LORA, width), F32)], axis=0).astype(BF16)
    a2p = jnp.concatenate([jnp.zeros((RWKV_DECAY_LORA, width), F32), a2], axis=0).astype(BF16)
    head_of = jnp.arange(width) // HEAD_DIM
    e = (head_of[:, None] == jnp.arange(LANES)[None, :]).astype(BF16)
    et = e.T
    vec = lambda x: x.reshape(1, -1)
    row = lambda n: pl.BlockSpec((1, n), lambda bi, i: (0, 0))
    full = lambda a: pl.BlockSpec(a.shape, lambda bi, i: (0, 0))
    tile = pl.BlockSpec((1, tm, width), lambda bi, i: (bi, i, 0))
    sds = jax.ShapeDtypeStruct((bsz, l, width), F32)
    g2b = g2.astype(BF16)
    r, lw, k, v, an, bn, g = pl.pallas_call(
        _rwkv_prep_kernel,
        grid=(bsz, l // tm),
        in_specs=[pl.BlockSpec((1, tm, win), lambda bi, i: (bi, i, 0)),
                  pl.BlockSpec((1, 8, win), lambda bi, i: (bi, jnp.maximum(i * (tm // 8) - 1, 0), 0)),
                  row(win), row(width), row(width), row(width), row(width),
                  full(w2p), full(a2p), full(g2b), full(e), full(et)],
        out_specs=[tile] * 7,
        out_shape=[sds] * 7,
        scratch_shapes=[pltpu.VMEM((tm + 8, win), F32)],
        compiler_params=_params("parallel", "parallel"),
        name="rwkv_prep",
    )(rw, rw, vec(mu), vec(w0), vec(a0), vec(k_k), vec(k_a), w2p, a2p, g2b, e, et)
    c = RWKV_CHUNK
    ctile = pl.BlockSpec((1, c, width), lambda bi, ci: (bi, ci, 0))
    crow = pl.BlockSpec((1, width), lambda bi, ci: (0, 0))
    return pl.pallas_call(
        _rwkv_scan_kernel,
        grid=(bsz, l // c),
        in_specs=[ctile] * 7 + [crow] * 3,
        out_specs=ctile,
        out_shape=jax.ShapeDtypeStruct((bsz, l, width), BF16),
        scratch_shapes=[pltpu.VMEM((width // LANES, LANES, LANES), F32)],
        compiler_params=_params("parallel", "arbitrary"),
        name="rwkv_scan",
    )(r, lw, k, v, an, bn, g, vec(r_k), vec(lnx_g), vec(lnx_b))


def _moba_kernel(qt_ref, k_ref, vt_ref, o_ref, kmean_ref, sel_ref):
    blk = MOBA_BLOCK
    nb = k_ref.shape[2]
    group = math.gcd(nb, MOBA_GROUP)
    qi = pl.program_id(2)
    scale = HEAD_DIM ** -0.5

    @pl.when(qi == 0)
    def _():
        kmean_ref[...] = jnp.mean(k_ref[0, 0], axis=1)

    qt = qt_ref[0, 0]
    gate = jnp.dot(kmean_ref[...], qt, preferred_element_type=F32, precision=lax.Precision.HIGHEST)
    rown = lax.broadcasted_iota(jnp.int32, (nb, blk), 0)
    gate = jnp.where(rown < qi, gate, -jnp.inf)
    sel = jnp.zeros((nb, blk), F32)
    for _ in range(MOBA_TOPK):
        mx = jnp.max(gate, axis=0, keepdims=True)
        first = jnp.min(jnp.where(gate == mx, rown, nb), axis=0, keepdims=True)
        pick = (rown == first) & (mx > -jnp.inf)
        sel = jnp.where(pick, 1.0, sel)
        gate = jnp.where(pick, -jnp.inf, gate)
    sel_ref[...] = sel

    qb = (qt * scale).astype(BF16)
    krow = lax.broadcasted_iota(jnp.int32, (blk, blk), 0)
    qcol = lax.broadcasted_iota(jnp.int32, (blk, blk), 1)
    s = _dot(k_ref[0, 0, qi], qb)
    s = jnp.where(krow <= qcol, s, NEG_BIG)
    m = jnp.max(s, axis=0, keepdims=True)
    p = jnp.exp(s - m)
    l = jnp.sum(p, axis=0, keepdims=True)
    acc = _dot(vt_ref[0, 0, qi], p)

    def body(gi, carry):
        m, l, acc = carry
        ss = []
        for g in range(group):
            n = gi * group + g
            s = _dot(k_ref[0, 0, n], qb)
            ss.append(jnp.where(sel_ref[pl.ds(n, 1), :] > 0.5, s, NEG_BIG))
        m_new = m
        for s in ss:
            m_new = jnp.maximum(m_new, jnp.max(s, axis=0, keepdims=True))
        alpha = jnp.exp(m - m_new)
        ps = [jnp.exp(s - m_new) for s in ss]
        l = alpha * l
        for p in ps:
            l = l + jnp.sum(p, axis=0, keepdims=True)
        vts = jnp.concatenate([vt_ref[0, 0, gi * group + g] for g in range(group)], axis=1)
        pcat = jnp.concatenate([p.astype(BF16) for p in ps], axis=0)
        acc = alpha * acc + _dot(vts, pcat)
        return m_new, l, acc

    m, l, acc = lax.fori_loop(0, (qi + group - 1) // group, body, (m, l, acc))
    o_ref[0, 0] = acc / l


def _moba_attention(qkv, bsz, s, heads):
    d = HEAD_DIM
    blk = MOBA_BLOCK
    assert s % blk == 0
    nb = s // blk
    qkv = qkv.reshape(bsz, s, 3, heads, d)
    qt = qkv[:, :, 0].transpose(0, 2, 3, 1)
    kb = qkv[:, :, 1].transpose(0, 2, 1, 3).reshape(bsz, heads, nb, blk, d)
    vt = qkv[:, :, 2].reshape(bsz, nb, blk, heads, d).transpose(0, 3, 1, 4, 2)
    ot = pl.pallas_call(
        _moba_kernel,
        grid=(bsz, heads, nb),
        in_specs=[pl.BlockSpec((1, 1, d, blk), lambda b, h, i: (b, h, 0, i)),
                  pl.BlockSpec((1, 1, nb, blk, d), lambda b, h, i: (b, h, 0, 0, 0)),
                  pl.BlockSpec((1, 1, nb, d, blk), lambda b, h, i: (b, h, 0, 0, 0))],
        out_specs=pl.BlockSpec((1, 1, d, blk), lambda b, h, i: (b, h, 0, i)),
        out_shape=jax.ShapeDtypeStruct((bsz, heads, d, s), F32),
        scratch_shapes=[pltpu.VMEM((nb, d), F32), pltpu.VMEM((nb, blk), F32)],
        compiler_params=_params("parallel", "parallel", "arbitrary"),
        name="moba_attention",
    )(qt, kb, vt)
    return ot.transpose(0, 3, 1, 2).reshape(bsz * s, heads * d)


def _row_tile(m):
    for t in (512, 256, 128, 64, 32, 16, 8):
        if m % t == 0:
            return t
    raise ValueError(f"row count {m} is not a multiple of 8")


def _col_tile(n, cap=1024):
    best = None
    for t in range(LANES, min(n, cap) + 1, LANES):
        if n % t == 0:
            best = t
    if best is None:
        raise ValueError(f"column count {n} is not a multiple of {LANES}")
    return best


def kernel(x, mem, even_w_in, ssd_conv_w, ssd_conv_b, ssd_dt_bias, ssd_a_log, ssd_d, ssd_norm_g, rwkv_mu, rwkv_w0, rwkv_w2, rwkv_a0, rwkv_a2, rwkv_g2, rwkv_k_k, rwkv_k_a, rwkv_r_k, rwkv_lnx_g, rwkv_lnx_b, even_w_out, odd_w_qkv, odd_w_out, ln_mix_g, ln_mix_b, xa_wq, xa_wkv, xa_wo, ln_xa_g, ln_xa_b, ffn_w13, ffn_w2, ln_ffn_g, ln_ffn_b):
    bsz, s, d = x.shape
    m = bsz * s
    tm = _row_tile(s)
    ssd_width = ssd_norm_g.shape[-1]
    ssd_heads = ssd_dt_bias.shape[-1]
    ssd_xbc = ssd_conv_b.shape[-1]
    ssd_in = ssd_width + ssd_xbc + ssd_heads
    rwkv_width = rwkv_w0.shape[-1]
    ffn_hidden = ffn_w2.shape[1]
    mem2 = mem.reshape(bsz * mem.shape[1], d)
    x2 = x.reshape(m, d)
    for layer in range(DEPTH):
        j = layer // 2
        if layer % 2 == 0:
            w_in = even_w_in[j].astype(BF16)
            w_z = w_in[:, :ssd_width]
            w_xbc = w_in[:, ssd_width:ssd_width + ssd_xbc]
            w_dt = jnp.pad(w_in[:, ssd_width + ssd_xbc:ssd_in], ((0, 0), (0, LANES - ssd_heads)))
            w_rw = w_in[:, ssd_in:]
            z = _matmul(x2, w_z, tm=tm, tn=_col_tile(ssd_width))
            xbc = _matmul(x2, w_xbc, tm=tm, tn=_col_tile(ssd_xbc))
            dt_pad = _matmul(x2, w_dt, tm=tm, tn=LANES)
            rw = _matmul(x2, w_rw, tm=tm, tn=_col_tile(w_rw.shape[1]))
            y_ssd = _ssd_mixer(z.reshape(bsz, s, -1), xbc.reshape(bsz, s, -1), dt_pad.reshape(bsz, s, -1),
                               ssd_conv_w[j], ssd_conv_b[j], ssd_dt_bias[j], ssd_a_log[j], ssd_d[j],
                               ssd_norm_g[j])
            y_rwkv = _rwkv7_mixer(rw.reshape(bsz, s, -1), rwkv_mu[j], rwkv_w0[j], rwkv_w2[j], rwkv_a0[j],
                                  rwkv_a2[j], rwkv_g2[j], rwkv_k_k[j], rwkv_k_a[j], rwkv_r_k[j],
                                  rwkv_lnx_g[j], rwkv_lnx_b[j], tm=min(tm, 256))
            w_out = even_w_out[j].astype(BF16)
            x2 = _matmul_residual_ln([y_ssd.reshape(m, -1), y_rwkv.reshape(m, -1)],
                                     [w_out[:ssd_width], w_out[ssd_width:]], x2,
                                     ln_mix_g[layer], ln_mix_b[layer], tm=tm)
        else:
            heads = d // HEAD_DIM
            qkv = _matmul(x2, odd_w_qkv[j].astype(BF16), tm=tm, tn=_col_tile(3 * d))
            attn = _moba_attention(qkv, bsz, s, heads)
            x2 = _matmul_residual_ln([attn], [odd_w_out[j].astype(BF16)], x2,
                                     ln_mix_g[layer], ln_mix_b[layer], tm=tm)
        kv = _matmul(mem2, xa_wkv[layer].astype(BF16), tm=_row_tile(mem2.shape[0]), tn=_col_tile(2 * d))
        x3 = _cross_attention_ln(x2.reshape(bsz, s, d), kv.reshape(bsz, -1, 2 * d),
                                 xa_wq[layer].astype(BF16), xa_wo[layer].astype(BF16),
                                 ln_xa_g[layer], ln_xa_b[layer], tm=tm)
        x2 = x3.reshape(m, d)
        w13 = ffn_w13[layer].astype(BF16)
        h = _swiglu_up(x2, w13[:, :ffn_hidden], w13[:, ffn_hidden:], tm=tm, tn=_col_tile(ffn_hidden, 1536))
        x2 = _matmul_residual_ln([h], [ffn_w2[layer].astype(BF16)], x2,
                                 ln_ffn_g[layer], ln_ffn_b[layer], tm=tm)
    return x2.reshape(bsz, s, d)
```

```python
import functools
import math

import jax
import jax.numpy as jnp
from jax import lax
from jax.experimental import pallas as pl
from jax.experimental.pallas import tpu as pltpu

F32 = jnp.float32
BF16 = jnp.bfloat16

HEAD_DIM = 64
LANES = 128
SSD_GROUPS = 2
SSD_STATE = 128
SSD_CONV = 4
SSD_CHUNK = 128
RWKV_DECAY_LORA = 64
RWKV_ICLR_LORA = 64
RWKV_GATE_LORA = 128
RWKV_CHUNK = 64
MOBA_BLOCK = 256
MOBA_TOPK = 3
MOBA_GROUP = 4
XATTN_HEADS = 4
DEPTH = 2
DEEPNORM_ALPHA = (2 * DEPTH) ** 0.25
LN_EPS = 1e-5
RMS_EPS = 1e-5
RWKV_LNX_EPS = 64e-5
NEG_BIG = -1e30
LOG2_E = math.log2(math.e)
VMEM_LIMIT = 56 * 1024 * 1024


def _params(*sem):
    return pltpu.CompilerParams(dimension_semantics=sem, vmem_limit_bytes=VMEM_LIMIT)


def _dot(a, b):
    return jnp.dot(a.astype(BF16), b.astype(BF16), preferred_element_type=F32)


def _dot_nt(a, b):
    return lax.dot_general(a.astype(BF16), b.astype(BF16), (((1,), (1,)), ((), ())),
                           preferred_element_type=F32)


def _dot_tn(a, b):
    return lax.dot_general(a.astype(BF16), b.astype(BF16), (((0,), (0,)), ((), ())),
                           preferred_element_type=F32)


def _split3(x):
    hi = x.astype(BF16)
    r1 = x - hi.astype(F32)
    mid = r1.astype(BF16)
    lo = (r1 - mid.astype(F32)).astype(BF16)
    return hi, mid, lo


def _dot_exact_lhs(m, x):
    hi, mid, lo = _split3(x)
    m = m.astype(BF16)
    return (jnp.dot(m, hi, preferred_element_type=F32) + jnp.dot(m, mid, preferred_element_type=F32)
            + jnp.dot(m, lo, preferred_element_type=F32))


def _dot_exact_rhs(x, m):
    hi, mid, lo = _split3(x)
    m = m.astype(BF16)
    return (jnp.dot(hi, m, preferred_element_type=F32) + jnp.dot(mid, m, preferred_element_type=F32)
            + jnp.dot(lo, m, preferred_element_type=F32))


def _sigmoid(x):
    return 1.0 / (1.0 + jnp.exp(-x))


def _softplus(x):
    return jnp.maximum(x, 0.0) + jnp.log1p(jnp.exp(-jnp.abs(x)))


def _layer_norm(v, g, b):
    mu = jnp.mean(v, axis=-1, keepdims=True)
    c = v - mu
    var = jnp.mean(c * c, axis=-1, keepdims=True)
    return c * lax.rsqrt(var + LN_EPS) * g + b


def _mm_kernel(x_ref, w_ref, o_ref):
    o_ref[...] = _dot(x_ref[...], w_ref[...]).astype(o_ref.dtype)


def _matmul(x, w, *, tm, tn, out_dtype=F32):
    m, k = x.shape
    n = w.shape[1]
    assert m % tm == 0 and n % tn == 0
    return pl.pallas_call(
        _mm_kernel,
        grid=(m // tm, n // tn),
        in_specs=[pl.BlockSpec((tm, k), lambda i, j: (i, 0)),
                  pl.BlockSpec((k, tn), lambda i, j: (0, j))],
        out_specs=pl.BlockSpec((tm, tn), lambda i, j: (i, j)),
        out_shape=jax.ShapeDtypeStruct((m, n), out_dtype),
        compiler_params=_params("parallel", "arbitrary"),
        name="matmul",
    )(x, w)


def _mm_res_ln_kernel(n_in, *refs):
    hs = refs[:n_in]
    ws = refs[n_in:2 * n_in]
    res_ref, g_ref, b_ref, o_ref = refs[2 * n_in:]
    acc = _dot(hs[0][...], ws[0][...])
    for h_ref, w_ref in zip(hs[1:], ws[1:]):
        acc = acc + _dot(h_ref[...], w_ref[...])
    o_ref[...] = _layer_norm(DEEPNORM_ALPHA * res_ref[...] + acc, g_ref[...], b_ref[...])


def _matmul_residual_ln(hs, ws, res, g, b, *, tm):
    m, d = res.shape
    n_in = len(hs)
    in_specs = ([pl.BlockSpec((tm, h.shape[1]), lambda i: (i, 0)) for h in hs]
                + [pl.BlockSpec(w.shape, lambda i: (0, 0)) for w in ws]
                + [pl.BlockSpec((tm, d), lambda i: (i, 0)),
                   pl.BlockSpec((1, d), lambda i: (0, 0)),
                   pl.BlockSpec((1, d), lambda i: (0, 0))])
    return pl.pallas_call(
        functools.partial(_mm_res_ln_kernel, n_in),
        grid=(m // tm,),
        in_specs=in_specs,
        out_specs=pl.BlockSpec((tm, d), lambda i: (i, 0)),
        out_shape=jax.ShapeDtypeStruct((m, d), F32),
        compiler_params=_params("parallel"),
        name="matmul_residual_ln",
    )(*hs, *ws, res, g.reshape(1, d), b.reshape(1, d))


def _swiglu_kernel(x_ref, w1_ref, w3_ref, o_ref):
    x = x_ref[...].astype(BF16)
    gate = jnp.dot(x, w1_ref[...], preferred_element_type=F32)
    up = jnp.dot(x, w3_ref[...], preferred_element_type=F32)
    o_ref[...] = (gate * _sigmoid(gate) * up).astype(o_ref.dtype)


def _swiglu_up(x, w1, w3, *, tm, tn):
    m, k = x.shape
    n = w1.shape[1]
    return pl.pallas_call(
        _swiglu_kernel,
        grid=(m // tm, n // tn),
        in_specs=[pl.BlockSpec((tm, k), lambda i, j: (i, 0)),
                  pl.BlockSpec((k, tn), lambda i, j: (0, j)),
                  pl.BlockSpec((k, tn), lambda i, j: (0, j))],
        out_specs=pl.BlockSpec((tm, tn), lambda i, j: (i, j)),
        out_shape=jax.ShapeDtypeStruct((m, n), BF16),
        compiler_params=_params("parallel", "arbitrary"),
        name="swiglu_up",
    )(x, w1, w3)


def _xattn_kernel(x_ref, kv_ref, wq_ref, wo_ref, g_ref, b_ref, o_ref):
    x = x_ref[0]
    d = x.shape[-1]
    hd = d // XATTN_HEADS
    q = _dot(x, wq_ref[...])
    kv = kv_ref[0]
    outs = []
    for h in range(XATTN_HEADS):
        qh = q[:, h * hd:(h + 1) * hd]
        kh = kv[:, h * hd:(h + 1) * hd]
        vh = kv[:, d + h * hd:d + (h + 1) * hd]
        s = _dot_nt(qh, kh) * (hd ** -0.5)
        s = s - jnp.max(s, axis=-1, keepdims=True)
        p = jnp.exp(s)
        p = p / jnp.sum(p, axis=-1, keepdims=True)
        outs.append(_dot(p, vh))
    o = jnp.concatenate(outs, axis=-1)
    xa = _dot(o, wo_ref[...])
    o_ref[0] = _layer_norm(DEEPNORM_ALPHA * x + xa, g_ref[...], b_ref[...])


def _cross_attention_ln(x, kv, wq, wo, g, b, *, tm):
    bsz, s, d = x.shape
    m = kv.shape[1]
    return pl.pallas_call(
        _xattn_kernel,
        grid=(bsz, s // tm),
        in_specs=[pl.BlockSpec((1, tm, d), lambda bi, i: (bi, i, 0)),
                  pl.BlockSpec((1, m, 2 * d), lambda bi, i: (bi, 0, 0)),
                  pl.BlockSpec((d, d), lambda bi, i: (0, 0)),
                  pl.BlockSpec((d, d), lambda bi, i: (0, 0)),
                  pl.BlockSpec((1, d), lambda bi, i: (0, 0)),
                  pl.BlockSpec((1, d), lambda bi, i: (0, 0))],
        out_specs=pl.BlockSpec((1, tm, d), lambda bi, i: (bi, i, 0)),
        out_shape=jax.ShapeDtypeStruct((bsz, s, d), F32),
        compiler_params=_params("parallel", "parallel"),
        name="cross_attention_ln",
    )(x, kv, wq, wo, g.reshape(1, d), b.reshape(1, d))


def _ssd_kernel(z_ref, xbc_ref, xbcp_ref, dt_ref, dtt_ref, cw_ref, cb_ref, dtb_ref, dtbt_ref,
                aneg_ref, anegt_ref, dskip_ref, ng_ref, y_ref, state_ref, ext_ref):
    q = SSD_CHUNK
    width = z_ref.shape[-1]
    n_pairs = width // LANES
    c = pl.program_id(1)

    @pl.when(c == 0)
    def _():
        state_ref[...] = jnp.zeros_like(state_ref)

    ext_ref[0:8, :] = jnp.where(c > 0, xbcp_ref[0], 0.0)
    ext_ref[8:8 + q, :] = xbc_ref[0]
    conv = cb_ref[...] + cw_ref[SSD_CONV - 1:SSD_CONV, :] * ext_ref[8:8 + q, :]
    for k in range(SSD_CONV - 1):
        off = 8 - (SSD_CONV - 1) + k
        conv = conv + cw_ref[k:k + 1, :] * ext_ref[off:off + q, :]
    xc = conv * _sigmoid(conv)
    xs = xc[:, :width]
    gn = SSD_GROUPS * SSD_STATE
    bm = xc[:, width:width + gn]
    cm = xc[:, width + gn:width + 2 * gn]

    dt = _softplus(dt_ref[0] + dtb_ref[...])
    a = dt * aneg_ref[...]
    dtt = _softplus(dtt_ref[0] + dtbt_ref[...])
    at = dtt * anegt_ref[...]
    row = lax.broadcasted_iota(jnp.int32, (q, q), 0)
    col = lax.broadcasted_iota(jnp.int32, (q, q), 1)
    causal = row >= col
    tri = jnp.where(causal, 1.0, 0.0)
    a_cum = _dot_exact_lhs(tri, a)
    a_cumt = _dot_exact_rhs(at, jnp.where(row <= col, 1.0, 0.0))

    lane = lax.broadcasted_iota(jnp.int32, (1, LANES), 1)
    lane_lo = lane < HEAD_DIM
    rowp = lax.broadcasted_iota(jnp.int32, (LANES, 1), 0)
    pairs_per_group = n_pairs // SSD_GROUPS
    ys = []
    for p in range(n_pairs):
        g = p // pairs_per_group
        h0, h1 = 2 * p, 2 * p + 1
        bg = bm[:, g * SSD_STATE:(g + 1) * SSD_STATE]
        cg = cm[:, g * SSD_STATE:(g + 1) * SSD_STATE]
        cb = _dot_nt(cg, bg)
        xs_p = xs[:, p * LANES:(p + 1) * LANES]
        dt_p = jnp.where(lane_lo, dt[:, h0:h0 + 1], dt[:, h1:h1 + 1])
        acum_p = jnp.where(lane_lo, a_cum[:, h0:h0 + 1], a_cum[:, h1:h1 + 1])
        xdt = xs_p * dt_p
        ms = []
        for h in (h0, h1):
            seg = a_cum[:, h:h + 1] - a_cumt[h:h + 1, :]
            ms.append(cb * jnp.exp(jnp.where(causal, seg, NEG_BIG)))
        m2 = jnp.concatenate(ms, axis=1)
        x2 = jnp.concatenate([jnp.where(lane_lo, xdt, 0.0), jnp.where(lane_lo, 0.0, xdt)], axis=0)
        y_diag = _dot(m2, x2)
        prev = state_ref[p]
        y_off = _dot_nt(cg, prev) * jnp.exp(acum_p)
        a_last = jnp.where(lane_lo, a_cum[q - 1:q, h0:h0 + 1], a_cum[q - 1:q, h1:h1 + 1])
        xdw = xdt * jnp.exp(a_last - acum_p)
        st = _dot_tn(xdw, bg)
        cd = jnp.where(rowp < HEAD_DIM, jnp.exp(a_cumt[h0:h0 + 1, q - 1:q]),
                       jnp.exp(a_cumt[h1:h1 + 1, q - 1:q]))
        state_ref[p] = prev * cd + st
        d_p = dskip_ref[:, p * LANES:(p + 1) * LANES]
        ys.append(y_diag + y_off + d_p * xs_p)
    y = jnp.concatenate(ys, axis=1)
    z = z_ref[0]
    y = y * (z * _sigmoid(z))
    gw = width // SSD_GROUPS
    outs = []
    for g in range(SSD_GROUPS):
        yg = y[:, g * gw:(g + 1) * gw]
        outs.append(yg * lax.rsqrt(jnp.mean(yg * yg, axis=-1, keepdims=True) + RMS_EPS))
    y_ref[0] = (jnp.concatenate(outs, axis=1) * ng_ref[...]).astype(y_ref.dtype)


def _ssd_mixer(z, xbc, dt_pad, conv_w, conv_b, dt_bias, a_log, d_skip, norm_g):
    bsz, l, width = z.shape
    heads = width // HEAD_DIM
    xw = xbc.shape[-1]
    q = SSD_CHUNK
    nc = l // q
    dtt = jnp.swapaxes(dt_pad[:, :, :heads], 1, 2)
    pad = LANES - heads
    dtb = jnp.pad(dt_bias, (0, pad)).reshape(1, LANES)
    a_neg = -jnp.exp(a_log.astype(F32))
    aneg = jnp.pad(a_neg, (0, pad)).reshape(1, LANES)
    dskip = jnp.repeat(d_skip, HEAD_DIM).reshape(1, width)
    row = lambda n: pl.BlockSpec((1, n), lambda bi, c: (0, 0))
    return pl.pallas_call(
        _ssd_kernel,
        grid=(bsz, nc),
        in_specs=[pl.BlockSpec((1, q, width), lambda bi, c: (bi, c, 0)),
                  pl.BlockSpec((1, q, xw), lambda bi, c: (bi, c, 0)),
                  pl.BlockSpec((1, 8, xw), lambda bi, c: (bi, jnp.maximum(c * (q // 8) - 1, 0), 0)),
                  pl.BlockSpec((1, q, LANES), lambda bi, c: (bi, c, 0)),
                  pl.BlockSpec((1, heads, q), lambda bi, c: (bi, 0, c)),
                  pl.BlockSpec((SSD_CONV, xw), lambda bi, c: (0, 0)),
                  row(xw), row(LANES),
                  pl.BlockSpec((heads, 1), lambda bi, c: (0, 0)),
                  row(LANES),
                  pl.BlockSpec((heads, 1), lambda bi, c: (0, 0)),
                  row(width), row(width)],
        out_specs=pl.BlockSpec((1, q, width), lambda bi, c: (bi, c, 0)),
        out_shape=jax.ShapeDtypeStruct((bsz, l, width), BF16),
        scratch_shapes=[pltpu.VMEM((width // LANES, LANES, SSD_STATE), F32),
                        pltpu.VMEM((q + 8, xw), F32)],
        compiler_params=_params("parallel", "arbitrary"),
        name="ssd_mixer",
    )(z, xbc, xbc, dt_pad, dtt, conv_w, conv_b.reshape(1, xw), dtb, dt_bias.reshape(heads, 1),
      aneg, a_neg.reshape(heads, 1), dskip, norm_g.reshape(1, width))


def _rwkv_prep_kernel(rw_ref, rwp_ref, mu_ref, w0_ref, a0_ref, kk_ref, ka_ref, w2_ref, a2_ref, g2_ref,
                      e_ref, et_ref, r_ref, lw_ref, k_ref, v_ref, an_ref, bn_ref, g_ref, ext_ref):
    tm = rw_ref.shape[1]
    width = r_ref.shape[-1]
    i = pl.program_id(1)
    rw = rw_ref[0]
    ext_ref[0:8, :] = jnp.where(i > 0, rwp_ref[0], 0.0)
    ext_ref[8:8 + tm, :] = rw
    shifted = ext_ref[7:7 + tm, :]
    s = rw + (shifted - rw) * mu_ref[...]
    r = s[:, :width]
    k = s[:, width:2 * width]
    v = s[:, 2 * width:3 * width]
    lo = s[:, 3 * width:3 * width + RWKV_DECAY_LORA + RWKV_ICLR_LORA]
    g_lo = s[:, 3 * width + RWKV_DECAY_LORA + RWKV_ICLR_LORA:]
    wv = w0_ref[...] + _dot(jnp.tanh(lo), w2_ref[...])
    w = -_softplus(-wv) - 0.5
    av = _sigmoid(a0_ref[...] + _dot(lo, a2_ref[...]))
    g = _dot(_sigmoid(g_lo), g2_ref[...])
    kkr = k * kk_ref[...]
    ss = _dot_exact_rhs(kkr * kkr, e_ref[...])
    inv = lax.rsqrt(jnp.maximum(ss, 1e-24))
    kk = kkr * _dot_exact_rhs(inv, et_ref[...])
    r_ref[0] = r
    lw_ref[0] = -jnp.exp(w)
    k_ref[0] = k * (1.0 + (av - 1.0) * ka_ref[...])
    v_ref[0] = v
    an_ref[0] = -kk
    bn_ref[0] = kk * av
    g_ref[0] = g


def _rwkv_scan_kernel(r_ref, lw_ref, k_ref, v_ref, an_ref, bn_ref, g_ref, rk_ref, lg_ref, lb_ref,
                      y_ref, state_ref):
    c = RWKV_CHUNK
    c2 = 2 * c
    width = r_ref.shape[-1]
    n_pairs = width // LANES
    ci = pl.program_id(1)

    @pl.when(ci == 0)
    def _():
        state_ref[...] = jnp.zeros_like(state_ref)

    row = lax.broadcasted_iota(jnp.int32, (c, c), 0)
    col = lax.broadcasted_iota(jnp.int32, (c, c), 1)
    tri = jnp.where(row >= col, 1.0, 0.0)
    row2 = lax.broadcasted_iota(jnp.int32, (c2, c2), 0)
    col2 = lax.broadcasted_iota(jnp.int32, (c2, c2), 1)
    same = (row2 // c) == (col2 // c)
    strict = jnp.where(same & ((row2 % c) > (col2 % c)), 1.0, 0.0)
    incl = jnp.where(same & ((row2 % c) >= (col2 % c)), 1.0, 0.0)
    eye = jnp.where(row2 == col2, 1.0, 0.0)
    head_avg = jnp.where((row2 // HEAD_DIM) == (col2 // HEAD_DIM), 1.0 / HEAD_DIM, 0.0)
    head_sum = jnp.where((row2 // HEAD_DIM) == (col2 // HEAD_DIM), 1.0, 0.0)
    lane = lax.broadcasted_iota(jnp.int32, (1, LANES), 1)
    m0 = jnp.where(lane < HEAD_DIM, 1.0, 0.0)
    m1 = 1.0 - m0

    def stack(x):
        return jnp.concatenate([x * m0, x * m1], axis=0)

    sls = [slice(p * LANES, (p + 1) * LANES) for p in range(n_pairs)]
    lhs4s, rhs4s, v_stks, bks, g_lasts = [], [], [], [], []
    for sl in sls:
        r = r_ref[0, :, sl]
        lw = lw_ref[0, :, sl]
        k = k_ref[0, :, sl]
        b = bn_ref[0, :, sl]
        cum = _dot_exact_lhs(tri, lw)
        cum_last = cum[c - 1:c, :]
        e_neg = jnp.exp(-cum)
        e_tail = jnp.exp(cum_last - cum)
        at = an_ref[0, :, sl] * jnp.exp(cum - lw)
        rt = r * jnp.exp(cum)
        bt = (b * e_neg).astype(BF16)
        kt = (k * e_neg).astype(BF16)
        lhs4s.append(jnp.concatenate([stack(at), stack(rt)], axis=0).astype(BF16))
        rhs4s.append(jnp.concatenate([bt, bt, kt, kt], axis=0))
        v_stks.append(stack(v_ref[0, :, sl]).astype(BF16))
        bks.append(jnp.concatenate([stack(b * e_tail), stack(k * e_tail)], axis=0).astype(BF16))
        g_lasts.append(jnp.exp(cum_last))
    gms = [_dot_nt(l, rh) for l, rh in zip(lhs4s, rhs4s)]
    pws = [gm[:c2, :c2] * strict for gm in gms]
    a_aks = [(gm[:c2, c2:] * strict).astype(BF16) for gm in gms]
    a_rbks = [jnp.concatenate([gm[c2:, :c2] * incl, gm[c2:, c2:] * incl], axis=1).astype(BF16) for gm in gms]
    akvs = [_dot(a_ak, v_stk) for a_ak, v_stk in zip(a_aks, v_stks)]
    tinvs = [eye + pw for pw in pws]
    for _ in range(int(math.log2(c)) - 1):
        pws = [_dot(pw, pw) for pw in pws]
        tinvs = [tinv + _dot(tinv, pw) for tinv, pw in zip(tinvs, pws)]
    sts = [state_ref[p] for p in range(n_pairs)]
    ahrhs = [_dot_nt(l, st) for l, st in zip(lhs4s, sts)]
    u_stks = [_dot(tinv, ahrh[:c2] + akv) for tinv, ahrh, akv in zip(tinvs, ahrhs, akvs)]
    uvs = [jnp.concatenate([u.astype(BF16), v_stk], axis=0) for u, v_stk in zip(u_stks, v_stks)]
    for p in range(n_pairs):
        state_ref[p] = sts[p] * g_lasts[p] + _dot_tn(uvs[p], bks[p])
    y_stks = [ahrh[c2:] + _dot(a_rbk, uv) for ahrh, a_rbk, uv in zip(ahrhs, a_rbks, uvs)]
    for p, sl in enumerate(sls):
        y = y_stks[p][:c] + y_stks[p][c:]
        ym = _dot_exact_rhs(y, head_avg)
        yc = y - ym
        yv = _dot_exact_rhs(yc * yc, head_avg)
        yn = yc * lax.rsqrt(yv + RWKV_LNX_EPS) * lg_ref[:, sl] + lb_ref[:, sl]
        bonus = _dot_exact_rhs(r_ref[0, :, sl] * k_ref[0, :, sl] * rk_ref[:, sl], head_sum) * v_ref[0, :, sl]
        y_ref[0, :, sl] = ((yn + bonus) * g_ref[0, :, sl]).astype(y_ref.dtype)


def _rwkv7_mixer(rw, mu, w0, w2, a0, a2, g2, k_k, k_a, r_k, lnx_g, lnx_b, *, tm):
    bsz, l, win = rw.shape
    width = w0.shape[0]
    heads = width // HEAD_DIM
    lora = RWKV_DECAY_LORA + RWKV_ICLR_LORA
    w2p = jnp.concatenate([w2, jnp.zeros((RWKV_ICLR_LORA, width), F32)], axis=0).astype(BF16)
    a2p = jnp.concatenate([jnp.zeros((RWKV_DECAY_LORA, width), F32), a2], axis=0).astype(BF16)
    head_of = jnp.arange(width) // HEAD_DIM
    e = (head_of[:, None] == jnp.arange(LANES)[None, :]).astype(BF16)
    et = e.T
    vec = lambda x: x.reshape(1, -1)
    row = lambda n: pl.BlockSpec((1, n), lambda bi, i: (0, 0))
    full = lambda a: pl.BlockSpec(a.shape, lambda bi, i: (0, 0))
    tile = pl.BlockSpec((1, tm, width), lambda bi, i: (bi, i, 0))
    sds = jax.ShapeDtypeStruct((bsz, l, width), F32)
    g2b = g2.astype(BF16)
    r, lw, k, v, an, bn, g = pl.pallas_call(
        _rwkv_prep_kernel,
        grid=(bsz, l // tm),
        in_specs=[pl.BlockSpec((1, tm, win), lambda bi, i: (bi, i, 0)),
                  pl.BlockSpec((1, 8, win), lambda bi, i: (bi, jnp.maximum(i * (tm // 8) - 1, 0), 0)),
                  row(win), row(width), row(width), row(width), row(width),
                  full(w2p), full(a2p), full(g2b), full(e), full(et)],
        out_specs=[tile] * 7,
        out_shape=[sds] * 7,
        scratch_shapes=[pltpu.VMEM((tm + 8, win), F32)],
        compiler_params=_params("parallel", "parallel"),
        name="rwkv_prep",
    )(rw, rw, vec(mu), vec(w0), vec(a0), vec(k_k), vec(k_a), w2p, a2p, g2b, e, et)
    c = RWKV_CHUNK
    ctile = pl.BlockSpec((1, c, width), lambda bi, ci: (bi, ci, 0))
    crow = pl.BlockSpec((1, width), lambda bi, ci: (0, 0))
    return pl.pallas_call(
        _rwkv_scan_kernel,
        grid=(bsz, l // c),
        in_specs=[ctile] * 7 + [crow] * 3,
        out_specs=ctile,
        out_shape=jax.ShapeDtypeStruct((bsz, l, width), BF16),
        scratch_shapes=[pltpu.VMEM((width // LANES, LANES, LANES), F32)],
        compiler_params=_params("parallel", "arbitrary"),
        name="rwkv_scan",
    )(r, lw, k, v, an, bn, g, vec(r_k), vec(lnx_g), vec(lnx_b))


def _moba_kernel(q_ref, k_ref, v_ref, o_ref, kb_ref, vt_ref, kmean_ref):
    blk = MOBA_BLOCK
    half = HEAD_DIM
    nb = k_ref.shape[0] // blk
    nbp = kmean_ref.shape[0]
    group = math.gcd(nb, MOBA_GROUP)
    qi = pl.program_id(2)

    @pl.when(qi == 0)
    def _():
        lane = lax.broadcasted_iota(jnp.int32, (blk, LANES), 1)
        rowp = lax.broadcasted_iota(jnp.int32, (LANES, blk), 0)
        if nbp > nb:
            kmean_ref[...] = jnp.zeros_like(kmean_ref)
        for n in range(nb):
            kn = k_ref[n * blk:(n + 1) * blk, :]
            kmean_ref[n:n + 1, :] = jnp.mean(kn, axis=0, keepdims=True)
            kb_ref[0, n] = jnp.where(lane < half, kn, jnp.where(lane == half + n, 1.0, 0.0)).astype(BF16)
            kb_ref[1, n] = jnp.where(lane >= half, kn, jnp.where(lane == n, 1.0, 0.0)).astype(BF16)
            vtn = v_ref[n * blk:(n + 1) * blk, :].T
            vt_ref[0, n] = jnp.where(rowp < half, vtn, jnp.where(rowp == half, 1.0, 0.0)).astype(BF16)
            vt_ref[1, n] = jnp.where(rowp >= half, vtn, jnp.where(rowp == 0, 1.0, 0.0)).astype(BF16)

    qt = q_ref[...].T
    rown = lax.broadcasted_iota(jnp.int32, (nbp, blk), 0)
    lane_k = lax.broadcasted_iota(jnp.int32, (nbp, LANES), 1)
    km = kmean_ref[...]
    q_own, q_past = [], []
    for h in range(2):
        kmh = jnp.where((lane_k < half) if h == 0 else (lane_k >= half), km, 0.0)
        gate = jnp.dot(kmh, qt, preferred_element_type=F32, precision=lax.Precision.HIGHEST)
        gate = jnp.where(rown < qi, gate, -jnp.inf)
        bias = jnp.full((nbp, blk), NEG_BIG, F32)
        for _ in range(MOBA_TOPK):
            mx = jnp.max(gate, axis=0, keepdims=True)
            first = jnp.min(jnp.where(gate == mx, rown, nbp), axis=0, keepdims=True)
            pick = (rown == first) & (mx > -jnp.inf)
            bias = jnp.where(pick, 0.0, bias)
            gate = jnp.where(pick, -jnp.inf, gate)
        qh = qt[h * half:(h + 1) * half] * (half ** -0.5 * LOG2_E)
        zeros = jnp.zeros((half, blk), F32)
        aug = jnp.concatenate([bias, jnp.zeros((half - nbp, blk), F32)], axis=0)
        q_own.append(jnp.concatenate([qh, zeros] if h == 0 else [zeros, qh], axis=0).astype(BF16))
        q_past.append(jnp.concatenate([qh, aug] if h == 0 else [aug, qh], axis=0).astype(BF16))

    causal = (lax.broadcasted_iota(jnp.int32, (blk, blk), 0) <= lax.broadcasted_iota(jnp.int32, (blk, blk), 1))
    carry = []
    for h in range(2):
        s = jnp.dot(kb_ref[h, qi], q_own[h], preferred_element_type=F32)
        s = jnp.where(causal, s, NEG_BIG)
        m = jnp.max(s, axis=0, keepdims=True)
        p = jnp.exp2(s - m).astype(BF16)
        carry += [m, jnp.dot(vt_ref[h, qi], p, preferred_element_type=F32)]

    def body(gi, carry):
        heads = range(2)
        sss = [[jnp.dot(kb_ref[h, gi * group + g], q_past[h], preferred_element_type=F32)
                for g in range(group)] for h in heads]
        m_news = []
        for h in heads:
            m_new = carry[2 * h]
            for s in sss[h]:
                m_new = jnp.maximum(m_new, jnp.max(s, axis=0, keepdims=True))
            m_news.append(m_new)
        pcats = [jnp.concatenate([jnp.exp2(s - m_news[h]).astype(BF16) for s in sss[h]], axis=0)
                 for h in heads]
        out = []
        for h in heads:
            vts = jnp.concatenate([vt_ref[h, gi * group + g] for g in range(group)], axis=1)
            alpha = jnp.exp2(carry[2 * h] - m_news[h])
            out += [m_news[h], alpha * carry[2 * h + 1] + jnp.dot(vts, pcats[h], preferred_element_type=F32)]
        return tuple(out)

    _, acc0, _, acc1 = lax.fori_loop(0, (qi + group - 1) // group, body, tuple(carry))
    rowq = lax.broadcasted_iota(jnp.int32, (LANES, blk), 0)
    out_t = jnp.where(rowq < half, acc0 / acc0[half:half + 1], acc1 / acc1[0:1])
    o_ref[...] = out_t.T.astype(o_ref.dtype)


def _moba_attention(qkv, bsz, s, heads):
    blk = MOBA_BLOCK
    assert s % blk == 0 and (heads * HEAD_DIM) % LANES == 0
    nb = s // blk
    assert nb <= HEAD_DIM
    nbp = -(-nb // 8) * 8
    pairs = heads * HEAD_DIM // LANES
    return pl.pallas_call(
        _moba_kernel,
        grid=(bsz, pairs, nb),
        in_specs=[pl.BlockSpec((blk, LANES), lambda b, p, i: (b * nb + i, p)),
                  pl.BlockSpec((s, LANES), lambda b, p, i: (b, pairs + p)),
                  pl.BlockSpec((s, LANES), lambda b, p, i: (b, 2 * pairs + p))],
        out_specs=pl.BlockSpec((blk, LANES), lambda b, p, i: (b * nb + i, p)),
        out_shape=jax.ShapeDtypeStruct((bsz * s, heads * HEAD_DIM), BF16),
        scratch_shapes=[pltpu.VMEM((2, nb, blk, LANES), BF16), pltpu.VMEM((2, nb, LANES, blk), BF16),
                        pltpu.VMEM((nbp, LANES), F32)],
        compiler_params=_params("parallel", "parallel", "arbitrary"),
        name="moba_attention",
    )(qkv, qkv, qkv)


def _row_tile(m):
    for t in (512, 256, 128, 64, 32, 16, 8):
        if m % t == 0:
            return t
    raise ValueError(f"row count {m} is not a multiple of 8")


def _col_tile(n, cap=2048):
    best = None
    for t in range(LANES, min(n, cap) + 1, LANES):
        if n % t == 0:
            best = t
    if best is None:
        raise ValueError(f"column count {n} is not a multiple of {LANES}")
    return best


def kernel(x, mem, even_w_in, ssd_conv_w, ssd_conv_b, ssd_dt_bias, ssd_a_log, ssd_d, ssd_norm_g, rwkv_mu, rwkv_w0, rwkv_w2, rwkv_a0, rwkv_a2, rwkv_g2, rwkv_k_k, rwkv_k_a, rwkv_r_k, rwkv_lnx_g, rwkv_lnx_b, even_w_out, odd_w_qkv, odd_w_out, ln_mix_g, ln_mix_b, xa_wq, xa_wkv, xa_wo, ln_xa_g, ln_xa_b, ffn_w13, ffn_w2, ln_ffn_g, ln_ffn_b):
    bsz, s, d = x.shape
    m = bsz * s
    tm = _row_tile(s)
    ssd_width = ssd_norm_g.shape[-1]
    ssd_heads = ssd_dt_bias.shape[-1]
    ssd_xbc = ssd_conv_b.shape[-1]
    ssd_in = ssd_width + ssd_xbc + ssd_heads
    rwkv_width = rwkv_w0.shape[-1]
    ffn_hidden = ffn_w2.shape[1]
    mem2 = mem.reshape(bsz * mem.shape[1], d)
    x2 = x.reshape(m, d)
    for layer in range(DEPTH):
        j = layer // 2
        if layer % 2 == 0:
            w_in = even_w_in[j].astype(BF16)
            w_z = w_in[:, :ssd_width]
            w_xbc = w_in[:, ssd_width:ssd_width + ssd_xbc]
            w_dt = jnp.pad(w_in[:, ssd_width + ssd_xbc:ssd_in], ((0, 0), (0, LANES - ssd_heads)))
            w_rw = w_in[:, ssd_in:]
            z = _matmul(x2, w_z, tm=tm, tn=_col_tile(ssd_width))
            xbc = _matmul(x2, w_xbc, tm=tm, tn=_col_tile(ssd_xbc))
            dt_pad = _matmul(x2, w_dt, tm=tm, tn=LANES)
            rw = _matmul(x2, w_rw, tm=tm, tn=_col_tile(w_rw.shape[1]))
            y_ssd = _ssd_mixer(z.reshape(bsz, s, -1), xbc.reshape(bsz, s, -1), dt_pad.reshape(bsz, s, -1),
                               ssd_conv_w[j], ssd_conv_b[j], ssd_dt_bias[j], ssd_a_log[j], ssd_d[j],
                               ssd_norm_g[j])
            y_rwkv = _rwkv7_mixer(rw.reshape(bsz, s, -1), rwkv_mu[j], rwkv_w0[j], rwkv_w2[j], rwkv_a0[j],
                                  rwkv_a2[j], rwkv_g2[j], rwkv_k_k[j], rwkv_k_a[j], rwkv_r_k[j],
                                  rwkv_lnx_g[j], rwkv_lnx_b[j], tm=min(tm, 256))
            w_out = even_w_out[j].astype(BF16)
            x2 = _matmul_residual_ln([y_ssd.reshape(m, -1), y_rwkv.reshape(m, -1)],
                                     [w_out[:ssd_width], w_out[ssd_width:]], x2,
                                     ln_mix_g[layer], ln_mix_b[layer], tm=tm)
        else:
            heads = d // HEAD_DIM
            qkv = _matmul(x2, odd_w_qkv[j].astype(BF16), tm=tm, tn=_col_tile(3 * d))
            attn = _moba_attention(qkv, bsz, s, heads)
            x2 = _matmul_residual_ln([attn], [odd_w_out[j].astype(BF16)], x2,
                                     ln_mix_g[layer], ln_mix_b[layer], tm=tm)
        kv = _matmul(mem2, xa_wkv[layer].astype(BF16), tm=_row_tile(mem2.shape[0]), tn=_col_tile(2 * d))
        x3 = _cross_attention_ln(x2.reshape(bsz, s, d), kv.reshape(bsz, -1, 2 * d),
                                 xa_wq[layer].astype(BF16), xa_wo[layer].astype(BF16),
                                 ln_xa_g[layer], ln_xa_b[layer], tm=tm)
        x2 = x3.reshape(m, d)
        w13 = ffn_w13[layer].astype(BF16)
        h = _swiglu_up(x2, w13[:, :ffn_hidden], w13[:, ffn_hidden:], tm=tm, tn=_col_tile(ffn_hidden, 1536))
        x2 = _matmul_residual_ln([h], [ffn_w2[layer].astype(BF16)], x2,
                                 ln_ffn_g[layer], ln_ffn_b[layer], tm=tm)
    return x2.reshape(bsz, s, d)
```

```python
import functools
import math

import jax
import jax.numpy as jnp
from jax import lax
from jax.experimental import pallas as pl
from jax.experimental.pallas import tpu as pltpu

F32 = jnp.float32
BF16 = jnp.bfloat16

HEAD_DIM = 64
LANES = 128
SSD_GROUPS = 2
SSD_STATE = 128
SSD_CONV = 4
SSD_CHUNK = 128
RWKV_DECAY_LORA = 64
RWKV_ICLR_LORA = 64
RWKV_GATE_LORA = 128
RWKV_CHUNK = 64
MOBA_BLOCK = 256
MOBA_TOPK = 3
MOBA_GROUP = 4
XATTN_HEADS = 4
DEPTH = 2
DEEPNORM_ALPHA = (2 * DEPTH) ** 0.25
LN_EPS = 1e-5
RMS_EPS = 1e-5
RWKV_LNX_EPS = 64e-5
NEG_BIG = -1e30
LOG2_E = math.log2(math.e)
VMEM_LIMIT = 56 * 1024 * 1024


def _params(*sem):
    return pltpu.CompilerParams(dimension_semantics=sem, vmem_limit_bytes=VMEM_LIMIT)


def _dot(a, b):
    return jnp.dot(a.astype(BF16), b.astype(BF16), preferred_element_type=F32)


def _dot_nt(a, b):
    return lax.dot_general(a.astype(BF16), b.astype(BF16), (((1,), (1,)), ((), ())),
                           preferred_element_type=F32)


def _dot_tn(a, b):
    return lax.dot_general(a.astype(BF16), b.astype(BF16), (((0,), (0,)), ((), ())),
                           preferred_element_type=F32)


def _split3(x):
    hi = x.astype(BF16)
    r1 = x - hi.astype(F32)
    mid = r1.astype(BF16)
    lo = (r1 - mid.astype(F32)).astype(BF16)
    return hi, mid, lo


def _dot_exact_lhs(m, x):
    hi, mid, lo = _split3(x)
    m = m.astype(BF16)
    return (jnp.dot(m, hi, preferred_element_type=F32) + jnp.dot(m, mid, preferred_element_type=F32)
            + jnp.dot(m, lo, preferred_element_type=F32))


def _dot_exact_rhs(x, m):
    hi, mid, lo = _split3(x)
    m = m.astype(BF16)
    return (jnp.dot(hi, m, preferred_element_type=F32) + jnp.dot(mid, m, preferred_element_type=F32)
            + jnp.dot(lo, m, preferred_element_type=F32))


def _sigmoid(x):
    return 1.0 / (1.0 + jnp.exp(-x))


def _softplus(x):
    return jnp.maximum(x, 0.0) + jnp.log1p(jnp.exp(-jnp.abs(x)))


def _layer_norm(v, g, b):
    mu = jnp.mean(v, axis=-1, keepdims=True)
    c = v - mu
    var = jnp.mean(c * c, axis=-1, keepdims=True)
    return c * lax.rsqrt(var + LN_EPS) * g + b


def _mm_kernel(x_ref, w_ref, o_ref):
    o_ref[...] = _dot(x_ref[...], w_ref[...]).astype(o_ref.dtype)


def _matmul(x, w, *, tm, tn, out_dtype=F32):
    m, k = x.shape
    n = w.shape[1]
    assert m % tm == 0 and n % tn == 0
    return pl.pallas_call(
        _mm_kernel,
        grid=(m // tm, n // tn),
        in_specs=[pl.BlockSpec((tm, k), lambda i, j: (i, 0)),
                  pl.BlockSpec((k, tn), lambda i, j: (0, j))],
        out_specs=pl.BlockSpec((tm, tn), lambda i, j: (i, j)),
        out_shape=jax.ShapeDtypeStruct((m, n), out_dtype),
        compiler_params=_params("parallel", "arbitrary"),
        name="matmul",
    )(x, w)


def _mm_res_ln_kernel(n_in, *refs):
    hs = refs[:n_in]
    ws = refs[n_in:2 * n_in]
    res_ref, g_ref, b_ref, o_ref = refs[2 * n_in:]
    acc = _dot(hs[0][...], ws[0][...])
    for h_ref, w_ref in zip(hs[1:], ws[1:]):
        acc = acc + _dot(h_ref[...], w_ref[...])
    o_ref[...] = _layer_norm(DEEPNORM_ALPHA * res_ref[...] + acc, g_ref[...], b_ref[...])


def _matmul_residual_ln(hs, ws, res, g, b, *, tm):
    m, d = res.shape
    n_in = len(hs)
    in_specs = ([pl.BlockSpec((tm, h.shape[1]), lambda i: (i, 0)) for h in hs]
                + [pl.BlockSpec(w.shape, lambda i: (0, 0)) for w in ws]
                + [pl.BlockSpec((tm, d), lambda i: (i, 0)),
                   pl.BlockSpec((1, d), lambda i: (0, 0)),
                   pl.BlockSpec((1, d), lambda i: (0, 0))])
    return pl.pallas_call(
        functools.partial(_mm_res_ln_kernel, n_in),
        grid=(m // tm,),
        in_specs=in_specs,
        out_specs=pl.BlockSpec((tm, d), lambda i: (i, 0)),
        out_shape=jax.ShapeDtypeStruct((m, d), F32),
        compiler_params=_params("parallel"),
        name="matmul_residual_ln",
    )(*hs, *ws, res, g.reshape(1, d), b.reshape(1, d))


def _swiglu_kernel(x_ref, w1_ref, w3_ref, o_ref):
    x = x_ref[...].astype(BF16)
    gate = jnp.dot(x, w1_ref[...], preferred_element_type=F32)
    up = jnp.dot(x, w3_ref[...], preferred_element_type=F32)
    o_ref[...] = (gate * _sigmoid(gate) * up).astype(o_ref.dtype)


def _swiglu_up(x, w1, w3, *, tm, tn):
    m, k = x.shape
    n = w1.shape[1]
    return pl.pallas_call(
        _swiglu_kernel,
        grid=(m // tm, n // tn),
        in_specs=[pl.BlockSpec((tm, k), lambda i, j: (i, 0)),
                  pl.BlockSpec((k, tn), lambda i, j: (0, j)),
                  pl.BlockSpec((k, tn), lambda i, j: (0, j))],
        out_specs=pl.BlockSpec((tm, tn), lambda i, j: (i, j)),
        out_shape=jax.ShapeDtypeStruct((m, n), BF16),
        compiler_params=_params("parallel", "arbitrary"),
        name="swiglu_up",
    )(x, w1, w3)


def _xattn_kernel(x_ref, kv_ref, wq_ref, wo_ref, g_ref, b_ref, o_ref):
    x = x_ref[0]
    d = x.shape[-1]
    hd = d // XATTN_HEADS
    q = _dot(x, wq_ref[...])
    kv = kv_ref[0]
    outs = []
    for h in range(XATTN_HEADS):
        qh = q[:, h * hd:(h + 1) * hd]
        kh = kv[:, h * hd:(h + 1) * hd]
        vh = kv[:, d + h * hd:d + (h + 1) * hd]
        s = _dot_nt(qh, kh) * (hd ** -0.5)
        s = s - jnp.max(s, axis=-1, keepdims=True)
        p = jnp.exp(s)
        p = p / jnp.sum(p, axis=-1, keepdims=True)
        outs.append(_dot(p, vh))
    o = jnp.concatenate(outs, axis=-1)
    xa = _dot(o, wo_ref[...])
    o_ref[0] = _layer_norm(DEEPNORM_ALPHA * x + xa, g_ref[...], b_ref[...])


def _cross_attention_ln(x, kv, wq, wo, g, b, *, tm):
    bsz, s, d = x.shape
    m = kv.shape[1]
    return pl.pallas_call(
        _xattn_kernel,
        grid=(bsz, s // tm),
        in_specs=[pl.BlockSpec((1, tm, d), lambda bi, i: (bi, i, 0)),
                  pl.BlockSpec((1, m, 2 * d), lambda bi, i: (bi, 0, 0)),
                  pl.BlockSpec((d, d), lambda bi, i: (0, 0)),
                  pl.BlockSpec((d, d), lambda bi, i: (0, 0)),
                  pl.BlockSpec((1, d), lambda bi, i: (0, 0)),
                  pl.BlockSpec((1, d), lambda bi, i: (0, 0))],
        out_specs=pl.BlockSpec((1, tm, d), lambda bi, i: (bi, i, 0)),
        out_shape=jax.ShapeDtypeStruct((bsz, s, d), F32),
        compiler_params=_params("parallel", "parallel"),
        name="cross_attention_ln",
    )(x, kv, wq, wo, g.reshape(1, d), b.reshape(1, d))


def _ssd_kernel(z_ref, xbc_ref, xbcp_ref, dt_ref, dtt_ref, cw_ref, cb_ref, dtb_ref, dtbt_ref,
                aneg_ref, anegt_ref, dskip_ref, ng_ref, y_ref, state_ref, ext_ref):
    q = SSD_CHUNK
    width = z_ref.shape[-1]
    n_pairs = width // LANES
    c = pl.program_id(1)

    @pl.when(c == 0)
    def _():
        state_ref[...] = jnp.zeros_like(state_ref)

    ext_ref[0:8, :] = jnp.where(c > 0, xbcp_ref[0], 0.0)
    ext_ref[8:8 + q, :] = xbc_ref[0]
    conv = cb_ref[...] + cw_ref[SSD_CONV - 1:SSD_CONV, :] * ext_ref[8:8 + q, :]
    for k in range(SSD_CONV - 1):
        off = 8 - (SSD_CONV - 1) + k
        conv = conv + cw_ref[k:k + 1, :] * ext_ref[off:off + q, :]
    xc = conv * _sigmoid(conv)
    xs = xc[:, :width]
    gn = SSD_GROUPS * SSD_STATE
    bm = xc[:, width:width + gn]
    cm = xc[:, width + gn:width + 2 * gn]

    dt = _softplus(dt_ref[0] + dtb_ref[...])
    a = dt * aneg_ref[...]
    dtt = _softplus(dtt_ref[0] + dtbt_ref[...])
    at = dtt * anegt_ref[...]
    row = lax.broadcasted_iota(jnp.int32, (q, q), 0)
    col = lax.broadcasted_iota(jnp.int32, (q, q), 1)
    causal = row >= col
    tri = jnp.where(causal, 1.0, 0.0)
    a_cum = _dot_exact_lhs(tri, a)
    a_cumt = _dot_exact_rhs(at, jnp.where(row <= col, 1.0, 0.0))

    lane = lax.broadcasted_iota(jnp.int32, (1, LANES), 1)
    lane_lo = lane < HEAD_DIM
    rowp = lax.broadcasted_iota(jnp.int32, (LANES, 1), 0)
    pairs_per_group = n_pairs // SSD_GROUPS
    ys = []
    for p in range(n_pairs):
        g = p // pairs_per_group
        h0, h1 = 2 * p, 2 * p + 1
        bg = bm[:, g * SSD_STATE:(g + 1) * SSD_STATE]
        cg = cm[:, g * SSD_STATE:(g + 1) * SSD_STATE]
        cb = _dot_nt(cg, bg)
        xs_p = xs[:, p * LANES:(p + 1) * LANES]
        dt_p = jnp.where(lane_lo, dt[:, h0:h0 + 1], dt[:, h1:h1 + 1])
        acum_p = jnp.where(lane_lo, a_cum[:, h0:h0 + 1], a_cum[:, h1:h1 + 1])
        xdt = xs_p * dt_p
        ms = []
        for h in (h0, h1):
            seg = a_cum[:, h:h + 1] - a_cumt[h:h + 1, :]
            ms.append(cb * jnp.exp(jnp.where(causal, seg, NEG_BIG)))
        m2 = jnp.concatenate(ms, axis=1)
        x2 = jnp.concatenate([jnp.where(lane_lo, xdt, 0.0), jnp.where(lane_lo, 0.0, xdt)], axis=0)
        y_diag = _dot(m2, x2)
        prev = state_ref[p]
        y_off = _dot_nt(cg, prev) * jnp.exp(acum_p)
        a_last = jnp.where(lane_lo, a_cum[q - 1:q, h0:h0 + 1], a_cum[q - 1:q, h1:h1 + 1])
        xdw = xdt * jnp.exp(a_last - acum_p)
        st = _dot_tn(xdw, bg)
        cd = jnp.where(rowp < HEAD_DIM, jnp.exp(a_cumt[h0:h0 + 1, q - 1:q]),
                       jnp.exp(a_cumt[h1:h1 + 1, q - 1:q]))
        state_ref[p] = prev * cd + st
        d_p = dskip_ref[:, p * LANES:(p + 1) * LANES]
        ys.append(y_diag + y_off + d_p * xs_p)
    y = jnp.concatenate(ys, axis=1)
    z = z_ref[0]
    y = y * (z * _sigmoid(z))
    gw = width // SSD_GROUPS
    outs = []
    for g in range(SSD_GROUPS):
        yg = y[:, g * gw:(g + 1) * gw]
        outs.append(yg * lax.rsqrt(jnp.mean(yg * yg, axis=-1, keepdims=True) + RMS_EPS))
    y_ref[0] = (jnp.concatenate(outs, axis=1) * ng_ref[...]).astype(y_ref.dtype)


def _ssd_mixer(z, xbc, dt_pad, conv_w, conv_b, dt_bias, a_log, d_skip, norm_g):
    bsz, l, width = z.shape
    heads = width // HEAD_DIM
    xw = xbc.shape[-1]
    q = SSD_CHUNK
    nc = l // q
    dtt = jnp.swapaxes(dt_pad[:, :, :heads], 1, 2)
    pad = LANES - heads
    dtb = jnp.pad(dt_bias, (0, pad)).reshape(1, LANES)
    a_neg = -jnp.exp(a_log.astype(F32))
    aneg = jnp.pad(a_neg, (0, pad)).reshape(1, LANES)
    dskip = jnp.repeat(d_skip, HEAD_DIM).reshape(1, width)
    row = lambda n: pl.BlockSpec((1, n), lambda bi, c: (0, 0))
    return pl.pallas_call(
        _ssd_kernel,
        grid=(bsz, nc),
        in_specs=[pl.BlockSpec((1, q, width), lambda bi, c: (bi, c, 0)),
                  pl.BlockSpec((1, q, xw), lambda bi, c: (bi, c, 0)),
                  pl.BlockSpec((1, 8, xw), lambda bi, c: (bi, jnp.maximum(c * (q // 8) - 1, 0), 0)),
                  pl.BlockSpec((1, q, LANES), lambda bi, c: (bi, c, 0)),
                  pl.BlockSpec((1, heads, q), lambda bi, c: (bi, 0, c)),
                  pl.BlockSpec((SSD_CONV, xw), lambda bi, c: (0, 0)),
                  row(xw), row(LANES),
                  pl.BlockSpec((heads, 1), lambda bi, c: (0, 0)),
                  row(LANES),
                  pl.BlockSpec((heads, 1), lambda bi, c: (0, 0)),
                  row(width), row(width)],
        out_specs=pl.BlockSpec((1, q, width), lambda bi, c: (bi, c, 0)),
        out_shape=jax.ShapeDtypeStruct((bsz, l, width), BF16),
        scratch_shapes=[pltpu.VMEM((width // LANES, LANES, SSD_STATE), F32),
                        pltpu.VMEM((q + 8, xw), F32)],
        compiler_params=_params("parallel", "arbitrary"),
        name="ssd_mixer",
    )(z, xbc, xbc, dt_pad, dtt, conv_w, conv_b.reshape(1, xw), dtb, dt_bias.reshape(heads, 1),
      aneg, a_neg.reshape(heads, 1), dskip, norm_g.reshape(1, width))


def _rwkv_prep_kernel(x_ref, xp_ref, win_ref, mu_ref, w0_ref, a0_ref, kk_ref, ka_ref, w2_ref, a2_ref, g2_ref,
                      e_ref, et_ref, r_ref, lw_ref, k_ref, v_ref, an_ref, bn_ref, g_ref, ext_ref):
    tm = x_ref.shape[1]
    width = r_ref.shape[-1]
    i = pl.program_id(1)
    rw = _dot(x_ref[0], win_ref[...])
    rw_prev = _dot(xp_ref[0], win_ref[...])
    ext_ref[0:8, :] = jnp.where(i > 0, rw_prev, 0.0)
    ext_ref[8:8 + tm, :] = rw
    shifted = ext_ref[7:7 + tm, :]
    s = rw + (shifted - rw) * mu_ref[...]
    r = s[:, :width]
    k = s[:, width:2 * width]
    v = s[:, 2 * width:3 * width]
    lo = s[:, 3 * width:3 * width + RWKV_DECAY_LORA + RWKV_ICLR_LORA]
    g_lo = s[:, 3 * width + RWKV_DECAY_LORA + RWKV_ICLR_LORA:]
    wv = w0_ref[...] + _dot(jnp.tanh(lo), w2_ref[...])
    w = -_softplus(-wv) - 0.5
    av = _sigmoid(a0_ref[...] + _dot(lo, a2_ref[...]))
    g = _dot(_sigmoid(g_lo), g2_ref[...])
    kkr = k * kk_ref[...]
    ss = _dot_exact_rhs(kkr * kkr, e_ref[...])
    inv = lax.rsqrt(jnp.maximum(ss, 1e-24))
    kk = kkr * _dot_exact_rhs(inv, et_ref[...])
    r_ref[0] = r.astype(r_ref.dtype)
    lw_ref[0] = -jnp.exp(w)
    k_ref[0] = (k * (1.0 + (av - 1.0) * ka_ref[...])).astype(k_ref.dtype)
    v_ref[0] = v.astype(v_ref.dtype)
    an_ref[0] = (-kk).astype(an_ref.dtype)
    bn_ref[0] = (kk * av).astype(bn_ref.dtype)
    g_ref[0] = g.astype(g_ref.dtype)


def _rwkv_scan_kernel(r_ref, lw_ref, k_ref, v_ref, an_ref, bn_ref, g_ref, rk_ref, lg_ref, lb_ref,
                      y_ref, state_ref):
    c = RWKV_CHUNK
    c2 = 2 * c
    width = r_ref.shape[-1]
    n_pairs = width // LANES
    ci = pl.program_id(1)

    @pl.when(ci == 0)
    def _():
        state_ref[...] = jnp.zeros_like(state_ref)

    row = lax.broadcasted_iota(jnp.int32, (c, c), 0)
    col = lax.broadcasted_iota(jnp.int32, (c, c), 1)
    tri = jnp.where(row >= col, 1.0, 0.0)
    row2 = lax.broadcasted_iota(jnp.int32, (c2, c2), 0)
    col2 = lax.broadcasted_iota(jnp.int32, (c2, c2), 1)
    same = (row2 // c) == (col2 // c)
    strict = jnp.where(same & ((row2 % c) > (col2 % c)), 1.0, 0.0)
    incl = jnp.where(same & ((row2 % c) >= (col2 % c)), 1.0, 0.0)
    eye = jnp.where(row2 == col2, 1.0, 0.0)
    head_sum = jnp.where((row2 // HEAD_DIM) == (col2 // HEAD_DIM), 1.0, 0.0)
    head_avg = head_sum * (1.0 / HEAD_DIM)
    lane = lax.broadcasted_iota(jnp.int32, (1, LANES), 1)
    m0 = jnp.where(lane < HEAD_DIM, 1.0, 0.0)
    m1 = 1.0 - m0

    def stack(x):
        return jnp.concatenate([x * m0, x * m1], axis=0)

    sls = [slice(p * LANES, (p + 1) * LANES) for p in range(n_pairs)]
    lhs4s, rhs4s, v_stks, bks, g_lasts = [], [], [], [], []
    for sl in sls:
        r = r_ref[0, :, sl].astype(F32)
        lw = lw_ref[0, :, sl]
        k = k_ref[0, :, sl].astype(F32)
        b = bn_ref[0, :, sl].astype(F32)
        cum = _dot_exact_lhs(tri, lw)
        cum_last = cum[c - 1:c, :]
        e_neg = jnp.exp(-cum)
        e_tail = jnp.exp(cum_last - cum)
        at = an_ref[0, :, sl].astype(F32) * jnp.exp(cum - lw)
        rt = r * jnp.exp(cum)
        bt = (b * e_neg).astype(BF16)
        kt = (k * e_neg).astype(BF16)
        lhs4s.append(jnp.concatenate([stack(at), stack(rt)], axis=0).astype(BF16))
        rhs4s.append(jnp.concatenate([bt, bt, kt, kt], axis=0))
        v_stks.append(stack(v_ref[0, :, sl].astype(F32)).astype(BF16))
        bks.append(jnp.concatenate([stack(b * e_tail), stack(k * e_tail)], axis=0).astype(BF16))
        g_lasts.append(jnp.exp(cum_last))
    gms = [_dot_nt(l, rh) for l, rh in zip(lhs4s, rhs4s)]
    pws = [gm[:c2, :c2] * strict for gm in gms]
    a_aks = [(gm[:c2, c2:] * strict).astype(BF16) for gm in gms]
    a_rbks = [jnp.concatenate([gm[c2:, :c2] * incl, gm[c2:, c2:] * incl], axis=1).astype(BF16) for gm in gms]
    akvs = [_dot(a_ak, v_stk) for a_ak, v_stk in zip(a_aks, v_stks)]
    tinvs = [eye + pw for pw in pws]
    for _ in range(int(math.log2(c)) - 1):
        pws = [_dot(pw, pw) for pw in pws]
        tinvs = [tinv + _dot(tinv, pw) for tinv, pw in zip(tinvs, pws)]
    sts = [state_ref[p] for p in range(n_pairs)]
    ahrhs = [_dot_nt(l, st) for l, st in zip(lhs4s, sts)]
    u_stks = [_dot(tinv, ahrh[:c2] + akv) for tinv, ahrh, akv in zip(tinvs, ahrhs, akvs)]
    uvs = [jnp.concatenate([u.astype(BF16), v_stk], axis=0) for u, v_stk in zip(u_stks, v_stks)]
    for p in range(n_pairs):
        state_ref[p] = sts[p] * g_lasts[p] + _dot_tn(uvs[p], bks[p])
    y_stks = [ahrh[c2:] + _dot(a_rbk, uv) for ahrh, a_rbk, uv in zip(ahrhs, a_rbks, uvs)]
    ys = [y_stk[:c] + y_stk[c:] for y_stk in y_stks]
    bonus = [_dot_exact_rhs(r_ref[0, :, sl].astype(F32) * k_ref[0, :, sl].astype(F32) * rk_ref[:, sl], head_sum)
             for sl in sls]
    ycs = [y - _dot_exact_rhs(y, head_avg) for y in ys]
    yvs = [_dot_exact_rhs(yc * yc, head_avg) for yc in ycs]
    for p, sl in enumerate(sls):
        yn = ycs[p] * lax.rsqrt(yvs[p] + RWKV_LNX_EPS) * lg_ref[:, sl] + lb_ref[:, sl]
        out = (yn + bonus[p] * v_ref[0, :, sl].astype(F32)) * g_ref[0, :, sl].astype(F32)
        y_ref[0, :, sl] = out.astype(y_ref.dtype)


def _rwkv7_mixer(x, w_in, mu, w0, w2, a0, a2, g2, k_k, k_a, r_k, lnx_g, lnx_b, *, tm):
    bsz, l, d = x.shape
    win = w_in.shape[1]
    width = w0.shape[0]
    w2p = jnp.concatenate([w2, jnp.zeros((RWKV_ICLR_LORA, width), F32)], axis=0).astype(BF16)
    a2p = jnp.concatenate([jnp.zeros((RWKV_DECAY_LORA, width), F32), a2], axis=0).astype(BF16)
    head_of = jnp.arange(width) // HEAD_DIM
    e = (head_of[:, None] == jnp.arange(LANES)[None, :]).astype(BF16)
    et = e.T
    vec = lambda x: x.reshape(1, -1)
    row = lambda n: pl.BlockSpec((1, n), lambda bi, i: (0, 0))
    full = lambda a: pl.BlockSpec(a.shape, lambda bi, i: (0, 0))
    tile = pl.BlockSpec((1, tm, width), lambda bi, i: (bi, i, 0))
    sds = lambda dt: jax.ShapeDtypeStruct((bsz, l, width), dt)
    g2b = g2.astype(BF16)
    r, lw, k, v, an, bn, g = pl.pallas_call(
        _rwkv_prep_kernel,
        grid=(bsz, l // tm),
        in_specs=[pl.BlockSpec((1, tm, d), lambda bi, i: (bi, i, 0)),
                  pl.BlockSpec((1, 8, d), lambda bi, i: (bi, jnp.maximum(i * (tm // 8) - 1, 0), 0)),
                  full(w_in), row(win), row(width), row(width), row(width), row(width),
                  full(w2p), full(a2p), full(g2b), full(e), full(et)],
        out_specs=[tile] * 7,
        out_shape=[sds(BF16), sds(F32)] + [sds(BF16)] * 5,
        scratch_shapes=[pltpu.VMEM((tm + 8, win), F32)],
        compiler_params=_params("parallel", "parallel"),
        name="rwkv_prep",
    )(x, x, w_in, vec(mu), vec(w0), vec(a0), vec(k_k), vec(k_a), w2p, a2p, g2b, e, et)
    c = RWKV_CHUNK
    ctile = pl.BlockSpec((1, c, width), lambda bi, ci: (bi, ci, 0))
    crow = pl.BlockSpec((1, width), lambda bi, ci: (0, 0))
    return pl.pallas_call(
        _rwkv_scan_kernel,
        grid=(bsz, l // c),
        in_specs=[ctile] * 7 + [crow] * 3,
        out_specs=ctile,
        out_shape=jax.ShapeDtypeStruct((bsz, l, width), BF16),
        scratch_shapes=[pltpu.VMEM((width // LANES, LANES, LANES), F32)],
        compiler_params=_params("parallel", "arbitrary"),
        name="rwkv_scan",
    )(r, lw, k, v, an, bn, g, vec(r_k), vec(lnx_g), vec(lnx_b))


def _moba_kernel(q_ref, k_ref, v_ref, o_ref, kb_ref, vt_ref, kmean_ref):
    blk = MOBA_BLOCK
    half = HEAD_DIM
    nb = k_ref.shape[0] // blk
    nbp = kmean_ref.shape[0]
    group = math.gcd(nb, MOBA_GROUP)
    qi = pl.program_id(2)

    @pl.when(qi == 0)
    def _():
        lane = lax.broadcasted_iota(jnp.int32, (blk, LANES), 1)
        rowp = lax.broadcasted_iota(jnp.int32, (LANES, blk), 0)
        if nbp > nb:
            kmean_ref[...] = jnp.zeros_like(kmean_ref)
        for n in range(nb):
            kn = k_ref[n * blk:(n + 1) * blk, :]
            kmean_ref[n:n + 1, :] = jnp.mean(kn, axis=0, keepdims=True)
            kb_ref[0, n] = jnp.where(lane < half, kn, jnp.where(lane == half + n, 1.0, 0.0)).astype(BF16)
            kb_ref[1, n] = jnp.where(lane >= half, kn, jnp.where(lane == n, 1.0, 0.0)).astype(BF16)
            vtn = v_ref[n * blk:(n + 1) * blk, :].T
            vt_ref[0, n] = jnp.where(rowp < half, vtn, jnp.where(rowp == half, 1.0, 0.0)).astype(BF16)
            vt_ref[1, n] = jnp.where(rowp >= half, vtn, jnp.where(rowp == 0, 1.0, 0.0)).astype(BF16)

    qt = q_ref[...].T
    rown = lax.broadcasted_iota(jnp.int32, (nbp, blk), 0)
    lane_k = lax.broadcasted_iota(jnp.int32, (nbp, LANES), 1)
    km = kmean_ref[...]
    q_own, q_past = [], []
    for h in range(2):
        kmh = jnp.where((lane_k < half) if h == 0 else (lane_k >= half), km, 0.0)
        gate = jnp.dot(kmh, qt, preferred_element_type=F32, precision=lax.Precision.HIGHEST)
        gate = jnp.where(rown < qi, gate, -jnp.inf)
        bias = jnp.full((nbp, blk), NEG_BIG, F32)
        for _ in range(MOBA_TOPK):
            mx = jnp.max(gate, axis=0, keepdims=True)
            first = jnp.min(jnp.where(gate == mx, rown, nbp), axis=0, keepdims=True)
            pick = (rown == first) & (mx > -jnp.inf)
            bias = jnp.where(pick, 0.0, bias)
            gate = jnp.where(pick, -jnp.inf, gate)
        qh = qt[h * half:(h + 1) * half] * (half ** -0.5 * LOG2_E)
        zeros = jnp.zeros((half, blk), F32)
        aug = jnp.concatenate([bias, jnp.zeros((half - nbp, blk), F32)], axis=0)
        q_own.append(jnp.concatenate([qh, zeros] if h == 0 else [zeros, qh], axis=0).astype(BF16))
        q_past.append(jnp.concatenate([qh, aug] if h == 0 else [aug, qh], axis=0).astype(BF16))

    causal = (lax.broadcasted_iota(jnp.int32, (blk, blk), 0) <= lax.broadcasted_iota(jnp.int32, (blk, blk), 1))
    carry = []
    for h in range(2):
        s = jnp.dot(kb_ref[h, qi], q_own[h], preferred_element_type=F32)
        s = jnp.where(causal, s, NEG_BIG)
        m = jnp.max(s, axis=0, keepdims=True)
        p = jnp.exp2(s - m).astype(BF16)
        carry += [m, jnp.dot(vt_ref[h, qi], p, preferred_element_type=F32)]

    def body(gi, carry):
        heads = range(2)
        sss = [[jnp.dot(kb_ref[h, gi * group + g], q_past[h], preferred_element_type=F32)
                for g in range(group)] for h in heads]
        m_news = []
        for h in heads:
            m_new = carry[2 * h]
            for s in sss[h]:
                m_new = jnp.maximum(m_new, jnp.max(s, axis=0, keepdims=True))
            m_news.append(m_new)
        pcats = [jnp.concatenate([jnp.exp2(s - m_news[h]).astype(BF16) for s in sss[h]], axis=0)
                 for h in heads]
        out = []
        for h in heads:
            vts = jnp.concatenate([vt_ref[h, gi * group + g] for g in range(group)], axis=1)
            alpha = jnp.exp2(carry[2 * h] - m_news[h])
            out += [m_news[h], alpha * carry[2 * h + 1] + jnp.dot(vts, pcats[h], preferred_element_type=F32)]
        return tuple(out)

    _, acc0, _, acc1 = lax.fori_loop(0, (qi + group - 1) // group, body, tuple(carry))
    rowq = lax.broadcasted_iota(jnp.int32, (LANES, blk), 0)
    out_t = jnp.where(rowq < half, acc0 / acc0[half:half + 1], acc1 / acc1[0:1])
    o_ref[...] = out_t.T.astype(o_ref.dtype)


def _moba_attention(qkv, bsz, s, heads):
    blk = MOBA_BLOCK
    assert s % blk == 0 and (heads * HEAD_DIM) % LANES == 0
    nb = s // blk
    assert nb <= HEAD_DIM
    nbp = -(-nb // 8) * 8
    pairs = heads * HEAD_DIM // LANES
    return pl.pallas_call(
        _moba_kernel,
        grid=(bsz, pairs, nb),
        in_specs=[pl.BlockSpec((blk, LANES), lambda b, p, i: (b * nb + i, p)),
                  pl.BlockSpec((s, LANES), lambda b, p, i: (b, pairs + p)),
                  pl.BlockSpec((s, LANES), lambda b, p, i: (b, 2 * pairs + p))],
        out_specs=pl.BlockSpec((blk, LANES), lambda b, p, i: (b * nb + i, p)),
        out_shape=jax.ShapeDtypeStruct((bsz * s, heads * HEAD_DIM), BF16),
        scratch_shapes=[pltpu.VMEM((2, nb, blk, LANES), BF16), pltpu.VMEM((2, nb, LANES, blk), BF16),
                        pltpu.VMEM((nbp, LANES), F32)],
        compiler_params=_params("parallel", "parallel", "arbitrary"),
        name="moba_attention",
    )(qkv, qkv, qkv)


def _row_tile(m):
    for t in (512, 256, 128, 64, 32, 16, 8):
        if m % t == 0:
            return t
    raise ValueError(f"row count {m} is not a multiple of 8")


def _col_tile(n, cap=2048):
    best = None
    for t in range(LANES, min(n, cap) + 1, LANES):
        if n % t == 0:
            best = t
    if best is None:
        raise ValueError(f"column count {n} is not a multiple of {LANES}")
    return best


def kernel(x, mem, even_w_in, ssd_conv_w, ssd_conv_b, ssd_dt_bias, ssd_a_log, ssd_d, ssd_norm_g, rwkv_mu, rwkv_w0, rwkv_w2, rwkv_a0, rwkv_a2, rwkv_g2, rwkv_k_k, rwkv_k_a, rwkv_r_k, rwkv_lnx_g, rwkv_lnx_b, even_w_out, odd_w_qkv, odd_w_out, ln_mix_g, ln_mix_b, xa_wq, xa_wkv, xa_wo, ln_xa_g, ln_xa_b, ffn_w13, ffn_w2, ln_ffn_g, ln_ffn_b):
    bsz, s, d = x.shape
    m = bsz * s
    tm = _row_tile(s)
    ssd_width = ssd_norm_g.shape[-1]
    ssd_heads = ssd_dt_bias.shape[-1]
    ssd_xbc = ssd_conv_b.shape[-1]
    ssd_in = ssd_width + ssd_xbc + ssd_heads
    rwkv_width = rwkv_w0.shape[-1]
    ffn_hidden = ffn_w2.shape[1]
    mem2 = mem.reshape(bsz * mem.shape[1], d)
    x2 = x.reshape(m, d)
    for layer in range(DEPTH):
        j = layer // 2
        if layer % 2 == 0:
            w_in = even_w_in[j].astype(BF16)
            w_z = w_in[:, :ssd_width]
            w_xbc = w_in[:, ssd_width:ssd_width + ssd_xbc]
            w_dt = jnp.pad(w_in[:, ssd_width + ssd_xbc:ssd_in], ((0, 0), (0, LANES - ssd_heads)))
            w_rw = w_in[:, ssd_in:]
            z = _matmul(x2, w_z, tm=tm, tn=_col_tile(ssd_width))
            xbc = _matmul(x2, w_xbc, tm=tm, tn=_col_tile(ssd_xbc))
            dt_pad = _matmul(x2, w_dt, tm=tm, tn=LANES)
            y_ssd = _ssd_mixer(z.reshape(bsz, s, -1), xbc.reshape(bsz, s, -1), dt_pad.reshape(bsz, s, -1),
                               ssd_conv_w[j], ssd_conv_b[j], ssd_dt_bias[j], ssd_a_log[j], ssd_d[j],
                               ssd_norm_g[j])
            y_rwkv = _rwkv7_mixer(x2.reshape(bsz, s, d), w_rw, rwkv_mu[j], rwkv_w0[j], rwkv_w2[j], rwkv_a0[j],
                                  rwkv_a2[j], rwkv_g2[j], rwkv_k_k[j], rwkv_k_a[j], rwkv_r_k[j],
                                  rwkv_lnx_g[j], rwkv_lnx_b[j], tm=min(tm, 256))
            w_out = even_w_out[j].astype(BF16)
            x2 = _matmul_residual_ln([y_ssd.reshape(m, -1), y_rwkv.reshape(m, -1)],
                                     [w_out[:ssd_width], w_out[ssd_width:]], x2,
                                     ln_mix_g[layer], ln_mix_b[layer], tm=tm)
        else:
            heads = d // HEAD_DIM
            qkv = _matmul(x2, odd_w_qkv[j].astype(BF16), tm=tm, tn=_col_tile(3 * d))
            attn = _moba_attention(qkv, bsz, s, heads)
            x2 = _matmul_residual_ln([attn], [odd_w_out[j].astype(BF16)], x2,
                                     ln_mix_g[layer], ln_mix_b[layer], tm=tm)
        kv = _matmul(mem2, xa_wkv[layer].astype(BF16), tm=_row_tile(mem2.shape[0]), tn=_col_tile(2 * d))
        x3 = _cross_attention_ln(x2.reshape(bsz, s, d), kv.reshape(bsz, -1, 2 * d),
                                 xa_wq[layer].astype(BF16), xa_wo[layer].astype(BF16),
                                 ln_xa_g[layer], ln_xa_b[layer], tm=tm)
        x2 = x3.reshape(m, d)
        w13 = ffn_w13[layer].astype(BF16)
        h = _swiglu_up(x2, w13[:, :ffn_hidden], w13[:, ffn_hidden:], tm=tm, tn=_col_tile(ffn_hidden, 1536))
        x2 = _matmul_residual_ln([h], [ffn_w2[layer].astype(BF16)], x2,
                                 ln_ffn_g[layer], ln_ffn_b[layer], tm=tm)
    return x2.reshape(bsz, s, d)
```

```python
import functools
import math

import jax
import jax.numpy as jnp
from jax import lax
from jax.experimental import pallas as pl
from jax.experimental.pallas import tpu as pltpu

F32 = jnp.float32
BF16 = jnp.bfloat16

HEAD_DIM = 64
LANES = 128
SSD_GROUPS = 2
SSD_STATE = 128
SSD_CONV = 4
SSD_CHUNK = 128
RWKV_DECAY_LORA = 64
RWKV_ICLR_LORA = 64
RWKV_GATE_LORA = 128
RWKV_CHUNK = 64
MOBA_BLOCK = 256
MOBA_TOPK = 3
MOBA_GROUP = 4
MOBA_PAIRS_PER_STEP = 2
XATTN_HEADS = 4
DEPTH = 2
DEEPNORM_ALPHA = (2 * DEPTH) ** 0.25
LN_EPS = 1e-5
RMS_EPS = 1e-5
RWKV_LNX_EPS = 64e-5
NEG_BIG = -1e30
LOG2_E = math.log2(math.e)
VMEM_LIMIT = 56 * 1024 * 1024


def _params(*sem):
    return pltpu.CompilerParams(dimension_semantics=sem, vmem_limit_bytes=VMEM_LIMIT)


def _dot(a, b):
    return jnp.dot(a.astype(BF16), b.astype(BF16), preferred_element_type=F32)


def _dot_nt(a, b):
    return lax.dot_general(a.astype(BF16), b.astype(BF16), (((1,), (1,)), ((), ())),
                           preferred_element_type=F32)


def _dot_tn(a, b):
    return lax.dot_general(a.astype(BF16), b.astype(BF16), (((0,), (0,)), ((), ())),
                           preferred_element_type=F32)


def _split3(x):
    hi = x.astype(BF16)
    r1 = x - hi.astype(F32)
    mid = r1.astype(BF16)
    lo = (r1 - mid.astype(F32)).astype(BF16)
    return hi, mid, lo


def _dot_exact_lhs(m, x):
    hi, mid, lo = _split3(x)
    m = m.astype(BF16)
    return (jnp.dot(m, hi, preferred_element_type=F32) + jnp.dot(m, mid, preferred_element_type=F32)
            + jnp.dot(m, lo, preferred_element_type=F32))


def _dot_exact_rhs(x, m):
    hi, mid, lo = _split3(x)
    m = m.astype(BF16)
    return (jnp.dot(hi, m, preferred_element_type=F32) + jnp.dot(mid, m, preferred_element_type=F32)
            + jnp.dot(lo, m, preferred_element_type=F32))


def _sigmoid(x):
    return 1.0 / (1.0 + jnp.exp(-x))


def _softplus(x):
    return jnp.maximum(x, 0.0) + jnp.log1p(jnp.exp(-jnp.abs(x)))


def _layer_norm(v, g, b):
    mu = jnp.mean(v, axis=-1, keepdims=True)
    c = v - mu
    var = jnp.mean(c * c, axis=-1, keepdims=True)
    return c * lax.rsqrt(var + LN_EPS) * g + b


def _mm_kernel(x_ref, w_ref, o_ref):
    o_ref[...] = _dot(x_ref[...], w_ref[...]).astype(o_ref.dtype)


def _matmul(x, w, *, tm, tn, out_dtype=F32):
    m, k = x.shape
    n = w.shape[1]
    assert m % tm == 0 and n % tn == 0
    return pl.pallas_call(
        _mm_kernel,
        grid=(m // tm, n // tn),
        in_specs=[pl.BlockSpec((tm, k), lambda i, j: (i, 0)),
                  pl.BlockSpec((k, tn), lambda i, j: (0, j))],
        out_specs=pl.BlockSpec((tm, tn), lambda i, j: (i, j)),
        out_shape=jax.ShapeDtypeStruct((m, n), out_dtype),
        compiler_params=_params("parallel", "arbitrary"),
        name="matmul",
    )(x, w)


def _mm_res_ln_kernel(n_in, *refs):
    hs = refs[:n_in]
    ws = refs[n_in:2 * n_in]
    res_ref, g_ref, b_ref, o_ref = refs[2 * n_in:]
    acc = _dot(hs[0][...], ws[0][...])
    for h_ref, w_ref in zip(hs[1:], ws[1:]):
        acc = acc + _dot(h_ref[...], w_ref[...])
    o_ref[...] = _layer_norm(DEEPNORM_ALPHA * res_ref[...] + acc, g_ref[...], b_ref[...])


def _matmul_residual_ln(hs, ws, res, g, b, *, tm):
    m, d = res.shape
    n_in = len(hs)
    in_specs = ([pl.BlockSpec((tm, h.shape[1]), lambda i: (i, 0)) for h in hs]
                + [pl.BlockSpec(w.shape, lambda i: (0, 0)) for w in ws]
                + [pl.BlockSpec((tm, d), lambda i: (i, 0)),
                   pl.BlockSpec((1, d), lambda i: (0, 0)),
                   pl.BlockSpec((1, d), lambda i: (0, 0))])
    return pl.pallas_call(
        functools.partial(_mm_res_ln_kernel, n_in),
        grid=(m // tm,),
        in_specs=in_specs,
        out_specs=pl.BlockSpec((tm, d), lambda i: (i, 0)),
        out_shape=jax.ShapeDtypeStruct((m, d), F32),
        compiler_params=_params("parallel"),
        name="matmul_residual_ln",
    )(*hs, *ws, res, g.reshape(1, d), b.reshape(1, d))


def _swiglu_kernel(x_ref, w1_ref, w3_ref, o_ref):
    x = x_ref[...].astype(BF16)
    gate = jnp.dot(x, w1_ref[...], preferred_element_type=F32)
    up = jnp.dot(x, w3_ref[...], preferred_element_type=F32)
    o_ref[...] = (gate * _sigmoid(gate) * up).astype(o_ref.dtype)


def _swiglu_up(x, w1, w3, *, tm, tn):
    m, k = x.shape
    n = w1.shape[1]
    return pl.pallas_call(
        _swiglu_kernel,
        grid=(m // tm, n // tn),
        in_specs=[pl.BlockSpec((tm, k), lambda i, j: (i, 0)),
                  pl.BlockSpec((k, tn), lambda i, j: (0, j)),
                  pl.BlockSpec((k, tn), lambda i, j: (0, j))],
        out_specs=pl.BlockSpec((tm, tn), lambda i, j: (i, j)),
        out_shape=jax.ShapeDtypeStruct((m, n), BF16),
        compiler_params=_params("parallel", "arbitrary"),
        name="swiglu_up",
    )(x, w1, w3)


def _xattn_kernel(x_ref, kv_ref, wq_ref, wo_ref, g_ref, b_ref, o_ref):
    x = x_ref[0]
    d = x.shape[-1]
    hd = d // XATTN_HEADS
    q = _dot(x, wq_ref[...])
    kv = kv_ref[0]
    outs = []
    for h in range(XATTN_HEADS):
        qh = q[:, h * hd:(h + 1) * hd]
        kh = kv[:, h * hd:(h + 1) * hd]
        vh = kv[:, d + h * hd:d + (h + 1) * hd]
        s = _dot_nt(qh, kh) * (hd ** -0.5)
        s = s - jnp.max(s, axis=-1, keepdims=True)
        p = jnp.exp(s)
        p = p / jnp.sum(p, axis=-1, keepdims=True)
        outs.append(_dot(p, vh))
    o = jnp.concatenate(outs, axis=-1)
    xa = _dot(o, wo_ref[...])
    o_ref[0] = _layer_norm(DEEPNORM_ALPHA * x + xa, g_ref[...], b_ref[...])


def _cross_attention_ln(x, kv, wq, wo, g, b, *, tm):
    bsz, s, d = x.shape
    m = kv.shape[1]
    return pl.pallas_call(
        _xattn_kernel,
        grid=(bsz, s // tm),
        in_specs=[pl.BlockSpec((1, tm, d), lambda bi, i: (bi, i, 0)),
                  pl.BlockSpec((1, m, 2 * d), lambda bi, i: (bi, 0, 0)),
                  pl.BlockSpec((d, d), lambda bi, i: (0, 0)),
                  pl.BlockSpec((d, d), lambda bi, i: (0, 0)),
                  pl.BlockSpec((1, d), lambda bi, i: (0, 0)),
                  pl.BlockSpec((1, d), lambda bi, i: (0, 0))],
        out_specs=pl.BlockSpec((1, tm, d), lambda bi, i: (bi, i, 0)),
        out_shape=jax.ShapeDtypeStruct((bsz, s, d), F32),
        compiler_params=_params("parallel", "parallel"),
        name="cross_attention_ln",
    )(x, kv, wq, wo, g.reshape(1, d), b.reshape(1, d))


def _ssd_kernel(z_ref, xbc_ref, xbcp_ref, dt_ref, dtt_ref, cw_ref, cb_ref, dtb_ref, dtbt_ref,
                aneg_ref, anegt_ref, dskip_ref, ng_ref, y_ref, state_ref, ext_ref):
    q = SSD_CHUNK
    width = z_ref.shape[-1]
    n_pairs = width // LANES
    c = pl.program_id(1)

    @pl.when(c == 0)
    def _():
        state_ref[...] = jnp.zeros_like(state_ref)

    ext_ref[0:8, :] = jnp.where(c > 0, xbcp_ref[0], 0.0)
    ext_ref[8:8 + q, :] = xbc_ref[0]
    conv = cb_ref[...] + cw_ref[SSD_CONV - 1:SSD_CONV, :] * ext_ref[8:8 + q, :]
    for k in range(SSD_CONV - 1):
        off = 8 - (SSD_CONV - 1) + k
        conv = conv + cw_ref[k:k + 1, :] * ext_ref[off:off + q, :]
    xc = conv * _sigmoid(conv)
    xs = xc[:, :width]
    gn = SSD_GROUPS * SSD_STATE
    bm = xc[:, width:width + gn]
    cm = xc[:, width + gn:width + 2 * gn]

    dt = _softplus(dt_ref[0] + dtb_ref[...])
    a = dt * aneg_ref[...]
    dtt = _softplus(dtt_ref[0] + dtbt_ref[...])
    at = dtt * anegt_ref[...]
    row = lax.broadcasted_iota(jnp.int32, (q, q), 0)
    col = lax.broadcasted_iota(jnp.int32, (q, q), 1)
    causal = row >= col
    tri = jnp.where(causal, 1.0, 0.0)
    a_cum = _dot_exact_lhs(tri, a)
    a_cumt = _dot_exact_rhs(at, jnp.where(row <= col, 1.0, 0.0))

    lane = lax.broadcasted_iota(jnp.int32, (1, LANES), 1)
    lane_lo = lane < HEAD_DIM
    rowp = lax.broadcasted_iota(jnp.int32, (LANES, 1), 0)
    pairs_per_group = n_pairs // SSD_GROUPS
    ys = []
    for p in range(n_pairs):
        g = p // pairs_per_group
        h0, h1 = 2 * p, 2 * p + 1
        bg = bm[:, g * SSD_STATE:(g + 1) * SSD_STATE]
        cg = cm[:, g * SSD_STATE:(g + 1) * SSD_STATE]
        cb = _dot_nt(cg, bg)
        xs_p = xs[:, p * LANES:(p + 1) * LANES]
        dt_p = jnp.where(lane_lo, dt[:, h0:h0 + 1], dt[:, h1:h1 + 1])
        acum_p = jnp.where(lane_lo, a_cum[:, h0:h0 + 1], a_cum[:, h1:h1 + 1])
        xdt = xs_p * dt_p
        ms = []
        for h in (h0, h1):
            seg = a_cum[:, h:h + 1] - a_cumt[h:h + 1, :]
            ms.append(cb * jnp.exp(jnp.where(causal, seg, NEG_BIG)))
        m2 = jnp.concatenate(ms, axis=1)
        x2 = jnp.concatenate([jnp.where(lane_lo, xdt, 0.0), jnp.where(lane_lo, 0.0, xdt)], axis=0)
        y_diag = _dot(m2, x2)
        prev = state_ref[p]
        y_off = _dot_nt(cg, prev) * jnp.exp(acum_p)
        a_last = jnp.where(lane_lo, a_cum[q - 1:q, h0:h0 + 1], a_cum[q - 1:q, h1:h1 + 1])
        xdw = xdt * jnp.exp(a_last - acum_p)
        st = _dot_tn(xdw, bg)
        cd = jnp.where(rowp < HEAD_DIM, jnp.exp(a_cumt[h0:h0 + 1, q - 1:q]),
                       jnp.exp(a_cumt[h1:h1 + 1, q - 1:q]))
        state_ref[p] = prev * cd + st
        d_p = dskip_ref[:, p * LANES:(p + 1) * LANES]
        ys.append(y_diag + y_off + d_p * xs_p)
    y = jnp.concatenate(ys, axis=1)
    z = z_ref[0]
    y = y * (z * _sigmoid(z))
    gw = width // SSD_GROUPS
    outs = []
    for g in range(SSD_GROUPS):
        yg = y[:, g * gw:(g + 1) * gw]
        outs.append(yg * lax.rsqrt(jnp.mean(yg * yg, axis=-1, keepdims=True) + RMS_EPS))
    y_ref[0] = (jnp.concatenate(outs, axis=1) * ng_ref[...]).astype(y_ref.dtype)


def _ssd_mixer(z, xbc, dt_pad, conv_w, conv_b, dt_bias, a_log, d_skip, norm_g):
    bsz, l, width = z.shape
    heads = width // HEAD_DIM
    xw = xbc.shape[-1]
    q = SSD_CHUNK
    nc = l // q
    dtt = jnp.swapaxes(dt_pad[:, :, :heads], 1, 2)
    pad = LANES - heads
    dtb = jnp.pad(dt_bias, (0, pad)).reshape(1, LANES)
    a_neg = -jnp.exp(a_log.astype(F32))
    aneg = jnp.pad(a_neg, (0, pad)).reshape(1, LANES)
    dskip = jnp.repeat(d_skip, HEAD_DIM).reshape(1, width)
    row = lambda n: pl.BlockSpec((1, n), lambda bi, c: (0, 0))
    return pl.pallas_call(
        _ssd_kernel,
        grid=(bsz, nc),
        in_specs=[pl.BlockSpec((1, q, width), lambda bi, c: (bi, c, 0)),
                  pl.BlockSpec((1, q, xw), lambda bi, c: (bi, c, 0)),
                  pl.BlockSpec((1, 8, xw), lambda bi, c: (bi, jnp.maximum(c * (q // 8) - 1, 0), 0)),
                  pl.BlockSpec((1, q, LANES), lambda bi, c: (bi, c, 0)),
                  pl.BlockSpec((1, heads, q), lambda bi, c: (bi, 0, c)),
                  pl.BlockSpec((SSD_CONV, xw), lambda bi, c: (0, 0)),
                  row(xw), row(LANES),
                  pl.BlockSpec((heads, 1), lambda bi, c: (0, 0)),
                  row(LANES),
                  pl.BlockSpec((heads, 1), lambda bi, c: (0, 0)),
                  row(width), row(width)],
        out_specs=pl.BlockSpec((1, q, width), lambda bi, c: (bi, c, 0)),
        out_shape=jax.ShapeDtypeStruct((bsz, l, width), BF16),
        scratch_shapes=[pltpu.VMEM((width // LANES, LANES, SSD_STATE), F32),
                        pltpu.VMEM((q + 8, xw), F32)],
        compiler_params=_params("parallel", "arbitrary"),
        name="ssd_mixer",
    )(z, xbc, xbc, dt_pad, dtt, conv_w, conv_b.reshape(1, xw), dtb, dt_bias.reshape(heads, 1),
      aneg, a_neg.reshape(heads, 1), dskip, norm_g.reshape(1, width))


def _rwkv_prep_kernel(x_ref, xp_ref, win_ref, mu_ref, w0_ref, a0_ref, kk_ref, ka_ref, w2_ref, a2_ref, g2_ref,
                      e_ref, et_ref, r_ref, lw_ref, k_ref, v_ref, an_ref, bn_ref, g_ref, ext_ref):
    tm = x_ref.shape[1]
    width = r_ref.shape[-1]
    i = pl.program_id(1)
    rw = _dot(x_ref[0], win_ref[...])
    rw_prev = _dot(xp_ref[0], win_ref[...])
    ext_ref[0:8, :] = jnp.where(i > 0, rw_prev, 0.0)
    ext_ref[8:8 + tm, :] = rw
    shifted = ext_ref[7:7 + tm, :]
    s = rw + (shifted - rw) * mu_ref[...]
    r = s[:, :width]
    k = s[:, width:2 * width]
    v = s[:, 2 * width:3 * width]
    lo = s[:, 3 * width:3 * width + RWKV_DECAY_LORA + RWKV_ICLR_LORA]
    g_lo = s[:, 3 * width + RWKV_DECAY_LORA + RWKV_ICLR_LORA:]
    wv = w0_ref[...] + _dot(jnp.tanh(lo), w2_ref[...])
    w = -_softplus(-wv) - 0.5
    av = _sigmoid(a0_ref[...] + _dot(lo, a2_ref[...]))
    g = _dot(_sigmoid(g_lo), g2_ref[...])
    kkr = k * kk_ref[...]
    ss = _dot_exact_rhs(kkr * kkr, e_ref[...])
    inv = lax.rsqrt(jnp.maximum(ss, 1e-24))
    kk = kkr * _dot_exact_rhs(inv, et_ref[...])
    r_ref[0] = r.astype(r_ref.dtype)
    lw_ref[0] = -jnp.exp(w)
    k_ref[0] = (k * (1.0 + (av - 1.0) * ka_ref[...])).astype(k_ref.dtype)
    v_ref[0] = v.astype(v_ref.dtype)
    an_ref[0] = (-kk).astype(an_ref.dtype)
    bn_ref[0] = (kk * av).astype(bn_ref.dtype)
    g_ref[0] = g.astype(g_ref.dtype)


def _rwkv_scan_kernel(r_ref, lw_ref, k_ref, v_ref, an_ref, bn_ref, g_ref, rk_ref, lg_ref, lb_ref,
                      y_ref, state_ref):
    c = RWKV_CHUNK
    c2 = 2 * c
    width = r_ref.shape[-1]
    n_pairs = width // LANES
    ci = pl.program_id(1)

    @pl.when(ci == 0)
    def _():
        state_ref[...] = jnp.zeros_like(state_ref)

    row = lax.broadcasted_iota(jnp.int32, (c, c), 0)
    col = lax.broadcasted_iota(jnp.int32, (c, c), 1)
    tri = jnp.where(row >= col, 1.0, 0.0)
    row2 = lax.broadcasted_iota(jnp.int32, (c2, c2), 0)
    col2 = lax.broadcasted_iota(jnp.int32, (c2, c2), 1)
    same = (row2 // c) == (col2 // c)
    strict = jnp.where(same & ((row2 % c) > (col2 % c)), 1.0, 0.0)
    incl = jnp.where(same & ((row2 % c) >= (col2 % c)), 1.0, 0.0)
    eye = jnp.where(row2 == col2, 1.0, 0.0)
    head_sum = jnp.where((row2 // HEAD_DIM) == (col2 // HEAD_DIM), 1.0, 0.0)
    head_avg = head_sum * (1.0 / HEAD_DIM)
    lane = lax.broadcasted_iota(jnp.int32, (1, LANES), 1)
    m0 = jnp.where(lane < HEAD_DIM, 1.0, 0.0)
    m1 = 1.0 - m0

    def stack(x):
        return jnp.concatenate([x * m0, x * m1], axis=0)

    sls = [slice(p * LANES, (p + 1) * LANES) for p in range(n_pairs)]
    lhs4s, rhs4s, v_stks, bks, g_lasts = [], [], [], [], []
    for sl in sls:
        r = r_ref[0, :, sl].astype(F32)
        lw = lw_ref[0, :, sl]
        k = k_ref[0, :, sl].astype(F32)
        b = bn_ref[0, :, sl].astype(F32)
        cum = _dot_exact_lhs(tri, lw)
        cum_last = cum[c - 1:c, :]
        e_neg = jnp.exp(-cum)
        e_tail = jnp.exp(cum_last - cum)
        at = an_ref[0, :, sl].astype(F32) * jnp.exp(cum - lw)
        rt = r * jnp.exp(cum)
        bt = (b * e_neg).astype(BF16)
        kt = (k * e_neg).astype(BF16)
        lhs4s.append(jnp.concatenate([stack(at), stack(rt)], axis=0).astype(BF16))
        rhs4s.append(jnp.concatenate([bt, bt, kt, kt], axis=0))
        v_stks.append(stack(v_ref[0, :, sl].astype(F32)).astype(BF16))
        bks.append(jnp.concatenate([stack(b * e_tail), stack(k * e_tail)], axis=0).astype(BF16))
        g_lasts.append(jnp.exp(cum_last))
    gms = [_dot_nt(l, rh) for l, rh in zip(lhs4s, rhs4s)]
    pws = [gm[:c2, :c2] * strict for gm in gms]
    a_aks = [(gm[:c2, c2:] * strict).astype(BF16) for gm in gms]
    a_rbks = [jnp.concatenate([gm[c2:, :c2] * incl, gm[c2:, c2:] * incl], axis=1).astype(BF16) for gm in gms]
    akvs = [_dot(a_ak, v_stk) for a_ak, v_stk in zip(a_aks, v_stks)]
    tinvs = [eye + pw for pw in pws]
    for _ in range(int(math.log2(c)) - 1):
        pws = [_dot(pw, pw) for pw in pws]
        tinvs = [tinv + _dot(tinv, pw) for tinv, pw in zip(tinvs, pws)]
    sts = [state_ref[p] for p in range(n_pairs)]
    ahrhs = [_dot_nt(l, st) for l, st in zip(lhs4s, sts)]
    u_stks = [_dot(tinv, ahrh[:c2] + akv) for tinv, ahrh, akv in zip(tinvs, ahrhs, akvs)]
    uvs = [jnp.concatenate([u.astype(BF16), v_stk], axis=0) for u, v_stk in zip(u_stks, v_stks)]
    for p in range(n_pairs):
        state_ref[p] = sts[p] * g_lasts[p] + _dot_tn(uvs[p], bks[p])
    y_stks = [ahrh[c2:] + _dot(a_rbk, uv) for ahrh, a_rbk, uv in zip(ahrhs, a_rbks, uvs)]
    ys = [y_stk[:c] + y_stk[c:] for y_stk in y_stks]
    bonus = [_dot_exact_rhs(r_ref[0, :, sl].astype(F32) * k_ref[0, :, sl].astype(F32) * rk_ref[:, sl], head_sum)
             for sl in sls]
    ycs = [y - _dot_exact_rhs(y, head_avg) for y in ys]
    yvs = [_dot_exact_rhs(yc * yc, head_avg) for yc in ycs]
    for p, sl in enumerate(sls):
        yn = ycs[p] * lax.rsqrt(yvs[p] + RWKV_LNX_EPS) * lg_ref[:, sl] + lb_ref[:, sl]
        out = (yn + bonus[p] * v_ref[0, :, sl].astype(F32)) * g_ref[0, :, sl].astype(F32)
        y_ref[0, :, sl] = out.astype(y_ref.dtype)


def _rwkv7_mixer(x, w_in, mu, w0, w2, a0, a2, g2, k_k, k_a, r_k, lnx_g, lnx_b, *, tm):
    bsz, l, d = x.shape
    win = w_in.shape[1]
    width = w0.shape[0]
    w2p = jnp.concatenate([w2, jnp.zeros((RWKV_ICLR_LORA, width), F32)], axis=0).astype(BF16)
    a2p = jnp.concatenate([jnp.zeros((RWKV_DECAY_LORA, width), F32), a2], axis=0).astype(BF16)
    head_of = jnp.arange(width) // HEAD_DIM
    e = (head_of[:, None] == jnp.arange(LANES)[None, :]).astype(BF16)
    et = e.T
    vec = lambda x: x.reshape(1, -1)
    row = lambda n: pl.BlockSpec((1, n), lambda bi, i: (0, 0))
    full = lambda a: pl.BlockSpec(a.shape, lambda bi, i: (0, 0))
    tile = pl.BlockSpec((1, tm, width), lambda bi, i: (bi, i, 0))
    sds = lambda dt: jax.ShapeDtypeStruct((bsz, l, width), dt)
    g2b = g2.astype(BF16)
    r, lw, k, v, an, bn, g = pl.pallas_call(
        _rwkv_prep_kernel,
        grid=(bsz, l // tm),
        in_specs=[pl.BlockSpec((1, tm, d), lambda bi, i: (bi, i, 0)),
                  pl.BlockSpec((1, 8, d), lambda bi, i: (bi, jnp.maximum(i * (tm // 8) - 1, 0), 0)),
                  full(w_in), row(win), row(width), row(width), row(width), row(width),
                  full(w2p), full(a2p), full(g2b), full(e), full(et)],
        out_specs=[tile] * 7,
        out_shape=[sds(BF16), sds(F32)] + [sds(BF16)] * 5,
        scratch_shapes=[pltpu.VMEM((tm + 8, win), F32)],
        compiler_params=_params("parallel", "parallel"),
        name="rwkv_prep",
    )(x, x, w_in, vec(mu), vec(w0), vec(a0), vec(k_k), vec(k_a), w2p, a2p, g2b, e, et)
    c = RWKV_CHUNK
    ctile = pl.BlockSpec((1, c, width), lambda bi, ci: (bi, ci, 0))
    crow = pl.BlockSpec((1, width), lambda bi, ci: (0, 0))
    return pl.pallas_call(
        _rwkv_scan_kernel,
        grid=(bsz, l // c),
        in_specs=[ctile] * 7 + [crow] * 3,
        out_specs=ctile,
        out_shape=jax.ShapeDtypeStruct((bsz, l, width), BF16),
        scratch_shapes=[pltpu.VMEM((width // LANES, LANES, LANES), F32)],
        compiler_params=_params("parallel", "arbitrary"),
        name="rwkv_scan",
    )(r, lw, k, v, an, bn, g, vec(r_k), vec(lnx_g), vec(lnx_b))


def _moba_kernel(q_ref, k_ref, v_ref, o_ref, kb_ref, vt_ref, kmean_ref):
    blk = MOBA_BLOCK
    half = HEAD_DIM
    nb = k_ref.shape[0] // blk
    nbp = kmean_ref.shape[0]
    npair = q_ref.shape[1] // LANES
    heads = range(2 * npair)
    group = math.gcd(nb, MOBA_GROUP)
    qi = pl.program_id(2)

    @pl.when(qi == 0)
    def _():
        lane = lax.broadcasted_iota(jnp.int32, (blk, LANES), 1)
        rowp = lax.broadcasted_iota(jnp.int32, (LANES, blk), 0)
        if nbp > nb:
            kmean_ref[...] = jnp.zeros_like(kmean_ref)
        for n in range(nb):
            rows = slice(n * blk, (n + 1) * blk)
            kmean_ref[n:n + 1, :] = jnp.mean(k_ref[rows, :], axis=0, keepdims=True)
            for pp in range(npair):
                cols = slice(pp * LANES, (pp + 1) * LANES)
                kn = k_ref[rows, cols]
                kb_ref[2 * pp, n] = jnp.where(lane < half, kn, jnp.where(lane == half + n, 1.0, 0.0)).astype(BF16)
                kb_ref[2 * pp + 1, n] = jnp.where(lane >= half, kn, jnp.where(lane == n, 1.0, 0.0)).astype(BF16)
                vtn = v_ref[rows, cols].T
                vt_ref[2 * pp, n] = jnp.where(rowp < half, vtn, jnp.where(rowp == half, 1.0, 0.0)).astype(BF16)
                vt_ref[2 * pp + 1, n] = jnp.where(rowp >= half, vtn, jnp.where(rowp == 0, 1.0, 0.0)).astype(BF16)

    rown = lax.broadcasted_iota(jnp.int32, (nbp, blk), 0)
    lane_k = lax.broadcasted_iota(jnp.int32, (nbp, LANES), 1)
    zeros = jnp.zeros((half, blk), F32)
    qts = [q_ref[:, pp * LANES:(pp + 1) * LANES].T for pp in range(npair)]
    gates, qhs = [], []
    for hh in heads:
        pp, h = divmod(hh, 2)
        km = kmean_ref[:, pp * LANES:(pp + 1) * LANES]
        kmh = jnp.where((lane_k < half) if h == 0 else (lane_k >= half), km, 0.0)
        gates.append(jnp.dot(kmh, qts[pp], preferred_element_type=F32, precision=lax.Precision.HIGHEST))
        qhs.append(qts[pp][h * half:(h + 1) * half] * (half ** -0.5 * LOG2_E))
    q_own = [jnp.concatenate([qhs[hh], zeros] if hh % 2 == 0 else [zeros, qhs[hh]], axis=0).astype(BF16)
             for hh in heads]
    s_own = [jnp.dot(kb_ref[h, qi], q_own[h], preferred_element_type=F32) for h in heads]
    q_past = []
    for hh in heads:
        gate = jnp.where(rown < qi, gates[hh], -jnp.inf)
        bias = jnp.full((nbp, blk), NEG_BIG, F32)
        for _ in range(MOBA_TOPK):
            mx = jnp.max(gate, axis=0, keepdims=True)
            first = jnp.min(jnp.where(gate == mx, rown, nbp), axis=0, keepdims=True)
            pick = (rown == first) & (mx > -jnp.inf)
            bias = jnp.where(pick, 0.0, bias)
            gate = jnp.where(pick, -jnp.inf, gate)
        aug = jnp.concatenate([bias, jnp.zeros((half - nbp, blk), F32)], axis=0)
        q_past.append(jnp.concatenate([qhs[hh], aug] if hh % 2 == 0 else [aug, qhs[hh]], axis=0).astype(BF16))

    causal = (lax.broadcasted_iota(jnp.int32, (blk, blk), 0) <= lax.broadcasted_iota(jnp.int32, (blk, blk), 1))
    ms, ps = [], []
    for h in heads:
        s = jnp.where(causal, s_own[h], NEG_BIG)
        ms.append(jnp.max(s, axis=0, keepdims=True))
        ps.append(jnp.exp2(s - ms[h]).astype(BF16))
    carry = []
    for h in heads:
        carry += [ms[h], jnp.dot(vt_ref[h, qi], ps[h], preferred_element_type=F32)]

    def body(gi, carry):
        sss = [[jnp.dot(kb_ref[h, gi * group + g], q_past[h], preferred_element_type=F32)
                for g in range(group)] for h in heads]
        m_news = []
        for h in heads:
            m_new = carry[2 * h]
            for s in sss[h]:
                m_new = jnp.maximum(m_new, jnp.max(s, axis=0, keepdims=True))
            m_news.append(m_new)
        pcats = [jnp.concatenate([jnp.exp2(s - m_news[h]).astype(BF16) for s in sss[h]], axis=0)
                 for h in heads]
        out = []
        for h in heads:
            vts = jnp.concatenate([vt_ref[h, gi * group + g] for g in range(group)], axis=1)
            alpha = jnp.exp2(carry[2 * h] - m_news[h])
            out += [m_news[h], alpha * carry[2 * h + 1] + jnp.dot(vts, pcats[h], preferred_element_type=F32)]
        return tuple(out)

    final = lax.fori_loop(0, (qi + group - 1) // group, body, tuple(carry))
    rowq = lax.broadcasted_iota(jnp.int32, (LANES, blk), 0)
    for pp in range(npair):
        acc0, acc1 = final[4 * pp + 1], final[4 * pp + 3]
        out_t = jnp.where(rowq < half, acc0 / acc0[half:half + 1], acc1 / acc1[0:1])
        o_ref[:, pp * LANES:(pp + 1) * LANES] = out_t.T.astype(o_ref.dtype)


def _moba_attention(qkv, bsz, s, heads):
    blk = MOBA_BLOCK
    assert s % blk == 0 and (heads * HEAD_DIM) % LANES == 0
    nb = s // blk
    assert nb <= HEAD_DIM
    nbp = -(-nb // 8) * 8
    pairs = heads * HEAD_DIM // LANES
    pps = math.gcd(pairs, MOBA_PAIRS_PER_STEP)
    cw = pps * LANES
    steps = pairs // pps
    return pl.pallas_call(
        _moba_kernel,
        grid=(bsz, steps, nb),
        in_specs=[pl.BlockSpec((blk, cw), lambda b, p, i: (b * nb + i, p)),
                  pl.BlockSpec((s, cw), lambda b, p, i: (b, steps + p)),
                  pl.BlockSpec((s, cw), lambda b, p, i: (b, 2 * steps + p))],
        out_specs=pl.BlockSpec((blk, cw), lambda b, p, i: (b * nb + i, p)),
        out_shape=jax.ShapeDtypeStruct((bsz * s, heads * HEAD_DIM), BF16),
        scratch_shapes=[pltpu.VMEM((2 * pps, nb, blk, LANES), BF16), pltpu.VMEM((2 * pps, nb, LANES, blk), BF16),
                        pltpu.VMEM((nbp, cw), F32)],
        compiler_params=_params("parallel", "parallel", "arbitrary"),
        name="moba_attention",
    )(qkv, qkv, qkv)


def _row_tile(m):
    for t in (512, 256, 128, 64, 32, 16, 8):
        if m % t == 0:
            return t
    raise ValueError(f"row count {m} is not a multiple of 8")


def _col_tile(n, cap=2048):
    best = None
    for t in range(LANES, min(n, cap) + 1, LANES):
        if n % t == 0:
            best = t
    if best is None:
        raise ValueError(f"column count {n} is not a multiple of {LANES}")
    return best


def kernel(x, mem, even_w_in, ssd_conv_w, ssd_conv_b, ssd_dt_bias, ssd_a_log, ssd_d, ssd_norm_g, rwkv_mu, rwkv_w0, rwkv_w2, rwkv_a0, rwkv_a2, rwkv_g2, rwkv_k_k, rwkv_k_a, rwkv_r_k, rwkv_lnx_g, rwkv_lnx_b, even_w_out, odd_w_qkv, odd_w_out, ln_mix_g, ln_mix_b, xa_wq, xa_wkv, xa_wo, ln_xa_g, ln_xa_b, ffn_w13, ffn_w2, ln_ffn_g, ln_ffn_b):
    bsz, s, d = x.shape
    m = bsz * s
    tm = _row_tile(s)
    ssd_width = ssd_norm_g.shape[-1]
    ssd_heads = ssd_dt_bias.shape[-1]
    ssd_xbc = ssd_conv_b.shape[-1]
    ssd_in = ssd_width + ssd_xbc + ssd_heads
    rwkv_width = rwkv_w0.shape[-1]
    ffn_hidden = ffn_w2.shape[1]
    mem2 = mem.reshape(bsz * mem.shape[1], d)
    x2 = x.reshape(m, d)
    for layer in range(DEPTH):
        j = layer // 2
        if layer % 2 == 0:
            w_in = even_w_in[j].astype(BF16)
            w_z = w_in[:, :ssd_width]
            w_xbc = w_in[:, ssd_width:ssd_width + ssd_xbc]
            w_dt = jnp.pad(w_in[:, ssd_width + ssd_xbc:ssd_in], ((0, 0), (0, LANES - ssd_heads)))
            w_rw = w_in[:, ssd_in:]
            z = _matmul(x2, w_z, tm=tm, tn=_col_tile(ssd_width))
            xbc = _matmul(x2, w_xbc, tm=tm, tn=_col_tile(ssd_xbc))
            dt_pad = _matmul(x2, w_dt, tm=tm, tn=LANES)
            y_ssd = _ssd_mixer(z.reshape(bsz, s, -1), xbc.reshape(bsz, s, -1), dt_pad.reshape(bsz, s, -1),
                               ssd_conv_w[j], ssd_conv_b[j], ssd_dt_bias[j], ssd_a_log[j], ssd_d[j],
                               ssd_norm_g[j])
            y_rwkv = _rwkv7_mixer(x2.reshape(bsz, s, d), w_rw, rwkv_mu[j], rwkv_w0[j], rwkv_w2[j], rwkv_a0[j],
                                  rwkv_a2[j], rwkv_g2[j], rwkv_k_k[j], rwkv_k_a[j], rwkv_r_k[j],
                                  rwkv_lnx_g[j], rwkv_lnx_b[j], tm=min(tm, 256))
            w_out = even_w_out[j].astype(BF16)
            x2 = _matmul_residual_ln([y_ssd.reshape(m, -1), y_rwkv.reshape(m, -1)],
                                     [w_out[:ssd_width], w_out[ssd_width:]], x2,
                                     ln_mix_g[layer], ln_mix_b[layer], tm=tm)
        else:
            heads = d // HEAD_DIM
            qkv = _matmul(x2, odd_w_qkv[j].astype(BF16), tm=tm, tn=_col_tile(3 * d))
            attn = _moba_attention(qkv, bsz, s, heads)
            x2 = _matmul_residual_ln([attn], [odd_w_out[j].astype(BF16)], x2,
                                     ln_mix_g[layer], ln_mix_b[layer], tm=tm)
        kv = _matmul(mem2, xa_wkv[layer].astype(BF16), tm=_row_tile(mem2.shape[0]), tn=_col_tile(2 * d))
        x3 = _cross_attention_ln(x2.reshape(bsz, s, d), kv.reshape(bsz, -1, 2 * d),
                                 xa_wq[layer].astype(BF16), xa_wo[layer].astype(BF16),
                                 ln_xa_g[layer], ln_xa_b[layer], tm=tm)
        x2 = x3.reshape(m, d)
        w13 = ffn_w13[layer].astype(BF16)
        h = _swiglu_up(x2, w13[:, :ffn_hidden], w13[:, ffn_hidden:], tm=tm, tn=_col_tile(ffn_hidden, 1536))
        x2 = _matmul_residual_ln([h], [ffn_w2[layer].astype(BF16)], x2,
                                 ln_ffn_g[layer], ln_ffn_b[layer], tm=tm)
    return x2.reshape(bsz, s, d)
```

```python
import functools
import math

import jax
import jax.numpy as jnp
from jax import lax
from jax.experimental import pallas as pl
from jax.experimental.pallas import tpu as pltpu

F32 = jnp.float32
BF16 = jnp.bfloat16

HEAD_DIM = 64
LANES = 128
SSD_GROUPS = 2
SSD_STATE = 128
SSD_CONV = 4
SSD_CHUNK = 128
RWKV_DECAY_LORA = 64
RWKV_ICLR_LORA = 64
RWKV_GATE_LORA = 128
RWKV_CHUNK = 64
MOBA_BLOCK = 256
MOBA_TOPK = 3
MOBA_GROUP = 4
MOBA_PAIRS_PER_STEP = 2
XATTN_HEADS = 4
DEPTH = 2
DEEPNORM_ALPHA = (2 * DEPTH) ** 0.25
LN_EPS = 1e-5
RMS_EPS = 1e-5
RWKV_LNX_EPS = 64e-5
NEG_BIG = -1e30
LOG2_E = math.log2(math.e)
VMEM_LIMIT = 56 * 1024 * 1024


def _params(*sem):
    return pltpu.CompilerParams(dimension_semantics=sem, vmem_limit_bytes=VMEM_LIMIT)


def _dot(a, b):
    return jnp.dot(a.astype(BF16), b.astype(BF16), preferred_element_type=F32)


def _dot_nt(a, b):
    return lax.dot_general(a.astype(BF16), b.astype(BF16), (((1,), (1,)), ((), ())),
                           preferred_element_type=F32)


def _dot_tn(a, b):
    return lax.dot_general(a.astype(BF16), b.astype(BF16), (((0,), (0,)), ((), ())),
                           preferred_element_type=F32)


def _split3(x):
    hi = x.astype(BF16)
    r1 = x - hi.astype(F32)
    mid = r1.astype(BF16)
    lo = (r1 - mid.astype(F32)).astype(BF16)
    return hi, mid, lo


def _dot_exact_lhs(m, x):
    hi, mid, lo = _split3(x)
    m = m.astype(BF16)
    return (jnp.dot(m, hi, preferred_element_type=F32) + jnp.dot(m, mid, preferred_element_type=F32)
            + jnp.dot(m, lo, preferred_element_type=F32))


def _dot_exact_rhs(x, m):
    hi, mid, lo = _split3(x)
    m = m.astype(BF16)
    return (jnp.dot(hi, m, preferred_element_type=F32) + jnp.dot(mid, m, preferred_element_type=F32)
            + jnp.dot(lo, m, preferred_element_type=F32))


def _sigmoid(x):
    return 1.0 / (1.0 + jnp.exp(-x))


def _softplus(x):
    return jnp.maximum(x, 0.0) + jnp.log1p(jnp.exp(-jnp.abs(x)))


def _layer_norm(v, g, b):
    mu = jnp.mean(v, axis=-1, keepdims=True)
    c = v - mu
    var = jnp.mean(c * c, axis=-1, keepdims=True)
    return c * lax.rsqrt(var + LN_EPS) * g + b


def _col_chunks(n, cap=1536):
    width = _col_tile(n, cap)
    return [slice(j, j + width) for j in range(0, n, width)]


def _mm_kernel(n_out, x_ref, *refs):
    x = x_ref[...].astype(BF16)
    for w_ref, o_ref in zip(refs[:n_out], refs[n_out:]):
        for sl in _col_chunks(w_ref.shape[1]):
            o_ref[:, sl] = jnp.dot(x, w_ref[:, sl], preferred_element_type=F32).astype(o_ref.dtype)


def _matmul(x, ws, *, tm, out_dtype=F32):
    m, k = x.shape
    assert m % tm == 0
    return pl.pallas_call(
        functools.partial(_mm_kernel, len(ws)),
        grid=(m // tm,),
        in_specs=[pl.BlockSpec((tm, k), lambda i: (i, 0))] + [pl.BlockSpec(w.shape, lambda i: (0, 0)) for w in ws],
        out_specs=[pl.BlockSpec((tm, w.shape[1]), lambda i: (i, 0)) for w in ws],
        out_shape=[jax.ShapeDtypeStruct((m, w.shape[1]), out_dtype) for w in ws],
        compiler_params=_params("parallel"),
        name="matmul",
    )(x, *ws)


def _mm_res_ln_kernel(n_in, *refs):
    hs = refs[:n_in]
    ws = refs[n_in:2 * n_in]
    res_ref, g_ref, b_ref, o_ref = refs[2 * n_in:]
    acc = _dot(hs[0][...], ws[0][...])
    for h_ref, w_ref in zip(hs[1:], ws[1:]):
        acc = acc + _dot(h_ref[...], w_ref[...])
    o_ref[...] = _layer_norm(DEEPNORM_ALPHA * res_ref[...] + acc, g_ref[...], b_ref[...])


def _matmul_residual_ln(hs, ws, res, g, b, *, tm):
    m, d = res.shape
    n_in = len(hs)
    in_specs = ([pl.BlockSpec((tm, h.shape[1]), lambda i: (i, 0)) for h in hs]
                + [pl.BlockSpec(w.shape, lambda i: (0, 0)) for w in ws]
                + [pl.BlockSpec((tm, d), lambda i: (i, 0)),
                   pl.BlockSpec((1, d), lambda i: (0, 0)),
                   pl.BlockSpec((1, d), lambda i: (0, 0))])
    return pl.pallas_call(
        functools.partial(_mm_res_ln_kernel, n_in),
        grid=(m // tm,),
        in_specs=in_specs,
        out_specs=pl.BlockSpec((tm, d), lambda i: (i, 0)),
        out_shape=jax.ShapeDtypeStruct((m, d), F32),
        compiler_params=_params("parallel"),
        name="matmul_residual_ln",
    )(*hs, *ws, res, g.reshape(1, d), b.reshape(1, d))


def _swiglu_kernel(x_ref, w1_ref, w3_ref, o_ref):
    x = x_ref[...].astype(BF16)
    for sl in _col_chunks(o_ref.shape[1]):
        gate = jnp.dot(x, w1_ref[:, sl], preferred_element_type=F32)
        up = jnp.dot(x, w3_ref[:, sl], preferred_element_type=F32)
        o_ref[:, sl] = (gate * _sigmoid(gate) * up).astype(o_ref.dtype)


def _swiglu_up(x, w1, w3, *, tm):
    m, k = x.shape
    n = w1.shape[1]
    return pl.pallas_call(
        _swiglu_kernel,
        grid=(m // tm,),
        in_specs=[pl.BlockSpec((tm, k), lambda i: (i, 0)),
                  pl.BlockSpec((k, n), lambda i: (0, 0)),
                  pl.BlockSpec((k, n), lambda i: (0, 0))],
        out_specs=pl.BlockSpec((tm, n), lambda i: (i, 0)),
        out_shape=jax.ShapeDtypeStruct((m, n), BF16),
        compiler_params=_params("parallel"),
        name="swiglu_up",
    )(x, w1, w3)


def _xattn_kernel(x_ref, kv_ref, wq_ref, wo_ref, g_ref, b_ref, o_ref):
    x = x_ref[0]
    d = x.shape[-1]
    hd = d // XATTN_HEADS
    q = _dot(x, wq_ref[...])
    kv = kv_ref[0]
    heads = range(XATTN_HEADS)
    ss = [_dot_nt(q[:, h * hd:(h + 1) * hd], kv[:, h * hd:(h + 1) * hd]) * (hd ** -0.5) for h in heads]
    ps = []
    for s in ss:
        p = jnp.exp(s - jnp.max(s, axis=-1, keepdims=True))
        ps.append(p / jnp.sum(p, axis=-1, keepdims=True))
    o = jnp.concatenate([_dot(ps[h], kv[:, d + h * hd:d + (h + 1) * hd]) for h in heads], axis=-1)
    xa = _dot(o, wo_ref[...])
    o_ref[0] = _layer_norm(DEEPNORM_ALPHA * x + xa, g_ref[...], b_ref[...])


def _cross_attention_ln(x, kv, wq, wo, g, b, *, tm):
    bsz, s, d = x.shape
    m = kv.shape[1]
    return pl.pallas_call(
        _xattn_kernel,
        grid=(bsz, s // tm),
        in_specs=[pl.BlockSpec((1, tm, d), lambda bi, i: (bi, i, 0)),
                  pl.BlockSpec((1, m, 2 * d), lambda bi, i: (bi, 0, 0)),
                  pl.BlockSpec((d, d), lambda bi, i: (0, 0)),
                  pl.BlockSpec((d, d), lambda bi, i: (0, 0)),
                  pl.BlockSpec((1, d), lambda bi, i: (0, 0)),
                  pl.BlockSpec((1, d), lambda bi, i: (0, 0))],
        out_specs=pl.BlockSpec((1, tm, d), lambda bi, i: (bi, i, 0)),
        out_shape=jax.ShapeDtypeStruct((bsz, s, d), F32),
        compiler_params=_params("parallel", "parallel"),
        name="cross_attention_ln",
    )(x, kv, wq, wo, g.reshape(1, d), b.reshape(1, d))


def _ssd_kernel(z_ref, xbc_ref, xbcp_ref, dt_ref, dtt_ref, cw_ref, cb_ref, dtb_ref, dtbt_ref,
                aneg_ref, anegt_ref, dskip_ref, ng_ref, y_ref, state_ref, ext_ref):
    q = SSD_CHUNK
    width = z_ref.shape[-1]
    n_pairs = width // LANES
    c = pl.program_id(1)

    @pl.when(c == 0)
    def _():
        state_ref[...] = jnp.zeros_like(state_ref)

    ext_ref[0:8, :] = jnp.where(c > 0, xbcp_ref[0], 0.0)
    ext_ref[8:8 + q, :] = xbc_ref[0]
    conv = cb_ref[...] + cw_ref[SSD_CONV - 1:SSD_CONV, :] * ext_ref[8:8 + q, :]
    for k in range(SSD_CONV - 1):
        off = 8 - (SSD_CONV - 1) + k
        conv = conv + cw_ref[k:k + 1, :] * ext_ref[off:off + q, :]
    xc = conv * _sigmoid(conv)
    xs = xc[:, :width]
    gn = SSD_GROUPS * SSD_STATE
    bm = xc[:, width:width + gn]
    cm = xc[:, width + gn:width + 2 * gn]

    dt = _softplus(dt_ref[0] + dtb_ref[...])
    a = dt * aneg_ref[...]
    dtt = _softplus(dtt_ref[0] + dtbt_ref[...])
    at = dtt * anegt_ref[...]
    row = lax.broadcasted_iota(jnp.int32, (q, q), 0)
    col = lax.broadcasted_iota(jnp.int32, (q, q), 1)
    causal = row >= col
    tri = jnp.where(causal, 1.0, 0.0)
    a_cum = _dot_exact_lhs(tri, a)
    a_cumt = _dot_exact_rhs(at, jnp.where(row <= col, 1.0, 0.0))

    lane = lax.broadcasted_iota(jnp.int32, (1, LANES), 1)
    lane_lo = lane < HEAD_DIM
    rowp = lax.broadcasted_iota(jnp.int32, (LANES, 1), 0)
    pairs_per_group = n_pairs // SSD_GROUPS
    ys = []
    for p in range(n_pairs):
        g = p // pairs_per_group
        h0, h1 = 2 * p, 2 * p + 1
        bg = bm[:, g * SSD_STATE:(g + 1) * SSD_STATE]
        cg = cm[:, g * SSD_STATE:(g + 1) * SSD_STATE]
        cb = _dot_nt(cg, bg)
        xs_p = xs[:, p * LANES:(p + 1) * LANES]
        dt_p = jnp.where(lane_lo, dt[:, h0:h0 + 1], dt[:, h1:h1 + 1])
        acum_p = jnp.where(lane_lo, a_cum[:, h0:h0 + 1], a_cum[:, h1:h1 + 1])
        xdt = xs_p * dt_p
        ms = []
        for h in (h0, h1):
            seg = a_cum[:, h:h + 1] - a_cumt[h:h + 1, :]
            ms.append(cb * jnp.exp(jnp.where(causal, seg, NEG_BIG)))
        m2 = jnp.concatenate(ms, axis=1)
        x2 = jnp.concatenate([jnp.where(lane_lo, xdt, 0.0), jnp.where(lane_lo, 0.0, xdt)], axis=0)
        y_diag = _dot(m2, x2)
        prev = state_ref[p]
        y_off = _dot_nt(cg, prev) * jnp.exp(acum_p)
        a_last = jnp.where(lane_lo, a_cum[q - 1:q, h0:h0 + 1], a_cum[q - 1:q, h1:h1 + 1])
        xdw = xdt * jnp.exp(a_last - acum_p)
        st = _dot_tn(xdw, bg)
        cd = jnp.where(rowp < HEAD_DIM, jnp.exp(a_cumt[h0:h0 + 1, q - 1:q]),
                       jnp.exp(a_cumt[h1:h1 + 1, q - 1:q]))
        state_ref[p] = prev * cd + st
        d_p = dskip_ref[:, p * LANES:(p + 1) * LANES]
        ys.append(y_diag + y_off + d_p * xs_p)
    y = jnp.concatenate(ys, axis=1)
    z = z_ref[0]
    y = y * (z * _sigmoid(z))
    gw = width // SSD_GROUPS
    outs = []
    for g in range(SSD_GROUPS):
        yg = y[:, g * gw:(g + 1) * gw]
        outs.append(yg * lax.rsqrt(jnp.mean(yg * yg, axis=-1, keepdims=True) + RMS_EPS))
    y_ref[0] = (jnp.concatenate(outs, axis=1) * ng_ref[...]).astype(y_ref.dtype)


def _ssd_mixer(z, xbc, dt_pad, conv_w, conv_b, dt_bias, a_log, d_skip, norm_g):
    bsz, l, width = z.shape
    heads = width // HEAD_DIM
    xw = xbc.shape[-1]
    q = SSD_CHUNK
    nc = l // q
    dtt = jnp.swapaxes(dt_pad[:, :, :heads], 1, 2)
    pad = LANES - heads
    dtb = jnp.pad(dt_bias, (0, pad)).reshape(1, LANES)
    a_neg = -jnp.exp(a_log.astype(F32))
    aneg = jnp.pad(a_neg, (0, pad)).reshape(1, LANES)
    dskip = jnp.repeat(d_skip, HEAD_DIM).reshape(1, width)
    row = lambda n: pl.BlockSpec((1, n), lambda bi, c: (0, 0))
    return pl.pallas_call(
        _ssd_kernel,
        grid=(bsz, nc),
        in_specs=[pl.BlockSpec((1, q, width), lambda bi, c: (bi, c, 0)),
                  pl.BlockSpec((1, q, xw), lambda bi, c: (bi, c, 0)),
                  pl.BlockSpec((1, 8, xw), lambda bi, c: (bi, jnp.maximum(c * (q // 8) - 1, 0), 0)),
                  pl.BlockSpec((1, q, LANES), lambda bi, c: (bi, c, 0)),
                  pl.BlockSpec((1, heads, q), lambda bi, c: (bi, 0, c)),
                  pl.BlockSpec((SSD_CONV, xw), lambda bi, c: (0, 0)),
                  row(xw), row(LANES),
                  pl.BlockSpec((heads, 1), lambda bi, c: (0, 0)),
                  row(LANES),
                  pl.BlockSpec((heads, 1), lambda bi, c: (0, 0)),
                  row(width), row(width)],
        out_specs=pl.BlockSpec((1, q, width), lambda bi, c: (bi, c, 0)),
        out_shape=jax.ShapeDtypeStruct((bsz, l, width), BF16),
        scratch_shapes=[pltpu.VMEM((width // LANES, LANES, SSD_STATE), F32),
                        pltpu.VMEM((q + 8, xw), F32)],
        compiler_params=_params("parallel", "arbitrary"),
        name="ssd_mixer",
    )(z, xbc, xbc, dt_pad, dtt, conv_w, conv_b.reshape(1, xw), dtb, dt_bias.reshape(heads, 1),
      aneg, a_neg.reshape(heads, 1), dskip, norm_g.reshape(1, width))


def _rwkv_prep_kernel(x_ref, xp_ref, win_ref, mu_ref, w0_ref, a0_ref, kk_ref, ka_ref, w2_ref, a2_ref, g2_ref,
                      e_ref, et_ref, r_ref, lw_ref, k_ref, v_ref, an_ref, bn_ref, g_ref, ext_ref):
    tm = x_ref.shape[1]
    width = r_ref.shape[-1]
    i = pl.program_id(1)
    xe = jnp.concatenate([jnp.where(i > 0, xp_ref[0], 0.0), x_ref[0]], axis=0)
    for sl in _col_chunks(ext_ref.shape[1], 1664):
        ext_ref[:, sl] = _dot(xe, win_ref[:, sl])
    rw = ext_ref[8:8 + tm, :]
    shifted = ext_ref[7:7 + tm, :]
    s = rw + (shifted - rw) * mu_ref[...]
    r = s[:, :width]
    k = s[:, width:2 * width]
    v = s[:, 2 * width:3 * width]
    lo = s[:, 3 * width:3 * width + RWKV_DECAY_LORA + RWKV_ICLR_LORA]
    g_lo = s[:, 3 * width + RWKV_DECAY_LORA + RWKV_ICLR_LORA:]
    wv = w0_ref[...] + _dot(jnp.tanh(lo), w2_ref[...])
    w = -_softplus(-wv) - 0.5
    av = _sigmoid(a0_ref[...] + _dot(lo, a2_ref[...]))
    g = _dot(_sigmoid(g_lo), g2_ref[...])
    kkr = k * kk_ref[...]
    ss = _dot_exact_rhs(kkr * kkr, e_ref[...])
    inv = lax.rsqrt(jnp.maximum(ss, 1e-24))
    kk = kkr * _dot_exact_rhs(inv, et_ref[...])
    r_ref[0] = r.astype(r_ref.dtype)
    lw_ref[0] = -jnp.exp(w)
    k_ref[0] = (k * (1.0 + (av - 1.0) * ka_ref[...])).astype(k_ref.dtype)
    v_ref[0] = v.astype(v_ref.dtype)
    an_ref[0] = (-kk).astype(an_ref.dtype)
    bn_ref[0] = (kk * av).astype(bn_ref.dtype)
    g_ref[0] = g.astype(g_ref.dtype)


def _rwkv_scan_kernel(r_ref, lw_ref, k_ref, v_ref, an_ref, bn_ref, g_ref, rk_ref, lg_ref, lb_ref,
                      y_ref, state_ref):
    c = RWKV_CHUNK
    c2 = 2 * c
    width = r_ref.shape[-1]
    n_pairs = width // LANES
    ci = pl.program_id(1)

    @pl.when(ci == 0)
    def _():
        state_ref[...] = jnp.zeros_like(state_ref)

    row = lax.broadcasted_iota(jnp.int32, (c, c), 0)
    col = lax.broadcasted_iota(jnp.int32, (c, c), 1)
    tri = jnp.where(row >= col, 1.0, 0.0)
    row2 = lax.broadcasted_iota(jnp.int32, (c2, c2), 0)
    col2 = lax.broadcasted_iota(jnp.int32, (c2, c2), 1)
    same = (row2 // c) == (col2 // c)
    strict = jnp.where(same & ((row2 % c) > (col2 % c)), 1.0, 0.0)
    incl = jnp.where(same & ((row2 % c) >= (col2 % c)), 1.0, 0.0)
    eye = jnp.where(row2 == col2, 1.0, 0.0)
    lane = lax.broadcasted_iota(jnp.int32, (1, LANES), 1)
    m0 = jnp.where(lane < HEAD_DIM, 1.0, 0.0)
    m1 = 1.0 - m0

    def stack(x):
        return jnp.concatenate([x * m0, x * m1], axis=0)

    sls = [slice(p * LANES, (p + 1) * LANES) for p in range(n_pairs)]
    lhs4s, rhs4s, v_stks, bks, g_lasts = [], [], [], [], []
    for sl in sls:
        r = r_ref[0, :, sl].astype(F32)
        lw = lw_ref[0, :, sl]
        k = k_ref[0, :, sl].astype(F32)
        b = bn_ref[0, :, sl].astype(F32)
        cum = _dot_exact_lhs(tri, lw)
        cum_last = cum[c - 1:c, :]
        e_neg = jnp.exp(-cum)
        e_tail = jnp.exp(cum_last - cum)
        at = an_ref[0, :, sl].astype(F32) * jnp.exp(cum - lw)
        rt = r * jnp.exp(cum)
        bt = (b * e_neg).astype(BF16)
        kt = (k * e_neg).astype(BF16)
        lhs4s.append(jnp.concatenate([stack(at), stack(rt)], axis=0).astype(BF16))
        rhs4s.append(jnp.concatenate([bt, bt, kt, kt], axis=0))
        v_stks.append(stack(v_ref[0, :, sl].astype(F32)).astype(BF16))
        bks.append(jnp.concatenate([stack(b * e_tail), stack(k * e_tail)], axis=0).astype(BF16))
        g_lasts.append(jnp.exp(cum_last))
    gms = [_dot_nt(l, rh) for l, rh in zip(lhs4s, rhs4s)]
    pws = [gm[:c2, :c2] * strict for gm in gms]
    a_aks = [(gm[:c2, c2:] * strict).astype(BF16) for gm in gms]
    a_rbks = [jnp.concatenate([gm[c2:, :c2] * incl, gm[c2:, c2:] * incl], axis=1).astype(BF16) for gm in gms]
    akvs = [_dot(a_ak, v_stk) for a_ak, v_stk in zip(a_aks, v_stks)]
    tinvs = [eye + pw for pw in pws]
    for _ in range(int(math.log2(c)) - 1):
        pws = [_dot(pw, pw) for pw in pws]
        tinvs = [tinv + _dot(tinv, pw) for tinv, pw in zip(tinvs, pws)]
    sts = [state_ref[p] for p in range(n_pairs)]
    ahrhs = [_dot_nt(l, st) for l, st in zip(lhs4s, sts)]
    u_stks = [_dot(tinv, ahrh[:c2] + akv) for tinv, ahrh, akv in zip(tinvs, ahrhs, akvs)]
    uvs = [jnp.concatenate([u.astype(BF16), v_stk], axis=0) for u, v_stk in zip(u_stks, v_stks)]
    for p in range(n_pairs):
        state_ref[p] = sts[p] * g_lasts[p] + _dot_tn(uvs[p], bks[p])
    y_stks = [ahrh[c2:] + _dot(a_rbk, uv) for ahrh, a_rbk, uv in zip(ahrhs, a_rbks, uvs)]
    ys = [y_stk[:c] + y_stk[c:] for y_stk in y_stks]
    lane_lo = lane < HEAD_DIM

    def head_sum(x):
        lo = jnp.sum(x * m0, axis=-1, keepdims=True)
        hi = jnp.sum(x * m1, axis=-1, keepdims=True)
        return jnp.where(lane_lo, lo, hi)

    bonus = [head_sum(r_ref[0, :, sl].astype(F32) * k_ref[0, :, sl].astype(F32) * rk_ref[:, sl]) for sl in sls]
    ycs = [y - head_sum(y) * (1.0 / HEAD_DIM) for y in ys]
    yvs = [head_sum(yc * yc) * (1.0 / HEAD_DIM) for yc in ycs]
    for p, sl in enumerate(sls):
        yn = ycs[p] * lax.rsqrt(yvs[p] + RWKV_LNX_EPS) * lg_ref[:, sl] + lb_ref[:, sl]
        out = (yn + bonus[p] * v_ref[0, :, sl].astype(F32)) * g_ref[0, :, sl].astype(F32)
        y_ref[0, :, sl] = out.astype(y_ref.dtype)


def _rwkv7_mixer(x, w_in, mu, w0, w2, a0, a2, g2, k_k, k_a, r_k, lnx_g, lnx_b, *, tm):
    bsz, l, d = x.shape
    win = w_in.shape[1]
    width = w0.shape[0]
    w2p = jnp.concatenate([w2, jnp.zeros((RWKV_ICLR_LORA, width), F32)], axis=0).astype(BF16)
    a2p = jnp.concatenate([jnp.zeros((RWKV_DECAY_LORA, width), F32), a2], axis=0).astype(BF16)
    head_of = jnp.arange(width) // HEAD_DIM
    e = (head_of[:, None] == jnp.arange(LANES)[None, :]).astype(BF16)
    et = e.T
    vec = lambda x: x.reshape(1, -1)
    row = lambda n: pl.BlockSpec((1, n), lambda bi, i: (0, 0))
    full = lambda a: pl.BlockSpec(a.shape, lambda bi, i: (0, 0))
    tile = pl.BlockSpec((1, tm, width), lambda bi, i: (bi, i, 0))
    sds = lambda dt: jax.ShapeDtypeStruct((bsz, l, width), dt)
    g2b = g2.astype(BF16)
    r, lw, k, v, an, bn, g = pl.pallas_call(
        _rwkv_prep_kernel,
        grid=(bsz, l // tm),
        in_specs=[pl.BlockSpec((1, tm, d), lambda bi, i: (bi, i, 0)),
                  pl.BlockSpec((1, 8, d), lambda bi, i: (bi, jnp.maximum(i * (tm // 8) - 1, 0), 0)),
                  full(w_in), row(win), row(width), row(width), row(width), row(width),
                  full(w2p), full(a2p), full(g2b), full(e), full(et)],
        out_specs=[tile] * 7,
        out_shape=[sds(BF16), sds(F32)] + [sds(BF16)] * 5,
        scratch_shapes=[pltpu.VMEM((tm + 8, win), F32)],
        compiler_params=_params("parallel", "parallel"),
        name="rwkv_prep",
    )(x, x, w_in, vec(mu), vec(w0), vec(a0), vec(k_k), vec(k_a), w2p, a2p, g2b, e, et)
    c = RWKV_CHUNK
    ctile = pl.BlockSpec((1, c, width), lambda bi, ci: (bi, ci, 0))
    crow = pl.BlockSpec((1, width), lambda bi, ci: (0, 0))
    return pl.pallas_call(
        _rwkv_scan_kernel,
        grid=(bsz, l // c),
        in_specs=[ctile] * 7 + [crow] * 3,
        out_specs=ctile,
        out_shape=jax.ShapeDtypeStruct((bsz, l, width), BF16),
        scratch_shapes=[pltpu.VMEM((width // LANES, LANES, LANES), F32)],
        compiler_params=_params("parallel", "arbitrary"),
        name="rwkv_scan",
    )(r, lw, k, v, an, bn, g, vec(r_k), vec(lnx_g), vec(lnx_b))


def _moba_kernel(q_ref, k_ref, v_ref, o_ref, kb_ref, vt_ref, kmean_ref):
    blk = MOBA_BLOCK
    half = HEAD_DIM
    nb = k_ref.shape[0] // blk
    nbp = kmean_ref.shape[0]
    npair = q_ref.shape[1] // LANES
    heads = range(2 * npair)
    group = math.gcd(nb, MOBA_GROUP)
    qi = pl.program_id(2)

    @pl.when(qi == 0)
    def _():
        lane = lax.broadcasted_iota(jnp.int32, (blk, LANES), 1)
        rowp = lax.broadcasted_iota(jnp.int32, (LANES, blk), 0)
        if nbp > nb:
            kmean_ref[...] = jnp.zeros_like(kmean_ref)
        for n in range(nb):
            rows = slice(n * blk, (n + 1) * blk)
            kmean_ref[n:n + 1, :] = jnp.mean(k_ref[rows, :], axis=0, keepdims=True)
            for pp in range(npair):
                cols = slice(pp * LANES, (pp + 1) * LANES)
                kn = k_ref[rows, cols]
                kb_ref[2 * pp, n] = jnp.where(lane < half, kn, jnp.where(lane == half + n, 1.0, 0.0)).astype(BF16)
                kb_ref[2 * pp + 1, n] = jnp.where(lane >= half, kn, jnp.where(lane == n, 1.0, 0.0)).astype(BF16)
                vtn = v_ref[rows, cols].T
                vt_ref[2 * pp, n] = jnp.where(rowp < half, vtn, jnp.where(rowp == half, 1.0, 0.0)).astype(BF16)
                vt_ref[2 * pp + 1, n] = jnp.where(rowp >= half, vtn, jnp.where(rowp == 0, 1.0, 0.0)).astype(BF16)

    rown = lax.broadcasted_iota(jnp.int32, (nbp, blk), 0)
    lane_k = lax.broadcasted_iota(jnp.int32, (nbp, LANES), 1)
    zeros = jnp.zeros((half, blk), F32)
    qts = [q_ref[:, pp * LANES:(pp + 1) * LANES].T for pp in range(npair)]
    gates, qhs = [], []
    for hh in heads:
        pp, h = divmod(hh, 2)
        km = kmean_ref[:, pp * LANES:(pp + 1) * LANES]
        kmh = jnp.where((lane_k < half) if h == 0 else (lane_k >= half), km, 0.0)
        gates.append(jnp.dot(kmh, qts[pp], preferred_element_type=F32, precision=lax.Precision.HIGHEST))
        qhs.append(qts[pp][h * half:(h + 1) * half] * (half ** -0.5 * LOG2_E))
    q_own = [jnp.concatenate([qhs[hh], zeros] if hh % 2 == 0 else [zeros, qhs[hh]], axis=0).astype(BF16)
             for hh in heads]
    s_own = [jnp.dot(kb_ref[h, qi], q_own[h], preferred_element_type=F32) for h in heads]
    q_past = []
    for hh in heads:
        gate = jnp.where(rown < qi, gates[hh], -jnp.inf)
        bias = jnp.full((nbp, blk), NEG_BIG, F32)
        for _ in range(MOBA_TOPK):
            mx = jnp.max(gate, axis=0, keepdims=True)
            first = jnp.min(jnp.where(gate == mx, rown, nbp), axis=0, keepdims=True)
            pick = (rown == first) & (mx > -jnp.inf)
            bias = jnp.where(pick, 0.0, bias)
            gate = jnp.where(pick, -jnp.inf, gate)
        aug = jnp.concatenate([bias, jnp.zeros((half - nbp, blk), F32)], axis=0)
        q_past.append(jnp.concatenate([qhs[hh], aug] if hh % 2 == 0 else [aug, qhs[hh]], axis=0).astype(BF16))

    causal = (lax.broadcasted_iota(jnp.int32, (blk, blk), 0) <= lax.broadcasted_iota(jnp.int32, (blk, blk), 1))
    ms, ps = [], []
    for h in heads:
        s = jnp.where(causal, s_own[h], NEG_BIG)
        ms.append(jnp.max(s, axis=0, keepdims=True))
        ps.append(jnp.exp2(s - ms[h]).astype(BF16))
    carry = []
    for h in heads:
        carry += [ms[h], jnp.dot(vt_ref[h, qi], ps[h], preferred_element_type=F32)]

    def body(gi, carry):
        sss = [[jnp.dot(kb_ref[h, gi * group + g], q_past[h], preferred_element_type=F32)
                for g in range(group)] for h in heads]
        m_news = []
        for h in heads:
            m_new = carry[2 * h]
            for s in sss[h]:
                m_new = jnp.maximum(m_new, jnp.max(s, axis=0, keepdims=True))
            m_news.append(m_new)
        pcats = [jnp.concatenate([jnp.exp2(s - m_news[h]).astype(BF16) for s in sss[h]], axis=0)
                 for h in heads]
        out = []
        for h in heads:
            vts = jnp.concatenate([vt_ref[h, gi * group + g] for g in range(group)], axis=1)
            alpha = jnp.exp2(carry[2 * h] - m_news[h])
            out += [m_news[h], alpha * carry[2 * h + 1] + jnp.dot(vts, pcats[h], preferred_element_type=F32)]
        return tuple(out)

    final = lax.fori_loop(0, (qi + group - 1) // group, body, tuple(carry))
    rowq = lax.broadcasted_iota(jnp.int32, (LANES, blk), 0)
    for pp in range(npair):
        acc0, acc1 = final[4 * pp + 1], final[4 * pp + 3]
        out_t = jnp.where(rowq < half, acc0 / acc0[half:half + 1], acc1 / acc1[0:1])
        o_ref[:, pp * LANES:(pp + 1) * LANES] = out_t.T.astype(o_ref.dtype)


def _moba_attention(qkv, bsz, s, heads):
    blk = MOBA_BLOCK
    assert s % blk == 0 and (heads * HEAD_DIM) % LANES == 0
    nb = s // blk
    assert nb <= HEAD_DIM
    nbp = -(-nb // 8) * 8
    pairs = heads * HEAD_DIM // LANES
    pps = math.gcd(pairs, MOBA_PAIRS_PER_STEP)
    cw = pps * LANES
    steps = pairs // pps
    return pl.pallas_call(
        _moba_kernel,
        grid=(bsz, steps, nb),
        in_specs=[pl.BlockSpec((blk, cw), lambda b, p, i: (b * nb + i, p)),
                  pl.BlockSpec((s, cw), lambda b, p, i: (b, steps + p)),
                  pl.BlockSpec((s, cw), lambda b, p, i: (b, 2 * steps + p))],
        out_specs=pl.BlockSpec((blk, cw), lambda b, p, i: (b * nb + i, p)),
        out_shape=jax.ShapeDtypeStruct((bsz * s, heads * HEAD_DIM), BF16),
        scratch_shapes=[pltpu.VMEM((2 * pps, nb, blk, LANES), BF16), pltpu.VMEM((2 * pps, nb, LANES, blk), BF16),
                        pltpu.VMEM((nbp, cw), F32)],
        compiler_params=_params("parallel", "parallel", "arbitrary"),
        name="moba_attention",
    )(qkv, qkv, qkv)


def _row_tile(m):
    for t in (512, 256, 128, 64, 32, 16, 8):
        if m % t == 0:
            return t
    raise ValueError(f"row count {m} is not a multiple of 8")


def _col_tile(n, cap=2048):
    best = None
    for t in range(LANES, min(n, cap) + 1, LANES):
        if n % t == 0:
            best = t
    if best is None:
        raise ValueError(f"column count {n} is not a multiple of {LANES}")
    return best


def kernel(x, mem, even_w_in, ssd_conv_w, ssd_conv_b, ssd_dt_bias, ssd_a_log, ssd_d, ssd_norm_g, rwkv_mu, rwkv_w0, rwkv_w2, rwkv_a0, rwkv_a2, rwkv_g2, rwkv_k_k, rwkv_k_a, rwkv_r_k, rwkv_lnx_g, rwkv_lnx_b, even_w_out, odd_w_qkv, odd_w_out, ln_mix_g, ln_mix_b, xa_wq, xa_wkv, xa_wo, ln_xa_g, ln_xa_b, ffn_w13, ffn_w2, ln_ffn_g, ln_ffn_b):
    bsz, s, d = x.shape
    m = bsz * s
    tm = _row_tile(s)
    ssd_width = ssd_norm_g.shape[-1]
    ssd_heads = ssd_dt_bias.shape[-1]
    ssd_xbc = ssd_conv_b.shape[-1]
    ssd_in = ssd_width + ssd_xbc + ssd_heads
    rwkv_width = rwkv_w0.shape[-1]
    ffn_hidden = ffn_w2.shape[1]
    mem2 = mem.reshape(bsz * mem.shape[1], d)
    x2 = x.reshape(m, d)
    for layer in range(DEPTH):
        j = layer // 2
        if layer % 2 == 0:
            w_in = even_w_in[j].astype(BF16)
            w_z = w_in[:, :ssd_width]
            w_xbc = w_in[:, ssd_width:ssd_width + ssd_xbc]
            w_dt = jnp.pad(w_in[:, ssd_width + ssd_xbc:ssd_in], ((0, 0), (0, LANES - ssd_heads)))
            w_rw = w_in[:, ssd_in:]
            z, xbc, dt_pad = _matmul(x2, [w_z, w_xbc, w_dt], tm=tm)
            y_ssd = _ssd_mixer(z.reshape(bsz, s, -1), xbc.reshape(bsz, s, -1), dt_pad.reshape(bsz, s, -1),
                               ssd_conv_w[j], ssd_conv_b[j], ssd_dt_bias[j], ssd_a_log[j], ssd_d[j],
                               ssd_norm_g[j])
            y_rwkv = _rwkv7_mixer(x2.reshape(bsz, s, d), w_rw, rwkv_mu[j], rwkv_w0[j], rwkv_w2[j], rwkv_a0[j],
                                  rwkv_a2[j], rwkv_g2[j], rwkv_k_k[j], rwkv_k_a[j], rwkv_r_k[j],
                                  rwkv_lnx_g[j], rwkv_lnx_b[j], tm=min(tm, 256))
            w_out = even_w_out[j].astype(BF16)
            x2 = _matmul_residual_ln([y_ssd.reshape(m, -1), y_rwkv.reshape(m, -1)],
                                     [w_out[:ssd_width], w_out[ssd_width:]], x2,
                                     ln_mix_g[layer], ln_mix_b[layer], tm=tm)
        else:
            heads = d // HEAD_DIM
            qkv, = _matmul(x2, [odd_w_qkv[j].astype(BF16)], tm=tm)
            attn = _moba_attention(qkv, bsz, s, heads)
            x2 = _matmul_residual_ln([attn], [odd_w_out[j].astype(BF16)], x2,
                                     ln_mix_g[layer], ln_mix_b[layer], tm=tm)
        kv, = _matmul(mem2, [xa_wkv[layer].astype(BF16)], tm=_row_tile(mem2.shape[0]))
        x3 = _cross_attention_ln(x2.reshape(bsz, s, d), kv.reshape(bsz, -1, 2 * d),
                                 xa_wq[layer].astype(BF16), xa_wo[layer].astype(BF16),
                                 ln_xa_g[layer], ln_xa_b[layer], tm=tm)
        x2 = x3.reshape(m, d)
        w13 = ffn_w13[layer].astype(BF16)
        h = _swiglu_up(x2, w13[:, :ffn_hidden], w13[:, ffn_hidden:], tm=tm)
        x2 = _matmul_residual_ln([h], [ffn_w2[layer].astype(BF16)], x2,
                                 ln_ffn_g[layer], ln_ffn_b[layer], tm=tm)
    return x2.reshape(bsz, s, d)
```

```python
import functools
import math

import jax
import jax.numpy as jnp
from jax import lax
from jax.experimental import pallas as pl
from jax.experimental.pallas import tpu as pltpu

F32 = jnp.float32
BF16 = jnp.bfloat16

HEAD_DIM = 64
LANES = 128
SSD_GROUPS = 2
SSD_STATE = 128
SSD_CONV = 4
SSD_CHUNK = 128
RWKV_DECAY_LORA = 64
RWKV_ICLR_LORA = 64
RWKV_GATE_LORA = 128
RWKV_CHUNK = 64
MOBA_BLOCK = 256
MOBA_TOPK = 3
MOBA_GROUP = 4
MOBA_PAIRS_PER_STEP = 2
XATTN_HEADS = 4
DEPTH = 2
DEEPNORM_ALPHA = (2 * DEPTH) ** 0.25
LN_EPS = 1e-5
RMS_EPS = 1e-5
RWKV_LNX_EPS = 64e-5
NEG_BIG = -1e30
LOG2_E = math.log2(math.e)
VMEM_LIMIT = 56 * 1024 * 1024


def _params(*sem):
    return pltpu.CompilerParams(dimension_semantics=sem, vmem_limit_bytes=VMEM_LIMIT)


def _dot(a, b):
    return jnp.dot(a.astype(BF16), b.astype(BF16), preferred_element_type=F32)


def _dot_nt(a, b):
    return lax.dot_general(a.astype(BF16), b.astype(BF16), (((1,), (1,)), ((), ())),
                           preferred_element_type=F32)


def _dot_tn(a, b):
    return lax.dot_general(a.astype(BF16), b.astype(BF16), (((0,), (0,)), ((), ())),
                           preferred_element_type=F32)


def _split3(x):
    hi = x.astype(BF16)
    r1 = x - hi.astype(F32)
    mid = r1.astype(BF16)
    lo = (r1 - mid.astype(F32)).astype(BF16)
    return hi, mid, lo


def _dot_exact_lhs(m, x):
    hi, mid, lo = _split3(x)
    m = m.astype(BF16)
    return (jnp.dot(m, hi, preferred_element_type=F32) + jnp.dot(m, mid, preferred_element_type=F32)
            + jnp.dot(m, lo, preferred_element_type=F32))


def _dot_exact_rhs(x, m):
    hi, mid, lo = _split3(x)
    m = m.astype(BF16)
    return (jnp.dot(hi, m, preferred_element_type=F32) + jnp.dot(mid, m, preferred_element_type=F32)
            + jnp.dot(lo, m, preferred_element_type=F32))


def _sigmoid(x):
    return 1.0 / (1.0 + jnp.exp(-x))


def _softplus(x):
    return jnp.maximum(x, 0.0) + jnp.log1p(jnp.exp(-jnp.abs(x)))


def _layer_norm(v, g, b):
    mu = jnp.mean(v, axis=-1, keepdims=True)
    c = v - mu
    var = jnp.mean(c * c, axis=-1, keepdims=True)
    return c * lax.rsqrt(var + LN_EPS) * g + b


def _col_chunks(n, cap=1536):
    width = _col_tile(n, cap)
    return [slice(j, j + width) for j in range(0, n, width)]


def _mm_kernel(n_out, x_ref, *refs):
    x = x_ref[...].astype(BF16)
    for w_ref, o_ref in zip(refs[:n_out], refs[n_out:]):
        for sl in _col_chunks(w_ref.shape[1]):
            o_ref[:, sl] = jnp.dot(x, w_ref[:, sl], preferred_element_type=F32).astype(o_ref.dtype)


def _matmul(x, ws, *, tm, out_dtype=F32):
    m, k = x.shape
    assert m % tm == 0
    return pl.pallas_call(
        functools.partial(_mm_kernel, len(ws)),
        grid=(m // tm,),
        in_specs=[pl.BlockSpec((tm, k), lambda i: (i, 0))] + [pl.BlockSpec(w.shape, lambda i: (0, 0)) for w in ws],
        out_specs=[pl.BlockSpec((tm, w.shape[1]), lambda i: (i, 0)) for w in ws],
        out_shape=[jax.ShapeDtypeStruct((m, w.shape[1]), out_dtype) for w in ws],
        compiler_params=_params("parallel"),
        name="matmul",
    )(x, *ws)


def _mm_res_ln_kernel(n_in, *refs):
    hs = refs[:n_in]
    ws = refs[n_in:2 * n_in]
    res_ref, g_ref, b_ref, o_ref = refs[2 * n_in:]
    acc = _dot(hs[0][...], ws[0][...])
    for h_ref, w_ref in zip(hs[1:], ws[1:]):
        acc = acc + _dot(h_ref[...], w_ref[...])
    o_ref[...] = _layer_norm(DEEPNORM_ALPHA * res_ref[...] + acc, g_ref[...], b_ref[...])


def _matmul_residual_ln(hs, ws, res, g, b, *, tm):
    m, d = res.shape
    n_in = len(hs)
    in_specs = ([pl.BlockSpec((tm, h.shape[1]), lambda i: (i, 0)) for h in hs]
                + [pl.BlockSpec(w.shape, lambda i: (0, 0)) for w in ws]
                + [pl.BlockSpec((tm, d), lambda i: (i, 0)),
                   pl.BlockSpec((1, d), lambda i: (0, 0)),
                   pl.BlockSpec((1, d), lambda i: (0, 0))])
    return pl.pallas_call(
        functools.partial(_mm_res_ln_kernel, n_in),
        grid=(m // tm,),
        in_specs=in_specs,
        out_specs=pl.BlockSpec((tm, d), lambda i: (i, 0)),
        out_shape=jax.ShapeDtypeStruct((m, d), F32),
        compiler_params=_params("parallel"),
        name="matmul_residual_ln",
    )(*hs, *ws, res, g.reshape(1, d), b.reshape(1, d))


def _swiglu_kernel(x_ref, w1_ref, w3_ref, o_ref):
    x = x_ref[...].astype(BF16)
    for sl in _col_chunks(o_ref.shape[1]):
        gate = jnp.dot(x, w1_ref[:, sl], preferred_element_type=F32)
        up = jnp.dot(x, w3_ref[:, sl], preferred_element_type=F32)
        o_ref[:, sl] = (gate * _sigmoid(gate) * up).astype(o_ref.dtype)


def _swiglu_up(x, w1, w3, *, tm):
    m, k = x.shape
    n = w1.shape[1]
    return pl.pallas_call(
        _swiglu_kernel,
        grid=(m // tm,),
        in_specs=[pl.BlockSpec((tm, k), lambda i: (i, 0)),
                  pl.BlockSpec((k, n), lambda i: (0, 0)),
                  pl.BlockSpec((k, n), lambda i: (0, 0))],
        out_specs=pl.BlockSpec((tm, n), lambda i: (i, 0)),
        out_shape=jax.ShapeDtypeStruct((m, n), BF16),
        compiler_params=_params("parallel"),
        name="swiglu_up",
    )(x, w1, w3)


def _xattn_kernel(x_ref, kv_ref, wq_ref, wo_ref, g_ref, b_ref, o_ref):
    x = x_ref[0]
    d = x.shape[-1]
    hd = d // XATTN_HEADS
    q = _dot(x, wq_ref[...])
    kv = kv_ref[0]
    heads = range(XATTN_HEADS)
    ss = [_dot_nt(q[:, h * hd:(h + 1) * hd], kv[:, h * hd:(h + 1) * hd]) * (hd ** -0.5) for h in heads]
    ps = []
    for s in ss:
        p = jnp.exp(s - jnp.max(s, axis=-1, keepdims=True))
        ps.append(p / jnp.sum(p, axis=-1, keepdims=True))
    o = jnp.concatenate([_dot(ps[h], kv[:, d + h * hd:d + (h + 1) * hd]) for h in heads], axis=-1)
    xa = _dot(o, wo_ref[...])
    o_ref[0] = _layer_norm(DEEPNORM_ALPHA * x + xa, g_ref[...], b_ref[...])


def _cross_attention_ln(x, kv, wq, wo, g, b, *, tm):
    bsz, s, d = x.shape
    m = kv.shape[1]
    return pl.pallas_call(
        _xattn_kernel,
        grid=(bsz, s // tm),
        in_specs=[pl.BlockSpec((1, tm, d), lambda bi, i: (bi, i, 0)),
                  pl.BlockSpec((1, m, 2 * d), lambda bi, i: (bi, 0, 0)),
                  pl.BlockSpec((d, d), lambda bi, i: (0, 0)),
                  pl.BlockSpec((d, d), lambda bi, i: (0, 0)),
                  pl.BlockSpec((1, d), lambda bi, i: (0, 0)),
                  pl.BlockSpec((1, d), lambda bi, i: (0, 0))],
        out_specs=pl.BlockSpec((1, tm, d), lambda bi, i: (bi, i, 0)),
        out_shape=jax.ShapeDtypeStruct((bsz, s, d), F32),
        compiler_params=_params("parallel", "parallel"),
        name="cross_attention_ln",
    )(x, kv, wq, wo, g.reshape(1, d), b.reshape(1, d))


def _ssd_chunk(c, z_ref, xbc_ref, xbcp_ref, dt_ref, dtt_ref, cw_ref, cb_ref, dtb_ref, dtbt_ref,
               aneg_ref, anegt_ref, dskip_ref, ng_ref, y_ref, state_ref, ext_ref):
    q = SSD_CHUNK
    width = z_ref.shape[-1]
    n_pairs = width // LANES

    ext_ref[0:8, :] = jnp.where(c > 0, xbcp_ref[0], 0.0)
    ext_ref[8:8 + q, :] = xbc_ref[0]
    xcs = []
    for sl in _col_chunks(ext_ref.shape[1], 256):
        conv = cb_ref[:, sl] + cw_ref[SSD_CONV - 1:SSD_CONV, sl] * ext_ref[8:8 + q, sl]
        for k in range(SSD_CONV - 1):
            off = 8 - (SSD_CONV - 1) + k
            conv = conv + cw_ref[k:k + 1, sl] * ext_ref[off:off + q, sl]
        xcs.append(conv * _sigmoid(conv))
        yield
    xc = jnp.concatenate(xcs, axis=1)
    xs = xc[:, :width]
    gn = SSD_GROUPS * SSD_STATE
    bm = xc[:, width:width + gn]
    cm = xc[:, width + gn:width + 2 * gn]

    dt = _softplus(dt_ref[0] + dtb_ref[...])
    a = dt * aneg_ref[...]
    dtt = _softplus(dtt_ref[0] + dtbt_ref[...])
    at = dtt * anegt_ref[...]
    row = lax.broadcasted_iota(jnp.int32, (q, q), 0)
    col = lax.broadcasted_iota(jnp.int32, (q, q), 1)
    causal = row >= col
    tri = jnp.where(causal, 1.0, 0.0)
    a_cum = _dot_exact_lhs(tri, a)
    a_cumt = _dot_exact_rhs(at, jnp.where(row <= col, 1.0, 0.0))
    yield

    lane = lax.broadcasted_iota(jnp.int32, (1, LANES), 1)
    lane_lo = lane < HEAD_DIM
    rowp = lax.broadcasted_iota(jnp.int32, (LANES, 1), 0)
    pairs_per_group = n_pairs // SSD_GROUPS
    ys = []
    for p in range(n_pairs):
        g = p // pairs_per_group
        h0, h1 = 2 * p, 2 * p + 1
        bg = bm[:, g * SSD_STATE:(g + 1) * SSD_STATE]
        cg = cm[:, g * SSD_STATE:(g + 1) * SSD_STATE]
        cb = _dot_nt(cg, bg)
        xs_p = xs[:, p * LANES:(p + 1) * LANES]
        dt_p = jnp.where(lane_lo, dt[:, h0:h0 + 1], dt[:, h1:h1 + 1])
        acum_p = jnp.where(lane_lo, a_cum[:, h0:h0 + 1], a_cum[:, h1:h1 + 1])
        xdt = xs_p * dt_p
        ms = []
        for h in (h0, h1):
            seg = a_cum[:, h:h + 1] - a_cumt[h:h + 1, :]
            ms.append(cb * jnp.exp(jnp.where(causal, seg, NEG_BIG)))
        m2 = jnp.concatenate(ms, axis=1)
        x2 = jnp.concatenate([jnp.where(lane_lo, xdt, 0.0), jnp.where(lane_lo, 0.0, xdt)], axis=0)
        y_diag = _dot(m2, x2)
        prev = state_ref[p]
        y_off = _dot_nt(cg, prev) * jnp.exp(acum_p)
        a_last = jnp.where(lane_lo, a_cum[q - 1:q, h0:h0 + 1], a_cum[q - 1:q, h1:h1 + 1])
        xdw = xdt * jnp.exp(a_last - acum_p)
        st = _dot_tn(xdw, bg)
        cd = jnp.where(rowp < HEAD_DIM, jnp.exp(a_cumt[h0:h0 + 1, q - 1:q]),
                       jnp.exp(a_cumt[h1:h1 + 1, q - 1:q]))
        state_ref[p] = prev * cd + st
        d_p = dskip_ref[:, p * LANES:(p + 1) * LANES]
        ys.append(y_diag + y_off + d_p * xs_p)
        yield
    y = jnp.concatenate(ys, axis=1)
    z = z_ref[0]
    y = y * (z * _sigmoid(z))
    gw = width // SSD_GROUPS
    outs = []
    for g in range(SSD_GROUPS):
        yg = y[:, g * gw:(g + 1) * gw]
        outs.append(yg * lax.rsqrt(jnp.mean(yg * yg, axis=-1, keepdims=True) + RMS_EPS))
    y_ref[0] = (jnp.concatenate(outs, axis=1) * ng_ref[...]).astype(y_ref.dtype)


def _ssd_operands(z, xbc, dt_pad, conv_w, conv_b, dt_bias, a_log, d_skip, norm_g):
    bsz, l, width = z.shape
    heads = width // HEAD_DIM
    xw = xbc.shape[-1]
    q = SSD_CHUNK
    dtt = jnp.swapaxes(dt_pad[:, :, :heads], 1, 2)
    pad = LANES - heads
    dtb = jnp.pad(dt_bias, (0, pad)).reshape(1, LANES)
    a_neg = -jnp.exp(a_log.astype(F32))
    aneg = jnp.pad(a_neg, (0, pad)).reshape(1, LANES)
    dskip = jnp.repeat(d_skip, HEAD_DIM).reshape(1, width)
    row = lambda n: pl.BlockSpec((1, n), lambda bi, c: (0, 0))
    specs = [pl.BlockSpec((1, q, width), lambda bi, c: (bi, c, 0)),
             pl.BlockSpec((1, q, xw), lambda bi, c: (bi, c, 0)),
             pl.BlockSpec((1, 8, xw), lambda bi, c: (bi, jnp.maximum(c * (q // 8) - 1, 0), 0)),
             pl.BlockSpec((1, q, LANES), lambda bi, c: (bi, c, 0)),
             pl.BlockSpec((1, heads, q), lambda bi, c: (bi, 0, c)),
             pl.BlockSpec((SSD_CONV, xw), lambda bi, c: (0, 0)),
             row(xw), row(LANES),
             pl.BlockSpec((heads, 1), lambda bi, c: (0, 0)),
             row(LANES),
             pl.BlockSpec((heads, 1), lambda bi, c: (0, 0)),
             row(width), row(width)]
    args = [z, xbc, xbc, dt_pad, dtt, conv_w, conv_b.reshape(1, xw), dtb, dt_bias.reshape(heads, 1),
            aneg, a_neg.reshape(heads, 1), dskip, norm_g.reshape(1, width)]
    scratch = [pltpu.VMEM((width // LANES, LANES, SSD_STATE), F32), pltpu.VMEM((q + 8, xw), F32)]
    return args, specs, scratch


def _rwkv_prep_kernel(x_ref, xp_ref, win_ref, mu_ref, w0_ref, a0_ref, kk_ref, ka_ref, w2_ref, a2_ref, g2_ref,
                      e_ref, et_ref, r_ref, lw_ref, k_ref, v_ref, an_ref, bn_ref, g_ref, ext_ref):
    tm = x_ref.shape[1]
    width = r_ref.shape[-1]
    i = pl.program_id(1)
    xe = jnp.concatenate([jnp.where(i > 0, xp_ref[0], 0.0), x_ref[0]], axis=0)
    for sl in _col_chunks(ext_ref.shape[1], 1664):
        ext_ref[:, sl] = _dot(xe, win_ref[:, sl])
    rw = ext_ref[8:8 + tm, :]
    shifted = ext_ref[7:7 + tm, :]
    s = rw + (shifted - rw) * mu_ref[...]
    r = s[:, :width]
    k = s[:, width:2 * width]
    v = s[:, 2 * width:3 * width]
    lo = s[:, 3 * width:3 * width + RWKV_DECAY_LORA + RWKV_ICLR_LORA]
    g_lo = s[:, 3 * width + RWKV_DECAY_LORA + RWKV_ICLR_LORA:]
    wv = w0_ref[...] + _dot(jnp.tanh(lo), w2_ref[...])
    w = -_softplus(-wv) - 0.5
    av = _sigmoid(a0_ref[...] + _dot(lo, a2_ref[...]))
    g = _dot(_sigmoid(g_lo), g2_ref[...])
    kkr = k * kk_ref[...]
    ss = _dot_exact_rhs(kkr * kkr, e_ref[...])
    inv = lax.rsqrt(jnp.maximum(ss, 1e-24))
    kk = kkr * _dot_exact_rhs(inv, et_ref[...])
    r_ref[0] = r.astype(r_ref.dtype)
    lw_ref[0] = -jnp.exp(w)
    k_ref[0] = (k * (1.0 + (av - 1.0) * ka_ref[...])).astype(k_ref.dtype)
    v_ref[0] = v.astype(v_ref.dtype)
    an_ref[0] = (-kk).astype(an_ref.dtype)
    bn_ref[0] = (kk * av).astype(bn_ref.dtype)
    g_ref[0] = g.astype(g_ref.dtype)


def _rwkv_chunk(rows, fill, r_ref, lw_ref, k_ref, v_ref, an_ref, bn_ref, g_ref, rk_ref, lg_ref, lb_ref,
                y_ref, state_ref):
    c = RWKV_CHUNK
    c2 = 2 * c
    width = r_ref.shape[-1]
    n_pairs = width // LANES

    row = lax.broadcasted_iota(jnp.int32, (c, c), 0)
    col = lax.broadcasted_iota(jnp.int32, (c, c), 1)
    tri = jnp.where(row >= col, 1.0, 0.0)
    row2 = lax.broadcasted_iota(jnp.int32, (c2, c2), 0)
    col2 = lax.broadcasted_iota(jnp.int32, (c2, c2), 1)
    same = (row2 // c) == (col2 // c)
    strict = jnp.where(same & ((row2 % c) > (col2 % c)), 1.0, 0.0)
    incl = jnp.where(same & ((row2 % c) >= (col2 % c)), 1.0, 0.0)
    eye = jnp.where(row2 == col2, 1.0, 0.0)
    lane = lax.broadcasted_iota(jnp.int32, (1, LANES), 1)
    m0 = jnp.where(lane < HEAD_DIM, 1.0, 0.0)
    m1 = 1.0 - m0

    def stack(x):
        return jnp.concatenate([x * m0, x * m1], axis=0)

    sls = [slice(p * LANES, (p + 1) * LANES) for p in range(n_pairs)]
    lhs4s, rhs4s, v_stks, bks, g_lasts = [], [], [], [], []
    for sl in sls:
        r = r_ref[0, rows, sl].astype(F32)
        lw = lw_ref[0, rows, sl]
        k = k_ref[0, rows, sl].astype(F32)
        b = bn_ref[0, rows, sl].astype(F32)
        cum = _dot_exact_lhs(tri, lw)
        cum_last = cum[c - 1:c, :]
        e_neg = jnp.exp(-cum)
        e_tail = jnp.exp(cum_last - cum)
        at = an_ref[0, rows, sl].astype(F32) * jnp.exp(cum - lw)
        rt = r * jnp.exp(cum)
        bt = (b * e_neg).astype(BF16)
        kt = (k * e_neg).astype(BF16)
        lhs4s.append(jnp.concatenate([stack(at), stack(rt)], axis=0).astype(BF16))
        rhs4s.append(jnp.concatenate([bt, bt, kt, kt], axis=0))
        v_stks.append(stack(v_ref[0, rows, sl].astype(F32)).astype(BF16))
        bks.append(jnp.concatenate([stack(b * e_tail), stack(k * e_tail)], axis=0).astype(BF16))
        g_lasts.append(jnp.exp(cum_last))
    gms = [_dot_nt(l, rh) for l, rh in zip(lhs4s, rhs4s)]
    pws = [gm[:c2, :c2] * strict for gm in gms]
    a_aks = [(gm[:c2, c2:] * strict).astype(BF16) for gm in gms]
    a_rbks = [jnp.concatenate([gm[c2:, :c2] * incl, gm[c2:, c2:] * incl], axis=1).astype(BF16) for gm in gms]
    akvs = [_dot(a_ak, v_stk) for a_ak, v_stk in zip(a_aks, v_stks)]
    fill()
    tinvs = [eye + pw for pw in pws]
    for _ in range(int(math.log2(c)) - 1):
        pws = [_dot(pw, pw) for pw in pws]
        tinvs = [tinv + _dot(tinv, pw) for tinv, pw in zip(tinvs, pws)]
        fill()
    sts = [state_ref[p] for p in range(n_pairs)]
    ahrhs = [_dot_nt(l, st) for l, st in zip(lhs4s, sts)]
    fill()
    u_stks = [_dot(tinv, ahrh[:c2] + akv) for tinv, ahrh, akv in zip(tinvs, ahrhs, akvs)]
    fill()
    uvs = [jnp.concatenate([u.astype(BF16), v_stk], axis=0) for u, v_stk in zip(u_stks, v_stks)]
    for p in range(n_pairs):
        state_ref[p] = sts[p] * g_lasts[p] + _dot_tn(uvs[p], bks[p])
    fill()
    y_stks = [ahrh[c2:] + _dot(a_rbk, uv) for ahrh, a_rbk, uv in zip(ahrhs, a_rbks, uvs)]
    ys = [y_stk[:c] + y_stk[c:] for y_stk in y_stks]
    lane_lo = lane < HEAD_DIM

    def head_sum(x):
        lo = jnp.sum(x * m0, axis=-1, keepdims=True)
        hi = jnp.sum(x * m1, axis=-1, keepdims=True)
        return jnp.where(lane_lo, lo, hi)

    bonus = [head_sum(r_ref[0, rows, sl].astype(F32) * k_ref[0, rows, sl].astype(F32) * rk_ref[:, sl]) for sl in sls]
    ycs = [y - head_sum(y) * (1.0 / HEAD_DIM) for y in ys]
    yvs = [head_sum(yc * yc) * (1.0 / HEAD_DIM) for yc in ycs]
    for p, sl in enumerate(sls):
        yn = ycs[p] * lax.rsqrt(yvs[p] + RWKV_LNX_EPS) * lg_ref[:, sl] + lb_ref[:, sl]
        out = (yn + bonus[p] * v_ref[0, rows, sl].astype(F32)) * g_ref[0, rows, sl].astype(F32)
        y_ref[0, rows, sl] = out.astype(y_ref.dtype)


def _rwkv_operands(x, w_in, mu, w0, w2, a0, a2, g2, k_k, k_a, r_k, lnx_g, lnx_b, *, tm):
    bsz, l, d = x.shape
    win = w_in.shape[1]
    width = w0.shape[0]
    w2p = jnp.concatenate([w2, jnp.zeros((RWKV_ICLR_LORA, width), F32)], axis=0).astype(BF16)
    a2p = jnp.concatenate([jnp.zeros((RWKV_DECAY_LORA, width), F32), a2], axis=0).astype(BF16)
    head_of = jnp.arange(width) // HEAD_DIM
    e = (head_of[:, None] == jnp.arange(LANES)[None, :]).astype(BF16)
    et = e.T
    vec = lambda x: x.reshape(1, -1)
    row = lambda n: pl.BlockSpec((1, n), lambda bi, i: (0, 0))
    full = lambda a: pl.BlockSpec(a.shape, lambda bi, i: (0, 0))
    tile = pl.BlockSpec((1, tm, width), lambda bi, i: (bi, i, 0))
    sds = lambda dt: jax.ShapeDtypeStruct((bsz, l, width), dt)
    g2b = g2.astype(BF16)
    r, lw, k, v, an, bn, g = pl.pallas_call(
        _rwkv_prep_kernel,
        grid=(bsz, l // tm),
        in_specs=[pl.BlockSpec((1, tm, d), lambda bi, i: (bi, i, 0)),
                  pl.BlockSpec((1, 8, d), lambda bi, i: (bi, jnp.maximum(i * (tm // 8) - 1, 0), 0)),
                  full(w_in), row(win), row(width), row(width), row(width), row(width),
                  full(w2p), full(a2p), full(g2b), full(e), full(et)],
        out_specs=[tile] * 7,
        out_shape=[sds(BF16), sds(F32)] + [sds(BF16)] * 5,
        scratch_shapes=[pltpu.VMEM((tm + 8, win), F32)],
        compiler_params=_params("parallel", "parallel"),
        name="rwkv_prep",
    )(x, x, w_in, vec(mu), vec(w0), vec(a0), vec(k_k), vec(k_a), w2p, a2p, g2b, e, et)
    ctile = pl.BlockSpec((1, SSD_CHUNK, width), lambda bi, ci: (bi, ci, 0))
    crow = pl.BlockSpec((1, width), lambda bi, ci: (0, 0))
    args = [r, lw, k, v, an, bn, g, vec(r_k), vec(lnx_g), vec(lnx_b)]
    scratch = [pltpu.VMEM((width // LANES, LANES, LANES), F32)]
    return args, [ctile] * 7 + [crow] * 3, scratch


def _mixers_kernel(n_ssd, n_rwkv, *refs):
    ssd_in = refs[:n_ssd]
    rwkv_in = refs[n_ssd:n_ssd + n_rwkv]
    y_ssd_ref, y_rwkv_ref, ssd_state, ssd_ext, rwkv_state = refs[n_ssd + n_rwkv:]
    c = pl.program_id(1)

    @pl.when(c == 0)
    def _():
        ssd_state[...] = jnp.zeros_like(ssd_state)
        rwkv_state[...] = jnp.zeros_like(rwkv_state)

    ssd_steps = _ssd_chunk(c, *ssd_in, y_ssd_ref, ssd_state, ssd_ext)
    fill = lambda: next(ssd_steps, None)
    for sub in range(SSD_CHUNK // RWKV_CHUNK):
        _rwkv_chunk(slice(sub * RWKV_CHUNK, (sub + 1) * RWKV_CHUNK), fill, *rwkv_in, y_rwkv_ref, rwkv_state)
    for _ in ssd_steps:
        pass


def _ssd_rwkv_mixers(ssd_ops, rwkv_ops, bsz, l, ssd_width, rwkv_width):
    ssd_args, ssd_specs, ssd_scratch = ssd_ops
    rwkv_args, rwkv_specs, rwkv_scratch = rwkv_ops
    q = SSD_CHUNK
    out_spec = lambda w: pl.BlockSpec((1, q, w), lambda bi, c: (bi, c, 0))
    return pl.pallas_call(
        functools.partial(_mixers_kernel, len(ssd_args), len(rwkv_args)),
        grid=(bsz, l // q),
        in_specs=ssd_specs + rwkv_specs,
        out_specs=[out_spec(ssd_width), out_spec(rwkv_width)],
        out_shape=[jax.ShapeDtypeStruct((bsz, l, ssd_width), BF16), jax.ShapeDtypeStruct((bsz, l, rwkv_width), BF16)],
        scratch_shapes=ssd_scratch + rwkv_scratch,
        compiler_params=_params("parallel", "arbitrary"),
        name="ssd_rwkv_mixers",
    )(*ssd_args, *rwkv_args)


def _moba_kernel(q_ref, k_ref, v_ref, o_ref, kb_ref, vt_ref, kmean_ref):
    blk = MOBA_BLOCK
    half = HEAD_DIM
    nb = k_ref.shape[0] // blk
    nbp = kmean_ref.shape[0]
    npair = q_ref.shape[1] // LANES
    heads = range(2 * npair)
    group = math.gcd(nb, MOBA_GROUP)
    qi = pl.program_id(2)

    @pl.when(qi == 0)
    def _():
        lane = lax.broadcasted_iota(jnp.int32, (blk, LANES), 1)
        rowp = lax.broadcasted_iota(jnp.int32, (LANES, blk), 0)
        if nbp > nb:
            kmean_ref[...] = jnp.zeros_like(kmean_ref)
        for n in range(nb):
            rows = slice(n * blk, (n + 1) * blk)
            kmean_ref[n:n + 1, :] = jnp.mean(k_ref[rows, :], axis=0, keepdims=True)
            for pp in range(npair):
                cols = slice(pp * LANES, (pp + 1) * LANES)
                kn = k_ref[rows, cols]
                kb_ref[2 * pp, n] = jnp.where(lane < half, kn, jnp.where(lane == half + n, 1.0, 0.0)).astype(BF16)
                kb_ref[2 * pp + 1, n] = jnp.where(lane >= half, kn, jnp.where(lane == n, 1.0, 0.0)).astype(BF16)
                vtn = v_ref[rows, cols].T
                vt_ref[2 * pp, n] = jnp.where(rowp < half, vtn, jnp.where(rowp == half, 1.0, 0.0)).astype(BF16)
                vt_ref[2 * pp + 1, n] = jnp.where(rowp >= half, vtn, jnp.where(rowp == 0, 1.0, 0.0)).astype(BF16)

    rown = lax.broadcasted_iota(jnp.int32, (nbp, blk), 0)
    lane_k = lax.broadcasted_iota(jnp.int32, (nbp, LANES), 1)
    zeros = jnp.zeros((half, blk), F32)
    qts = [q_ref[:, pp * LANES:(pp + 1) * LANES].T for pp in range(npair)]
    gates, qhs = [], []
    for hh in heads:
        pp, h = divmod(hh, 2)
        km = kmean_ref[:, pp * LANES:(pp + 1) * LANES]
        kmh = jnp.where((lane_k < half) if h == 0 else (lane_k >= half), km, 0.0)
        gates.append(jnp.dot(kmh, qts[pp], preferred_element_type=F32, precision=lax.Precision.HIGHEST))
        qhs.append(qts[pp][h * half:(h + 1) * half] * (half ** -0.5 * LOG2_E))
    q_own = [jnp.concatenate([qhs[hh], zeros] if hh % 2 == 0 else [zeros, qhs[hh]], axis=0).astype(BF16)
             for hh in heads]
    s_own = [jnp.dot(kb_ref[h, qi], q_own[h], preferred_element_type=F32) for h in heads]
    q_past = []
    for hh in heads:
        gate = jnp.where(rown < qi, gates[hh], -jnp.inf)
        bias = jnp.full((nbp, blk), NEG_BIG, F32)
        for _ in range(MOBA_TOPK):
            mx = jnp.max(gate, axis=0, keepdims=True)
            first = jnp.min(jnp.where(gate == mx, rown, nbp), axis=0, keepdims=True)
            pick = (rown == first) & (mx > -jnp.inf)
            bias = jnp.where(pick, 0.0, bias)
            gate = jnp.where(pick, -jnp.inf, gate)
        aug = jnp.concatenate([bias, jnp.zeros((half - nbp, blk), F32)], axis=0)
        q_past.append(jnp.concatenate([qhs[hh], aug] if hh % 2 == 0 else [aug, qhs[hh]], axis=0).astype(BF16))

    causal = (lax.broadcasted_iota(jnp.int32, (blk, blk), 0) <= lax.broadcasted_iota(jnp.int32, (blk, blk), 1))
    ms, ps = [], []
    for h in heads:
        s = jnp.where(causal, s_own[h], NEG_BIG)
        ms.append(jnp.max(s, axis=0, keepdims=True))
        ps.append(jnp.exp2(s - ms[h]).astype(BF16))
    carry = []
    for h in heads:
        carry += [ms[h], jnp.dot(vt_ref[h, qi], ps[h], preferred_element_type=F32)]

    def body(gi, carry):
        sss = [[jnp.dot(kb_ref[h, gi * group + g], q_past[h], preferred_element_type=F32)
                for g in range(group)] for h in heads]
        m_news = []
        for h in heads:
            m_new = carry[2 * h]
            for s in sss[h]:
                m_new = jnp.maximum(m_new, jnp.max(s, axis=0, keepdims=True))
            m_news.append(m_new)
        pcats = [jnp.concatenate([jnp.exp2(s - m_news[h]).astype(BF16) for s in sss[h]], axis=0)
                 for h in heads]
        out = []
        for h in heads:
            vts = jnp.concatenate([vt_ref[h, gi * group + g] for g in range(group)], axis=1)
            alpha = jnp.exp2(carry[2 * h] - m_news[h])
            out += [m_news[h], alpha * carry[2 * h + 1] + jnp.dot(vts, pcats[h], preferred_element_type=F32)]
        return tuple(out)

    final = lax.fori_loop(0, (qi + group - 1) // group, body, tuple(carry))
    rowq = lax.broadcasted_iota(jnp.int32, (LANES, blk), 0)
    for pp in range(npair):
        acc0, acc1 = final[4 * pp + 1], final[4 * pp + 3]
        out_t = jnp.where(rowq < half, acc0 / acc0[half:half + 1], acc1 / acc1[0:1])
        o_ref[:, pp * LANES:(pp + 1) * LANES] = out_t.T.astype(o_ref.dtype)


def _moba_attention(qkv, bsz, s, heads):
    blk = MOBA_BLOCK
    assert s % blk == 0 and (heads * HEAD_DIM) % LANES == 0
    nb = s // blk
    assert nb <= HEAD_DIM
    nbp = -(-nb // 8) * 8
    pairs = heads * HEAD_DIM // LANES
    pps = math.gcd(pairs, MOBA_PAIRS_PER_STEP)
    cw = pps * LANES
    steps = pairs // pps
    return pl.pallas_call(
        _moba_kernel,
        grid=(bsz, steps, nb),
        in_specs=[pl.BlockSpec((blk, cw), lambda b, p, i: (b * nb + i, p)),
                  pl.BlockSpec((s, cw), lambda b, p, i: (b, steps + p)),
                  pl.BlockSpec((s, cw), lambda b, p, i: (b, 2 * steps + p))],
        out_specs=pl.BlockSpec((blk, cw), lambda b, p, i: (b * nb + i, p)),
        out_shape=jax.ShapeDtypeStruct((bsz * s, heads * HEAD_DIM), BF16),
        scratch_shapes=[pltpu.VMEM((2 * pps, nb, blk, LANES), BF16), pltpu.VMEM((2 * pps, nb, LANES, blk), BF16),
                        pltpu.VMEM((nbp, cw), F32)],
        compiler_params=_params("parallel", "parallel", "arbitrary"),
        name="moba_attention",
    )(qkv, qkv, qkv)


def _row_tile(m):
    for t in (512, 256, 128, 64, 32, 16, 8):
        if m % t == 0:
            return t
    raise ValueError(f"row count {m} is not a multiple of 8")


def _col_tile(n, cap=2048):
    best = None
    for t in range(LANES, min(n, cap) + 1, LANES):
        if n % t == 0:
            best = t
    if best is None:
        raise ValueError(f"column count {n} is not a multiple of {LANES}")
    return best


def kernel(x, mem, even_w_in, ssd_conv_w, ssd_conv_b, ssd_dt_bias, ssd_a_log, ssd_d, ssd_norm_g, rwkv_mu, rwkv_w0, rwkv_w2, rwkv_a0, rwkv_a2, rwkv_g2, rwkv_k_k, rwkv_k_a, rwkv_r_k, rwkv_lnx_g, rwkv_lnx_b, even_w_out, odd_w_qkv, odd_w_out, ln_mix_g, ln_mix_b, xa_wq, xa_wkv, xa_wo, ln_xa_g, ln_xa_b, ffn_w13, ffn_w2, ln_ffn_g, ln_ffn_b):
    bsz, s, d = x.shape
    m = bsz * s
    tm = _row_tile(s)
    ssd_width = ssd_norm_g.shape[-1]
    ssd_heads = ssd_dt_bias.shape[-1]
    ssd_xbc = ssd_conv_b.shape[-1]
    ssd_in = ssd_width + ssd_xbc + ssd_heads
    rwkv_width = rwkv_w0.shape[-1]
    ffn_hidden = ffn_w2.shape[1]
    mem2 = mem.reshape(bsz * mem.shape[1], d)
    x2 = x.reshape(m, d)
    for layer in range(DEPTH):
        j = layer // 2
        if layer % 2 == 0:
            w_in = even_w_in[j].astype(BF16)
            w_z = w_in[:, :ssd_width]
            w_xbc = w_in[:, ssd_width:ssd_width + ssd_xbc]
            w_dt = jnp.pad(w_in[:, ssd_width + ssd_xbc:ssd_in], ((0, 0), (0, LANES - ssd_heads)))
            w_rw = w_in[:, ssd_in:]
            z, xbc, dt_pad = _matmul(x2, [w_z, w_xbc, w_dt], tm=tm)
            ssd_ops = _ssd_operands(z.reshape(bsz, s, -1), xbc.reshape(bsz, s, -1), dt_pad.reshape(bsz, s, -1),
                                    ssd_conv_w[j], ssd_conv_b[j], ssd_dt_bias[j], ssd_a_log[j], ssd_d[j],
                                    ssd_norm_g[j])
            rwkv_ops = _rwkv_operands(x2.reshape(bsz, s, d), w_rw, rwkv_mu[j], rwkv_w0[j], rwkv_w2[j],
                                      rwkv_a0[j], rwkv_a2[j], rwkv_g2[j], rwkv_k_k[j], rwkv_k_a[j], rwkv_r_k[j],
                                      rwkv_lnx_g[j], rwkv_lnx_b[j], tm=min(tm, 256))
            y_ssd, y_rwkv = _ssd_rwkv_mixers(ssd_ops, rwkv_ops, bsz, s, ssd_width, rwkv_width)
            w_out = even_w_out[j].astype(BF16)
            x2 = _matmul_residual_ln([y_ssd.reshape(m, -1), y_rwkv.reshape(m, -1)],
                                     [w_out[:ssd_width], w_out[ssd_width:]], x2,
                                     ln_mix_g[layer], ln_mix_b[layer], tm=tm)
        else:
            heads = d // HEAD_DIM
            qkv, = _matmul(x2, [odd_w_qkv[j].astype(BF16)], tm=tm)
            attn = _moba_attention(qkv, bsz, s, heads)
            x2 = _matmul_residual_ln([attn], [odd_w_out[j].astype(BF16)], x2,
                                     ln_mix_g[layer], ln_mix_b[layer], tm=tm)
        kv, = _matmul(mem2, [xa_wkv[layer].astype(BF16)], tm=_row_tile(mem2.shape[0]))
        x3 = _cross_attention_ln(x2.reshape(bsz, s, d), kv.reshape(bsz, -1, 2 * d),
                                 xa_wq[layer].astype(BF16), xa_wo[layer].astype(BF16),
                                 ln_xa_g[layer], ln_xa_b[layer], tm=tm)
        x2 = x3.reshape(m, d)
        w13 = ffn_w13[layer].astype(BF16)
        h = _swiglu_up(x2, w13[:, :ffn_hidden], w13[:, ffn_hidden:], tm=tm)
        x2 = _matmul_residual_ln([h], [ffn_w2[layer].astype(BF16)], x2,
                                 ln_ffn_g[layer], ln_ffn_b[layer], tm=tm)
    return x2.reshape(bsz, s, d)
```

```python
import functools
import math

import jax
import jax.numpy as jnp
from jax import lax
from jax.experimental import pallas as pl
from jax.experimental.pallas import tpu as pltpu

F32 = jnp.float32
BF16 = jnp.bfloat16

HEAD_DIM = 64
LANES = 128
SSD_GROUPS = 2
SSD_STATE = 128
SSD_CONV = 4
SSD_CHUNK = 128
RWKV_DECAY_LORA = 64
RWKV_ICLR_LORA = 64
RWKV_GATE_LORA = 128
RWKV_CHUNK = 64
MOBA_BLOCK = 256
MOBA_TOPK = 3
MOBA_GROUP = 4
MOBA_LAG_LIMIT = 64.0
MOBA_PAIRS_PER_STEP = 2
XATTN_HEADS = 4
DEPTH = 2
DEEPNORM_ALPHA = (2 * DEPTH) ** 0.25
LN_EPS = 1e-5
RMS_EPS = 1e-5
RWKV_LNX_EPS = 64e-5
NEG_BIG = -1e30
LOG2_E = math.log2(math.e)
VMEM_LIMIT = 56 * 1024 * 1024


def _params(*sem):
    return pltpu.CompilerParams(dimension_semantics=sem, vmem_limit_bytes=VMEM_LIMIT)


def _dot(a, b):
    return jnp.dot(a.astype(BF16), b.astype(BF16), preferred_element_type=F32)


def _dot_nt(a, b):
    return lax.dot_general(a.astype(BF16), b.astype(BF16), (((1,), (1,)), ((), ())),
                           preferred_element_type=F32)


def _dot_tn(a, b):
    return lax.dot_general(a.astype(BF16), b.astype(BF16), (((0,), (0,)), ((), ())),
                           preferred_element_type=F32)


def _split3(x):
    hi = x.astype(BF16)
    r1 = x - hi.astype(F32)
    mid = r1.astype(BF16)
    lo = (r1 - mid.astype(F32)).astype(BF16)
    return hi, mid, lo


def _dot_exact_lhs(m, x):
    hi, mid, lo = _split3(x)
    m = m.astype(BF16)
    return (jnp.dot(m, hi, preferred_element_type=F32) + jnp.dot(m, mid, preferred_element_type=F32)
            + jnp.dot(m, lo, preferred_element_type=F32))


def _dot_exact_rhs(x, m):
    hi, mid, lo = _split3(x)
    m = m.astype(BF16)
    return (jnp.dot(hi, m, preferred_element_type=F32) + jnp.dot(mid, m, preferred_element_type=F32)
            + jnp.dot(lo, m, preferred_element_type=F32))


def _sigmoid(x):
    return 1.0 / (1.0 + jnp.exp(-x))


def _softplus(x):
    return jnp.maximum(x, 0.0) + jnp.log1p(jnp.exp(-jnp.abs(x)))


def _layer_norm(v, g, b):
    mu = jnp.mean(v, axis=-1, keepdims=True)
    c = v - mu
    var = jnp.mean(c * c, axis=-1, keepdims=True)
    return c * lax.rsqrt(var + LN_EPS) * g + b


def _col_chunks(n, cap=1536):
    width = _col_tile(n, cap)
    return [slice(j, j + width) for j in range(0, n, width)]


def _mm_kernel(n_out, x_ref, *refs):
    x = x_ref[...].astype(BF16)
    for w_ref, o_ref in zip(refs[:n_out], refs[n_out:]):
        for sl in _col_chunks(w_ref.shape[1]):
            o_ref[:, sl] = jnp.dot(x, w_ref[:, sl], preferred_element_type=F32).astype(o_ref.dtype)


def _matmul(x, ws, *, tm, out_dtype=F32):
    m, k = x.shape
    assert m % tm == 0
    return pl.pallas_call(
        functools.partial(_mm_kernel, len(ws)),
        grid=(m // tm,),
        in_specs=[pl.BlockSpec((tm, k), lambda i: (i, 0))] + [pl.BlockSpec(w.shape, lambda i: (0, 0)) for w in ws],
        out_specs=[pl.BlockSpec((tm, w.shape[1]), lambda i: (i, 0)) for w in ws],
        out_shape=[jax.ShapeDtypeStruct((m, w.shape[1]), out_dtype) for w in ws],
        compiler_params=_params("parallel"),
        name="matmul",
    )(x, *ws)


def _mm_res_ln_kernel(n_in, *refs):
    hs = refs[:n_in]
    ws = refs[n_in:2 * n_in]
    res_ref, g_ref, b_ref, o_ref = refs[2 * n_in:]
    acc = _dot(hs[0][...], ws[0][...])
    for h_ref, w_ref in zip(hs[1:], ws[1:]):
        acc = acc + _dot(h_ref[...], w_ref[...])
    o_ref[...] = _layer_norm(DEEPNORM_ALPHA * res_ref[...] + acc, g_ref[...], b_ref[...])


def _matmul_residual_ln(hs, w, res, g, b, *, tm):
    m, d = res.shape
    n_in = len(hs)
    kw = w.shape[0] // n_in
    assert all(h.shape[1] == kw for h in hs)
    in_specs = ([pl.BlockSpec((tm, kw), lambda i: (i, 0)) for _ in hs]
                + [pl.BlockSpec((kw, d), lambda i, j=j: (j, 0)) for j in range(n_in)]
                + [pl.BlockSpec((tm, d), lambda i: (i, 0)),
                   pl.BlockSpec((1, d), lambda i: (0, 0)),
                   pl.BlockSpec((1, d), lambda i: (0, 0))])
    return pl.pallas_call(
        functools.partial(_mm_res_ln_kernel, n_in),
        grid=(m // tm,),
        in_specs=in_specs,
        out_specs=pl.BlockSpec((tm, d), lambda i: (i, 0)),
        out_shape=jax.ShapeDtypeStruct((m, d), F32),
        compiler_params=_params("parallel"),
        name="matmul_residual_ln",
    )(*hs, *([w] * n_in), res, g.reshape(1, d), b.reshape(1, d))


def _swiglu_kernel(x_ref, w13_ref, o_ref):
    x = x_ref[...].astype(BF16)
    n = o_ref.shape[1]
    for sl in _col_chunks(n):
        gate = jnp.dot(x, w13_ref[:, sl], preferred_element_type=F32)
        up = jnp.dot(x, w13_ref[:, slice(n + sl.start, n + sl.stop)], preferred_element_type=F32)
        o_ref[:, sl] = (gate * _sigmoid(gate) * up).astype(o_ref.dtype)


def _swiglu_up(x, w13, *, tm):
    m, k = x.shape
    n = w13.shape[1] // 2
    return pl.pallas_call(
        _swiglu_kernel,
        grid=(m // tm,),
        in_specs=[pl.BlockSpec((tm, k), lambda i: (i, 0)),
                  pl.BlockSpec((k, 2 * n), lambda i: (0, 0))],
        out_specs=pl.BlockSpec((tm, n), lambda i: (i, 0)),
        out_shape=jax.ShapeDtypeStruct((m, n), BF16),
        compiler_params=_params("parallel"),
        name="swiglu_up",
    )(x, w13)


def _xattn_kernel(x_ref, kv_ref, wq_ref, wo_ref, g_ref, b_ref, o_ref):
    x = x_ref[0]
    d = x.shape[-1]
    hd = d // XATTN_HEADS
    q = _dot(x, wq_ref[...])
    kv = kv_ref[0]
    heads = range(XATTN_HEADS)
    ss = [_dot_nt(q[:, h * hd:(h + 1) * hd], kv[:, h * hd:(h + 1) * hd]) * (hd ** -0.5) for h in heads]
    ps = []
    for s in ss:
        p = jnp.exp(s - jnp.max(s, axis=-1, keepdims=True))
        ps.append(p / jnp.sum(p, axis=-1, keepdims=True))
    o = jnp.concatenate([_dot(ps[h], kv[:, d + h * hd:d + (h + 1) * hd]) for h in heads], axis=-1)
    xa = _dot(o, wo_ref[...])
    o_ref[0] = _layer_norm(DEEPNORM_ALPHA * x + xa, g_ref[...], b_ref[...])


def _cross_attention_ln(x, kv, wq, wo, g, b, *, tm):
    bsz, s, d = x.shape
    m = kv.shape[1]
    return pl.pallas_call(
        _xattn_kernel,
        grid=(bsz, s // tm),
        in_specs=[pl.BlockSpec((1, tm, d), lambda bi, i: (bi, i, 0)),
                  pl.BlockSpec((1, m, 2 * d), lambda bi, i: (bi, 0, 0)),
                  pl.BlockSpec((d, d), lambda bi, i: (0, 0)),
                  pl.BlockSpec((d, d), lambda bi, i: (0, 0)),
                  pl.BlockSpec((1, d), lambda bi, i: (0, 0)),
                  pl.BlockSpec((1, d), lambda bi, i: (0, 0))],
        out_specs=pl.BlockSpec((1, tm, d), lambda bi, i: (bi, i, 0)),
        out_shape=jax.ShapeDtypeStruct((bsz, s, d), F32),
        compiler_params=_params("parallel", "parallel"),
        name="cross_attention_ln",
    )(x, kv, wq, wo, g.reshape(1, d), b.reshape(1, d))


def _ssd_chunk(c, z_ref, xbc_ref, xbcp_ref, dt_ref, dtt_ref, cw_ref, cb_ref, dtb_ref, dtbt_ref,
               aneg_ref, anegt_ref, dskip_ref, ng_ref, y_ref, state_ref, ext_ref):
    q = SSD_CHUNK
    width = z_ref.shape[-1]
    n_pairs = width // LANES

    ext_ref[0:8, :] = jnp.where(c > 0, xbcp_ref[0], 0.0)
    ext_ref[8:8 + q, :] = xbc_ref[0]
    xcs = []
    for sl in _col_chunks(ext_ref.shape[1], 256):
        conv = cb_ref[:, sl] + cw_ref[SSD_CONV - 1:SSD_CONV, sl] * ext_ref[8:8 + q, sl]
        for k in range(SSD_CONV - 1):
            off = 8 - (SSD_CONV - 1) + k
            conv = conv + cw_ref[k:k + 1, sl] * ext_ref[off:off + q, sl]
        xcs.append(conv * _sigmoid(conv))
        yield
    xc = jnp.concatenate(xcs, axis=1)
    xs = xc[:, :width]
    gn = SSD_GROUPS * SSD_STATE
    bm = xc[:, width:width + gn]
    cm = xc[:, width + gn:width + 2 * gn]

    dt = _softplus(dt_ref[0] + dtb_ref[...])
    a = dt * aneg_ref[...]
    dtt = _softplus(dtt_ref[0] + dtbt_ref[...])
    at = dtt * anegt_ref[...]
    row = lax.broadcasted_iota(jnp.int32, (q, q), 0)
    col = lax.broadcasted_iota(jnp.int32, (q, q), 1)
    causal = row >= col
    tri = jnp.where(causal, 1.0, 0.0)
    a_cum = _dot_exact_lhs(tri, a)
    a_cumt = _dot_exact_rhs(at, jnp.where(row <= col, 1.0, 0.0))
    yield

    lane = lax.broadcasted_iota(jnp.int32, (1, LANES), 1)
    lane_lo = lane < HEAD_DIM
    rowp = lax.broadcasted_iota(jnp.int32, (LANES, 1), 0)
    pairs_per_group = n_pairs // SSD_GROUPS
    ys = []
    for p in range(n_pairs):
        g = p // pairs_per_group
        h0, h1 = 2 * p, 2 * p + 1
        bg = bm[:, g * SSD_STATE:(g + 1) * SSD_STATE]
        cg = cm[:, g * SSD_STATE:(g + 1) * SSD_STATE]
        cb = _dot_nt(cg, bg)
        xs_p = xs[:, p * LANES:(p + 1) * LANES]
        dt_p = jnp.where(lane_lo, dt[:, h0:h0 + 1], dt[:, h1:h1 + 1])
        acum_p = jnp.where(lane_lo, a_cum[:, h0:h0 + 1], a_cum[:, h1:h1 + 1])
        xdt = xs_p * dt_p
        ms = []
        for h in (h0, h1):
            seg = a_cum[:, h:h + 1] - a_cumt[h:h + 1, :]
            ms.append(cb * jnp.exp(jnp.where(causal, seg, NEG_BIG)))
        m2 = jnp.concatenate(ms, axis=1)
        x2 = jnp.concatenate([jnp.where(lane_lo, xdt, 0.0), jnp.where(lane_lo, 0.0, xdt)], axis=0)
        y_diag = _dot(m2, x2)
        prev = state_ref[p]
        y_off = _dot_nt(cg, prev) * jnp.exp(acum_p)
        a_last = jnp.where(lane_lo, a_cum[q - 1:q, h0:h0 + 1], a_cum[q - 1:q, h1:h1 + 1])
        xdw = xdt * jnp.exp(a_last - acum_p)
        st = _dot_tn(xdw, bg)
        cd = jnp.where(rowp < HEAD_DIM, jnp.exp(a_cumt[h0:h0 + 1, q - 1:q]),
                       jnp.exp(a_cumt[h1:h1 + 1, q - 1:q]))
        state_ref[p] = prev * cd + st
        d_p = dskip_ref[:, p * LANES:(p + 1) * LANES]
        ys.append(y_diag + y_off + d_p * xs_p)
        yield
    y = jnp.concatenate(ys, axis=1)
    z = z_ref[0]
    y = y * (z * _sigmoid(z))
    gw = width // SSD_GROUPS
    outs = []
    for g in range(SSD_GROUPS):
        yg = y[:, g * gw:(g + 1) * gw]
        outs.append(yg * lax.rsqrt(jnp.mean(yg * yg, axis=-1, keepdims=True) + RMS_EPS))
    y_ref[0] = (jnp.concatenate(outs, axis=1) * ng_ref[...]).astype(y_ref.dtype)


def _ssd_operands(z, xbc, dt_pad, conv_w, conv_b, dt_bias, a_log, d_skip, norm_g):
    bsz, l, width = z.shape
    heads = width // HEAD_DIM
    xw = xbc.shape[-1]
    q = SSD_CHUNK
    dtt = jnp.swapaxes(dt_pad[:, :, :heads], 1, 2)
    pad = LANES - heads
    dtb = jnp.pad(dt_bias, (0, pad)).reshape(1, LANES)
    a_neg = -jnp.exp(a_log.astype(F32))
    aneg = jnp.pad(a_neg, (0, pad)).reshape(1, LANES)
    dskip = jnp.repeat(d_skip, HEAD_DIM).reshape(1, width)
    row = lambda n: pl.BlockSpec((1, n), lambda bi, c: (0, 0))
    specs = [pl.BlockSpec((1, q, width), lambda bi, c: (bi, c, 0)),
             pl.BlockSpec((1, q, xw), lambda bi, c: (bi, c, 0)),
             pl.BlockSpec((1, 8, xw), lambda bi, c: (bi, jnp.maximum(c * (q // 8) - 1, 0), 0)),
             pl.BlockSpec((1, q, LANES), lambda bi, c: (bi, c, 0)),
             pl.BlockSpec((1, heads, q), lambda bi, c: (bi, 0, c)),
             pl.BlockSpec((SSD_CONV, xw), lambda bi, c: (0, 0)),
             row(xw), row(LANES),
             pl.BlockSpec((heads, 1), lambda bi, c: (0, 0)),
             row(LANES),
             pl.BlockSpec((heads, 1), lambda bi, c: (0, 0)),
             row(width), row(width)]
    args = [z, xbc, xbc, dt_pad, dtt, conv_w, conv_b.reshape(1, xw), dtb, dt_bias.reshape(heads, 1),
            aneg, a_neg.reshape(heads, 1), dskip, norm_g.reshape(1, width)]
    scratch = [pltpu.VMEM((width // LANES, LANES, SSD_STATE), F32), pltpu.VMEM((q + 8, xw), F32)]
    return args, specs, scratch


def _rwkv_prep_kernel(x_ref, xp_ref, win_ref, mu_ref, w0_ref, a0_ref, kk_ref, ka_ref, w2_ref, a2_ref, g2_ref,
                      e_ref, et_ref, r_ref, lw_ref, k_ref, v_ref, an_ref, bn_ref, g_ref, ext_ref):
    tm = x_ref.shape[1]
    width = r_ref.shape[-1]
    i = pl.program_id(1)
    xe = jnp.concatenate([jnp.where(i > 0, xp_ref[0], 0.0), x_ref[0]], axis=0)
    for sl in _col_chunks(ext_ref.shape[1], 1664):
        ext_ref[:, sl] = _dot(xe, win_ref[:, sl])
    rw = ext_ref[8:8 + tm, :]
    shifted = ext_ref[7:7 + tm, :]
    s = rw + (shifted - rw) * mu_ref[...]
    r = s[:, :width]
    k = s[:, width:2 * width]
    v = s[:, 2 * width:3 * width]
    lo = s[:, 3 * width:3 * width + RWKV_DECAY_LORA + RWKV_ICLR_LORA]
    g_lo = s[:, 3 * width + RWKV_DECAY_LORA + RWKV_ICLR_LORA:]
    wv = w0_ref[...] + _dot(jnp.tanh(lo), w2_ref[...])
    w = -_softplus(-wv) - 0.5
    av = _sigmoid(a0_ref[...] + _dot(lo, a2_ref[...]))
    g = _dot(_sigmoid(g_lo), g2_ref[...])
    kkr = k * kk_ref[...]
    ss = _dot_exact_rhs(kkr * kkr, e_ref[...])
    inv = lax.rsqrt(jnp.maximum(ss, 1e-24))
    kk = kkr * _dot_exact_rhs(inv, et_ref[...])
    r_ref[0] = r.astype(r_ref.dtype)
    lw_ref[0] = -jnp.exp(w)
    k_ref[0] = (k * (1.0 + (av - 1.0) * ka_ref[...])).astype(k_ref.dtype)
    v_ref[0] = v.astype(v_ref.dtype)
    an_ref[0] = (-kk).astype(an_ref.dtype)
    bn_ref[0] = (kk * av).astype(bn_ref.dtype)
    g_ref[0] = g.astype(g_ref.dtype)


def _rwkv_chunk(rows, fill, r_ref, lw_ref, k_ref, v_ref, an_ref, bn_ref, g_ref, rk_ref, lg_ref, lb_ref,
                y_ref, state_ref):
    c = RWKV_CHUNK
    c2 = 2 * c
    width = r_ref.shape[-1]
    n_pairs = width // LANES

    row = lax.broadcasted_iota(jnp.int32, (c, c), 0)
    col = lax.broadcasted_iota(jnp.int32, (c, c), 1)
    tri = jnp.where(row >= col, 1.0, 0.0)
    row2 = lax.broadcasted_iota(jnp.int32, (c2, c2), 0)
    col2 = lax.broadcasted_iota(jnp.int32, (c2, c2), 1)
    same = (row2 // c) == (col2 // c)
    strict = jnp.where(same & ((row2 % c) > (col2 % c)), 1.0, 0.0)
    incl = jnp.where(same & ((row2 % c) >= (col2 % c)), 1.0, 0.0)
    eye = jnp.where(row2 == col2, 1.0, 0.0)
    lane = lax.broadcasted_iota(jnp.int32, (1, LANES), 1)
    m0 = jnp.where(lane < HEAD_DIM, 1.0, 0.0)
    m1 = 1.0 - m0

    def stack(x):
        return jnp.concatenate([x * m0, x * m1], axis=0)

    sls = [slice(p * LANES, (p + 1) * LANES) for p in range(n_pairs)]
    lhs4s, rhs4s, v_stks, bks, g_lasts = [], [], [], [], []
    for sl in sls:
        r = r_ref[0, rows, sl].astype(F32)
        lw = lw_ref[0, rows, sl]
        k = k_ref[0, rows, sl].astype(F32)
        b = bn_ref[0, rows, sl].astype(F32)
        cum = _dot_exact_lhs(tri, lw)
        cum_last = cum[c - 1:c, :]
        e_neg = jnp.exp(-cum)
        e_tail = jnp.exp(cum_last - cum)
        at = an_ref[0, rows, sl].astype(F32) * jnp.exp(cum - lw)
        rt = r * jnp.exp(cum)
        bt = (b * e_neg).astype(BF16)
        kt = (k * e_neg).astype(BF16)
        lhs4s.append(jnp.concatenate([stack(at), stack(rt)], axis=0).astype(BF16))
        rhs4s.append(jnp.concatenate([bt, bt, kt, kt], axis=0))
        v_stks.append(stack(v_ref[0, rows, sl].astype(F32)).astype(BF16))
        bks.append(jnp.concatenate([stack(b * e_tail), stack(k * e_tail)], axis=0).astype(BF16))
        g_lasts.append(jnp.exp(cum_last))
    gms = [_dot_nt(l, rh) for l, rh in zip(lhs4s, rhs4s)]
    pws = [gm[:c2, :c2] * strict for gm in gms]
    a_aks = [(gm[:c2, c2:] * strict).astype(BF16) for gm in gms]
    a_rbks = [jnp.concatenate([gm[c2:, :c2] * incl, gm[c2:, c2:] * incl], axis=1).astype(BF16) for gm in gms]
    akvs = [_dot(a_ak, v_stk) for a_ak, v_stk in zip(a_aks, v_stks)]
    fill()
    tinvs = [eye + pw for pw in pws]
    for _ in range(int(math.log2(c)) - 1):
        pws = [_dot(pw, pw) for pw in pws]
        tinvs = [tinv + _dot(tinv, pw) for tinv, pw in zip(tinvs, pws)]
        fill()
    sts = [state_ref[p] for p in range(n_pairs)]
    ahrhs = [_dot_nt(l, st) for l, st in zip(lhs4s, sts)]
    fill()
    u_stks = [_dot(tinv, ahrh[:c2] + akv) for tinv, ahrh, akv in zip(tinvs, ahrhs, akvs)]
    fill()
    uvs = [jnp.concatenate([u.astype(BF16), v_stk], axis=0) for u, v_stk in zip(u_stks, v_stks)]
    for p in range(n_pairs):
        state_ref[p] = sts[p] * g_lasts[p] + _dot_tn(uvs[p], bks[p])
    fill()
    y_stks = [ahrh[c2:] + _dot(a_rbk, uv) for ahrh, a_rbk, uv in zip(ahrhs, a_rbks, uvs)]
    ys = [y_stk[:c] + y_stk[c:] for y_stk in y_stks]
    lane_lo = lane < HEAD_DIM

    def head_sum(x):
        lo = jnp.sum(x * m0, axis=-1, keepdims=True)
        hi = jnp.sum(x * m1, axis=-1, keepdims=True)
        return jnp.where(lane_lo, lo, hi)

    bonus = [head_sum(r_ref[0, rows, sl].astype(F32) * k_ref[0, rows, sl].astype(F32) * rk_ref[:, sl]) for sl in sls]
    ycs = [y - head_sum(y) * (1.0 / HEAD_DIM) for y in ys]
    yvs = [head_sum(yc * yc) * (1.0 / HEAD_DIM) for yc in ycs]
    for p, sl in enumerate(sls):
        yn = ycs[p] * lax.rsqrt(yvs[p] + RWKV_LNX_EPS) * lg_ref[:, sl] + lb_ref[:, sl]
        out = (yn + bonus[p] * v_ref[0, rows, sl].astype(F32)) * g_ref[0, rows, sl].astype(F32)
        y_ref[0, rows, sl] = out.astype(y_ref.dtype)


def _rwkv_operands(x, w_in, mu, w0, w2, a0, a2, g2, k_k, k_a, r_k, lnx_g, lnx_b, *, tm):
    bsz, l, d = x.shape
    win = w_in.shape[1]
    width = w0.shape[0]
    w2p = jnp.concatenate([w2, jnp.zeros((RWKV_ICLR_LORA, width), F32)], axis=0).astype(BF16)
    a2p = jnp.concatenate([jnp.zeros((RWKV_DECAY_LORA, width), F32), a2], axis=0).astype(BF16)
    head_of = jnp.arange(width) // HEAD_DIM
    e = (head_of[:, None] == jnp.arange(LANES)[None, :]).astype(BF16)
    et = e.T
    vec = lambda x: x.reshape(1, -1)
    row = lambda n: pl.BlockSpec((1, n), lambda bi, i: (0, 0))
    full = lambda a: pl.BlockSpec(a.shape, lambda bi, i: (0, 0))
    tile = pl.BlockSpec((1, tm, width), lambda bi, i: (bi, i, 0))
    sds = lambda dt: jax.ShapeDtypeStruct((bsz, l, width), dt)
    g2b = g2.astype(BF16)
    r, lw, k, v, an, bn, g = pl.pallas_call(
        _rwkv_prep_kernel,
        grid=(bsz, l // tm),
        in_specs=[pl.BlockSpec((1, tm, d), lambda bi, i: (bi, i, 0)),
                  pl.BlockSpec((1, 8, d), lambda bi, i: (bi, jnp.maximum(i * (tm // 8) - 1, 0), 0)),
                  full(w_in), row(win), row(width), row(width), row(width), row(width),
                  full(w2p), full(a2p), full(g2b), full(e), full(et)],
        out_specs=[tile] * 7,
        out_shape=[sds(BF16), sds(F32)] + [sds(BF16)] * 5,
        scratch_shapes=[pltpu.VMEM((tm + 8, win), F32)],
        compiler_params=_params("parallel", "parallel"),
        name="rwkv_prep",
    )(x, x, w_in, vec(mu), vec(w0), vec(a0), vec(k_k), vec(k_a), w2p, a2p, g2b, e, et)
    ctile = pl.BlockSpec((1, SSD_CHUNK, width), lambda bi, ci: (bi, ci, 0))
    crow = pl.BlockSpec((1, width), lambda bi, ci: (0, 0))
    args = [r, lw, k, v, an, bn, g, vec(r_k), vec(lnx_g), vec(lnx_b)]
    scratch = [pltpu.VMEM((width // LANES, LANES, LANES), F32)]
    return args, [ctile] * 7 + [crow] * 3, scratch


def _mixers_kernel(n_ssd, n_rwkv, *refs):
    ssd_in = refs[:n_ssd]
    rwkv_in = refs[n_ssd:n_ssd + n_rwkv]
    y_ssd_ref, y_rwkv_ref, ssd_state, ssd_ext, rwkv_state = refs[n_ssd + n_rwkv:]
    c = pl.program_id(1)

    @pl.when(c == 0)
    def _():
        ssd_state[...] = jnp.zeros_like(ssd_state)
        rwkv_state[...] = jnp.zeros_like(rwkv_state)

    ssd_steps = _ssd_chunk(c, *ssd_in, y_ssd_ref, ssd_state, ssd_ext)
    fill = lambda: next(ssd_steps, None)
    for sub in range(SSD_CHUNK // RWKV_CHUNK):
        _rwkv_chunk(slice(sub * RWKV_CHUNK, (sub + 1) * RWKV_CHUNK), fill, *rwkv_in, y_rwkv_ref, rwkv_state)
    for _ in ssd_steps:
        pass


def _ssd_rwkv_mixers(ssd_ops, rwkv_ops, bsz, l, ssd_width, rwkv_width):
    ssd_args, ssd_specs, ssd_scratch = ssd_ops
    rwkv_args, rwkv_specs, rwkv_scratch = rwkv_ops
    q = SSD_CHUNK
    out_spec = lambda w: pl.BlockSpec((1, q, w), lambda bi, c: (bi, c, 0))
    return pl.pallas_call(
        functools.partial(_mixers_kernel, len(ssd_args), len(rwkv_args)),
        grid=(bsz, l // q),
        in_specs=ssd_specs + rwkv_specs,
        out_specs=[out_spec(ssd_width), out_spec(rwkv_width)],
        out_shape=[jax.ShapeDtypeStruct((bsz, l, ssd_width), BF16), jax.ShapeDtypeStruct((bsz, l, rwkv_width), BF16)],
        scratch_shapes=ssd_scratch + rwkv_scratch,
        compiler_params=_params("parallel", "arbitrary"),
        name="ssd_rwkv_mixers",
    )(*ssd_args, *rwkv_args)


def _moba_kernel(q_ref, k_ref, v_ref, o_ref, kb_ref, vt_ref, kmean_ref):
    blk = MOBA_BLOCK
    half = HEAD_DIM
    nb = k_ref.shape[0] // blk
    nbp = kmean_ref.shape[0]
    npair = q_ref.shape[1] // LANES
    heads = range(2 * npair)
    group = math.gcd(nb, MOBA_GROUP)
    qi = pl.program_id(2)

    @pl.when(qi == 0)
    def _():
        lane = lax.broadcasted_iota(jnp.int32, (blk, LANES), 1)
        rowp = lax.broadcasted_iota(jnp.int32, (LANES, blk), 0)
        if nbp > nb:
            kmean_ref[...] = jnp.zeros_like(kmean_ref)
        for n in range(nb):
            rows = slice(n * blk, (n + 1) * blk)
            kmean_ref[n:n + 1, :] = jnp.mean(k_ref[rows, :], axis=0, keepdims=True)
            for pp in range(npair):
                cols = slice(pp * LANES, (pp + 1) * LANES)
                kn = k_ref[rows, cols]
                kb_ref[2 * pp, n] = jnp.where(lane < half, kn, jnp.where(lane == half + n, 1.0, 0.0)).astype(BF16)
                kb_ref[2 * pp + 1, n] = jnp.where(lane >= half, kn, jnp.where(lane == n, 1.0, 0.0)).astype(BF16)
                vtn = v_ref[rows, cols].T
                vt_ref[2 * pp, n] = jnp.where(rowp < half, vtn, jnp.where(rowp == half, 1.0, 0.0)).astype(BF16)
                vt_ref[2 * pp + 1, n] = jnp.where(rowp >= half, vtn, jnp.where(rowp == 0, 1.0, 0.0)).astype(BF16)

    rown = lax.broadcasted_iota(jnp.int32, (nbp, blk), 0)
    lane_k = lax.broadcasted_iota(jnp.int32, (nbp, LANES), 1)
    zeros = jnp.zeros((half, blk), F32)
    qts = [q_ref[:, pp * LANES:(pp + 1) * LANES].T for pp in range(npair)]
    gates, qhs = [], []
    for hh in heads:
        pp, h = divmod(hh, 2)
        km = kmean_ref[:, pp * LANES:(pp + 1) * LANES]
        kmh = jnp.where((lane_k < half) if h == 0 else (lane_k >= half), km, 0.0)
        gates.append(jnp.dot(kmh, qts[pp], preferred_element_type=F32, precision=lax.Precision.HIGHEST))
        qhs.append(qts[pp][h * half:(h + 1) * half] * (half ** -0.5 * LOG2_E))
    q_own = [jnp.concatenate([qhs[hh], zeros] if hh % 2 == 0 else [zeros, qhs[hh]], axis=0).astype(BF16)
             for hh in heads]
    s_own = [jnp.dot(kb_ref[h, qi], q_own[h], preferred_element_type=F32) for h in heads]
    q_past = []
    for hh in heads:
        gate = jnp.where(rown < qi, gates[hh], -jnp.inf)
        bias = jnp.full((nbp, blk), NEG_BIG, F32)
        for _ in range(MOBA_TOPK):
            mx = jnp.max(gate, axis=0, keepdims=True)
            first = jnp.min(jnp.where(gate == mx, rown, nbp), axis=0, keepdims=True)
            pick = (rown == first) & (mx > -jnp.inf)
            bias = jnp.where(pick, 0.0, bias)
            gate = jnp.where(pick, -jnp.inf, gate)
        aug = jnp.concatenate([bias, jnp.zeros((half - nbp, blk), F32)], axis=0)
        q_past.append(jnp.concatenate([qhs[hh], aug] if hh % 2 == 0 else [aug, qhs[hh]], axis=0).astype(BF16))

    causal = (lax.broadcasted_iota(jnp.int32, (blk, blk), 0) <= lax.broadcasted_iota(jnp.int32, (blk, blk), 1))
    ms, ps = [], []
    for h in heads:
        s = jnp.where(causal, s_own[h], NEG_BIG)
        ms.append(jnp.max(s, axis=0, keepdims=True))
        ps.append(jnp.exp2(s - ms[h]).astype(BF16))
    carry = []
    for h in heads:
        carry += [ms[h], jnp.dot(vt_ref[h, qi], ps[h], preferred_element_type=F32)]

    def scores(gi, h):
        return [jnp.dot(kb_ref[h, gi * group + g], q_past[h], preferred_element_type=F32) for g in range(group)]

    def values(gi, h):
        return jnp.concatenate([vt_ref[h, gi * group + g] for g in range(group)], axis=1)

    def body_lagged(gi, carry):
        excess = carry[-1]
        out = []
        sss = [scores(gi, h) for h in heads]
        pcats, gmaxs = [], []
        for h in heads:
            m_run = carry[2 * h]
            gmax = None
            ps = []
            for s in sss[h]:
                cm = jnp.max(s, axis=0, keepdims=True)
                gmax = cm if gmax is None else jnp.maximum(gmax, cm)
                ps.append(jnp.exp2(s - m_run).astype(BF16))
            pcats.append(jnp.concatenate(ps, axis=0))
            gmaxs.append(gmax)
        for h in heads:
            m_run, acc = carry[2 * h], carry[2 * h + 1]
            m_new = jnp.maximum(m_run, gmaxs[h])
            excess = jnp.maximum(excess, gmaxs[h] - m_run)
            acc = jnp.exp2(m_run - m_new) * (acc + jnp.dot(values(gi, h), pcats[h], preferred_element_type=F32))
            out += [m_new, acc]
        return tuple(out) + (excess,)

    def body_exact_max(gi, carry):
        sss = [scores(gi, h) for h in heads]
        m_news = []
        for h in heads:
            m_new = carry[2 * h]
            for s in sss[h]:
                m_new = jnp.maximum(m_new, jnp.max(s, axis=0, keepdims=True))
            m_news.append(m_new)
        pcats = [jnp.concatenate([jnp.exp2(s - m_news[h]).astype(BF16) for s in sss[h]], axis=0)
                 for h in heads]
        out = []
        for h in heads:
            alpha = jnp.exp2(carry[2 * h] - m_news[h])
            out += [m_news[h], alpha * carry[2 * h + 1]
                    + jnp.dot(values(gi, h), pcats[h], preferred_element_type=F32)]
        return tuple(out)

    rowq = lax.broadcasted_iota(jnp.int32, (LANES, blk), 0)

    def write_out(final):
        for pp in range(npair):
            acc0, acc1 = final[4 * pp + 1], final[4 * pp + 3]
            out_t = jnp.where(rowq < half, acc0 / acc0[half:half + 1], acc1 / acc1[0:1])
            o_ref[:, pp * LANES:(pp + 1) * LANES] = out_t.T.astype(o_ref.dtype)

    n_groups = (qi + group - 1) // group
    final = lax.fori_loop(0, n_groups, body_lagged, tuple(carry) + (jnp.full((1, blk), NEG_BIG, F32),))
    write_out(final)

    @pl.when(jnp.max(final[-1]) > MOBA_LAG_LIMIT)
    def _():
        write_out(lax.fori_loop(0, n_groups, body_exact_max, tuple(carry)))


def _moba_attention(qkv, bsz, s, heads):
    blk = MOBA_BLOCK
    assert s % blk == 0 and (heads * HEAD_DIM) % LANES == 0
    nb = s // blk
    assert nb <= HEAD_DIM
    nbp = -(-nb // 8) * 8
    pairs = heads * HEAD_DIM // LANES
    pps = math.gcd(pairs, MOBA_PAIRS_PER_STEP)
    cw = pps * LANES
    steps = pairs // pps
    return pl.pallas_call(
        _moba_kernel,
        grid=(bsz, steps, nb),
        in_specs=[pl.BlockSpec((blk, cw), lambda b, p, i: (b * nb + i, p)),
                  pl.BlockSpec((s, cw), lambda b, p, i: (b, steps + p)),
                  pl.BlockSpec((s, cw), lambda b, p, i: (b, 2 * steps + p))],
        out_specs=pl.BlockSpec((blk, cw), lambda b, p, i: (b * nb + i, p)),
        out_shape=jax.ShapeDtypeStruct((bsz * s, heads * HEAD_DIM), BF16),
        scratch_shapes=[pltpu.VMEM((2 * pps, nb, blk, LANES), BF16), pltpu.VMEM((2 * pps, nb, LANES, blk), BF16),
                        pltpu.VMEM((nbp, cw), F32)],
        compiler_params=_params("parallel", "parallel", "arbitrary"),
        name="moba_attention",
    )(qkv, qkv, qkv)


def _row_tile(m):
    for t in (512, 256, 128, 64, 32, 16, 8):
        if m % t == 0:
            return t
    raise ValueError(f"row count {m} is not a multiple of 8")


def _col_tile(n, cap=2048):
    best = None
    for t in range(LANES, min(n, cap) + 1, LANES):
        if n % t == 0:
            best = t
    if best is None:
        raise ValueError(f"column count {n} is not a multiple of {LANES}")
    return best


def kernel(x, mem, even_w_in, ssd_conv_w, ssd_conv_b, ssd_dt_bias, ssd_a_log, ssd_d, ssd_norm_g, rwkv_mu, rwkv_w0, rwkv_w2, rwkv_a0, rwkv_a2, rwkv_g2, rwkv_k_k, rwkv_k_a, rwkv_r_k, rwkv_lnx_g, rwkv_lnx_b, even_w_out, odd_w_qkv, odd_w_out, ln_mix_g, ln_mix_b, xa_wq, xa_wkv, xa_wo, ln_xa_g, ln_xa_b, ffn_w13, ffn_w2, ln_ffn_g, ln_ffn_b):
    bsz, s, d = x.shape
    m = bsz * s
    tm = _row_tile(s)
    ssd_width = ssd_norm_g.shape[-1]
    ssd_heads = ssd_dt_bias.shape[-1]
    ssd_xbc = ssd_conv_b.shape[-1]
    ssd_in = ssd_width + ssd_xbc + ssd_heads
    rwkv_width = rwkv_w0.shape[-1]
    mem2 = mem.reshape(bsz * mem.shape[1], d)
    x2 = x.reshape(m, d)
    for layer in range(DEPTH):
        j = layer // 2
        if layer % 2 == 0:
            w_in = even_w_in[j].astype(BF16)
            w_z = w_in[:, :ssd_width]
            w_xbc = w_in[:, ssd_width:ssd_width + ssd_xbc]
            w_dt = jnp.pad(w_in[:, ssd_width + ssd_xbc:ssd_in], ((0, 0), (0, LANES - ssd_heads)))
            w_rw = w_in[:, ssd_in:]
            z, xbc, dt_pad = _matmul(x2, [w_z, w_xbc, w_dt], tm=tm)
            ssd_ops = _ssd_operands(z.reshape(bsz, s, -1), xbc.reshape(bsz, s, -1), dt_pad.reshape(bsz, s, -1),
                                    ssd_conv_w[j], ssd_conv_b[j], ssd_dt_bias[j], ssd_a_log[j], ssd_d[j],
                                    ssd_norm_g[j])
            rwkv_ops = _rwkv_operands(x2.reshape(bsz, s, d), w_rw, rwkv_mu[j], rwkv_w0[j], rwkv_w2[j],
                                      rwkv_a0[j], rwkv_a2[j], rwkv_g2[j], rwkv_k_k[j], rwkv_k_a[j], rwkv_r_k[j],
                                      rwkv_lnx_g[j], rwkv_lnx_b[j], tm=min(tm, 256))
            y_ssd, y_rwkv = _ssd_rwkv_mixers(ssd_ops, rwkv_ops, bsz, s, ssd_width, rwkv_width)
            x2 = _matmul_residual_ln([y_ssd.reshape(m, -1), y_rwkv.reshape(m, -1)],
                                     even_w_out[j].astype(BF16), x2,
                                     ln_mix_g[layer], ln_mix_b[layer], tm=tm)
        else:
            heads = d // HEAD_DIM
            qkv, = _matmul(x2, [odd_w_qkv[j].astype(BF16)], tm=tm)
            attn = _moba_attention(qkv, bsz, s, heads)
            x2 = _matmul_residual_ln([attn], odd_w_out[j].astype(BF16), x2,
                                     ln_mix_g[layer], ln_mix_b[layer], tm=tm)
        kv, = _matmul(mem2, [xa_wkv[layer].astype(BF16)], tm=_row_tile(mem2.shape[0]))
        x3 = _cross_attention_ln(x2.reshape(bsz, s, d), kv.reshape(bsz, -1, 2 * d),
                                 xa_wq[layer].astype(BF16), xa_wo[layer].astype(BF16),
                                 ln_xa_g[layer], ln_xa_b[layer], tm=tm)
        x2 = x3.reshape(m, d)
        h = _swiglu_up(x2, ffn_w13[layer].astype(BF16), tm=tm)
        x2 = _matmul_residual_ln([h], ffn_w2[layer].astype(BF16), x2,
                                 ln_ffn_g[layer], ln_ffn_b[layer], tm=tm)
    return x2.reshape(bsz, s, d)
```

```python
import functools
import math

import jax
import jax.numpy as jnp
from jax import lax
from jax.experimental import pallas as pl
from jax.experimental.pallas import tpu as pltpu

F32 = jnp.float32
BF16 = jnp.bfloat16

HEAD_DIM = 64
LANES = 128
SSD_GROUPS = 2
SSD_STATE = 128
SSD_CONV = 4
SSD_CHUNK = 128
RWKV_DECAY_LORA = 64
RWKV_ICLR_LORA = 64
RWKV_GATE_LORA = 128
RWKV_CHUNK = 64
MOBA_BLOCK = 256
MOBA_TOPK = 3
MOBA_GROUP = 4
MOBA_LAG_LIMIT = 64.0
MOBA_PAIRS_PER_STEP = 2
XATTN_HEADS = 4
DEPTH = 2
DEEPNORM_ALPHA = (2 * DEPTH) ** 0.25
LN_EPS = 1e-5
RMS_EPS = 1e-5
RWKV_LNX_EPS = 64e-5
NEG_BIG = -1e30
LOG2_E = math.log2(math.e)
VMEM_LIMIT = 56 * 1024 * 1024


def _params(*sem):
    return pltpu.CompilerParams(dimension_semantics=sem, vmem_limit_bytes=VMEM_LIMIT)


def _dot(a, b):
    return jnp.dot(a.astype(BF16), b.astype(BF16), preferred_element_type=F32)


def _dot_nt(a, b):
    return lax.dot_general(a.astype(BF16), b.astype(BF16), (((1,), (1,)), ((), ())),
                           preferred_element_type=F32)


def _dot_tn(a, b):
    return lax.dot_general(a.astype(BF16), b.astype(BF16), (((0,), (0,)), ((), ())),
                           preferred_element_type=F32)


def _split3(x):
    hi = x.astype(BF16)
    r1 = x - hi.astype(F32)
    mid = r1.astype(BF16)
    lo = (r1 - mid.astype(F32)).astype(BF16)
    return hi, mid, lo


def _dot_exact_lhs(m, x):
    hi, mid, lo = _split3(x)
    m = m.astype(BF16)
    return (jnp.dot(m, hi, preferred_element_type=F32) + jnp.dot(m, mid, preferred_element_type=F32)
            + jnp.dot(m, lo, preferred_element_type=F32))


def _dot_exact_rhs(x, m):
    hi, mid, lo = _split3(x)
    m = m.astype(BF16)
    return (jnp.dot(hi, m, preferred_element_type=F32) + jnp.dot(mid, m, preferred_element_type=F32)
            + jnp.dot(lo, m, preferred_element_type=F32))


def _sigmoid(x):
    return 1.0 / (1.0 + jnp.exp(-x))


def _softplus(x):
    return jnp.maximum(x, 0.0) + jnp.log1p(jnp.exp(-jnp.abs(x)))


def _layer_norm(v, g, b):
    mu = jnp.mean(v, axis=-1, keepdims=True)
    c = v - mu
    var = jnp.mean(c * c, axis=-1, keepdims=True)
    return c * lax.rsqrt(var + LN_EPS) * g + b


def _col_chunks(n, cap=1536):
    width = _col_tile(n, cap)
    return [slice(j, j + width) for j in range(0, n, width)]


def _mm_kernel(n_out, x_ref, *refs):
    x = x_ref[...].astype(BF16)
    for w_ref, o_ref in zip(refs[:n_out], refs[n_out:]):
        for sl in _col_chunks(w_ref.shape[1]):
            o_ref[:, sl] = jnp.dot(x, w_ref[:, sl], preferred_element_type=F32).astype(o_ref.dtype)


def _matmul(x, ws, *, tm, out_dtype=F32):
    m, k = x.shape
    assert m % tm == 0
    return pl.pallas_call(
        functools.partial(_mm_kernel, len(ws)),
        grid=(m // tm,),
        in_specs=[pl.BlockSpec((tm, k), lambda i: (i, 0))] + [pl.BlockSpec(w.shape, lambda i: (0, 0)) for w in ws],
        out_specs=[pl.BlockSpec((tm, w.shape[1]), lambda i: (i, 0)) for w in ws],
        out_shape=[jax.ShapeDtypeStruct((m, w.shape[1]), out_dtype) for w in ws],
        compiler_params=_params("parallel"),
        name="matmul",
    )(x, *ws)


def _mm_res_ln_kernel(n_in, *refs):
    hs = refs[:n_in]
    ws = refs[n_in:2 * n_in]
    res_ref, g_ref, b_ref, o_ref = refs[2 * n_in:]
    acc = _dot(hs[0][...], ws[0][...])
    for h_ref, w_ref in zip(hs[1:], ws[1:]):
        acc = acc + _dot(h_ref[...], w_ref[...])
    o_ref[...] = _layer_norm(DEEPNORM_ALPHA * res_ref[...] + acc, g_ref[...], b_ref[...])


def _matmul_residual_ln(hs, w, res, g, b, *, tm):
    m, d = res.shape
    n_in = len(hs)
    kw = w.shape[0] // n_in
    assert all(h.shape[1] == kw for h in hs)
    in_specs = ([pl.BlockSpec((tm, kw), lambda i: (i, 0)) for _ in hs]
                + [pl.BlockSpec((kw, d), lambda i, j=j: (j, 0)) for j in range(n_in)]
                + [pl.BlockSpec((tm, d), lambda i: (i, 0)),
                   pl.BlockSpec((1, d), lambda i: (0, 0)),
                   pl.BlockSpec((1, d), lambda i: (0, 0))])
    return pl.pallas_call(
        functools.partial(_mm_res_ln_kernel, n_in),
        grid=(m // tm,),
        in_specs=in_specs,
        out_specs=pl.BlockSpec((tm, d), lambda i: (i, 0)),
        out_shape=jax.ShapeDtypeStruct((m, d), F32),
        compiler_params=_params("parallel"),
        name="matmul_residual_ln",
    )(*hs, *([w] * n_in), res, g.reshape(1, d), b.reshape(1, d))


def _swiglu_kernel(x_ref, w13_ref, o_ref):
    x = x_ref[...].astype(BF16)
    n = o_ref.shape[1]
    for sl in _col_chunks(n):
        gate = jnp.dot(x, w13_ref[:, sl], preferred_element_type=F32)
        up = jnp.dot(x, w13_ref[:, slice(n + sl.start, n + sl.stop)], preferred_element_type=F32)
        o_ref[:, sl] = (gate * _sigmoid(gate) * up).astype(o_ref.dtype)


def _swiglu_up(x, w13, *, tm):
    m, k = x.shape
    n = w13.shape[1] // 2
    return pl.pallas_call(
        _swiglu_kernel,
        grid=(m // tm,),
        in_specs=[pl.BlockSpec((tm, k), lambda i: (i, 0)),
                  pl.BlockSpec((k, 2 * n), lambda i: (0, 0))],
        out_specs=pl.BlockSpec((tm, n), lambda i: (i, 0)),
        out_shape=jax.ShapeDtypeStruct((m, n), BF16),
        compiler_params=_params("parallel"),
        name="swiglu_up",
    )(x, w13)


def _xattn_kernel(x_ref, kv_ref, wq_ref, wo_ref, g_ref, b_ref, o_ref):
    x = x_ref[0]
    d = x.shape[-1]
    hd = d // XATTN_HEADS
    q = _dot(x, wq_ref[...])
    kv = kv_ref[0]
    heads = range(XATTN_HEADS)
    ss = [_dot_nt(q[:, h * hd:(h + 1) * hd], kv[:, h * hd:(h + 1) * hd]) * (hd ** -0.5) for h in heads]
    ps = []
    for s in ss:
        p = jnp.exp(s - jnp.max(s, axis=-1, keepdims=True))
        ps.append(p / jnp.sum(p, axis=-1, keepdims=True))
    o = jnp.concatenate([_dot(ps[h], kv[:, d + h * hd:d + (h + 1) * hd]) for h in heads], axis=-1)
    xa = _dot(o, wo_ref[...])
    o_ref[0] = _layer_norm(DEEPNORM_ALPHA * x + xa, g_ref[...], b_ref[...])


def _cross_attention_ln(x, kv, wq, wo, g, b, *, tm):
    bsz, s, d = x.shape
    m = kv.shape[1]
    return pl.pallas_call(
        _xattn_kernel,
        grid=(bsz, s // tm),
        in_specs=[pl.BlockSpec((1, tm, d), lambda bi, i: (bi, i, 0)),
                  pl.BlockSpec((1, m, 2 * d), lambda bi, i: (bi, 0, 0)),
                  pl.BlockSpec((d, d), lambda bi, i: (0, 0)),
                  pl.BlockSpec((d, d), lambda bi, i: (0, 0)),
                  pl.BlockSpec((1, d), lambda bi, i: (0, 0)),
                  pl.BlockSpec((1, d), lambda bi, i: (0, 0))],
        out_specs=pl.BlockSpec((1, tm, d), lambda bi, i: (bi, i, 0)),
        out_shape=jax.ShapeDtypeStruct((bsz, s, d), F32),
        compiler_params=_params("parallel", "parallel"),
        name="cross_attention_ln",
    )(x, kv, wq, wo, g.reshape(1, d), b.reshape(1, d))


def _ssd_chunk(c, z_ref, xbc_ref, xbcp_ref, dt_ref, dtt_ref, cw_ref, cb_ref, dtb_ref, dtbt_ref,
               aneg_ref, anegt_ref, dskip_ref, ng_ref, y_ref, state_ref, ext_ref):
    q = SSD_CHUNK
    width = z_ref.shape[-1]
    n_pairs = width // LANES

    ext_ref[0:8, :] = jnp.where(c > 0, xbcp_ref[0], 0.0)
    ext_ref[8:8 + q, :] = xbc_ref[0]
    xcs = []
    for sl in _col_chunks(ext_ref.shape[1], 256):
        conv = cb_ref[:, sl] + cw_ref[SSD_CONV - 1:SSD_CONV, sl] * ext_ref[8:8 + q, sl]
        for k in range(SSD_CONV - 1):
            off = 8 - (SSD_CONV - 1) + k
            conv = conv + cw_ref[k:k + 1, sl] * ext_ref[off:off + q, sl]
        xcs.append(conv * _sigmoid(conv))
        yield
    xc = jnp.concatenate(xcs, axis=1)
    xs = xc[:, :width]
    gn = SSD_GROUPS * SSD_STATE
    bm = xc[:, width:width + gn]
    cm = xc[:, width + gn:width + 2 * gn]

    dt = _softplus(dt_ref[0] + dtb_ref[...])
    a = dt * aneg_ref[...]
    dtt = _softplus(dtt_ref[0] + dtbt_ref[...])
    at = dtt * anegt_ref[...]
    row = lax.broadcasted_iota(jnp.int32, (q, q), 0)
    col = lax.broadcasted_iota(jnp.int32, (q, q), 1)
    causal = row >= col
    tri = jnp.where(causal, 1.0, 0.0)
    a_cum = _dot_exact_lhs(tri, a)
    a_cumt = _dot_exact_rhs(at, jnp.where(row <= col, 1.0, 0.0))
    yield

    lane = lax.broadcasted_iota(jnp.int32, (1, LANES), 1)
    lane_lo = lane < HEAD_DIM
    rowp = lax.broadcasted_iota(jnp.int32, (LANES, 1), 0)
    pairs_per_group = n_pairs // SSD_GROUPS
    ys = []
    for p in range(n_pairs):
        g = p // pairs_per_group
        h0, h1 = 2 * p, 2 * p + 1
        bg = bm[:, g * SSD_STATE:(g + 1) * SSD_STATE]
        cg = cm[:, g * SSD_STATE:(g + 1) * SSD_STATE]
        cb = _dot_nt(cg, bg)
        xs_p = xs[:, p * LANES:(p + 1) * LANES]
        dt_p = jnp.where(lane_lo, dt[:, h0:h0 + 1], dt[:, h1:h1 + 1])
        acum_p = jnp.where(lane_lo, a_cum[:, h0:h0 + 1], a_cum[:, h1:h1 + 1])
        xdt = xs_p * dt_p
        ms = []
        for h in (h0, h1):
            seg = a_cum[:, h:h + 1] - a_cumt[h:h + 1, :]
            ms.append(cb * jnp.exp(jnp.where(causal, seg, NEG_BIG)))
        m2 = jnp.concatenate(ms, axis=1)
        x2 = jnp.concatenate([jnp.where(lane_lo, xdt, 0.0), jnp.where(lane_lo, 0.0, xdt)], axis=0)
        y_diag = _dot(m2, x2)
        prev = state_ref[p]
        y_off = _dot_nt(cg, prev) * jnp.exp(acum_p)
        a_last = jnp.where(lane_lo, a_cum[q - 1:q, h0:h0 + 1], a_cum[q - 1:q, h1:h1 + 1])
        xdw = xdt * jnp.exp(a_last - acum_p)
        st = _dot_tn(xdw, bg)
        cd = jnp.where(rowp < HEAD_DIM, jnp.exp(a_cumt[h0:h0 + 1, q - 1:q]),
                       jnp.exp(a_cumt[h1:h1 + 1, q - 1:q]))
        state_ref[p] = prev * cd + st
        d_p = dskip_ref[:, p * LANES:(p + 1) * LANES]
        ys.append(y_diag + y_off + d_p * xs_p)
        yield
    y = jnp.concatenate(ys, axis=1)
    z = z_ref[0]
    y = y * (z * _sigmoid(z))
    gw = width // SSD_GROUPS
    outs = []
    for g in range(SSD_GROUPS):
        yg = y[:, g * gw:(g + 1) * gw]
        outs.append(yg * lax.rsqrt(jnp.mean(yg * yg, axis=-1, keepdims=True) + RMS_EPS))
    y_ref[0] = (jnp.concatenate(outs, axis=1) * ng_ref[...]).astype(y_ref.dtype)


def _ssd_operands(z, xbc, dt_pad, conv_w, conv_b, dt_bias, a_log, d_skip, norm_g):
    bsz, l, width = z.shape
    heads = width // HEAD_DIM
    xw = xbc.shape[-1]
    q = SSD_CHUNK
    dtt = jnp.swapaxes(dt_pad[:, :, :heads], 1, 2)
    pad = LANES - heads
    dtb = jnp.pad(dt_bias, (0, pad)).reshape(1, LANES)
    a_neg = -jnp.exp(a_log.astype(F32))
    aneg = jnp.pad(a_neg, (0, pad)).reshape(1, LANES)
    dskip = jnp.repeat(d_skip, HEAD_DIM).reshape(1, width)
    row = lambda n: pl.BlockSpec((1, n), lambda bi, c: (0, 0))
    specs = [pl.BlockSpec((1, q, width), lambda bi, c: (bi, c, 0)),
             pl.BlockSpec((1, q, xw), lambda bi, c: (bi, c, 0)),
             pl.BlockSpec((1, 8, xw), lambda bi, c: (bi, jnp.maximum(c * (q // 8) - 1, 0), 0)),
             pl.BlockSpec((1, q, LANES), lambda bi, c: (bi, c, 0)),
             pl.BlockSpec((1, heads, q), lambda bi, c: (bi, 0, c)),
             pl.BlockSpec((SSD_CONV, xw), lambda bi, c: (0, 0)),
             row(xw), row(LANES),
             pl.BlockSpec((heads, 1), lambda bi, c: (0, 0)),
             row(LANES),
             pl.BlockSpec((heads, 1), lambda bi, c: (0, 0)),
             row(width), row(width)]
    args = [z, xbc, xbc, dt_pad, dtt, conv_w, conv_b.reshape(1, xw), dtb, dt_bias.reshape(heads, 1),
            aneg, a_neg.reshape(heads, 1), dskip, norm_g.reshape(1, width)]
    scratch = [pltpu.VMEM((width // LANES, LANES, SSD_STATE), F32), pltpu.VMEM((q + 8, xw), F32)]
    return args, specs, scratch


def _rwkv_prep_kernel(x_ref, xp_ref, win_ref, mu_ref, w0_ref, a0_ref, kk_ref, ka_ref, w2_ref, a2_ref, g2_ref,
                      e_ref, et_ref, r_ref, lw_ref, k_ref, v_ref, an_ref, bn_ref, g_ref, ext_ref):
    tm = x_ref.shape[1]
    width = r_ref.shape[-1]
    i = pl.program_id(1)
    xe = jnp.concatenate([jnp.where(i > 0, xp_ref[0], 0.0), x_ref[0]], axis=0)
    for sl in _col_chunks(ext_ref.shape[1], 1664):
        ext_ref[:, sl] = _dot(xe, win_ref[:, sl])
    rw = ext_ref[8:8 + tm, :]
    shifted = ext_ref[7:7 + tm, :]
    s = rw + (shifted - rw) * mu_ref[...]
    r = s[:, :width]
    k = s[:, width:2 * width]
    v = s[:, 2 * width:3 * width]
    lo = s[:, 3 * width:3 * width + RWKV_DECAY_LORA + RWKV_ICLR_LORA]
    g_lo = s[:, 3 * width + RWKV_DECAY_LORA + RWKV_ICLR_LORA:]
    wv = w0_ref[...] + _dot(jnp.tanh(lo), w2_ref[...])
    w = -_softplus(-wv) - 0.5
    av = _sigmoid(a0_ref[...] + _dot(lo, a2_ref[...]))
    g = _dot(_sigmoid(g_lo), g2_ref[...])
    kkr = k * kk_ref[...]
    ss = _dot_exact_rhs(kkr * kkr, e_ref[...])
    inv = lax.rsqrt(jnp.maximum(ss, 1e-24))
    kk = kkr * _dot_exact_rhs(inv, et_ref[...])
    r_ref[0] = r.astype(r_ref.dtype)
    lw_ref[0] = -jnp.exp(w)
    k_ref[0] = (k * (1.0 + (av - 1.0) * ka_ref[...])).astype(k_ref.dtype)
    v_ref[0] = v.astype(v_ref.dtype)
    an_ref[0] = (-kk).astype(an_ref.dtype)
    bn_ref[0] = (kk * av).astype(bn_ref.dtype)
    g_ref[0] = g.astype(g_ref.dtype)


def _rwkv_chunk(rows, fill, r_ref, lw_ref, k_ref, v_ref, an_ref, bn_ref, g_ref, rk_ref, lg_ref, lb_ref,
                y_ref, state_ref):
    c = RWKV_CHUNK
    c2 = 2 * c
    width = r_ref.shape[-1]
    n_pairs = width // LANES

    row = lax.broadcasted_iota(jnp.int32, (c, c), 0)
    col = lax.broadcasted_iota(jnp.int32, (c, c), 1)
    tri = jnp.where(row >= col, 1.0, 0.0)
    roww = lax.broadcasted_iota(jnp.int32, (c, c2), 0)
    colw = lax.broadcasted_iota(jnp.int32, (c, c2), 1) % c
    strict = jnp.where(roww > colw, 1.0, 0.0)
    incl = jnp.where(roww >= colw, 1.0, 0.0)
    eye = jnp.where(roww == colw, 1.0, 0.0)
    same_head = jnp.where((lax.broadcasted_iota(jnp.int32, (LANES, LANES), 0) // HEAD_DIM)
                          == (lax.broadcasted_iota(jnp.int32, (LANES, LANES), 1) // HEAD_DIM), 1.0, 0.0)
    lane = lax.broadcasted_iota(jnp.int32, (1, LANES), 1)
    m0 = jnp.where(lane < HEAD_DIM, 1.0, 0.0)
    m1 = 1.0 - m0

    lane_lo2 = lax.broadcasted_iota(jnp.int32, (c, LANES), 1) < HEAD_DIM

    def stack(x):
        xb = x.astype(BF16)
        zero = jnp.zeros_like(xb)
        return jnp.concatenate([jnp.where(lane_lo2, xb, zero), jnp.where(lane_lo2, zero, xb)], axis=0)

    sls = [slice(p * LANES, (p + 1) * LANES) for p in range(n_pairs)]
    lhs2s, rhs4s, vs, v_stks, bks, g_lasts = [], [], [], [], [], []
    for sl in sls:
        r = r_ref[0, rows, sl].astype(F32)
        lw = lw_ref[0, rows, sl]
        k = k_ref[0, rows, sl].astype(F32)
        b = bn_ref[0, rows, sl].astype(F32)
        v = v_ref[0, rows, sl].astype(F32)
        cum = _dot_exact_lhs(tri, lw)
        cum_last = cum[c - 1:c, :]
        e_neg = jnp.exp(-cum)
        e_tail = jnp.exp(cum_last - cum)
        at = an_ref[0, rows, sl].astype(F32) * jnp.exp(cum - lw)
        rt = r * jnp.exp(cum)
        lhs2s.append(jnp.concatenate([at, rt], axis=0).astype(BF16))
        rhs4s.append(jnp.concatenate([stack(b * e_neg), stack(k * e_neg)], axis=0))
        vs.append(v.astype(BF16))
        v_stks.append(stack(v))
        bks.append(jnp.concatenate([b * e_tail, k * e_tail], axis=0).astype(BF16))
        g_lasts.append(jnp.exp(cum_last))
    gms = [_dot_nt(l, rh) for l, rh in zip(lhs2s, rhs4s)]
    pws = [gm[:c, :c2] * strict for gm in gms]
    a_aks = [(gm[:c, c2:] * strict).astype(BF16) for gm in gms]
    a_rbks = [jnp.concatenate([gm[c:, :c2] * incl, gm[c:, c2:] * incl], axis=1).astype(BF16) for gm in gms]
    akvs = [_dot(a_ak, v_stk) for a_ak, v_stk in zip(a_aks, v_stks)]
    fill()
    tinvs = [eye + pw for pw in pws]
    pw_stks = [stack(pw) for pw in pws]
    for _ in range(int(math.log2(c)) - 1):
        pws = [_dot(pw, pw_stk) for pw, pw_stk in zip(pws, pw_stks)]
        pw_stks = [stack(pw) for pw in pws]
        tinvs = [tinv + _dot(tinv, pw_stk) for tinv, pw_stk in zip(tinvs, pw_stks)]
        fill()
    sts = [state_ref[p] for p in range(n_pairs)]
    ahrhs = [_dot_nt(l, st) for l, st in zip(lhs2s, sts)]
    fill()
    us = [_dot(tinv, stack(ahrh[:c] + akv)) for tinv, ahrh, akv in zip(tinvs, ahrhs, akvs)]
    fill()
    for p in range(n_pairs):
        uv = jnp.concatenate([us[p].astype(BF16), vs[p]], axis=0)
        state_ref[p] = sts[p] * g_lasts[p] + same_head * _dot_tn(uv, bks[p])
    fill()
    ys = [ahrh[c:] + _dot(a_rbk, jnp.concatenate([stack(u), v_stk], axis=0))
          for ahrh, a_rbk, u, v_stk in zip(ahrhs, a_rbks, us, v_stks)]
    lane_lo = lane < HEAD_DIM

    def head_sum(x):
        lo = jnp.sum(x * m0, axis=-1, keepdims=True)
        hi = jnp.sum(x * m1, axis=-1, keepdims=True)
        return jnp.where(lane_lo, lo, hi)

    bonus = [head_sum(r_ref[0, rows, sl].astype(F32) * k_ref[0, rows, sl].astype(F32) * rk_ref[:, sl]) for sl in sls]
    ycs = [y - head_sum(y) * (1.0 / HEAD_DIM) for y in ys]
    yvs = [head_sum(yc * yc) * (1.0 / HEAD_DIM) for yc in ycs]
    for p, sl in enumerate(sls):
        yn = ycs[p] * lax.rsqrt(yvs[p] + RWKV_LNX_EPS) * lg_ref[:, sl] + lb_ref[:, sl]
        out = (yn + bonus[p] * v_ref[0, rows, sl].astype(F32)) * g_ref[0, rows, sl].astype(F32)
        y_ref[0, rows, sl] = out.astype(y_ref.dtype)


def _rwkv_operands(x, w_in, mu, w0, w2, a0, a2, g2, k_k, k_a, r_k, lnx_g, lnx_b, *, tm):
    bsz, l, d = x.shape
    win = w_in.shape[1]
    width = w0.shape[0]
    w2p = jnp.concatenate([w2, jnp.zeros((RWKV_ICLR_LORA, width), F32)], axis=0).astype(BF16)
    a2p = jnp.concatenate([jnp.zeros((RWKV_DECAY_LORA, width), F32), a2], axis=0).astype(BF16)
    head_of = jnp.arange(width) // HEAD_DIM
    e = (head_of[:, None] == jnp.arange(LANES)[None, :]).astype(BF16)
    et = e.T
    vec = lambda x: x.reshape(1, -1)
    row = lambda n: pl.BlockSpec((1, n), lambda bi, i: (0, 0))
    full = lambda a: pl.BlockSpec(a.shape, lambda bi, i: (0, 0))
    tile = pl.BlockSpec((1, tm, width), lambda bi, i: (bi, i, 0))
    sds = lambda dt: jax.ShapeDtypeStruct((bsz, l, width), dt)
    g2b = g2.astype(BF16)
    r, lw, k, v, an, bn, g = pl.pallas_call(
        _rwkv_prep_kernel,
        grid=(bsz, l // tm),
        in_specs=[pl.BlockSpec((1, tm, d), lambda bi, i: (bi, i, 0)),
                  pl.BlockSpec((1, 8, d), lambda bi, i: (bi, jnp.maximum(i * (tm // 8) - 1, 0), 0)),
                  full(w_in), row(win), row(width), row(width), row(width), row(width),
                  full(w2p), full(a2p), full(g2b), full(e), full(et)],
        out_specs=[tile] * 7,
        out_shape=[sds(BF16), sds(F32)] + [sds(BF16)] * 5,
        scratch_shapes=[pltpu.VMEM((tm + 8, win), F32)],
        compiler_params=_params("parallel", "parallel"),
        name="rwkv_prep",
    )(x, x, w_in, vec(mu), vec(w0), vec(a0), vec(k_k), vec(k_a), w2p, a2p, g2b, e, et)
    ctile = pl.BlockSpec((1, SSD_CHUNK, width), lambda bi, ci: (bi, ci, 0))
    crow = pl.BlockSpec((1, width), lambda bi, ci: (0, 0))
    args = [r, lw, k, v, an, bn, g, vec(r_k), vec(lnx_g), vec(lnx_b)]
    scratch = [pltpu.VMEM((width // LANES, LANES, LANES), F32)]
    return args, [ctile] * 7 + [crow] * 3, scratch


def _mixers_kernel(n_ssd, n_rwkv, *refs):
    ssd_in = refs[:n_ssd]
    rwkv_in = refs[n_ssd:n_ssd + n_rwkv]
    y_ssd_ref, y_rwkv_ref, ssd_state, ssd_ext, rwkv_state = refs[n_ssd + n_rwkv:]
    c = pl.program_id(1)

    @pl.when(c == 0)
    def _():
        ssd_state[...] = jnp.zeros_like(ssd_state)
        rwkv_state[...] = jnp.zeros_like(rwkv_state)

    ssd_steps = _ssd_chunk(c, *ssd_in, y_ssd_ref, ssd_state, ssd_ext)
    fill = lambda: next(ssd_steps, None)
    for sub in range(SSD_CHUNK // RWKV_CHUNK):
        _rwkv_chunk(slice(sub * RWKV_CHUNK, (sub + 1) * RWKV_CHUNK), fill, *rwkv_in, y_rwkv_ref, rwkv_state)
    for _ in ssd_steps:
        pass


def _ssd_rwkv_mixers(ssd_ops, rwkv_ops, bsz, l, ssd_width, rwkv_width):
    ssd_args, ssd_specs, ssd_scratch = ssd_ops
    rwkv_args, rwkv_specs, rwkv_scratch = rwkv_ops
    q = SSD_CHUNK
    out_spec = lambda w: pl.BlockSpec((1, q, w), lambda bi, c: (bi, c, 0))
    return pl.pallas_call(
        functools.partial(_mixers_kernel, len(ssd_args), len(rwkv_args)),
        grid=(bsz, l // q),
        in_specs=ssd_specs + rwkv_specs,
        out_specs=[out_spec(ssd_width), out_spec(rwkv_width)],
        out_shape=[jax.ShapeDtypeStruct((bsz, l, ssd_width), BF16), jax.ShapeDtypeStruct((bsz, l, rwkv_width), BF16)],
        scratch_shapes=ssd_scratch + rwkv_scratch,
        compiler_params=_params("parallel", "arbitrary"),
        name="ssd_rwkv_mixers",
    )(*ssd_args, *rwkv_args)


def _moba_kernel(q_ref, k_ref, v_ref, o_ref, kb_ref, vt_ref, kmean_ref):
    blk = MOBA_BLOCK
    half = HEAD_DIM
    nb = k_ref.shape[0] // blk
    nbp = kmean_ref.shape[0]
    npair = q_ref.shape[1] // LANES
    heads = range(2 * npair)
    group = math.gcd(nb, MOBA_GROUP)
    qi = pl.program_id(2)

    @pl.when(qi == 0)
    def _():
        lane = lax.broadcasted_iota(jnp.int32, (blk, LANES), 1)
        rowp = lax.broadcasted_iota(jnp.int32, (LANES, blk), 0)
        if nbp > nb:
            kmean_ref[...] = jnp.zeros_like(kmean_ref)
        for n in range(nb):
            rows = slice(n * blk, (n + 1) * blk)
            kmean_ref[n:n + 1, :] = jnp.mean(k_ref[rows, :], axis=0, keepdims=True)
            for pp in range(npair):
                cols = slice(pp * LANES, (pp + 1) * LANES)
                kn = k_ref[rows, cols]
                kb_ref[2 * pp, n] = jnp.where(lane < half, kn, jnp.where(lane == half + n, 1.0, 0.0)).astype(BF16)
                kb_ref[2 * pp + 1, n] = jnp.where(lane >= half, kn, jnp.where(lane == n, 1.0, 0.0)).astype(BF16)
                vtn = v_ref[rows, cols].T
                vt_ref[2 * pp, n] = jnp.where(rowp < half, vtn, jnp.where(rowp == half, 1.0, 0.0)).astype(BF16)
                vt_ref[2 * pp + 1, n] = jnp.where(rowp >= half, vtn, jnp.where(rowp == 0, 1.0, 0.0)).astype(BF16)

    rown = lax.broadcasted_iota(jnp.int32, (nbp, blk), 0)
    lane_k = lax.broadcasted_iota(jnp.int32, (nbp, LANES), 1)
    zeros = jnp.zeros((half, blk), F32)
    qts = [q_ref[:, pp * LANES:(pp + 1) * LANES].T for pp in range(npair)]
    gates, qhs = [], []
    for hh in heads:
        pp, h = divmod(hh, 2)
        km = kmean_ref[:, pp * LANES:(pp + 1) * LANES]
        kmh = jnp.where((lane_k < half) if h == 0 else (lane_k >= half), km, 0.0)
        gates.append(jnp.dot(kmh, qts[pp], preferred_element_type=F32, precision=lax.Precision.HIGHEST))
        qhs.append(qts[pp][h * half:(h + 1) * half] * (half ** -0.5 * LOG2_E))
    q_own = [jnp.concatenate([qhs[hh], zeros] if hh % 2 == 0 else [zeros, qhs[hh]], axis=0).astype(BF16)
             for hh in heads]
    s_own = [jnp.dot(kb_ref[h, qi], q_own[h], preferred_element_type=F32) for h in heads]
    q_past = []
    for hh in heads:
        gate = jnp.where(rown < qi, gates[hh], -jnp.inf)
        bias = jnp.full((nbp, blk), NEG_BIG, F32)
        for _ in range(MOBA_TOPK):
            mx = jnp.max(gate, axis=0, keepdims=True)
            first = jnp.min(jnp.where(gate == mx, rown, nbp), axis=0, keepdims=True)
            pick = (rown == first) & (mx > -jnp.inf)
            bias = jnp.where(pick, 0.0, bias)
            gate = jnp.where(pick, -jnp.inf, gate)
        aug = jnp.concatenate([bias, jnp.zeros((half - nbp, blk), F32)], axis=0)
        q_past.append(jnp.concatenate([qhs[hh], aug] if hh % 2 == 0 else [aug, qhs[hh]], axis=0).astype(BF16))

    causal = (lax.broadcasted_iota(jnp.int32, (blk, blk), 0) <= lax.broadcasted_iota(jnp.int32, (blk, blk), 1))
    ms, ps = [], []
    for h in heads:
        s = jnp.where(causal, s_own[h], NEG_BIG)
        ms.append(jnp.max(s, axis=0, keepdims=True))
        ps.append(jnp.exp2(s - ms[h]).astype(BF16))
    carry = []
    for h in heads:
        carry += [ms[h], jnp.dot(vt_ref[h, qi], ps[h], preferred_element_type=F32)]

    def scores(gi, h):
        return [jnp.dot(kb_ref[h, gi * group + g], q_past[h], preferred_element_type=F32) for g in range(group)]

    def values(gi, h):
        return jnp.concatenate([vt_ref[h, gi * group + g] for g in range(group)], axis=1)

    def body_lagged(gi, carry):
        excess = carry[-1]
        out = []
        sss = [scores(gi, h) for h in heads]
        pcats, gmaxs = [], []
        for h in heads:
            m_run = carry[2 * h]
            gmax = None
            ps = []
            for s in sss[h]:
                cm = jnp.max(s, axis=0, keepdims=True)
                gmax = cm if gmax is None else jnp.maximum(gmax, cm)
                ps.append(jnp.exp2(s - m_run).astype(BF16))
            pcats.append(jnp.concatenate(ps, axis=0))
            gmaxs.append(gmax)
        for h in heads:
            m_run, acc = carry[2 * h], carry[2 * h + 1]
            m_new = jnp.maximum(m_run, gmaxs[h])
            excess = jnp.maximum(excess, gmaxs[h] - m_run)
            acc = jnp.exp2(m_run - m_new) * (acc + jnp.dot(values(gi, h), pcats[h], preferred_element_type=F32))
            out += [m_new, acc]
        return tuple(out) + (excess,)

    def body_exact_max(gi, carry):
        sss = [scores(gi, h) for h in heads]
        m_news = []
        for h in heads:
            m_new = carry[2 * h]
            for s in sss[h]:
                m_new = jnp.maximum(m_new, jnp.max(s, axis=0, keepdims=True))
            m_news.append(m_new)
        pcats = [jnp.concatenate([jnp.exp2(s - m_news[h]).astype(BF16) for s in sss[h]], axis=0)
                 for h in heads]
        out = []
        for h in heads:
            alpha = jnp.exp2(carry[2 * h] - m_news[h])
            out += [m_news[h], alpha * carry[2 * h + 1]
                    + jnp.dot(values(gi, h), pcats[h], preferred_element_type=F32)]
        return tuple(out)

    rowq = lax.broadcasted_iota(jnp.int32, (LANES, blk), 0)

    def write_out(final):
        for pp in range(npair):
            acc0, acc1 = final[4 * pp + 1], final[4 * pp + 3]
            out_t = jnp.where(rowq < half, acc0 / acc0[half:half + 1], acc1 / acc1[0:1])
            o_ref[:, pp * LANES:(pp + 1) * LANES] = out_t.T.astype(o_ref.dtype)

    n_groups = (qi + group - 1) // group
    final = lax.fori_loop(0, n_groups, body_lagged, tuple(carry) + (jnp.full((1, blk), NEG_BIG, F32),))
    write_out(final)

    @pl.when(jnp.max(final[-1]) > MOBA_LAG_LIMIT)
    def _():
        write_out(lax.fori_loop(0, n_groups, body_exact_max, tuple(carry)))


def _moba_attention(qkv, bsz, s, heads):
    blk = MOBA_BLOCK
    assert s % blk == 0 and (heads * HEAD_DIM) % LANES == 0
    nb = s // blk
    assert nb <= HEAD_DIM
    nbp = -(-nb // 8) * 8
    pairs = heads * HEAD_DIM // LANES
    pps = math.gcd(pairs, MOBA_PAIRS_PER_STEP)
    cw = pps * LANES
    steps = pairs // pps
    return pl.pallas_call(
        _moba_kernel,
        grid=(bsz, steps, nb),
        in_specs=[pl.BlockSpec((blk, cw), lambda b, p, i: (b * nb + i, p)),
                  pl.BlockSpec((s, cw), lambda b, p, i: (b, steps + p)),
                  pl.BlockSpec((s, cw), lambda b, p, i: (b, 2 * steps + p))],
        out_specs=pl.BlockSpec((blk, cw), lambda b, p, i: (b * nb + i, p)),
        out_shape=jax.ShapeDtypeStruct((bsz * s, heads * HEAD_DIM), BF16),
        scratch_shapes=[pltpu.VMEM((2 * pps, nb, blk, LANES), BF16), pltpu.VMEM((2 * pps, nb, LANES, blk), BF16),
                        pltpu.VMEM((nbp, cw), F32)],
        compiler_params=_params("parallel", "parallel", "arbitrary"),
        name="moba_attention",
    )(qkv, qkv, qkv)


def _row_tile(m):
    for t in (512, 256, 128, 64, 32, 16, 8):
        if m % t == 0:
            return t
    raise ValueError(f"row count {m} is not a multiple of 8")


def _col_tile(n, cap=2048):
    best = None
    for t in range(LANES, min(n, cap) + 1, LANES):
        if n % t == 0:
            best = t
    if best is None:
        raise ValueError(f"column count {n} is not a multiple of {LANES}")
    return best


def kernel(x, mem, even_w_in, ssd_conv_w, ssd_conv_b, ssd_dt_bias, ssd_a_log, ssd_d, ssd_norm_g, rwkv_mu, rwkv_w0, rwkv_w2, rwkv_a0, rwkv_a2, rwkv_g2, rwkv_k_k, rwkv_k_a, rwkv_r_k, rwkv_lnx_g, rwkv_lnx_b, even_w_out, odd_w_qkv, odd_w_out, ln_mix_g, ln_mix_b, xa_wq, xa_wkv, xa_wo, ln_xa_g, ln_xa_b, ffn_w13, ffn_w2, ln_ffn_g, ln_ffn_b):
    bsz, s, d = x.shape
    m = bsz * s
    tm = _row_tile(s)
    ssd_width = ssd_norm_g.shape[-1]
    ssd_heads = ssd_dt_bias.shape[-1]
    ssd_xbc = ssd_conv_b.shape[-1]
    ssd_in = ssd_width + ssd_xbc + ssd_heads
    rwkv_width = rwkv_w0.shape[-1]
    mem2 = mem.reshape(bsz * mem.shape[1], d)
    x2 = x.reshape(m, d)
    for layer in range(DEPTH):
        j = layer // 2
        if layer % 2 == 0:
            w_in = even_w_in[j].astype(BF16)
            w_z = w_in[:, :ssd_width]
            w_xbc = w_in[:, ssd_width:ssd_width + ssd_xbc]
            w_dt = jnp.pad(w_in[:, ssd_width + ssd_xbc:ssd_in], ((0, 0), (0, LANES - ssd_heads)))
            w_rw = w_in[:, ssd_in:]
            z, xbc, dt_pad = _matmul(x2, [w_z, w_xbc, w_dt], tm=tm)
            ssd_ops = _ssd_operands(z.reshape(bsz, s, -1), xbc.reshape(bsz, s, -1), dt_pad.reshape(bsz, s, -1),
                                    ssd_conv_w[j], ssd_conv_b[j], ssd_dt_bias[j], ssd_a_log[j], ssd_d[j],
                                    ssd_norm_g[j])
            rwkv_ops = _rwkv_operands(x2.reshape(bsz, s, d), w_rw, rwkv_mu[j], rwkv_w0[j], rwkv_w2[j],
                                      rwkv_a0[j], rwkv_a2[j], rwkv_g2[j], rwkv_k_k[j], rwkv_k_a[j], rwkv_r_k[j],
                                      rwkv_lnx_g[j], rwkv_lnx_b[j], tm=min(tm, 256))
            y_ssd, y_rwkv = _ssd_rwkv_mixers(ssd_ops, rwkv_ops, bsz, s, ssd_width, rwkv_width)
            x2 = _matmul_residual_ln([y_ssd.reshape(m, -1), y_rwkv.reshape(m, -1)],
                                     even_w_out[j].astype(BF16), x2,
                                     ln_mix_g[layer], ln_mix_b[layer], tm=tm)
        else:
            heads = d // HEAD_DIM
            qkv, = _matmul(x2, [odd_w_qkv[j].astype(BF16)], tm=tm)
            attn = _moba_attention(qkv, bsz, s, heads)
            x2 = _matmul_residual_ln([attn], odd_w_out[j].astype(BF16), x2,
                                     ln_mix_g[layer], ln_mix_b[layer], tm=tm)
        kv, = _matmul(mem2, [xa_wkv[layer].astype(BF16)], tm=_row_tile(mem2.shape[0]))
        x3 = _cross_attention_ln(x2.reshape(bsz, s, d), kv.reshape(bsz, -1, 2 * d),
                                 xa_wq[layer].astype(BF16), xa_wo[layer].astype(BF16),
                                 ln_xa_g[layer], ln_xa_b[layer], tm=tm)
        x2 = x3.reshape(m, d)
        h = _swiglu_up(x2, ffn_w13[layer].astype(BF16), tm=tm)
        x2 = _matmul_residual_ln([h], ffn_w2[layer].astype(BF16), x2,
                                 ln_ffn_g[layer], ln_ffn_b[layer], tm=tm)
    return x2.reshape(bsz, s, d)
```

```python
import functools
import math

import jax
import jax.numpy as jnp
from jax import lax
from jax.experimental import pallas as pl
from jax.experimental.pallas import tpu as pltpu

F32 = jnp.float32
BF16 = jnp.bfloat16

HEAD_DIM = 64
LANES = 128
SUBLANES = 8
SSD_GROUPS = 2
SSD_STATE = 128
SSD_CONV = 4
SSD_CHUNK = 128
RWKV_DECAY_LORA = 64
RWKV_ICLR_LORA = 64
RWKV_GATE_LORA = 128
RWKV_CHUNK = 64
MOBA_BLOCK = 256
MOBA_TOPK = 3
MOBA_GROUP = 4
MOBA_LAG_LIMIT = 64.0
MOBA_PAIRS_PER_STEP = 2
XATTN_HEADS = 4
DEPTH = 2
DEEPNORM_ALPHA = (2 * DEPTH) ** 0.25
LN_EPS = 1e-5
RMS_EPS = 1e-5
RWKV_LNX_EPS = 64e-5
NEG_BIG = -1e30
LOG2_E = math.log2(math.e)
VMEM_LIMIT = 56 * 1024 * 1024


def _params(*sem):
    return pltpu.CompilerParams(dimension_semantics=sem, vmem_limit_bytes=VMEM_LIMIT)


def _dot(a, b):
    return jnp.dot(a.astype(BF16), b.astype(BF16), preferred_element_type=F32)


def _dot_nt(a, b):
    return lax.dot_general(a.astype(BF16), b.astype(BF16), (((1,), (1,)), ((), ())),
                           preferred_element_type=F32)


def _dot_tn(a, b):
    return lax.dot_general(a.astype(BF16), b.astype(BF16), (((0,), (0,)), ((), ())),
                           preferred_element_type=F32)


def _split3(x):
    hi = x.astype(BF16)
    r1 = x - hi.astype(F32)
    mid = r1.astype(BF16)
    lo = (r1 - mid.astype(F32)).astype(BF16)
    return hi, mid, lo


def _dot_exact_lhs(m, x):
    hi, mid, lo = _split3(x)
    m = m.astype(BF16)
    return (jnp.dot(m, hi, preferred_element_type=F32) + jnp.dot(m, mid, preferred_element_type=F32)
            + jnp.dot(m, lo, preferred_element_type=F32))


def _dot_exact_rhs(x, m):
    hi, mid, lo = _split3(x)
    m = m.astype(BF16)
    return (jnp.dot(hi, m, preferred_element_type=F32) + jnp.dot(mid, m, preferred_element_type=F32)
            + jnp.dot(lo, m, preferred_element_type=F32))


def _sigmoid(x):
    return 1.0 / (1.0 + jnp.exp(-x))


def _softplus(x):
    return jnp.maximum(x, 0.0) + jnp.log1p(jnp.exp(-jnp.abs(x)))


def _layer_norm(v, g, b):
    mu = jnp.mean(v, axis=-1, keepdims=True)
    c = v - mu
    var = jnp.mean(c * c, axis=-1, keepdims=True)
    return c * lax.rsqrt(var + LN_EPS) * g + b


def _col_chunks(n, cap=1536):
    width = _col_tile(n, cap)
    return [slice(j, j + width) for j in range(0, n, width)]


def _mm_kernel(n_out, x_ref, *refs):
    x = x_ref[...].astype(BF16)
    for w_ref, o_ref in zip(refs[:n_out], refs[n_out:]):
        for sl in _col_chunks(w_ref.shape[1]):
            o_ref[:, sl] = jnp.dot(x, w_ref[:, sl], preferred_element_type=F32).astype(o_ref.dtype)


def _matmul(x, ws, *, tm, out_dtype=F32):
    m, k = x.shape
    assert m % tm == 0
    return pl.pallas_call(
        functools.partial(_mm_kernel, len(ws)),
        grid=(m // tm,),
        in_specs=[pl.BlockSpec((tm, k), lambda i: (i, 0))] + [pl.BlockSpec(w.shape, lambda i: (0, 0)) for w in ws],
        out_specs=[pl.BlockSpec((tm, w.shape[1]), lambda i: (i, 0)) for w in ws],
        out_shape=[jax.ShapeDtypeStruct((m, w.shape[1]), out_dtype) for w in ws],
        compiler_params=_params("parallel"),
        name="matmul",
    )(x, *ws)


def _mm_res_ln_kernel(n_in, *refs):
    hs = refs[:n_in]
    ws = refs[n_in:2 * n_in]
    res_ref, g_ref, b_ref, o_ref = refs[2 * n_in:]
    acc = _dot(hs[0][...], ws[0][...])
    for h_ref, w_ref in zip(hs[1:], ws[1:]):
        acc = acc + _dot(h_ref[...], w_ref[...])
    o_ref[...] = _layer_norm(DEEPNORM_ALPHA * res_ref[...] + acc, g_ref[...], b_ref[...])


def _matmul_residual_ln(hs, w, res, g, b, *, tm):
    m, d = res.shape
    n_in = len(hs)
    kw = w.shape[0] // n_in
    assert all(h.shape[1] == kw for h in hs)
    in_specs = ([pl.BlockSpec((tm, kw), lambda i: (i, 0)) for _ in hs]
                + [pl.BlockSpec((kw, d), lambda i, j=j: (j, 0)) for j in range(n_in)]
                + [pl.BlockSpec((tm, d), lambda i: (i, 0)),
                   pl.BlockSpec((1, d), lambda i: (0, 0)),
                   pl.BlockSpec((1, d), lambda i: (0, 0))])
    return pl.pallas_call(
        functools.partial(_mm_res_ln_kernel, n_in),
        grid=(m // tm,),
        in_specs=in_specs,
        out_specs=pl.BlockSpec((tm, d), lambda i: (i, 0)),
        out_shape=jax.ShapeDtypeStruct((m, d), F32),
        compiler_params=_params("parallel"),
        name="matmul_residual_ln",
    )(*hs, *([w] * n_in), res, g.reshape(1, d), b.reshape(1, d))


def _swiglu_kernel(x_ref, w13_ref, o_ref):
    x = x_ref[...].astype(BF16)
    n = o_ref.shape[1]
    for sl in _col_chunks(n):
        gate = jnp.dot(x, w13_ref[:, sl], preferred_element_type=F32)
        up = jnp.dot(x, w13_ref[:, slice(n + sl.start, n + sl.stop)], preferred_element_type=F32)
        o_ref[:, sl] = (gate * _sigmoid(gate) * up).astype(o_ref.dtype)


def _swiglu_up(x, w13, *, tm):
    m, k = x.shape
    n = w13.shape[1] // 2
    return pl.pallas_call(
        _swiglu_kernel,
        grid=(m // tm,),
        in_specs=[pl.BlockSpec((tm, k), lambda i: (i, 0)),
                  pl.BlockSpec((k, 2 * n), lambda i: (0, 0))],
        out_specs=pl.BlockSpec((tm, n), lambda i: (i, 0)),
        out_shape=jax.ShapeDtypeStruct((m, n), BF16),
        compiler_params=_params("parallel"),
        name="swiglu_up",
    )(x, w13)


def _xattn_kernel(x_ref, kv_ref, wq_ref, wo_ref, g_ref, b_ref, o_ref):
    x = x_ref[0]
    d = x.shape[-1]
    hd = d // XATTN_HEADS
    q = _dot(x, wq_ref[...])
    kv = kv_ref[0]
    heads = range(XATTN_HEADS)
    ss = [_dot_nt(q[:, h * hd:(h + 1) * hd], kv[:, h * hd:(h + 1) * hd]) * (hd ** -0.5) for h in heads]
    ps = []
    for s in ss:
        p = jnp.exp(s - jnp.max(s, axis=-1, keepdims=True))
        ps.append(p / jnp.sum(p, axis=-1, keepdims=True))
    o = jnp.concatenate([_dot(ps[h], kv[:, d + h * hd:d + (h + 1) * hd]) for h in heads], axis=-1)
    xa = _dot(o, wo_ref[...])
    o_ref[0] = _layer_norm(DEEPNORM_ALPHA * x + xa, g_ref[...], b_ref[...])


def _cross_attention_ln(x, kv, wq, wo, g, b, *, tm):
    bsz, s, d = x.shape
    m = kv.shape[1]
    return pl.pallas_call(
        _xattn_kernel,
        grid=(bsz, s // tm),
        in_specs=[pl.BlockSpec((1, tm, d), lambda bi, i: (bi, i, 0)),
                  pl.BlockSpec((1, m, 2 * d), lambda bi, i: (bi, 0, 0)),
                  pl.BlockSpec((d, d), lambda bi, i: (0, 0)),
                  pl.BlockSpec((d, d), lambda bi, i: (0, 0)),
                  pl.BlockSpec((1, d), lambda bi, i: (0, 0)),
                  pl.BlockSpec((1, d), lambda bi, i: (0, 0))],
        out_specs=pl.BlockSpec((1, tm, d), lambda bi, i: (bi, i, 0)),
        out_shape=jax.ShapeDtypeStruct((bsz, s, d), F32),
        compiler_params=_params("parallel", "parallel"),
        name="cross_attention_ln",
    )(x, kv, wq, wo, g.reshape(1, d), b.reshape(1, d))


def _ssd_chunk(c, z_ref, xbc_ref, xbcp_ref, dt_ref, dtt_ref, cw_ref, cb_ref, dtb_ref, dtbt_ref,
               aneg_ref, anegt_ref, dskip_ref, ng_ref, y_ref, state_ref, ext_ref):
    q = SSD_CHUNK
    width = z_ref.shape[-1]
    n_pairs = width // LANES

    ext_ref[0:SUBLANES, :] = jnp.where(c > 0, xbcp_ref[0], 0.0)
    ext_ref[SUBLANES:SUBLANES + q, :] = xbc_ref[0]
    xcs = []
    for sl in _col_chunks(ext_ref.shape[1], 256):
        conv = cb_ref[:, sl] + cw_ref[SSD_CONV - 1:SSD_CONV, sl] * ext_ref[SUBLANES:SUBLANES + q, sl]
        for k in range(SSD_CONV - 1):
            off = SUBLANES - (SSD_CONV - 1) + k
            conv = conv + cw_ref[k:k + 1, sl] * ext_ref[off:off + q, sl]
        xcs.append(conv * _sigmoid(conv))
        yield
    xc = jnp.concatenate(xcs, axis=1)
    xs = xc[:, :width]
    gn = SSD_GROUPS * SSD_STATE
    bm = xc[:, width:width + gn]
    cm = xc[:, width + gn:width + 2 * gn]

    dt = _softplus(dt_ref[0] + dtb_ref[...])
    a = dt * aneg_ref[...]
    dtt = _softplus(dtt_ref[0] + dtbt_ref[...])
    at = dtt * anegt_ref[...]
    row = lax.broadcasted_iota(jnp.int32, (q, q), 0)
    col = lax.broadcasted_iota(jnp.int32, (q, q), 1)
    causal = row >= col
    tri = jnp.where(causal, 1.0, 0.0)
    a_cum = _dot_exact_lhs(tri, a)
    a_cumt = _dot_exact_rhs(at, jnp.where(row <= col, 1.0, 0.0))
    yield

    lane = lax.broadcasted_iota(jnp.int32, (1, LANES), 1)
    lane_lo = lane < HEAD_DIM
    rowp = lax.broadcasted_iota(jnp.int32, (LANES, 1), 0)
    pairs_per_group = n_pairs // SSD_GROUPS
    ys = []
    for p in range(n_pairs):
        g = p // pairs_per_group
        h0, h1 = 2 * p, 2 * p + 1
        bg = bm[:, g * SSD_STATE:(g + 1) * SSD_STATE]
        cg = cm[:, g * SSD_STATE:(g + 1) * SSD_STATE]
        cb = _dot_nt(cg, bg)
        xs_p = xs[:, p * LANES:(p + 1) * LANES]
        dt_p = jnp.where(lane_lo, dt[:, h0:h0 + 1], dt[:, h1:h1 + 1])
        acum_p = jnp.where(lane_lo, a_cum[:, h0:h0 + 1], a_cum[:, h1:h1 + 1])
        xdt = xs_p * dt_p
        ms = []
        for h in (h0, h1):
            seg = a_cum[:, h:h + 1] - a_cumt[h:h + 1, :]
            ms.append(cb * jnp.exp(jnp.where(causal, seg, NEG_BIG)))
        m2 = jnp.concatenate(ms, axis=1)
        x2 = jnp.concatenate([jnp.where(lane_lo, xdt, 0.0), jnp.where(lane_lo, 0.0, xdt)], axis=0)
        y_diag = _dot(m2, x2)
        prev = state_ref[p]
        y_off = _dot_nt(cg, prev) * jnp.exp(acum_p)
        a_last = jnp.where(lane_lo, a_cum[q - 1:q, h0:h0 + 1], a_cum[q - 1:q, h1:h1 + 1])
        xdw = xdt * jnp.exp(a_last - acum_p)
        st = _dot_tn(xdw, bg)
        cd = jnp.where(rowp < HEAD_DIM, jnp.exp(a_cumt[h0:h0 + 1, q - 1:q]),
                       jnp.exp(a_cumt[h1:h1 + 1, q - 1:q]))
        state_ref[p] = prev * cd + st
        d_p = dskip_ref[:, p * LANES:(p + 1) * LANES]
        ys.append(y_diag + y_off + d_p * xs_p)
        yield
    y = jnp.concatenate(ys, axis=1)
    z = z_ref[0]
    y = y * (z * _sigmoid(z))
    gw = width // SSD_GROUPS
    outs = []
    for g in range(SSD_GROUPS):
        yg = y[:, g * gw:(g + 1) * gw]
        outs.append(yg * lax.rsqrt(jnp.mean(yg * yg, axis=-1, keepdims=True) + RMS_EPS))
    y_ref[0] = (jnp.concatenate(outs, axis=1) * ng_ref[...]).astype(y_ref.dtype)


def _ssd_operands(z, xbc, dt_pad, conv_w, conv_b, dt_bias, a_log, d_skip, norm_g):
    bsz, l, width = z.shape
    heads = width // HEAD_DIM
    xw = xbc.shape[-1]
    q = SSD_CHUNK
    dtt = jnp.swapaxes(dt_pad[:, :, :heads], 1, 2)
    pad = LANES - heads
    dtb = jnp.pad(dt_bias, (0, pad)).reshape(1, LANES)
    a_neg = -jnp.exp(a_log.astype(F32))
    aneg = jnp.pad(a_neg, (0, pad)).reshape(1, LANES)
    dskip = jnp.repeat(d_skip, HEAD_DIM).reshape(1, width)
    row = lambda n: pl.BlockSpec((1, n), lambda bi, c: (0, 0))
    specs = [pl.BlockSpec((1, q, width), lambda bi, c: (bi, c, 0)),
             pl.BlockSpec((1, q, xw), lambda bi, c: (bi, c, 0)),
             pl.BlockSpec((1, SUBLANES, xw), lambda bi, c: (bi, jnp.maximum(c * (q // SUBLANES) - 1, 0), 0)),
             pl.BlockSpec((1, q, LANES), lambda bi, c: (bi, c, 0)),
             pl.BlockSpec((1, heads, q), lambda bi, c: (bi, 0, c)),
             pl.BlockSpec((SSD_CONV, xw), lambda bi, c: (0, 0)),
             row(xw), row(LANES),
             pl.BlockSpec((heads, 1), lambda bi, c: (0, 0)),
             row(LANES),
             pl.BlockSpec((heads, 1), lambda bi, c: (0, 0)),
             row(width), row(width)]
    args = [z, xbc, xbc, dt_pad, dtt, conv_w, conv_b.reshape(1, xw), dtb, dt_bias.reshape(heads, 1),
            aneg, a_neg.reshape(heads, 1), dskip, norm_g.reshape(1, width)]
    scratch = [pltpu.VMEM((width // LANES, LANES, SSD_STATE), F32), pltpu.VMEM((q + SUBLANES, xw), F32)]
    return args, specs, scratch


def _rwkv_prep_kernel(x_ref, xp_ref, win_ref, mu_ref, w0_ref, a0_ref, kk_ref, ka_ref, w2_ref, a2_ref, g2_ref,
                      e_ref, et_ref, r_ref, lw_ref, k_ref, v_ref, an_ref, bn_ref, g_ref, ext_ref):
    tm = x_ref.shape[1]
    width = r_ref.shape[-1]
    i = pl.program_id(1)
    xe = jnp.concatenate([jnp.where(i > 0, xp_ref[0], 0.0), x_ref[0]], axis=0)
    for sl in _col_chunks(ext_ref.shape[1], 1664):
        ext_ref[:, sl] = _dot(xe, win_ref[:, sl])
    rw = ext_ref[SUBLANES:SUBLANES + tm, :]
    shifted = ext_ref[SUBLANES - 1:SUBLANES - 1 + tm, :]
    s = rw + (shifted - rw) * mu_ref[...]
    r = s[:, :width]
    k = s[:, width:2 * width]
    v = s[:, 2 * width:3 * width]
    lo = s[:, 3 * width:3 * width + RWKV_DECAY_LORA + RWKV_ICLR_LORA]
    g_lo = s[:, 3 * width + RWKV_DECAY_LORA + RWKV_ICLR_LORA:]
    wv = w0_ref[...] + _dot(jnp.tanh(lo), w2_ref[...])
    w = -_softplus(-wv) - 0.5
    av = _sigmoid(a0_ref[...] + _dot(lo, a2_ref[...]))
    g = _dot(_sigmoid(g_lo), g2_ref[...])
    kkr = k * kk_ref[...]
    ss = _dot_exact_rhs(kkr * kkr, e_ref[...])
    inv = lax.rsqrt(jnp.maximum(ss, 1e-24))
    kk = kkr * _dot_exact_rhs(inv, et_ref[...])
    r_ref[0] = r.astype(r_ref.dtype)
    lw_ref[0] = -jnp.exp(w)
    k_ref[0] = (k * (1.0 + (av - 1.0) * ka_ref[...])).astype(k_ref.dtype)
    v_ref[0] = v.astype(v_ref.dtype)
    an_ref[0] = (-kk).astype(an_ref.dtype)
    bn_ref[0] = (kk * av).astype(bn_ref.dtype)
    g_ref[0] = g.astype(g_ref.dtype)


def _rwkv_chunk(rows, fill, r_ref, lw_ref, k_ref, v_ref, an_ref, bn_ref, g_ref, rk_ref, lg_ref, lb_ref,
                y_ref, state_ref):
    c = RWKV_CHUNK
    c2 = 2 * c
    width = r_ref.shape[-1]
    n_pairs = width // LANES

    row = lax.broadcasted_iota(jnp.int32, (c, c), 0)
    col = lax.broadcasted_iota(jnp.int32, (c, c), 1)
    tri = jnp.where(row >= col, 1.0, 0.0)
    roww = lax.broadcasted_iota(jnp.int32, (c, c2), 0)
    colw = lax.broadcasted_iota(jnp.int32, (c, c2), 1) % c
    strict = jnp.where(roww > colw, 1.0, 0.0)
    incl = jnp.where(roww >= colw, 1.0, 0.0)
    eye = jnp.where(roww == colw, 1.0, 0.0)
    same_head = jnp.where((lax.broadcasted_iota(jnp.int32, (LANES, LANES), 0) // HEAD_DIM)
                          == (lax.broadcasted_iota(jnp.int32, (LANES, LANES), 1) // HEAD_DIM), 1.0, 0.0)
    lane = lax.broadcasted_iota(jnp.int32, (1, LANES), 1)
    m0 = jnp.where(lane < HEAD_DIM, 1.0, 0.0)
    m1 = 1.0 - m0

    lane_lo2 = lax.broadcasted_iota(jnp.int32, (c, LANES), 1) < HEAD_DIM

    def stack(x):
        xb = x.astype(BF16)
        zero = jnp.zeros_like(xb)
        return jnp.concatenate([jnp.where(lane_lo2, xb, zero), jnp.where(lane_lo2, zero, xb)], axis=0)

    sls = [slice(p * LANES, (p + 1) * LANES) for p in range(n_pairs)]
    lhs2s, rhs4s, vs, v_stks, bks, g_lasts = [], [], [], [], [], []
    for sl in sls:
        r = r_ref[0, rows, sl].astype(F32)
        lw = lw_ref[0, rows, sl]
        k = k_ref[0, rows, sl].astype(F32)
        b = bn_ref[0, rows, sl].astype(F32)
        v = v_ref[0, rows, sl].astype(F32)
        cum = _dot_exact_lhs(tri, lw)
        cum_last = cum[c - 1:c, :]
        e_neg = jnp.exp(-cum)
        e_tail = jnp.exp(cum_last - cum)
        at = an_ref[0, rows, sl].astype(F32) * jnp.exp(cum - lw)
        rt = r * jnp.exp(cum)
        lhs2s.append(jnp.concatenate([at, rt], axis=0).astype(BF16))
        rhs4s.append(jnp.concatenate([stack(b * e_neg), stack(k * e_neg)], axis=0))
        vs.append(v.astype(BF16))
        v_stks.append(stack(v))
        bks.append(jnp.concatenate([b * e_tail, k * e_tail], axis=0).astype(BF16))
        g_lasts.append(jnp.exp(cum_last))
    gms = [_dot_nt(l, rh) for l, rh in zip(lhs2s, rhs4s)]
    pws = [gm[:c, :c2] * strict for gm in gms]
    a_aks = [(gm[:c, c2:] * strict).astype(BF16) for gm in gms]
    a_rbks = [jnp.concatenate([gm[c:, :c2] * incl, gm[c:, c2:] * incl], axis=1).astype(BF16) for gm in gms]
    akvs = [_dot(a_ak, v_stk) for a_ak, v_stk in zip(a_aks, v_stks)]
    fill()
    tinvs = [eye + pw for pw in pws]
    pw_stks = [stack(pw) for pw in pws]
    for _ in range(int(math.log2(c)) - 1):
        pws = [_dot(pw, pw_stk) for pw, pw_stk in zip(pws, pw_stks)]
        pw_stks = [stack(pw) for pw in pws]
        tinvs = [tinv + _dot(tinv, pw_stk) for tinv, pw_stk in zip(tinvs, pw_stks)]
        fill()
    sts = [state_ref[p] for p in range(n_pairs)]
    ahrhs = [_dot_nt(l, st) for l, st in zip(lhs2s, sts)]
    fill()
    us = [_dot(tinv, stack(ahrh[:c] + akv)) for tinv, ahrh, akv in zip(tinvs, ahrhs, akvs)]
    fill()
    for p in range(n_pairs):
        uv = jnp.concatenate([us[p].astype(BF16), vs[p]], axis=0)
        state_ref[p] = sts[p] * g_lasts[p] + same_head * _dot_tn(uv, bks[p])
    fill()
    ys = [ahrh[c:] + _dot(a_rbk, jnp.concatenate([stack(u), v_stk], axis=0))
          for ahrh, a_rbk, u, v_stk in zip(ahrhs, a_rbks, us, v_stks)]
    lane_lo = lane < HEAD_DIM

    def head_sum(x):
        lo = jnp.sum(x * m0, axis=-1, keepdims=True)
        hi = jnp.sum(x * m1, axis=-1, keepdims=True)
        return jnp.where(lane_lo, lo, hi)

    bonus = [head_sum(r_ref[0, rows, sl].astype(F32) * k_ref[0, rows, sl].astype(F32) * rk_ref[:, sl]) for sl in sls]
    ycs = [y - head_sum(y) * (1.0 / HEAD_DIM) for y in ys]
    yvs = [head_sum(yc * yc) * (1.0 / HEAD_DIM) for yc in ycs]
    for p, sl in enumerate(sls):
        yn = ycs[p] * lax.rsqrt(yvs[p] + RWKV_LNX_EPS) * lg_ref[:, sl] + lb_ref[:, sl]
        out = (yn + bonus[p] * v_ref[0, rows, sl].astype(F32)) * g_ref[0, rows, sl].astype(F32)
        y_ref[0, rows, sl] = out.astype(y_ref.dtype)


def _rwkv_operands(x, w_in, mu, w0, w2, a0, a2, g2, k_k, k_a, r_k, lnx_g, lnx_b, *, tm):
    bsz, l, d = x.shape
    win = w_in.shape[1]
    width = w0.shape[0]
    w2p = jnp.concatenate([w2, jnp.zeros((RWKV_ICLR_LORA, width), F32)], axis=0).astype(BF16)
    a2p = jnp.concatenate([jnp.zeros((RWKV_DECAY_LORA, width), F32), a2], axis=0).astype(BF16)
    head_of = jnp.arange(width) // HEAD_DIM
    e = (head_of[:, None] == jnp.arange(LANES)[None, :]).astype(BF16)
    et = e.T
    vec = lambda x: x.reshape(1, -1)
    row = lambda n: pl.BlockSpec((1, n), lambda bi, i: (0, 0))
    full = lambda a: pl.BlockSpec(a.shape, lambda bi, i: (0, 0))
    tile = pl.BlockSpec((1, tm, width), lambda bi, i: (bi, i, 0))
    sds = lambda dt: jax.ShapeDtypeStruct((bsz, l, width), dt)
    g2b = g2.astype(BF16)
    r, lw, k, v, an, bn, g = pl.pallas_call(
        _rwkv_prep_kernel,
        grid=(bsz, l // tm),
        in_specs=[pl.BlockSpec((1, tm, d), lambda bi, i: (bi, i, 0)),
                  pl.BlockSpec((1, SUBLANES, d), lambda bi, i: (bi, jnp.maximum(i * (tm // SUBLANES) - 1, 0), 0)),
                  full(w_in), row(win), row(width), row(width), row(width), row(width),
                  full(w2p), full(a2p), full(g2b), full(e), full(et)],
        out_specs=[tile] * 7,
        out_shape=[sds(F32)] * 7,
        scratch_shapes=[pltpu.VMEM((tm + SUBLANES, win), F32)],
        compiler_params=_params("parallel", "parallel"),
        name="rwkv_prep",
    )(x, x, w_in, vec(mu), vec(w0), vec(a0), vec(k_k), vec(k_a), w2p, a2p, g2b, e, et)
    ctile = pl.BlockSpec((1, SSD_CHUNK, width), lambda bi, ci: (bi, ci, 0))
    crow = pl.BlockSpec((1, width), lambda bi, ci: (0, 0))
    args = [r, lw, k, v, an, bn, g, vec(r_k), vec(lnx_g), vec(lnx_b)]
    scratch = [pltpu.VMEM((width // LANES, LANES, LANES), F32)]
    return args, [ctile] * 7 + [crow] * 3, scratch


def _mixers_kernel(n_ssd, n_rwkv, *refs):
    ssd_in = refs[:n_ssd]
    rwkv_in = refs[n_ssd:n_ssd + n_rwkv]
    y_ssd_ref, y_rwkv_ref, ssd_state, ssd_ext, rwkv_state = refs[n_ssd + n_rwkv:]
    c = pl.program_id(1)

    @pl.when(c == 0)
    def _():
        ssd_state[...] = jnp.zeros_like(ssd_state)
        rwkv_state[...] = jnp.zeros_like(rwkv_state)

    ssd_steps = _ssd_chunk(c, *ssd_in, y_ssd_ref, ssd_state, ssd_ext)
    fill = lambda: next(ssd_steps, None)
    for sub in range(SSD_CHUNK // RWKV_CHUNK):
        _rwkv_chunk(slice(sub * RWKV_CHUNK, (sub + 1) * RWKV_CHUNK), fill, *rwkv_in, y_rwkv_ref, rwkv_state)
    for _ in ssd_steps:
        pass


def _ssd_rwkv_mixers(ssd_ops, rwkv_ops, bsz, l, ssd_width, rwkv_width):
    ssd_args, ssd_specs, ssd_scratch = ssd_ops
    rwkv_args, rwkv_specs, rwkv_scratch = rwkv_ops
    q = SSD_CHUNK
    out_spec = lambda w: pl.BlockSpec((1, q, w), lambda bi, c: (bi, c, 0))
    return pl.pallas_call(
        functools.partial(_mixers_kernel, len(ssd_args), len(rwkv_args)),
        grid=(bsz, l // q),
        in_specs=ssd_specs + rwkv_specs,
        out_specs=[out_spec(ssd_width), out_spec(rwkv_width)],
        out_shape=[jax.ShapeDtypeStruct((bsz, l, ssd_width), BF16), jax.ShapeDtypeStruct((bsz, l, rwkv_width), BF16)],
        scratch_shapes=ssd_scratch + rwkv_scratch,
        compiler_params=_params("parallel", "arbitrary"),
        name="ssd_rwkv_mixers",
    )(*ssd_args, *rwkv_args)


def _moba_kernel(q_ref, qn_ref, k_ref, v_ref, o_ref, kb_ref, vt_ref, kmean_ref, qo_ref, qp_ref):
    blk = MOBA_BLOCK
    half = HEAD_DIM
    nb = k_ref.shape[0] // blk
    nbp = kmean_ref.shape[0]
    npair = q_ref.shape[1] // LANES
    heads = range(2 * npair)
    group = math.gcd(nb, MOBA_GROUP)
    qi = pl.program_id(2)

    @pl.when(qi == 0)
    def _():
        lane = lax.broadcasted_iota(jnp.int32, (blk, LANES), 1)
        rowp = lax.broadcasted_iota(jnp.int32, (LANES, blk), 0)
        if nbp > nb:
            kmean_ref[...] = jnp.zeros_like(kmean_ref)
        for n in range(nb):
            rows = slice(n * blk, (n + 1) * blk)
            kmean_ref[n:n + 1, :] = jnp.mean(k_ref[rows, :], axis=0, keepdims=True)
            for pp in range(npair):
                cols = slice(pp * LANES, (pp + 1) * LANES)
                kn = k_ref[rows, cols]
                kb_ref[2 * pp, n] = jnp.where(lane < half, kn, jnp.where(lane == half + n, 1.0, 0.0)).astype(BF16)
                kb_ref[2 * pp + 1, n] = jnp.where(lane >= half, kn, jnp.where(lane == n, 1.0, 0.0)).astype(BF16)
                vtn = v_ref[rows, cols].T
                vt_ref[2 * pp, n] = jnp.where(rowp < half, vtn, jnp.where(rowp == half, 1.0, 0.0)).astype(BF16)
                vt_ref[2 * pp + 1, n] = jnp.where(rowp >= half, vtn, jnp.where(rowp == 0, 1.0, 0.0)).astype(BF16)

    rown = lax.broadcasted_iota(jnp.int32, (nbp, blk), 0)
    lane_k = lax.broadcasted_iota(jnp.int32, (nbp, LANES), 1)
    zeros = jnp.zeros((half, blk), F32)

    def prepare_queries(src_ref, tile):
        qts = [src_ref[:, pp * LANES:(pp + 1) * LANES].T for pp in range(npair)]
        gates, qhs = [], []
        for hh in heads:
            pp, h = divmod(hh, 2)
            km = kmean_ref[:, pp * LANES:(pp + 1) * LANES]
            kmh = jnp.where((lane_k < half) if h == 0 else (lane_k >= half), km, 0.0)
            gates.append(jnp.dot(kmh, qts[pp], preferred_element_type=F32, precision=lax.Precision.HIGHEST))
            qhs.append(qts[pp][h * half:(h + 1) * half] * (half ** -0.5 * LOG2_E))
        for hh in heads:
            gate = jnp.where(rown < tile, gates[hh], -jnp.inf)
            bias = jnp.full((nbp, blk), NEG_BIG, F32)
            for _ in range(MOBA_TOPK):
                mx = jnp.max(gate, axis=0, keepdims=True)
                first = jnp.min(jnp.where(gate == mx, rown, nbp), axis=0, keepdims=True)
                pick = (rown == first) & (mx > -jnp.inf)
                bias = jnp.where(pick, 0.0, bias)
                gate = jnp.where(pick, -jnp.inf, gate)
            aug = jnp.concatenate([bias, jnp.zeros((half - nbp, blk), F32)], axis=0)
            qo_ref[hh] = jnp.concatenate([qhs[hh], zeros] if hh % 2 == 0 else [zeros, qhs[hh]],
                                         axis=0).astype(BF16)
            qp_ref[hh] = jnp.concatenate([qhs[hh], aug] if hh % 2 == 0 else [aug, qhs[hh]], axis=0).astype(BF16)

    @pl.when(qi == 0)
    def _():
        prepare_queries(q_ref, 0)

    q_own = [qo_ref[h] for h in heads]
    q_past = [qp_ref[h] for h in heads]
    s_own = [jnp.dot(kb_ref[h, qi], q_own[h], preferred_element_type=F32) for h in heads]
    prepare_queries(qn_ref, qi + 1)

    causal = (lax.broadcasted_iota(jnp.int32, (blk, blk), 0) <= lax.broadcasted_iota(jnp.int32, (blk, blk), 1))
    ms, ps = [], []
    for h in heads:
        s = jnp.where(causal, s_own[h], NEG_BIG)
        ms.append(jnp.max(s, axis=0, keepdims=True))
        ps.append(jnp.exp2(s - ms[h]).astype(BF16))
    carry = []
    for h in heads:
        carry += [ms[h], jnp.dot(vt_ref[h, qi], ps[h], preferred_element_type=F32)]

    def scores(gi, h):
        return [jnp.dot(kb_ref[h, gi * group + g], q_past[h], preferred_element_type=F32) for g in range(group)]

    def values(gi, h):
        return jnp.concatenate([vt_ref[h, gi * group + g] for g in range(group)], axis=1)

    def body_lagged(gi, carry):
        excess = carry[-1]
        out = []
        sss = [scores(gi, h) for h in heads]
        pcats, gmaxs = [], []
        for h in heads:
            m_run = carry[2 * h]
            gmax = None
            ps = []
            for s in sss[h]:
                cm = jnp.max(s, axis=0, keepdims=True)
                gmax = cm if gmax is None else jnp.maximum(gmax, cm)
                ps.append(jnp.exp2(s - m_run).astype(BF16))
            pcats.append(jnp.concatenate(ps, axis=0))
            gmaxs.append(gmax)
        for h in heads:
            m_run, acc = carry[2 * h], carry[2 * h + 1]
            m_new = jnp.maximum(m_run, gmaxs[h])
            excess = jnp.maximum(excess, gmaxs[h] - m_run)
            acc = jnp.exp2(m_run - m_new) * (acc + jnp.dot(values(gi, h), pcats[h], preferred_element_type=F32))
            out += [m_new, acc]
        return tuple(out) + (excess,)

    def body_exact_max(gi, carry):
        sss = [scores(gi, h) for h in heads]
        m_news = []
        for h in heads:
            m_new = carry[2 * h]
            for s in sss[h]:
                m_new = jnp.maximum(m_new, jnp.max(s, axis=0, keepdims=True))
            m_news.append(m_new)
        pcats = [jnp.concatenate([jnp.exp2(s - m_news[h]).astype(BF16) for s in sss[h]], axis=0)
                 for h in heads]
        out = []
        for h in heads:
            alpha = jnp.exp2(carry[2 * h] - m_news[h])
            out += [m_news[h], alpha * carry[2 * h + 1]
                    + jnp.dot(values(gi, h), pcats[h], preferred_element_type=F32)]
        return tuple(out)

    rowq = lax.broadcasted_iota(jnp.int32, (LANES, blk), 0)

    def write_out(final):
        for pp in range(npair):
            acc0, acc1 = final[4 * pp + 1], final[4 * pp + 3]
            out_t = jnp.where(rowq < half, acc0 / acc0[half:half + 1], acc1 / acc1[0:1])
            o_ref[:, pp * LANES:(pp + 1) * LANES] = out_t.T.astype(o_ref.dtype)

    n_groups = (qi + group - 1) // group
    final = lax.fori_loop(0, n_groups, body_lagged, tuple(carry) + (jnp.full((1, blk), NEG_BIG, F32),))
    write_out(final)

    @pl.when(jnp.max(final[-1]) > MOBA_LAG_LIMIT)
    def _():
        write_out(lax.fori_loop(0, n_groups, body_exact_max, tuple(carry)))


def _moba_attention(qkv, bsz, s, heads):
    blk = MOBA_BLOCK
    assert s % blk == 0 and (heads * HEAD_DIM) % LANES == 0
    nb = s // blk
    assert nb <= HEAD_DIM
    nbp = -(-nb // SUBLANES) * SUBLANES
    pairs = heads * HEAD_DIM // LANES
    pps = math.gcd(pairs, MOBA_PAIRS_PER_STEP)
    cw = pps * LANES
    steps = pairs // pps
    return pl.pallas_call(
        _moba_kernel,
        grid=(bsz, steps, nb),
        in_specs=[pl.BlockSpec((blk, cw), lambda b, p, i: (b * nb + i, p)),
                  pl.BlockSpec((blk, cw), lambda b, p, i: (b * nb + jnp.minimum(i + 1, nb - 1), p)),
                  pl.BlockSpec((s, cw), lambda b, p, i: (b, steps + p)),
                  pl.BlockSpec((s, cw), lambda b, p, i: (b, 2 * steps + p))],
        out_specs=pl.BlockSpec((blk, cw), lambda b, p, i: (b * nb + i, p)),
        out_shape=jax.ShapeDtypeStruct((bsz * s, heads * HEAD_DIM), BF16),
        scratch_shapes=[pltpu.VMEM((2 * pps, nb, blk, LANES), BF16), pltpu.VMEM((2 * pps, nb, LANES, blk), BF16),
                        pltpu.VMEM((nbp, cw), F32),
                        pltpu.VMEM((2 * pps, LANES, blk), BF16), pltpu.VMEM((2 * pps, LANES, blk), BF16)],
        compiler_params=_params("parallel", "parallel", "arbitrary"),
        name="moba_attention",
    )(qkv, qkv, qkv, qkv)


def _row_tile(m):
    for t in (512, 256, 128, 64, 32, 16, 8):
        if m % t == 0:
            return t
    raise ValueError(f"row count {m} is not a multiple of 8")


def _col_tile(n, cap=2048):
    best = None
    for t in range(LANES, min(n, cap) + 1, LANES):
        if n % t == 0:
            best = t
    if best is None:
        raise ValueError(f"column count {n} is not a multiple of {LANES}")
    return best


def kernel(x, mem, even_w_in, ssd_conv_w, ssd_conv_b, ssd_dt_bias, ssd_a_log, ssd_d, ssd_norm_g, rwkv_mu, rwkv_w0, rwkv_w2, rwkv_a0, rwkv_a2, rwkv_g2, rwkv_k_k, rwkv_k_a, rwkv_r_k, rwkv_lnx_g, rwkv_lnx_b, even_w_out, odd_w_qkv, odd_w_out, ln_mix_g, ln_mix_b, xa_wq, xa_wkv, xa_wo, ln_xa_g, ln_xa_b, ffn_w13, ffn_w2, ln_ffn_g, ln_ffn_b):
    bsz, s, d = x.shape
    m = bsz * s
    tm = _row_tile(s)
    ssd_width = ssd_norm_g.shape[-1]
    ssd_heads = ssd_dt_bias.shape[-1]
    ssd_xbc = ssd_conv_b.shape[-1]
    ssd_in = ssd_width + ssd_xbc + ssd_heads
    rwkv_width = rwkv_w0.shape[-1]
    mem2 = mem.reshape(bsz * mem.shape[1], d)
    x2 = x.reshape(m, d)
    for layer in range(DEPTH):
        j = layer // 2
        if layer % 2 == 0:
            w_in = even_w_in[j].astype(BF16)
            w_z = w_in[:, :ssd_width]
            w_xbc = w_in[:, ssd_width:ssd_width + ssd_xbc]
            w_dt = jnp.pad(w_in[:, ssd_width + ssd_xbc:ssd_in], ((0, 0), (0, LANES - ssd_heads)))
            w_rw = w_in[:, ssd_in:]
            z, xbc, dt_pad = _matmul(x2, [w_z, w_xbc, w_dt], tm=tm)
            ssd_ops = _ssd_operands(z.reshape(bsz, s, -1), xbc.reshape(bsz, s, -1), dt_pad.reshape(bsz, s, -1),
                                    ssd_conv_w[j], ssd_conv_b[j], ssd_dt_bias[j], ssd_a_log[j], ssd_d[j],
                                    ssd_norm_g[j])
            rwkv_ops = _rwkv_operands(x2.reshape(bsz, s, d), w_rw, rwkv_mu[j], rwkv_w0[j], rwkv_w2[j],
                                      rwkv_a0[j], rwkv_a2[j], rwkv_g2[j], rwkv_k_k[j], rwkv_k_a[j], rwkv_r_k[j],
                                      rwkv_lnx_g[j], rwkv_lnx_b[j], tm=min(tm, 256))
            y_ssd, y_rwkv = _ssd_rwkv_mixers(ssd_ops, rwkv_ops, bsz, s, ssd_width, rwkv_width)
            x2 = _matmul_residual_ln([y_ssd.reshape(m, -1), y_rwkv.reshape(m, -1)],
                                     even_w_out[j].astype(BF16), x2,
                                     ln_mix_g[layer], ln_mix_b[layer], tm=tm)
        else:
            heads = d // HEAD_DIM
            qkv, = _matmul(x2, [odd_w_qkv[j].astype(BF16)], tm=tm)
            attn = _moba_attention(qkv, bsz, s, heads)
            x2 = _matmul_residual_ln([attn], odd_w_out[j].astype(BF16), x2,
                                     ln_mix_g[layer], ln_mix_b[layer], tm=tm)
        kv, = _matmul(mem2, [xa_wkv[layer].astype(BF16)], tm=_row_tile(mem2.shape[0]))
        x3 = _cross_attention_ln(x2.reshape(bsz, s, d), kv.reshape(bsz, -1, 2 * d),
                                 xa_wq[layer].astype(BF16), xa_wo[layer].astype(BF16),
                                 ln_xa_g[layer], ln_xa_b[layer], tm=tm)
        x2 = x3.reshape(m, d)
        h = _swiglu_up(x2, ffn_w13[layer].astype(BF16), tm=tm)
        x2 = _matmul_residual_ln([h], ffn_w2[layer].astype(BF16), x2,
                                 ln_ffn_g[layer], ln_ffn_b[layer], tm=tm)
    return x2.reshape(bsz, s, d)
```

```python
import functools
import math

import jax
import jax.numpy as jnp
from jax import lax
from jax.experimental import pallas as pl
from jax.experimental.pallas import tpu as pltpu

F32 = jnp.float32
BF16 = jnp.bfloat16

HEAD_DIM = 64
LANES = 128
SUBLANES = 8
SSD_GROUPS = 2
SSD_STATE = 128
SSD_CONV = 4
SSD_CHUNK = 128
RWKV_DECAY_LORA = 64
RWKV_ICLR_LORA = 64
RWKV_GATE_LORA = 128
RWKV_CHUNK = 64
MOBA_BLOCK = 256
MOBA_TOPK = 3
MOBA_GROUP = 4
MOBA_LAG_LIMIT = 64.0
MOBA_PAIRS_PER_STEP = 2
XATTN_HEADS = 4
DEPTH = 2
DEEPNORM_ALPHA = (2 * DEPTH) ** 0.25
LN_EPS = 1e-5
RMS_EPS = 1e-5
RWKV_LNX_EPS = 64e-5
NEG_BIG = -1e30
LOG2_E = math.log2(math.e)
VMEM_LIMIT = 56 * 1024 * 1024


def _params(*sem):
    return pltpu.CompilerParams(dimension_semantics=sem, vmem_limit_bytes=VMEM_LIMIT)


def _dot(a, b):
    return jnp.dot(a.astype(BF16), b.astype(BF16), preferred_element_type=F32)


def _dot_nt(a, b):
    return lax.dot_general(a.astype(BF16), b.astype(BF16), (((1,), (1,)), ((), ())),
                           preferred_element_type=F32)


def _dot_tn(a, b):
    return lax.dot_general(a.astype(BF16), b.astype(BF16), (((0,), (0,)), ((), ())),
                           preferred_element_type=F32)


def _split3(x):
    hi = x.astype(BF16)
    r1 = x - hi.astype(F32)
    mid = r1.astype(BF16)
    lo = (r1 - mid.astype(F32)).astype(BF16)
    return hi, mid, lo


def _dot_exact_lhs(m, x):
    hi, mid, lo = _split3(x)
    m = m.astype(BF16)
    return (jnp.dot(m, hi, preferred_element_type=F32) + jnp.dot(m, mid, preferred_element_type=F32)
            + jnp.dot(m, lo, preferred_element_type=F32))


def _dot_exact_rhs(x, m):
    hi, mid, lo = _split3(x)
    m = m.astype(BF16)
    return (jnp.dot(hi, m, preferred_element_type=F32) + jnp.dot(mid, m, preferred_element_type=F32)
            + jnp.dot(lo, m, preferred_element_type=F32))


def _sigmoid(x):
    return 1.0 / (1.0 + jnp.exp(-x))


def _softplus(x):
    return jnp.maximum(x, 0.0) + jnp.log1p(jnp.exp(-jnp.abs(x)))


def _layer_norm(v, g, b):
    mu = jnp.mean(v, axis=-1, keepdims=True)
    c = v - mu
    var = jnp.mean(c * c, axis=-1, keepdims=True)
    return c * lax.rsqrt(var + LN_EPS) * g + b


def _col_chunks(n, cap=1536):
    width = _col_tile(n, cap)
    return [slice(j, j + width) for j in range(0, n, width)]


def _mm_kernel(n_out, x_ref, *refs):
    x = x_ref[...].astype(BF16)
    for w_ref, o_ref in zip(refs[:n_out], refs[n_out:]):
        for sl in _col_chunks(w_ref.shape[1]):
            o_ref[:, sl] = jnp.dot(x, w_ref[:, sl], preferred_element_type=F32).astype(o_ref.dtype)


def _matmul(x, ws, *, tm, out_dtype=F32):
    m, k = x.shape
    assert m % tm == 0
    return pl.pallas_call(
        functools.partial(_mm_kernel, len(ws)),
        grid=(m // tm,),
        in_specs=[pl.BlockSpec((tm, k), lambda i: (i, 0))] + [pl.BlockSpec(w.shape, lambda i: (0, 0)) for w in ws],
        out_specs=[pl.BlockSpec((tm, w.shape[1]), lambda i: (i, 0)) for w in ws],
        out_shape=[jax.ShapeDtypeStruct((m, w.shape[1]), out_dtype) for w in ws],
        compiler_params=_params("parallel"),
        name="matmul",
    )(x, *ws)


def _mm_res_ln_kernel(n_in, *refs):
    hs = refs[:n_in]
    ws = refs[n_in:2 * n_in]
    res_ref, g_ref, b_ref, o_ref = refs[2 * n_in:]
    acc = _dot(hs[0][...], ws[0][...])
    for h_ref, w_ref in zip(hs[1:], ws[1:]):
        acc = acc + _dot(h_ref[...], w_ref[...])
    o_ref[...] = _layer_norm(DEEPNORM_ALPHA * res_ref[...] + acc, g_ref[...], b_ref[...])


def _matmul_residual_ln(hs, w, res, g, b, *, tm):
    m, d = res.shape
    n_in = len(hs)
    kw = w.shape[0] // n_in
    assert all(h.shape[1] == kw for h in hs)
    in_specs = ([pl.BlockSpec((tm, kw), lambda i: (i, 0)) for _ in hs]
                + [pl.BlockSpec((kw, d), lambda i, j=j: (j, 0)) for j in range(n_in)]
                + [pl.BlockSpec((tm, d), lambda i: (i, 0)),
                   pl.BlockSpec((1, d), lambda i: (0, 0)),
                   pl.BlockSpec((1, d), lambda i: (0, 0))])
    return pl.pallas_call(
        functools.partial(_mm_res_ln_kernel, n_in),
        grid=(m // tm,),
        in_specs=in_specs,
        out_specs=pl.BlockSpec((tm, d), lambda i: (i, 0)),
        out_shape=jax.ShapeDtypeStruct((m, d), F32),
        compiler_params=_params("parallel"),
        name="matmul_residual_ln",
    )(*hs, *([w] * n_in), res, g.reshape(1, d), b.reshape(1, d))


def _swiglu_kernel(x_ref, w13_ref, o_ref):
    x = x_ref[...].astype(BF16)
    n = o_ref.shape[1]
    for sl in _col_chunks(n):
        gate = jnp.dot(x, w13_ref[:, sl], preferred_element_type=F32)
        up = jnp.dot(x, w13_ref[:, slice(n + sl.start, n + sl.stop)], preferred_element_type=F32)
        o_ref[:, sl] = (gate * _sigmoid(gate) * up).astype(o_ref.dtype)


def _swiglu_up(x, w13, *, tm):
    m, k = x.shape
    n = w13.shape[1] // 2
    return pl.pallas_call(
        _swiglu_kernel,
        grid=(m // tm,),
        in_specs=[pl.BlockSpec((tm, k), lambda i: (i, 0)),
                  pl.BlockSpec((k, 2 * n), lambda i: (0, 0))],
        out_specs=pl.BlockSpec((tm, n), lambda i: (i, 0)),
        out_shape=jax.ShapeDtypeStruct((m, n), BF16),
        compiler_params=_params("parallel"),
        name="swiglu_up",
    )(x, w13)


def _xattn_kernel(x_ref, kv_ref, wq_ref, wo_ref, g_ref, b_ref, o_ref):
    x = x_ref[0]
    d = x.shape[-1]
    hd = d // XATTN_HEADS
    q = _dot(x, wq_ref[...])
    kv = kv_ref[0]
    heads = range(XATTN_HEADS)
    ss = [_dot_nt(q[:, h * hd:(h + 1) * hd], kv[:, h * hd:(h + 1) * hd]) * (hd ** -0.5) for h in heads]
    ps = []
    for s in ss:
        p = jnp.exp(s - jnp.max(s, axis=-1, keepdims=True))
        ps.append(p / jnp.sum(p, axis=-1, keepdims=True))
    o = jnp.concatenate([_dot(ps[h], kv[:, d + h * hd:d + (h + 1) * hd]) for h in heads], axis=-1)
    xa = _dot(o, wo_ref[...])
    o_ref[0] = _layer_norm(DEEPNORM_ALPHA * x + xa, g_ref[...], b_ref[...])


def _cross_attention_ln(x, kv, wq, wo, g, b, *, tm):
    bsz, s, d = x.shape
    m = kv.shape[1]
    return pl.pallas_call(
        _xattn_kernel,
        grid=(bsz, s // tm),
        in_specs=[pl.BlockSpec((1, tm, d), lambda bi, i: (bi, i, 0)),
                  pl.BlockSpec((1, m, 2 * d), lambda bi, i: (bi, 0, 0)),
                  pl.BlockSpec((d, d), lambda bi, i: (0, 0)),
                  pl.BlockSpec((d, d), lambda bi, i: (0, 0)),
                  pl.BlockSpec((1, d), lambda bi, i: (0, 0)),
                  pl.BlockSpec((1, d), lambda bi, i: (0, 0))],
        out_specs=pl.BlockSpec((1, tm, d), lambda bi, i: (bi, i, 0)),
        out_shape=jax.ShapeDtypeStruct((bsz, s, d), F32),
        compiler_params=_params("parallel", "parallel"),
        name="cross_attention_ln",
    )(x, kv, wq, wo, g.reshape(1, d), b.reshape(1, d))


def _ssd_chunk(c, z_ref, xbc_ref, xbcp_ref, dt_ref, dtt_ref, cw_ref, cb_ref, dtb_ref, dtbt_ref,
               aneg_ref, anegt_ref, dskip_ref, ng_ref, y_ref, state_ref, ext_ref):
    q = SSD_CHUNK
    width = z_ref.shape[-1]
    n_pairs = width // LANES

    ext_ref[0:SUBLANES, :] = jnp.where(c > 0, xbcp_ref[0], 0.0)
    ext_ref[SUBLANES:SUBLANES + q, :] = xbc_ref[0]
    xcs = []
    for sl in _col_chunks(ext_ref.shape[1], 256):
        conv = cb_ref[:, sl] + cw_ref[SSD_CONV - 1:SSD_CONV, sl] * ext_ref[SUBLANES:SUBLANES + q, sl]
        for k in range(SSD_CONV - 1):
            off = SUBLANES - (SSD_CONV - 1) + k
            conv = conv + cw_ref[k:k + 1, sl] * ext_ref[off:off + q, sl]
        xcs.append(conv * _sigmoid(conv))
        yield
    xc = jnp.concatenate(xcs, axis=1)
    xs = xc[:, :width]
    gn = SSD_GROUPS * SSD_STATE
    bm = xc[:, width:width + gn]
    cm = xc[:, width + gn:width + 2 * gn]

    dt = _softplus(dt_ref[0] + dtb_ref[...])
    a = dt * aneg_ref[...]
    dtt = _softplus(dtt_ref[0] + dtbt_ref[...])
    at = dtt * anegt_ref[...]
    row = lax.broadcasted_iota(jnp.int32, (q, q), 0)
    col = lax.broadcasted_iota(jnp.int32, (q, q), 1)
    causal = row >= col
    tri = jnp.where(causal, 1.0, 0.0)
    a_cum = _dot_exact_lhs(tri, a)
    a_cumt = _dot_exact_rhs(at, jnp.where(row <= col, 1.0, 0.0))
    yield

    lane = lax.broadcasted_iota(jnp.int32, (1, LANES), 1)
    lane_lo = lane < HEAD_DIM
    rowp = lax.broadcasted_iota(jnp.int32, (LANES, 1), 0)
    pairs_per_group = n_pairs // SSD_GROUPS
    ys = []
    for p in range(n_pairs):
        g = p // pairs_per_group
        h0, h1 = 2 * p, 2 * p + 1
        bg = bm[:, g * SSD_STATE:(g + 1) * SSD_STATE]
        cg = cm[:, g * SSD_STATE:(g + 1) * SSD_STATE]
        cb = _dot_nt(cg, bg)
        xs_p = xs[:, p * LANES:(p + 1) * LANES]
        dt_p = jnp.where(lane_lo, dt[:, h0:h0 + 1], dt[:, h1:h1 + 1])
        acum_p = jnp.where(lane_lo, a_cum[:, h0:h0 + 1], a_cum[:, h1:h1 + 1])
        xdt = xs_p * dt_p
        ms = []
        for h in (h0, h1):
            seg = a_cum[:, h:h + 1] - a_cumt[h:h + 1, :]
            ms.append(cb * jnp.exp(jnp.where(causal, seg, NEG_BIG)))
        m2 = jnp.concatenate(ms, axis=1)
        x2 = jnp.concatenate([jnp.where(lane_lo, xdt, 0.0), jnp.where(lane_lo, 0.0, xdt)], axis=0)
        y_diag = _dot(m2, x2)
        prev = state_ref[p]
        y_off = _dot_nt(cg, prev) * jnp.exp(acum_p)
        a_last = jnp.where(lane_lo, a_cum[q - 1:q, h0:h0 + 1], a_cum[q - 1:q, h1:h1 + 1])
        xdw = xdt * jnp.exp(a_last - acum_p)
        st = _dot_tn(xdw, bg)
        cd = jnp.where(rowp < HEAD_DIM, jnp.exp(a_cumt[h0:h0 + 1, q - 1:q]),
                       jnp.exp(a_cumt[h1:h1 + 1, q - 1:q]))
        state_ref[p] = prev * cd + st
        d_p = dskip_ref[:, p * LANES:(p + 1) * LANES]
        ys.append(y_diag + y_off + d_p * xs_p)
        yield
    y = jnp.concatenate(ys, axis=1)
    z = z_ref[0]
    y = y * (z * _sigmoid(z))
    gw = width // SSD_GROUPS
    outs = []
    for g in range(SSD_GROUPS):
        yg = y[:, g * gw:(g + 1) * gw]
        outs.append(yg * lax.rsqrt(jnp.mean(yg * yg, axis=-1, keepdims=True) + RMS_EPS))
    y_ref[0] = (jnp.concatenate(outs, axis=1) * ng_ref[...]).astype(y_ref.dtype)


def _ssd_operands(z, xbc, dt_pad, conv_w, conv_b, dt_bias, a_log, d_skip, norm_g):
    bsz, l, width = z.shape
    heads = width // HEAD_DIM
    xw = xbc.shape[-1]
    q = SSD_CHUNK
    dtt = jnp.swapaxes(dt_pad[:, :, :heads], 1, 2)
    pad = LANES - heads
    dtb = jnp.pad(dt_bias, (0, pad)).reshape(1, LANES)
    a_neg = -jnp.exp(a_log.astype(F32))
    aneg = jnp.pad(a_neg, (0, pad)).reshape(1, LANES)
    dskip = jnp.repeat(d_skip, HEAD_DIM).reshape(1, width)
    row = lambda n: pl.BlockSpec((1, n), lambda bi, c: (0, 0))
    specs = [pl.BlockSpec((1, q, width), lambda bi, c: (bi, c, 0)),
             pl.BlockSpec((1, q, xw), lambda bi, c: (bi, c, 0)),
             pl.BlockSpec((1, SUBLANES, xw), lambda bi, c: (bi, jnp.maximum(c * (q // SUBLANES) - 1, 0), 0)),
             pl.BlockSpec((1, q, LANES), lambda bi, c: (bi, c, 0)),
             pl.BlockSpec((1, heads, q), lambda bi, c: (bi, 0, c)),
             pl.BlockSpec((SSD_CONV, xw), lambda bi, c: (0, 0)),
             row(xw), row(LANES),
             pl.BlockSpec((heads, 1), lambda bi, c: (0, 0)),
             row(LANES),
             pl.BlockSpec((heads, 1), lambda bi, c: (0, 0)),
             row(width), row(width)]
    args = [z, xbc, xbc, dt_pad, dtt, conv_w, conv_b.reshape(1, xw), dtb, dt_bias.reshape(heads, 1),
            aneg, a_neg.reshape(heads, 1), dskip, norm_g.reshape(1, width)]
    scratch = [pltpu.VMEM((width // LANES, LANES, SSD_STATE), F32), pltpu.VMEM((q + SUBLANES, xw), F32)]
    return args, specs, scratch


def _rwkv_prep_kernel(n_extra, x_ref, xp_ref, win_ref, mu_ref, w0_ref, a0_ref, kk_ref, ka_ref, w2_ref, a2_ref,
                      g2_ref, e_ref, et_ref, *refs):
    extra_w = refs[:n_extra]
    r_ref, lw_ref, k_ref, v_ref, an_ref, bn_ref, g_ref = refs[n_extra:n_extra + 7]
    extra_o = refs[n_extra + 7:2 * n_extra + 7]
    ext_ref = refs[-1]
    tm = x_ref.shape[1]
    width = r_ref.shape[-1]
    i = pl.program_id(1)
    xe = jnp.concatenate([jnp.where(i > 0, xp_ref[0], 0.0), x_ref[0]], axis=0).astype(BF16)
    win = ext_ref.shape[1]
    lora = RWKV_DECAY_LORA + RWKV_ICLR_LORA

    def project(c0, c1):
        ext_ref[:, c0:c1] = jnp.dot(xe, win_ref[:, c0:c1], preferred_element_type=F32)

    def shifted_mix(c0, c1):
        rw = ext_ref[SUBLANES:SUBLANES + tm, c0:c1]
        prev = ext_ref[SUBLANES - 1:SUBLANES - 1 + tm, c0:c1]
        return rw + (prev - rw) * mu_ref[:, c0:c1]

    def projections():
        half = width // 2
        extras = [(w_ref, o_ref, sl) for w_ref, o_ref in zip(extra_w, extra_o)
                  for sl in _col_chunks(w_ref.shape[1], 512)]
        own = [(3 * width, win), (width, width + half), (width + half, 2 * width), (0, half), (half, width),
               (2 * width, 2 * width + half), (2 * width + half, 3 * width)]
        for n, cols in enumerate(own):
            project(*cols)
            yield
            if n >= 2 and extras:
                w_ref, o_ref, sl = extras.pop(0)
                o_ref[0, :, sl] = jnp.dot(xe[SUBLANES:], w_ref[:, sl], preferred_element_type=F32)
                yield
        for w_ref, o_ref, sl in extras:
            o_ref[0, :, sl] = jnp.dot(xe[SUBLANES:], w_ref[:, sl], preferred_element_type=F32)
            yield

    pieces = projections()

    def mxu(n=1):
        for _ in range(n):
            next(pieces, None)

    mxu(3)
    s_lo = shifted_mix(3 * width, win)
    lo = s_lo[:, :lora]
    wv = w0_ref[...] + _dot(jnp.tanh(lo), w2_ref[...])
    av = _sigmoid(a0_ref[...] + _dot(lo, a2_ref[...]))
    g_ref[0] = _dot(_sigmoid(s_lo[:, lora:]), g2_ref[...])
    mxu(1)
    for sl in _col_chunks(width, 512):
        lw_ref[0, :, sl] = -jnp.exp(-_softplus(-wv[:, sl]) - 0.5)
        mxu(1)
    k = shifted_mix(width, 2 * width)
    kkr = k * kk_ref[...]
    mxu(1)
    ss = _dot_exact_rhs(kkr * kkr, e_ref[...])
    k_ref[0] = k * (1.0 + (av - 1.0) * ka_ref[...])
    mxu(2)
    inv = lax.rsqrt(jnp.maximum(ss, 1e-24))
    kk = kkr * _dot_exact_rhs(inv, et_ref[...])
    mxu(1)
    an_ref[0] = -kk
    bn_ref[0] = kk * av
    mxu(2)
    r_ref[0] = shifted_mix(0, width)
    mxu(1)
    v_ref[0] = shifted_mix(2 * width, 3 * width)
    for _ in pieces:
        pass


def _rwkv_chunk(rows, fill, r_ref, lw_ref, k_ref, v_ref, an_ref, bn_ref, g_ref, rk_ref, lg_ref, lb_ref,
                y_ref, state_ref):
    c = RWKV_CHUNK
    c2 = 2 * c
    width = r_ref.shape[-1]
    n_pairs = width // LANES

    row = lax.broadcasted_iota(jnp.int32, (c, c), 0)
    col = lax.broadcasted_iota(jnp.int32, (c, c), 1)
    tri = jnp.where(row >= col, 1.0, 0.0)
    roww = lax.broadcasted_iota(jnp.int32, (c, c2), 0)
    colw = lax.broadcasted_iota(jnp.int32, (c, c2), 1) % c
    strict = jnp.where(roww > colw, 1.0, 0.0)
    incl = jnp.where(roww >= colw, 1.0, 0.0)
    eye = jnp.where(roww == colw, 1.0, 0.0)
    same_head = jnp.where((lax.broadcasted_iota(jnp.int32, (LANES, LANES), 0) // HEAD_DIM)
                          == (lax.broadcasted_iota(jnp.int32, (LANES, LANES), 1) // HEAD_DIM), 1.0, 0.0)
    lane = lax.broadcasted_iota(jnp.int32, (1, LANES), 1)
    m0 = jnp.where(lane < HEAD_DIM, 1.0, 0.0)
    m1 = 1.0 - m0

    lane_lo2 = lax.broadcasted_iota(jnp.int32, (c, LANES), 1) < HEAD_DIM

    def stack(x):
        xb = x.astype(BF16)
        zero = jnp.zeros_like(xb)
        return jnp.concatenate([jnp.where(lane_lo2, xb, zero), jnp.where(lane_lo2, zero, xb)], axis=0)

    sls = [slice(p * LANES, (p + 1) * LANES) for p in range(n_pairs)]
    lhs2s, rhs4s, vs, v_stks, bks, g_lasts = [], [], [], [], [], []
    for sl in sls:
        r = r_ref[0, rows, sl].astype(F32)
        lw = lw_ref[0, rows, sl]
        k = k_ref[0, rows, sl].astype(F32)
        b = bn_ref[0, rows, sl].astype(F32)
        v = v_ref[0, rows, sl].astype(F32)
        cum = _dot_exact_lhs(tri, lw)
        cum_last = cum[c - 1:c, :]
        e_neg = jnp.exp(-cum)
        e_tail = jnp.exp(cum_last - cum)
        at = an_ref[0, rows, sl].astype(F32) * jnp.exp(cum - lw)
        rt = r * jnp.exp(cum)
        lhs2s.append(jnp.concatenate([at, rt], axis=0).astype(BF16))
        rhs4s.append(jnp.concatenate([stack(b * e_neg), stack(k * e_neg)], axis=0))
        vs.append(v.astype(BF16))
        v_stks.append(stack(v))
        bks.append(jnp.concatenate([b * e_tail, k * e_tail], axis=0).astype(BF16))
        g_lasts.append(jnp.exp(cum_last))
    gms = [_dot_nt(l, rh) for l, rh in zip(lhs2s, rhs4s)]
    pws = [gm[:c, :c2] * strict for gm in gms]
    a_aks = [(gm[:c, c2:] * strict).astype(BF16) for gm in gms]
    a_rbks = [jnp.concatenate([gm[c:, :c2] * incl, gm[c:, c2:] * incl], axis=1).astype(BF16) for gm in gms]
    akvs = [_dot(a_ak, v_stk) for a_ak, v_stk in zip(a_aks, v_stks)]
    fill()
    tinvs = [eye + pw for pw in pws]
    pw_stks = [stack(pw) for pw in pws]
    for _ in range(int(math.log2(c)) - 1):
        pws = [_dot(pw, pw_stk) for pw, pw_stk in zip(pws, pw_stks)]
        pw_stks = [stack(pw) for pw in pws]
        tinvs = [tinv + _dot(tinv, pw_stk) for tinv, pw_stk in zip(tinvs, pw_stks)]
        fill()
    sts = [state_ref[p] for p in range(n_pairs)]
    ahrhs = [_dot_nt(l, st) for l, st in zip(lhs2s, sts)]
    fill()
    us = [_dot(tinv, stack(ahrh[:c] + akv)) for tinv, ahrh, akv in zip(tinvs, ahrhs, akvs)]
    fill()
    for p in range(n_pairs):
        uv = jnp.concatenate([us[p].astype(BF16), vs[p]], axis=0)
        state_ref[p] = sts[p] * g_lasts[p] + same_head * _dot_tn(uv, bks[p])
    fill()
    ys = [ahrh[c:] + _dot(a_rbk, jnp.concatenate([stack(u), v_stk], axis=0))
          for ahrh, a_rbk, u, v_stk in zip(ahrhs, a_rbks, us, v_stks)]
    lane_lo = lane < HEAD_DIM

    def head_sum(x):
        lo = jnp.sum(x * m0, axis=-1, keepdims=True)
        hi = jnp.sum(x * m1, axis=-1, keepdims=True)
        return jnp.where(lane_lo, lo, hi)

    bonus = [head_sum(r_ref[0, rows, sl].astype(F32) * k_ref[0, rows, sl].astype(F32) * rk_ref[:, sl]) for sl in sls]
    ycs = [y - head_sum(y) * (1.0 / HEAD_DIM) for y in ys]
    yvs = [head_sum(yc * yc) * (1.0 / HEAD_DIM) for yc in ycs]
    for p, sl in enumerate(sls):
        yn = ycs[p] * lax.rsqrt(yvs[p] + RWKV_LNX_EPS) * lg_ref[:, sl] + lb_ref[:, sl]
        out = (yn + bonus[p] * v_ref[0, rows, sl].astype(F32)) * g_ref[0, rows, sl].astype(F32)
        y_ref[0, rows, sl] = out.astype(y_ref.dtype)


def _rwkv_operands(x, w_in, extra_ws, mu, w0, w2, a0, a2, g2, k_k, k_a, r_k, lnx_g, lnx_b, *, tm):
    bsz, l, d = x.shape
    n_extra = len(extra_ws)
    win = w_in.shape[1]
    width = w0.shape[0]
    w2p = jnp.concatenate([w2, jnp.zeros((RWKV_ICLR_LORA, width), F32)], axis=0).astype(BF16)
    a2p = jnp.concatenate([jnp.zeros((RWKV_DECAY_LORA, width), F32), a2], axis=0).astype(BF16)
    head_of = jnp.arange(width) // HEAD_DIM
    e = (head_of[:, None] == jnp.arange(LANES)[None, :]).astype(BF16)
    et = e.T
    vec = lambda x: x.reshape(1, -1)
    row = lambda n: pl.BlockSpec((1, n), lambda bi, i: (0, 0))
    full = lambda a: pl.BlockSpec(a.shape, lambda bi, i: (0, 0))
    single = lambda a: pl.BlockSpec(a.shape, lambda bi, i: (0, 0), pipeline_mode=pl.Buffered(1))
    tile = lambda w: pl.BlockSpec((1, tm, w), lambda bi, i: (bi, i, 0))
    sds = lambda w: jax.ShapeDtypeStruct((bsz, l, w), F32)
    g2b = g2.astype(BF16)
    outs = pl.pallas_call(
        functools.partial(_rwkv_prep_kernel, n_extra),
        grid=(bsz, l // tm),
        in_specs=[pl.BlockSpec((1, tm, d), lambda bi, i: (bi, i, 0)),
                  pl.BlockSpec((1, SUBLANES, d), lambda bi, i: (bi, jnp.maximum(i * (tm // SUBLANES) - 1, 0), 0)),
                  single(w_in), row(win), row(width), row(width), row(width), row(width),
                  full(w2p), full(a2p), full(g2b), full(e), full(et)] + [single(w) for w in extra_ws],
        out_specs=[tile(width)] * 7 + [tile(w.shape[1]) for w in extra_ws],
        out_shape=[sds(width)] * 7 + [sds(w.shape[1]) for w in extra_ws],
        scratch_shapes=[pltpu.VMEM((tm + SUBLANES, win), F32)],
        compiler_params=_params("parallel", "parallel"),
        name="rwkv_prep",
    )(x, x, w_in, vec(mu), vec(w0), vec(a0), vec(k_k), vec(k_a), w2p, a2p, g2b, e, et, *extra_ws)
    r, lw, k, v, an, bn, g = outs[:7]
    ctile = pl.BlockSpec((1, SSD_CHUNK, width), lambda bi, ci: (bi, ci, 0))
    crow = pl.BlockSpec((1, width), lambda bi, ci: (0, 0))
    args = [r, lw, k, v, an, bn, g, vec(r_k), vec(lnx_g), vec(lnx_b)]
    scratch = [pltpu.VMEM((width // LANES, LANES, LANES), F32)]
    return outs[7:], (args, [ctile] * 7 + [crow] * 3, scratch)


def _mixers_kernel(n_ssd, n_rwkv, *refs):
    ssd_in = refs[:n_ssd]
    rwkv_in = refs[n_ssd:n_ssd + n_rwkv]
    y_ssd_ref, y_rwkv_ref, ssd_state, ssd_ext, rwkv_state = refs[n_ssd + n_rwkv:]
    c = pl.program_id(1)

    @pl.when(c == 0)
    def _():
        ssd_state[...] = jnp.zeros_like(ssd_state)
        rwkv_state[...] = jnp.zeros_like(rwkv_state)

    ssd_steps = _ssd_chunk(c, *ssd_in, y_ssd_ref, ssd_state, ssd_ext)
    fill = lambda: next(ssd_steps, None)
    for sub in range(SSD_CHUNK // RWKV_CHUNK):
        _rwkv_chunk(slice(sub * RWKV_CHUNK, (sub + 1) * RWKV_CHUNK), fill, *rwkv_in, y_rwkv_ref, rwkv_state)
    for _ in ssd_steps:
        pass


def _ssd_rwkv_mixers(ssd_ops, rwkv_ops, bsz, l, ssd_width, rwkv_width):
    ssd_args, ssd_specs, ssd_scratch = ssd_ops
    rwkv_args, rwkv_specs, rwkv_scratch = rwkv_ops
    q = SSD_CHUNK
    out_spec = lambda w: pl.BlockSpec((1, q, w), lambda bi, c: (bi, c, 0))
    return pl.pallas_call(
        functools.partial(_mixers_kernel, len(ssd_args), len(rwkv_args)),
        grid=(bsz, l // q),
        in_specs=ssd_specs + rwkv_specs,
        out_specs=[out_spec(ssd_width), out_spec(rwkv_width)],
        out_shape=[jax.ShapeDtypeStruct((bsz, l, ssd_width), BF16), jax.ShapeDtypeStruct((bsz, l, rwkv_width), BF16)],
        scratch_shapes=ssd_scratch + rwkv_scratch,
        compiler_params=_params("parallel", "arbitrary"),
        name="ssd_rwkv_mixers",
    )(*ssd_args, *rwkv_args)


def _moba_kernel(q_ref, qn_ref, k_ref, v_ref, o_ref, kb_ref, vt_ref, kmean_ref, qo_ref, qp_ref):
    blk = MOBA_BLOCK
    half = HEAD_DIM
    nb = k_ref.shape[0] // blk
    nbp = kmean_ref.shape[0]
    npair = q_ref.shape[1] // LANES
    heads = range(2 * npair)
    group = math.gcd(nb, MOBA_GROUP)
    qi = pl.program_id(2)

    @pl.when(qi == 0)
    def _():
        lane = lax.broadcasted_iota(jnp.int32, (blk, LANES), 1)
        rowp = lax.broadcasted_iota(jnp.int32, (LANES, blk), 0)
        if nbp > nb:
            kmean_ref[...] = jnp.zeros_like(kmean_ref)
        for n in range(nb):
            rows = slice(n * blk, (n + 1) * blk)
            kmean_ref[n:n + 1, :] = jnp.mean(k_ref[rows, :], axis=0, keepdims=True)
            for pp in range(npair):
                cols = slice(pp * LANES, (pp + 1) * LANES)
                kn = k_ref[rows, cols]
                kb_ref[2 * pp, n] = jnp.where(lane < half, kn, jnp.where(lane == half + n, 1.0, 0.0)).astype(BF16)
                kb_ref[2 * pp + 1, n] = jnp.where(lane >= half, kn, jnp.where(lane == n, 1.0, 0.0)).astype(BF16)
                vtn = v_ref[rows, cols].T
                vt_ref[2 * pp, n] = jnp.where(rowp < half, vtn, jnp.where(rowp == half, 1.0, 0.0)).astype(BF16)
                vt_ref[2 * pp + 1, n] = jnp.where(rowp >= half, vtn, jnp.where(rowp == 0, 1.0, 0.0)).astype(BF16)

    rown = lax.broadcasted_iota(jnp.int32, (nbp, blk), 0)
    lane_k = lax.broadcasted_iota(jnp.int32, (nbp, LANES), 1)
    zeros = jnp.zeros((half, blk), F32)

    def prepare_queries(src_ref, tile):
        qts = [src_ref[:, pp * LANES:(pp + 1) * LANES].T for pp in range(npair)]
        gates, qhs = [], []
        for hh in heads:
            pp, h = divmod(hh, 2)
            km = kmean_ref[:, pp * LANES:(pp + 1) * LANES]
            kmh = jnp.where((lane_k < half) if h == 0 else (lane_k >= half), km, 0.0)
            gates.append(jnp.dot(kmh, qts[pp], preferred_element_type=F32, precision=lax.Precision.HIGHEST))
            qhs.append(qts[pp][h * half:(h + 1) * half] * (half ** -0.5 * LOG2_E))
        for hh in heads:
            gate = jnp.where(rown < tile, gates[hh], -jnp.inf)
            bias = jnp.full((nbp, blk), NEG_BIG, F32)
            for _ in range(MOBA_TOPK):
                mx = jnp.max(gate, axis=0, keepdims=True)
                first = jnp.min(jnp.where(gate == mx, rown, nbp), axis=0, keepdims=True)
                pick = (rown == first) & (mx > -jnp.inf)
                bias = jnp.where(pick, 0.0, bias)
                gate = jnp.where(pick, -jnp.inf, gate)
            aug = jnp.concatenate([bias, jnp.zeros((half - nbp, blk), F32)], axis=0)
            qo_ref[hh] = jnp.concatenate([qhs[hh], zeros] if hh % 2 == 0 else [zeros, qhs[hh]],
                                         axis=0).astype(BF16)
            qp_ref[hh] = jnp.concatenate([qhs[hh], aug] if hh % 2 == 0 else [aug, qhs[hh]], axis=0).astype(BF16)

    @pl.when(qi == 0)
    def _():
        prepare_queries(q_ref, 0)

    q_own = [qo_ref[h] for h in heads]
    q_past = [qp_ref[h] for h in heads]
    s_own = [jnp.dot(kb_ref[h, qi], q_own[h], preferred_element_type=F32) for h in heads]
    prepare_queries(qn_ref, qi + 1)

    causal = (lax.broadcasted_iota(jnp.int32, (blk, blk), 0) <= lax.broadcasted_iota(jnp.int32, (blk, blk), 1))
    ms, ps = [], []
    for h in heads:
        s = jnp.where(causal, s_own[h], NEG_BIG)
        ms.append(jnp.max(s, axis=0, keepdims=True))
        ps.append(jnp.exp2(s - ms[h]).astype(BF16))
    carry = []
    for h in heads:
        carry += [ms[h], jnp.dot(vt_ref[h, qi], ps[h], preferred_element_type=F32)]

    def scores(gi, h):
        return [jnp.dot(kb_ref[h, gi * group + g], q_past[h], preferred_element_type=F32) for g in range(group)]

    def values(gi, h):
        return jnp.concatenate([vt_ref[h, gi * group + g] for g in range(group)], axis=1)

    def body_lagged(gi, carry):
        excess = carry[-1]
        out = []
        sss = [scores(gi, h) for h in heads]
        pcats, gmaxs = [], []
        for h in heads:
            m_run = carry[2 * h]
            gmax = None
            ps = []
            for s in sss[h]:
                cm = jnp.max(s, axis=0, keepdims=True)
                gmax = cm if gmax is None else jnp.maximum(gmax, cm)
                ps.append(jnp.exp2(s - m_run).astype(BF16))
            pcats.append(jnp.concatenate(ps, axis=0))
            gmaxs.append(gmax)
        for h in heads:
            m_run, acc = carry[2 * h], carry[2 * h + 1]
            m_new = jnp.maximum(m_run, gmaxs[h])
            excess = jnp.maximum(excess, gmaxs[h] - m_run)
            acc = jnp.exp2(m_run - m_new) * (acc + jnp.dot(values(gi, h), pcats[h], preferred_element_type=F32))
            out += [m_new, acc]
        return tuple(out) + (excess,)

    def body_exact_max(gi, carry):
        sss = [scores(gi, h) for h in heads]
        m_news = []
        for h in heads:
            m_new = carry[2 * h]
            for s in sss[h]:
                m_new = jnp.maximum(m_new, jnp.max(s, axis=0, keepdims=True))
            m_news.append(m_new)
        pcats = [jnp.concatenate([jnp.exp2(s - m_news[h]).astype(BF16) for s in sss[h]], axis=0)
                 for h in heads]
        out = []
        for h in heads:
            alpha = jnp.exp2(carry[2 * h] - m_news[h])
            out += [m_news[h], alpha * carry[2 * h + 1]
                    + jnp.dot(values(gi, h), pcats[h], preferred_element_type=F32)]
        return tuple(out)

    rowq = lax.broadcasted_iota(jnp.int32, (LANES, blk), 0)

    def write_out(final):
        for pp in range(npair):
            acc0, acc1 = final[4 * pp + 1], final[4 * pp + 3]
            out_t = jnp.where(rowq < half, acc0 / acc0[half:half + 1], acc1 / acc1[0:1])
            o_ref[:, pp * LANES:(pp + 1) * LANES] = out_t.T.astype(o_ref.dtype)

    n_groups = (qi + group - 1) // group
    final = lax.fori_loop(0, n_groups, body_lagged, tuple(carry) + (jnp.full((1, blk), NEG_BIG, F32),))
    write_out(final)

    @pl.when(jnp.max(final[-1]) > MOBA_LAG_LIMIT)
    def _():
        write_out(lax.fori_loop(0, n_groups, body_exact_max, tuple(carry)))


def _moba_attention(qkv, bsz, s, heads):
    blk = MOBA_BLOCK
    assert s % blk == 0 and (heads * HEAD_DIM) % LANES == 0
    nb = s // blk
    assert nb <= HEAD_DIM
    nbp = -(-nb // SUBLANES) * SUBLANES
    pairs = heads * HEAD_DIM // LANES
    pps = math.gcd(pairs, MOBA_PAIRS_PER_STEP)
    cw = pps * LANES
    steps = pairs // pps
    return pl.pallas_call(
        _moba_kernel,
        grid=(bsz, steps, nb),
        in_specs=[pl.BlockSpec((blk, cw), lambda b, p, i: (b * nb + i, p)),
                  pl.BlockSpec((blk, cw), lambda b, p, i: (b * nb + jnp.minimum(i + 1, nb - 1), p)),
                  pl.BlockSpec((s, cw), lambda b, p, i: (b, steps + p)),
                  pl.BlockSpec((s, cw), lambda b, p, i: (b, 2 * steps + p))],
        out_specs=pl.BlockSpec((blk, cw), lambda b, p, i: (b * nb + i, p)),
        out_shape=jax.ShapeDtypeStruct((bsz * s, heads * HEAD_DIM), BF16),
        scratch_shapes=[pltpu.VMEM((2 * pps, nb, blk, LANES), BF16), pltpu.VMEM((2 * pps, nb, LANES, blk), BF16),
                        pltpu.VMEM((nbp, cw), F32),
                        pltpu.VMEM((2 * pps, LANES, blk), BF16), pltpu.VMEM((2 * pps, LANES, blk), BF16)],
        compiler_params=_params("parallel", "parallel", "arbitrary"),
        name="moba_attention",
    )(qkv, qkv, qkv, qkv)


def _row_tile(m):
    for t in (512, 256, 128, 64, 32, 16, 8):
        if m % t == 0:
            return t
    raise ValueError(f"row count {m} is not a multiple of 8")


def _col_tile(n, cap=2048):
    best = None
    for t in range(LANES, min(n, cap) + 1, LANES):
        if n % t == 0:
            best = t
    if best is None:
        raise ValueError(f"column count {n} is not a multiple of {LANES}")
    return best


def kernel(x, mem, even_w_in, ssd_conv_w, ssd_conv_b, ssd_dt_bias, ssd_a_log, ssd_d, ssd_norm_g, rwkv_mu, rwkv_w0, rwkv_w2, rwkv_a0, rwkv_a2, rwkv_g2, rwkv_k_k, rwkv_k_a, rwkv_r_k, rwkv_lnx_g, rwkv_lnx_b, even_w_out, odd_w_qkv, odd_w_out, ln_mix_g, ln_mix_b, xa_wq, xa_wkv, xa_wo, ln_xa_g, ln_xa_b, ffn_w13, ffn_w2, ln_ffn_g, ln_ffn_b):
    bsz, s, d = x.shape
    m = bsz * s
    tm = _row_tile(s)
    ssd_width = ssd_norm_g.shape[-1]
    ssd_heads = ssd_dt_bias.shape[-1]
    ssd_xbc = ssd_conv_b.shape[-1]
    ssd_in = ssd_width + ssd_xbc + ssd_heads
    rwkv_width = rwkv_w0.shape[-1]
    mem2 = mem.reshape(bsz * mem.shape[1], d)
    x2 = x.reshape(m, d)
    for layer in range(DEPTH):
        j = layer // 2
        if layer % 2 == 0:
            w_in = even_w_in[j].astype(BF16)
            w_z = w_in[:, :ssd_width]
            w_xbc = w_in[:, ssd_width:ssd_width + ssd_xbc]
            w_dt = jnp.pad(w_in[:, ssd_width + ssd_xbc:ssd_in], ((0, 0), (0, LANES - ssd_heads)))
            w_rw = w_in[:, ssd_in:]
            (z, xbc, dt_pad), rwkv_ops = _rwkv_operands(
                x2.reshape(bsz, s, d), w_rw, [w_z, w_xbc, w_dt], rwkv_mu[j], rwkv_w0[j], rwkv_w2[j],
                rwkv_a0[j], rwkv_a2[j], rwkv_g2[j], rwkv_k_k[j], rwkv_k_a[j], rwkv_r_k[j],
                rwkv_lnx_g[j], rwkv_lnx_b[j], tm=min(tm, 256))
            ssd_ops = _ssd_operands(z, xbc, dt_pad, ssd_conv_w[j], ssd_conv_b[j], ssd_dt_bias[j],
                                    ssd_a_log[j], ssd_d[j], ssd_norm_g[j])
            y_ssd, y_rwkv = _ssd_rwkv_mixers(ssd_ops, rwkv_ops, bsz, s, ssd_width, rwkv_width)
            x2 = _matmul_residual_ln([y_ssd.reshape(m, -1), y_rwkv.reshape(m, -1)],
                                     even_w_out[j].astype(BF16), x2,
                                     ln_mix_g[layer], ln_mix_b[layer], tm=tm)
        else:
            heads = d // HEAD_DIM
            qkv, = _matmul(x2, [odd_w_qkv[j].astype(BF16)], tm=tm)
            attn = _moba_attention(qkv, bsz, s, heads)
            x2 = _matmul_residual_ln([attn], odd_w_out[j].astype(BF16), x2,
                                     ln_mix_g[layer], ln_mix_b[layer], tm=tm)
        kv, = _matmul(mem2, [xa_wkv[layer].astype(BF16)], tm=_row_tile(mem2.shape[0]))
        x3 = _cross_attention_ln(x2.reshape(bsz, s, d), kv.reshape(bsz, -1, 2 * d),
                                 xa_wq[layer].astype(BF16), xa_wo[layer].astype(BF16),
                                 ln_xa_g[layer], ln_xa_b[layer], tm=tm)
        x2 = x3.reshape(m, d)
        h = _swiglu_up(x2, ffn_w13[layer].astype(BF16), tm=tm)
        x2 = _matmul_residual_ln([h], ffn_w2[layer].astype(BF16), x2,
                                 ln_ffn_g[layer], ln_ffn_b[layer], tm=tm)
    return x2.reshape(bsz, s, d)
```

```python
import functools
import math

import jax
import jax.numpy as jnp
from jax import lax
from jax.experimental import pallas as pl
from jax.experimental.pallas import tpu as pltpu

F32 = jnp.float32
BF16 = jnp.bfloat16

HEAD_DIM = 64
LANES = 128
SUBLANES = 8
SSD_GROUPS = 2
SSD_STATE = 128
SSD_CONV = 4
SSD_CHUNK = 128
RWKV_DECAY_LORA = 64
RWKV_ICLR_LORA = 64
RWKV_GATE_LORA = 128
RWKV_CHUNK = 64
RWKV_PREP_ROWS = 256
MOBA_BLOCK = 256
MOBA_TOPK = 3
MOBA_GROUP = 4
MOBA_LAG_LIMIT = 64.0
MOBA_PAIRS_PER_STEP = 2
XATTN_HEADS = 4
DEPTH = 2
DEEPNORM_ALPHA = (2 * DEPTH) ** 0.25
LN_EPS = 1e-5
RMS_EPS = 1e-5
RWKV_LNX_EPS = 64e-5
NEG_BIG = -1e30
LOG2_E = math.log2(math.e)
VMEM_LIMIT = 60 * 1024 * 1024


def _params(*sem):
    return pltpu.CompilerParams(dimension_semantics=sem, vmem_limit_bytes=VMEM_LIMIT)


def _dot(a, b):
    return jnp.dot(a.astype(BF16), b.astype(BF16), preferred_element_type=F32)


def _dot_nt(a, b):
    return lax.dot_general(a.astype(BF16), b.astype(BF16), (((1,), (1,)), ((), ())),
                           preferred_element_type=F32)


def _dot_tn(a, b):
    return lax.dot_general(a.astype(BF16), b.astype(BF16), (((0,), (0,)), ((), ())),
                           preferred_element_type=F32)


def _split3(x):
    hi = x.astype(BF16)
    r1 = x - hi.astype(F32)
    mid = r1.astype(BF16)
    lo = (r1 - mid.astype(F32)).astype(BF16)
    return hi, mid, lo


def _dot_exact_lhs(m, x):
    hi, mid, lo = _split3(x)
    m = m.astype(BF16)
    return (jnp.dot(m, hi, preferred_element_type=F32) + jnp.dot(m, mid, preferred_element_type=F32)
            + jnp.dot(m, lo, preferred_element_type=F32))


def _dot_exact_rhs(x, m):
    hi, mid, lo = _split3(x)
    m = m.astype(BF16)
    return (jnp.dot(hi, m, preferred_element_type=F32) + jnp.dot(mid, m, preferred_element_type=F32)
            + jnp.dot(lo, m, preferred_element_type=F32))


def _sigmoid(x):
    return 1.0 / (1.0 + jnp.exp(-x))


def _softplus(x):
    return jnp.maximum(x, 0.0) + jnp.log1p(jnp.exp(-jnp.abs(x)))


def _layer_norm(v, g, b):
    mu = jnp.mean(v, axis=-1, keepdims=True)
    c = v - mu
    var = jnp.mean(c * c, axis=-1, keepdims=True)
    return c * lax.rsqrt(var + LN_EPS) * g + b


def _col_chunks(n, cap=1536):
    width = _col_tile(n, cap)
    return [slice(j, j + width) for j in range(0, n, width)]


def _mm_kernel(n_out, x_ref, *refs):
    x = x_ref[...].astype(BF16)
    for w_ref, o_ref in zip(refs[:n_out], refs[n_out:]):
        for sl in _col_chunks(w_ref.shape[1]):
            o_ref[:, sl] = jnp.dot(x, w_ref[:, sl], preferred_element_type=F32).astype(o_ref.dtype)


def _matmul(x, ws, *, tm, out_dtype=F32):
    m, k = x.shape
    assert m % tm == 0
    return pl.pallas_call(
        functools.partial(_mm_kernel, len(ws)),
        grid=(m // tm,),
        in_specs=[pl.BlockSpec((tm, k), lambda i: (i, 0))] + [pl.BlockSpec(w.shape, lambda i: (0, 0)) for w in ws],
        out_specs=[pl.BlockSpec((tm, w.shape[1]), lambda i: (i, 0)) for w in ws],
        out_shape=[jax.ShapeDtypeStruct((m, w.shape[1]), out_dtype) for w in ws],
        compiler_params=_params("parallel"),
        name="matmul",
    )(x, *ws)


def _mm_res_ln_kernel(n_in, *refs):
    hs = refs[:n_in]
    ws = refs[n_in:2 * n_in]
    res_ref, g_ref, b_ref, o_ref = refs[2 * n_in:]
    acc = _dot(hs[0][...], ws[0][...])
    for h_ref, w_ref in zip(hs[1:], ws[1:]):
        acc = acc + _dot(h_ref[...], w_ref[...])
    o_ref[...] = _layer_norm(DEEPNORM_ALPHA * res_ref[...] + acc, g_ref[...], b_ref[...])


def _matmul_residual_ln(hs, w, res, g, b, *, tm):
    m, d = res.shape
    n_in = len(hs)
    kw = w.shape[0] // n_in
    assert all(h.shape[1] == kw for h in hs)
    in_specs = ([pl.BlockSpec((tm, kw), lambda i: (i, 0)) for _ in hs]
                + [pl.BlockSpec((kw, d), lambda i, j=j: (j, 0)) for j in range(n_in)]
                + [pl.BlockSpec((tm, d), lambda i: (i, 0)),
                   pl.BlockSpec((1, d), lambda i: (0, 0)),
                   pl.BlockSpec((1, d), lambda i: (0, 0))])
    return pl.pallas_call(
        functools.partial(_mm_res_ln_kernel, n_in),
        grid=(m // tm,),
        in_specs=in_specs,
        out_specs=pl.BlockSpec((tm, d), lambda i: (i, 0)),
        out_shape=jax.ShapeDtypeStruct((m, d), F32),
        compiler_params=_params("parallel"),
        name="matmul_residual_ln",
    )(*hs, *([w] * n_in), res, g.reshape(1, d), b.reshape(1, d))


def _swiglu_kernel(x_ref, w13_ref, o_ref):
    x = x_ref[...].astype(BF16)
    n = o_ref.shape[1]
    for sl in _col_chunks(n):
        gate = jnp.dot(x, w13_ref[:, sl], preferred_element_type=F32)
        up = jnp.dot(x, w13_ref[:, slice(n + sl.start, n + sl.stop)], preferred_element_type=F32)
        o_ref[:, sl] = (gate * _sigmoid(gate) * up).astype(o_ref.dtype)


def _swiglu_up(x, w13, *, tm):
    m, k = x.shape
    n = w13.shape[1] // 2
    return pl.pallas_call(
        _swiglu_kernel,
        grid=(m // tm,),
        in_specs=[pl.BlockSpec((tm, k), lambda i: (i, 0)),
                  pl.BlockSpec((k, 2 * n), lambda i: (0, 0))],
        out_specs=pl.BlockSpec((tm, n), lambda i: (i, 0)),
        out_shape=jax.ShapeDtypeStruct((m, n), BF16),
        compiler_params=_params("parallel"),
        name="swiglu_up",
    )(x, w13)


def _xattn_kernel(x_ref, kv_ref, wq_ref, wo_ref, g_ref, b_ref, o_ref):
    x = x_ref[0]
    d = x.shape[-1]
    hd = d // XATTN_HEADS
    q = _dot(x, wq_ref[...])
    kv = kv_ref[0]
    heads = range(XATTN_HEADS)
    ss = [_dot_nt(q[:, h * hd:(h + 1) * hd], kv[:, h * hd:(h + 1) * hd]) * (hd ** -0.5) for h in heads]
    ps = []
    for s in ss:
        p = jnp.exp(s - jnp.max(s, axis=-1, keepdims=True))
        ps.append(p / jnp.sum(p, axis=-1, keepdims=True))
    o = jnp.concatenate([_dot(ps[h], kv[:, d + h * hd:d + (h + 1) * hd]) for h in heads], axis=-1)
    xa = _dot(o, wo_ref[...])
    o_ref[0] = _layer_norm(DEEPNORM_ALPHA * x + xa, g_ref[...], b_ref[...])


def _cross_attention_ln(x, kv, wq, wo, g, b, *, tm):
    bsz, s, d = x.shape
    m = kv.shape[1]
    return pl.pallas_call(
        _xattn_kernel,
        grid=(bsz, s // tm),
        in_specs=[pl.BlockSpec((1, tm, d), lambda bi, i: (bi, i, 0)),
                  pl.BlockSpec((1, m, 2 * d), lambda bi, i: (bi, 0, 0)),
                  pl.BlockSpec((d, d), lambda bi, i: (0, 0)),
                  pl.BlockSpec((d, d), lambda bi, i: (0, 0)),
                  pl.BlockSpec((1, d), lambda bi, i: (0, 0)),
                  pl.BlockSpec((1, d), lambda bi, i: (0, 0))],
        out_specs=pl.BlockSpec((1, tm, d), lambda bi, i: (bi, i, 0)),
        out_shape=jax.ShapeDtypeStruct((bsz, s, d), F32),
        compiler_params=_params("parallel", "parallel"),
        name="cross_attention_ln",
    )(x, kv, wq, wo, g.reshape(1, d), b.reshape(1, d))


def _ssd_chunk(c, z_ref, xbc_ref, xbcp_ref, dt_ref, dtt_ref, cw_ref, cb_ref, dtb_ref, dtbt_ref,
               aneg_ref, anegt_ref, dskip_ref, ng_ref, y_ref, state_ref, ext_ref):
    q = SSD_CHUNK
    width = z_ref.shape[-1]
    n_pairs = width // LANES

    ext_ref[0:SUBLANES, :] = jnp.where(c > 0, xbcp_ref[0], 0.0)
    ext_ref[SUBLANES:SUBLANES + q, :] = xbc_ref[0]
    xcs = []
    for sl in _col_chunks(ext_ref.shape[1], 256):
        conv = cb_ref[:, sl] + cw_ref[SSD_CONV - 1:SSD_CONV, sl] * ext_ref[SUBLANES:SUBLANES + q, sl]
        for k in range(SSD_CONV - 1):
            off = SUBLANES - (SSD_CONV - 1) + k
            conv = conv + cw_ref[k:k + 1, sl] * ext_ref[off:off + q, sl]
        xcs.append(conv * _sigmoid(conv))
        yield
    xc = jnp.concatenate(xcs, axis=1)
    xs = xc[:, :width]
    gn = SSD_GROUPS * SSD_STATE
    bm = xc[:, width:width + gn]
    cm = xc[:, width + gn:width + 2 * gn]

    dt = _softplus(dt_ref[0] + dtb_ref[...])
    a = dt * aneg_ref[...]
    dtt = _softplus(dtt_ref[0] + dtbt_ref[...])
    at = dtt * anegt_ref[...]
    row = lax.broadcasted_iota(jnp.int32, (q, q), 0)
    col = lax.broadcasted_iota(jnp.int32, (q, q), 1)
    causal = row >= col
    tri = jnp.where(causal, 1.0, 0.0)
    a_cum = _dot_exact_lhs(tri, a)
    a_cumt = _dot_exact_rhs(at, jnp.where(row <= col, 1.0, 0.0))
    yield

    lane = lax.broadcasted_iota(jnp.int32, (1, LANES), 1)
    lane_lo = lane < HEAD_DIM
    rowp = lax.broadcasted_iota(jnp.int32, (LANES, 1), 0)
    pairs_per_group = n_pairs // SSD_GROUPS
    ys = []
    for p in range(n_pairs):
        g = p // pairs_per_group
        h0, h1 = 2 * p, 2 * p + 1
        bg = bm[:, g * SSD_STATE:(g + 1) * SSD_STATE]
        cg = cm[:, g * SSD_STATE:(g + 1) * SSD_STATE]
        cb = _dot_nt(cg, bg)
        xs_p = xs[:, p * LANES:(p + 1) * LANES]
        dt_p = jnp.where(lane_lo, dt[:, h0:h0 + 1], dt[:, h1:h1 + 1])
        acum_p = jnp.where(lane_lo, a_cum[:, h0:h0 + 1], a_cum[:, h1:h1 + 1])
        xdt = xs_p * dt_p
        ms = []
        for h in (h0, h1):
            seg = a_cum[:, h:h + 1] - a_cumt[h:h + 1, :]
            ms.append(cb * jnp.exp(jnp.where(causal, seg, NEG_BIG)))
        m2 = jnp.concatenate(ms, axis=1)
        x2 = jnp.concatenate([jnp.where(lane_lo, xdt, 0.0), jnp.where(lane_lo, 0.0, xdt)], axis=0)
        y_diag = _dot(m2, x2)
        prev = state_ref[p]
        y_off = _dot_nt(cg, prev) * jnp.exp(acum_p)
        a_last = jnp.where(lane_lo, a_cum[q - 1:q, h0:h0 + 1], a_cum[q - 1:q, h1:h1 + 1])
        xdw = xdt * jnp.exp(a_last - acum_p)
        st = _dot_tn(xdw, bg)
        cd = jnp.where(rowp < HEAD_DIM, jnp.exp(a_cumt[h0:h0 + 1, q - 1:q]),
                       jnp.exp(a_cumt[h1:h1 + 1, q - 1:q]))
        state_ref[p] = prev * cd + st
        d_p = dskip_ref[:, p * LANES:(p + 1) * LANES]
        ys.append(y_diag + y_off + d_p * xs_p)
        yield
    y = jnp.concatenate(ys, axis=1)
    z = z_ref[0]
    y = y * (z * _sigmoid(z))
    gw = width // SSD_GROUPS
    outs = []
    for g in range(SSD_GROUPS):
        yg = y[:, g * gw:(g + 1) * gw]
        outs.append(yg * lax.rsqrt(jnp.mean(yg * yg, axis=-1, keepdims=True) + RMS_EPS))
    y_ref[0] = (jnp.concatenate(outs, axis=1) * ng_ref[...]).astype(y_ref.dtype)


def _ssd_operands(z, xbc, dt_pad, conv_w, conv_b, dt_bias, a_log, d_skip, norm_g):
    bsz, l, width = z.shape
    heads = width // HEAD_DIM
    xw = xbc.shape[-1]
    q = SSD_CHUNK
    dtt = jnp.swapaxes(dt_pad[:, :, :heads], 1, 2)
    pad = LANES - heads
    dtb = jnp.pad(dt_bias, (0, pad)).reshape(1, LANES)
    a_neg = -jnp.exp(a_log.astype(F32))
    aneg = jnp.pad(a_neg, (0, pad)).reshape(1, LANES)
    dskip = jnp.repeat(d_skip, HEAD_DIM).reshape(1, width)
    row = lambda n: pl.BlockSpec((1, n), lambda bi, c: (0, 0))
    specs = [pl.BlockSpec((1, q, width), lambda bi, c: (bi, c, 0)),
             pl.BlockSpec((1, q, xw), lambda bi, c: (bi, c, 0)),
             pl.BlockSpec((1, SUBLANES, xw), lambda bi, c: (bi, jnp.maximum(c * (q // SUBLANES) - 1, 0), 0)),
             pl.BlockSpec((1, q, LANES), lambda bi, c: (bi, c, 0)),
             pl.BlockSpec((1, heads, q), lambda bi, c: (bi, 0, c)),
             pl.BlockSpec((SSD_CONV, xw), lambda bi, c: (0, 0)),
             row(xw), row(LANES),
             pl.BlockSpec((heads, 1), lambda bi, c: (0, 0)),
             row(LANES),
             pl.BlockSpec((heads, 1), lambda bi, c: (0, 0)),
             row(width), row(width)]
    args = [z, xbc, xbc, dt_pad, dtt, conv_w, conv_b.reshape(1, xw), dtb, dt_bias.reshape(heads, 1),
            aneg, a_neg.reshape(heads, 1), dskip, norm_g.reshape(1, width)]
    scratch = [pltpu.VMEM((width // LANES, LANES, SSD_STATE), F32), pltpu.VMEM((q + SUBLANES, xw), F32)]
    return args, specs, scratch


def _rwkv_prep_kernel(n_extra, x_ref, xp_ref, win_ref, mu_ref, w0_ref, a0_ref, kk_ref, ka_ref, w2_ref, a2_ref,
                      g2_ref, e_ref, et_ref, *refs):
    extra_w = refs[:n_extra]
    r_ref, lw_ref, k_ref, v_ref, an_ref, bn_ref, g_ref = refs[n_extra:n_extra + 7]
    extra_o = refs[n_extra + 7:2 * n_extra + 7]
    ext_ref = refs[-1]
    tm = x_ref.shape[1]
    width = r_ref.shape[-1]
    i = pl.program_id(1)
    xe = jnp.concatenate([jnp.where(i > 0, xp_ref[0], 0.0), x_ref[0]], axis=0).astype(BF16)
    win = ext_ref.shape[1]
    lora = RWKV_DECAY_LORA + RWKV_ICLR_LORA

    def project(c0, c1):
        ext_ref[:, c0:c1] = jnp.dot(xe, win_ref[:, c0:c1], preferred_element_type=F32)

    sub = min(tm, RWKV_PREP_ROWS)
    rows = slice(0, sub)

    def shifted_mix(c0, c1):
        rw = ext_ref[SUBLANES + rows.start:SUBLANES + rows.stop, c0:c1]
        prev = ext_ref[SUBLANES - 1 + rows.start:SUBLANES - 1 + rows.stop, c0:c1]
        return rw + (prev - rw) * mu_ref[:, c0:c1]

    def projections():
        half = width // 2
        extras = [(w_ref, o_ref, sl) for w_ref, o_ref in zip(extra_w, extra_o)
                  for sl in _col_chunks(w_ref.shape[1], 256)]
        own = [(3 * width, win), (width, width + half), (width + half, 2 * width), (0, half), (half, width),
               (2 * width, 2 * width + half), (2 * width + half, 3 * width)]
        for n, cols in enumerate(own):
            project(*cols)
            yield
            if n >= 2 and extras:
                w_ref, o_ref, sl = extras.pop(0)
                o_ref[0, :, sl] = jnp.dot(xe[SUBLANES:], w_ref[:, sl], preferred_element_type=F32)
                yield
        for w_ref, o_ref, sl in extras:
            o_ref[0, :, sl] = jnp.dot(xe[SUBLANES:], w_ref[:, sl], preferred_element_type=F32)
            yield

    pieces = projections()

    def mxu(n=1):
        for _ in range(n):
            next(pieces, None)

    for r0 in range(0, tm, sub):
        rows = slice(r0, r0 + sub)
        mxu(3)
        s_lo = shifted_mix(3 * width, win)
        lo = s_lo[:, :lora]
        wv = w0_ref[...] + _dot(jnp.tanh(lo), w2_ref[...])
        av = _sigmoid(a0_ref[...] + _dot(lo, a2_ref[...]))
        g_ref[0, rows, :] = _dot(_sigmoid(s_lo[:, lora:]), g2_ref[...]).astype(g_ref.dtype)
        mxu(1)
        for sl in _col_chunks(width, 512):
            lw_ref[0, rows, sl] = -jnp.exp(-_softplus(-wv[:, sl]) - 0.5)
            mxu(1)
        k = shifted_mix(width, 2 * width)
        kkr = k * kk_ref[...]
        mxu(1)
        ss = _dot_exact_rhs(kkr * kkr, e_ref[...])
        k_ref[0, rows, :] = (k * (1.0 + (av - 1.0) * ka_ref[...])).astype(k_ref.dtype)
        mxu(2)
        inv = lax.rsqrt(jnp.maximum(ss, 1e-24))
        kk = kkr * _dot_exact_rhs(inv, et_ref[...])
        mxu(1)
        an_ref[0, rows, :] = (-kk).astype(an_ref.dtype)
        bn_ref[0, rows, :] = (kk * av).astype(bn_ref.dtype)
        mxu(2)
        r_ref[0, rows, :] = shifted_mix(0, width).astype(r_ref.dtype)
        mxu(1)
        v_ref[0, rows, :] = shifted_mix(2 * width, 3 * width).astype(v_ref.dtype)
    for _ in pieces:
        pass


def _rwkv_chunk(rows, fill, r_ref, lw_ref, k_ref, v_ref, an_ref, bn_ref, g_ref, rk_ref, lg_ref, lb_ref,
                y_ref, state_ref):
    c = RWKV_CHUNK
    c2 = 2 * c
    width = r_ref.shape[-1]
    n_pairs = width // LANES

    row = lax.broadcasted_iota(jnp.int32, (c, c), 0)
    col = lax.broadcasted_iota(jnp.int32, (c, c), 1)
    tri = jnp.where(row >= col, 1.0, 0.0)
    roww = lax.broadcasted_iota(jnp.int32, (c, c2), 0)
    colw = lax.broadcasted_iota(jnp.int32, (c, c2), 1) % c
    strict = jnp.where(roww > colw, 1.0, 0.0)
    incl = jnp.where(roww >= colw, 1.0, 0.0)
    eye = jnp.where(roww == colw, 1.0, 0.0)
    same_head = jnp.where((lax.broadcasted_iota(jnp.int32, (LANES, LANES), 0) // HEAD_DIM)
                          == (lax.broadcasted_iota(jnp.int32, (LANES, LANES), 1) // HEAD_DIM), 1.0, 0.0)
    lane = lax.broadcasted_iota(jnp.int32, (1, LANES), 1)
    m0 = jnp.where(lane < HEAD_DIM, 1.0, 0.0)
    m1 = 1.0 - m0

    lane_lo2 = lax.broadcasted_iota(jnp.int32, (c, LANES), 1) < HEAD_DIM

    def stack(x):
        xb = x.astype(BF16)
        zero = jnp.zeros_like(xb)
        return jnp.concatenate([jnp.where(lane_lo2, xb, zero), jnp.where(lane_lo2, zero, xb)], axis=0)

    sls = [slice(p * LANES, (p + 1) * LANES) for p in range(n_pairs)]
    lhs2s, rhs4s, vs, v_stks, bks, g_lasts = [], [], [], [], [], []
    for sl in sls:
        r = r_ref[0, rows, sl].astype(F32)
        lw = lw_ref[0, rows, sl]
        k = k_ref[0, rows, sl].astype(F32)
        b = bn_ref[0, rows, sl].astype(F32)
        v = v_ref[0, rows, sl].astype(F32)
        cum = _dot_exact_lhs(tri, lw)
        cum_last = cum[c - 1:c, :]
        e_neg = jnp.exp(-cum)
        e_tail = jnp.exp(cum_last - cum)
        at = an_ref[0, rows, sl].astype(F32) * jnp.exp(cum - lw)
        rt = r * jnp.exp(cum)
        lhs2s.append(jnp.concatenate([at, rt], axis=0).astype(BF16))
        rhs4s.append(jnp.concatenate([stack(b * e_neg), stack(k * e_neg)], axis=0))
        vs.append(v.astype(BF16))
        v_stks.append(stack(v))
        bks.append(jnp.concatenate([b * e_tail, k * e_tail], axis=0).astype(BF16))
        g_lasts.append(jnp.exp(cum_last))
    gms = [_dot_nt(l, rh) for l, rh in zip(lhs2s, rhs4s)]
    pws = [gm[:c, :c2] * strict for gm in gms]
    a_aks = [(gm[:c, c2:] * strict).astype(BF16) for gm in gms]
    a_rbks = [jnp.concatenate([gm[c:, :c2] * incl, gm[c:, c2:] * incl], axis=1).astype(BF16) for gm in gms]
    akvs = [_dot(a_ak, v_stk) for a_ak, v_stk in zip(a_aks, v_stks)]
    fill()
    tinvs = [eye + pw for pw in pws]
    pw_stks = [stack(pw) for pw in pws]
    for _ in range(int(math.log2(c)) - 1):
        pws = [_dot(pw, pw_stk) for pw, pw_stk in zip(pws, pw_stks)]
        pw_stks = [stack(pw) for pw in pws]
        tinvs = [tinv + _dot(tinv, pw_stk) for tinv, pw_stk in zip(tinvs, pw_stks)]
        fill()
    sts = [state_ref[p] for p in range(n_pairs)]
    ahrhs = [_dot_nt(l, st) for l, st in zip(lhs2s, sts)]
    fill()
    us = [_dot(tinv, stack(ahrh[:c] + akv)) for tinv, ahrh, akv in zip(tinvs, ahrhs, akvs)]
    fill()
    for p in range(n_pairs):
        uv = jnp.concatenate([us[p].astype(BF16), vs[p]], axis=0)
        state_ref[p] = sts[p] * g_lasts[p] + same_head * _dot_tn(uv, bks[p])
    fill()
    ys = [ahrh[c:] + _dot(a_rbk, jnp.concatenate([stack(u), v_stk], axis=0))
          for ahrh, a_rbk, u, v_stk in zip(ahrhs, a_rbks, us, v_stks)]
    lane_lo = lane < HEAD_DIM

    def head_sum(x):
        lo = jnp.sum(x * m0, axis=-1, keepdims=True)
        hi = jnp.sum(x * m1, axis=-1, keepdims=True)
        return jnp.where(lane_lo, lo, hi)

    bonus = [head_sum(r_ref[0, rows, sl].astype(F32) * k_ref[0, rows, sl].astype(F32) * rk_ref[:, sl]) for sl in sls]
    ycs = [y - head_sum(y) * (1.0 / HEAD_DIM) for y in ys]
    yvs = [head_sum(yc * yc) * (1.0 / HEAD_DIM) for yc in ycs]
    for p, sl in enumerate(sls):
        yn = ycs[p] * lax.rsqrt(yvs[p] + RWKV_LNX_EPS) * lg_ref[:, sl] + lb_ref[:, sl]
        out = (yn + bonus[p] * v_ref[0, rows, sl].astype(F32)) * g_ref[0, rows, sl].astype(F32)
        y_ref[0, rows, sl] = out.astype(y_ref.dtype)


def _rwkv_operands(x, w_in, extra_ws, mu, w0, w2, a0, a2, g2, k_k, k_a, r_k, lnx_g, lnx_b, *, tm):
    bsz, l, d = x.shape
    n_extra = len(extra_ws)
    win = w_in.shape[1]
    width = w0.shape[0]
    w2p = jnp.concatenate([w2, jnp.zeros((RWKV_ICLR_LORA, width), F32)], axis=0).astype(BF16)
    a2p = jnp.concatenate([jnp.zeros((RWKV_DECAY_LORA, width), F32), a2], axis=0).astype(BF16)
    head_of = jnp.arange(width) // HEAD_DIM
    e = (head_of[:, None] == jnp.arange(LANES)[None, :]).astype(BF16)
    et = e.T
    vec = lambda x: x.reshape(1, -1)
    row = lambda n: pl.BlockSpec((1, n), lambda bi, i: (0, 0))
    full = lambda a: pl.BlockSpec(a.shape, lambda bi, i: (0, 0))
    single = lambda a: pl.BlockSpec(a.shape, lambda bi, i: (0, 0), pipeline_mode=pl.Buffered(1))
    tile = lambda w: pl.BlockSpec((1, tm, w), lambda bi, i: (bi, i, 0))
    sds = lambda w, dt=F32: jax.ShapeDtypeStruct((bsz, l, w), dt)
    g2b = g2.astype(BF16)
    outs = pl.pallas_call(
        functools.partial(_rwkv_prep_kernel, n_extra),
        grid=(bsz, l // tm),
        in_specs=[pl.BlockSpec((1, tm, d), lambda bi, i: (bi, i, 0)),
                  pl.BlockSpec((1, SUBLANES, d), lambda bi, i: (bi, jnp.maximum(i * (tm // SUBLANES) - 1, 0), 0)),
                  single(w_in), row(win), row(width), row(width), row(width), row(width),
                  full(w2p), full(a2p), full(g2b), full(e), full(et)] + [single(w) for w in extra_ws],
        out_specs=[tile(width)] * 7 + [tile(w.shape[1]) for w in extra_ws],
        out_shape=([sds(width, BF16), sds(width)] + [sds(width, BF16)] * 5
                   + [sds(w.shape[1]) for w in extra_ws]),
        scratch_shapes=[pltpu.VMEM((tm + SUBLANES, win), F32)],
        compiler_params=_params("parallel", "parallel"),
        name="rwkv_prep",
    )(x, x, w_in, vec(mu), vec(w0), vec(a0), vec(k_k), vec(k_a), w2p, a2p, g2b, e, et, *extra_ws)
    r, lw, k, v, an, bn, g = outs[:7]
    ctile = pl.BlockSpec((1, SSD_CHUNK, width), lambda bi, ci: (bi, ci, 0))
    crow = pl.BlockSpec((1, width), lambda bi, ci: (0, 0))
    args = [r, lw, k, v, an, bn, g, vec(r_k), vec(lnx_g), vec(lnx_b)]
    scratch = [pltpu.VMEM((width // LANES, LANES, LANES), F32)]
    return outs[7:], (args, [ctile] * 7 + [crow] * 3, scratch)


def _mixers_kernel(n_ssd, n_rwkv, *refs):
    ssd_in = refs[:n_ssd]
    rwkv_in = refs[n_ssd:n_ssd + n_rwkv]
    y_ssd_ref, y_rwkv_ref, ssd_state, ssd_ext, rwkv_state = refs[n_ssd + n_rwkv:]
    c = pl.program_id(1)

    @pl.when(c == 0)
    def _():
        ssd_state[...] = jnp.zeros_like(ssd_state)
        rwkv_state[...] = jnp.zeros_like(rwkv_state)

    ssd_steps = _ssd_chunk(c, *ssd_in, y_ssd_ref, ssd_state, ssd_ext)
    fill = lambda: next(ssd_steps, None)
    for sub in range(SSD_CHUNK // RWKV_CHUNK):
        _rwkv_chunk(slice(sub * RWKV_CHUNK, (sub + 1) * RWKV_CHUNK), fill, *rwkv_in, y_rwkv_ref, rwkv_state)
    for _ in ssd_steps:
        pass


def _ssd_rwkv_mixers(ssd_ops, rwkv_ops, bsz, l, ssd_width, rwkv_width):
    ssd_args, ssd_specs, ssd_scratch = ssd_ops
    rwkv_args, rwkv_specs, rwkv_scratch = rwkv_ops
    q = SSD_CHUNK
    out_spec = lambda w: pl.BlockSpec((1, q, w), lambda bi, c: (bi, c, 0))
    return pl.pallas_call(
        functools.partial(_mixers_kernel, len(ssd_args), len(rwkv_args)),
        grid=(bsz, l // q),
        in_specs=ssd_specs + rwkv_specs,
        out_specs=[out_spec(ssd_width), out_spec(rwkv_width)],
        out_shape=[jax.ShapeDtypeStruct((bsz, l, ssd_width), BF16), jax.ShapeDtypeStruct((bsz, l, rwkv_width), BF16)],
        scratch_shapes=ssd_scratch + rwkv_scratch,
        compiler_params=_params("parallel", "arbitrary"),
        name="ssd_rwkv_mixers",
    )(*ssd_args, *rwkv_args)


def _moba_kernel(q_ref, qn_ref, k_ref, v_ref, o_ref, kb_ref, vt_ref, kmean_ref, qo_ref, qp_ref):
    blk = MOBA_BLOCK
    half = HEAD_DIM
    nb = k_ref.shape[0] // blk
    nbp = kmean_ref.shape[0]
    npair = q_ref.shape[1] // LANES
    heads = range(2 * npair)
    group = math.gcd(nb, MOBA_GROUP)
    qi = pl.program_id(2)

    @pl.when(qi == 0)
    def _():
        lane = lax.broadcasted_iota(jnp.int32, (blk, LANES), 1)
        rowp = lax.broadcasted_iota(jnp.int32, (LANES, blk), 0)
        if nbp > nb:
            kmean_ref[...] = jnp.zeros_like(kmean_ref)
        for n in range(nb):
            rows = slice(n * blk, (n + 1) * blk)
            kmean_ref[n:n + 1, :] = jnp.mean(k_ref[rows, :], axis=0, keepdims=True)
            for pp in range(npair):
                cols = slice(pp * LANES, (pp + 1) * LANES)
                kn = k_ref[rows, cols]
                kb_ref[2 * pp, n] = jnp.where(lane < half, kn, jnp.where(lane == half + n, 1.0, 0.0)).astype(BF16)
                kb_ref[2 * pp + 1, n] = jnp.where(lane >= half, kn, jnp.where(lane == n, 1.0, 0.0)).astype(BF16)
                vtn = v_ref[rows, cols].T
                vt_ref[2 * pp, n] = jnp.where(rowp < half, vtn, jnp.where(rowp == half, 1.0, 0.0)).astype(BF16)
                vt_ref[2 * pp + 1, n] = jnp.where(rowp >= half, vtn, jnp.where(rowp == 0, 1.0, 0.0)).astype(BF16)

    rown = lax.broadcasted_iota(jnp.int32, (nbp, blk), 0)
    lane_k = lax.broadcasted_iota(jnp.int32, (nbp, LANES), 1)
    zeros = jnp.zeros((half, blk), F32)

    def prepare_queries(src_ref, tile):
        qts = [src_ref[:, pp * LANES:(pp + 1) * LANES].T for pp in range(npair)]
        gates, qhs = [], []
        for hh in heads:
            pp, h = divmod(hh, 2)
            km = kmean_ref[:, pp * LANES:(pp + 1) * LANES]
            kmh = jnp.where((lane_k < half) if h == 0 else (lane_k >= half), km, 0.0)
            gates.append(jnp.dot(kmh, qts[pp], preferred_element_type=F32, precision=lax.Precision.HIGHEST))
            qhs.append(qts[pp][h * half:(h + 1) * half] * (half ** -0.5 * LOG2_E))
        for hh in heads:
            gate = jnp.where(rown < tile, gates[hh], -jnp.inf)
            bias = jnp.full((nbp, blk), NEG_BIG, F32)
            for _ in range(MOBA_TOPK):
                mx = jnp.max(gate, axis=0, keepdims=True)
                first = jnp.min(jnp.where(gate == mx, rown, nbp), axis=0, keepdims=True)
                pick = (rown == first) & (mx > -jnp.inf)
                bias = jnp.where(pick, 0.0, bias)
                gate = jnp.where(pick, -jnp.inf, gate)
            aug = jnp.concatenate([bias, jnp.zeros((half - nbp, blk), F32)], axis=0)
            qo_ref[hh] = jnp.concatenate([qhs[hh], zeros] if hh % 2 == 0 else [zeros, qhs[hh]],
                                         axis=0).astype(BF16)
            qp_ref[hh] = jnp.concatenate([qhs[hh], aug] if hh % 2 == 0 else [aug, qhs[hh]], axis=0).astype(BF16)

    @pl.when(qi == 0)
    def _():
        prepare_queries(q_ref, 0)

    q_own = [qo_ref[h] for h in heads]
    q_past = [qp_ref[h] for h in heads]
    s_own = [jnp.dot(kb_ref[h, qi], q_own[h], preferred_element_type=F32) for h in heads]
    prepare_queries(qn_ref, qi + 1)

    causal = (lax.broadcasted_iota(jnp.int32, (blk, blk), 0) <= lax.broadcasted_iota(jnp.int32, (blk, blk), 1))
    ms, ps = [], []
    for h in heads:
        s = jnp.where(causal, s_own[h], NEG_BIG)
        ms.append(jnp.max(s, axis=0, keepdims=True))
        ps.append(jnp.exp2(s - ms[h]).astype(BF16))
    carry = []
    for h in heads:
        carry += [ms[h], jnp.dot(vt_ref[h, qi], ps[h], preferred_element_type=F32)]

    def scores(gi, h):
        return [jnp.dot(kb_ref[h, gi * group + g], q_past[h], preferred_element_type=F32) for g in range(group)]

    def values(gi, h):
        return jnp.concatenate([vt_ref[h, gi * group + g] for g in range(group)], axis=1)

    def body_lagged(gi, carry):
        excess = carry[-1]
        out = []
        sss = [scores(gi, h) for h in heads]
        pcats, gmaxs = [], []
        for h in heads:
            m_run = carry[2 * h]
            gmax = None
            ps = []
            for s in sss[h]:
                cm = jnp.max(s, axis=0, keepdims=True)
                gmax = cm if gmax is None else jnp.maximum(gmax, cm)
                ps.append(jnp.exp2(s - m_run).astype(BF16))
            pcats.append(jnp.concatenate(ps, axis=0))
            gmaxs.append(gmax)
        for h in heads:
            m_run, acc = carry[2 * h], carry[2 * h + 1]
            m_new = jnp.maximum(m_run, gmaxs[h])
            excess = jnp.maximum(excess, gmaxs[h] - m_run)
            acc = jnp.exp2(m_run - m_new) * (acc + jnp.dot(values(gi, h), pcats[h], preferred_element_type=F32))
            out += [m_new, acc]
        return tuple(out) + (excess,)

    def body_exact_max(gi, carry):
        sss = [scores(gi, h) for h in heads]
        m_news = []
        for h in heads:
            m_new = carry[2 * h]
            for s in sss[h]:
                m_new = jnp.maximum(m_new, jnp.max(s, axis=0, keepdims=True))
            m_news.append(m_new)
        pcats = [jnp.concatenate([jnp.exp2(s - m_news[h]).astype(BF16) for s in sss[h]], axis=0)
                 for h in heads]
        out = []
        for h in heads:
            alpha = jnp.exp2(carry[2 * h] - m_news[h])
            out += [m_news[h], alpha * carry[2 * h + 1]
                    + jnp.dot(values(gi, h), pcats[h], preferred_element_type=F32)]
        return tuple(out)

    rowq = lax.broadcasted_iota(jnp.int32, (LANES, blk), 0)

    def write_out(final):
        for pp in range(npair):
            acc0, acc1 = final[4 * pp + 1], final[4 * pp + 3]
            out_t = jnp.where(rowq < half, acc0 / acc0[half:half + 1], acc1 / acc1[0:1])
            o_ref[:, pp * LANES:(pp + 1) * LANES] = out_t.T.astype(o_ref.dtype)

    n_groups = (qi + group - 1) // group
    final = lax.fori_loop(0, n_groups, body_lagged, tuple(carry) + (jnp.full((1, blk), NEG_BIG, F32),))
    write_out(final)

    @pl.when(jnp.max(final[-1]) > MOBA_LAG_LIMIT)
    def _():
        write_out(lax.fori_loop(0, n_groups, body_exact_max, tuple(carry)))


def _moba_attention(qkv, bsz, s, heads):
    blk = MOBA_BLOCK
    assert s % blk == 0 and (heads * HEAD_DIM) % LANES == 0
    nb = s // blk
    assert nb <= HEAD_DIM
    nbp = -(-nb // SUBLANES) * SUBLANES
    pairs = heads * HEAD_DIM // LANES
    pps = math.gcd(pairs, MOBA_PAIRS_PER_STEP)
    cw = pps * LANES
    steps = pairs // pps
    return pl.pallas_call(
        _moba_kernel,
        grid=(bsz, steps, nb),
        in_specs=[pl.BlockSpec((blk, cw), lambda b, p, i: (b * nb + i, p)),
                  pl.BlockSpec((blk, cw), lambda b, p, i: (b * nb + jnp.minimum(i + 1, nb - 1), p)),
                  pl.BlockSpec((s, cw), lambda b, p, i: (b, steps + p)),
                  pl.BlockSpec((s, cw), lambda b, p, i: (b, 2 * steps + p))],
        out_specs=pl.BlockSpec((blk, cw), lambda b, p, i: (b * nb + i, p)),
        out_shape=jax.ShapeDtypeStruct((bsz * s, heads * HEAD_DIM), BF16),
        scratch_shapes=[pltpu.VMEM((2 * pps, nb, blk, LANES), BF16), pltpu.VMEM((2 * pps, nb, LANES, blk), BF16),
                        pltpu.VMEM((nbp, cw), F32),
                        pltpu.VMEM((2 * pps, LANES, blk), BF16), pltpu.VMEM((2 * pps, LANES, blk), BF16)],
        compiler_params=_params("parallel", "parallel", "arbitrary"),
        name="moba_attention",
    )(qkv, qkv, qkv, qkv)


def _row_tile(m):
    for t in (512, 256, 128, 64, 32, 16, 8):
        if m % t == 0:
            return t
    raise ValueError(f"row count {m} is not a multiple of 8")


def _col_tile(n, cap=2048):
    best = None
    for t in range(LANES, min(n, cap) + 1, LANES):
        if n % t == 0:
            best = t
    if best is None:
        raise ValueError(f"column count {n} is not a multiple of {LANES}")
    return best


def kernel(x, mem, even_w_in, ssd_conv_w, ssd_conv_b, ssd_dt_bias, ssd_a_log, ssd_d, ssd_norm_g, rwkv_mu, rwkv_w0, rwkv_w2, rwkv_a0, rwkv_a2, rwkv_g2, rwkv_k_k, rwkv_k_a, rwkv_r_k, rwkv_lnx_g, rwkv_lnx_b, even_w_out, odd_w_qkv, odd_w_out, ln_mix_g, ln_mix_b, xa_wq, xa_wkv, xa_wo, ln_xa_g, ln_xa_b, ffn_w13, ffn_w2, ln_ffn_g, ln_ffn_b):
    bsz, s, d = x.shape
    m = bsz * s
    tm = _row_tile(s)
    ssd_width = ssd_norm_g.shape[-1]
    ssd_heads = ssd_dt_bias.shape[-1]
    ssd_xbc = ssd_conv_b.shape[-1]
    ssd_in = ssd_width + ssd_xbc + ssd_heads
    rwkv_width = rwkv_w0.shape[-1]
    mem2 = mem.reshape(bsz * mem.shape[1], d)
    x2 = x.reshape(m, d)
    for layer in range(DEPTH):
        j = layer // 2
        if layer % 2 == 0:
            w_in = even_w_in[j].astype(BF16)
            w_z = w_in[:, :ssd_width]
            w_xbc = w_in[:, ssd_width:ssd_width + ssd_xbc]
            w_dt = jnp.pad(w_in[:, ssd_width + ssd_xbc:ssd_in], ((0, 0), (0, LANES - ssd_heads)))
            w_rw = w_in[:, ssd_in:]
            (z, xbc, dt_pad), rwkv_ops = _rwkv_operands(
                x2.reshape(bsz, s, d), w_rw, [w_z, w_xbc, w_dt], rwkv_mu[j], rwkv_w0[j], rwkv_w2[j],
                rwkv_a0[j], rwkv_a2[j], rwkv_g2[j], rwkv_k_k[j], rwkv_k_a[j], rwkv_r_k[j],
                rwkv_lnx_g[j], rwkv_lnx_b[j], tm=tm)
            ssd_ops = _ssd_operands(z, xbc, dt_pad, ssd_conv_w[j], ssd_conv_b[j], ssd_dt_bias[j],
                                    ssd_a_log[j], ssd_d[j], ssd_norm_g[j])
            y_ssd, y_rwkv = _ssd_rwkv_mixers(ssd_ops, rwkv_ops, bsz, s, ssd_width, rwkv_width)
            x2 = _matmul_residual_ln([y_ssd.reshape(m, -1), y_rwkv.reshape(m, -1)],
                                     even_w_out[j].astype(BF16), x2,
                                     ln_mix_g[layer], ln_mix_b[layer], tm=tm)
        else:
            heads = d // HEAD_DIM
            qkv, = _matmul(x2, [odd_w_qkv[j].astype(BF16)], tm=tm)
            attn = _moba_attention(qkv, bsz, s, heads)
            x2 = _matmul_residual_ln([attn], odd_w_out[j].astype(BF16), x2,
                                     ln_mix_g[layer], ln_mix_b[layer], tm=tm)
        kv, = _matmul(mem2, [xa_wkv[layer].astype(BF16)], tm=_row_tile(mem2.shape[0]))
        x3 = _cross_attention_ln(x2.reshape(bsz, s, d), kv.reshape(bsz, -1, 2 * d),
                                 xa_wq[layer].astype(BF16), xa_wo[layer].astype(BF16),
                                 ln_xa_g[layer], ln_xa_b[layer], tm=tm)
        x2 = x3.reshape(m, d)
        h = _swiglu_up(x2, ffn_w13[layer].astype(BF16), tm=tm)
        x2 = _matmul_residual_ln([h], ffn_w2[layer].astype(BF16), x2,
                                 ln_ffn_g[layer], ln_ffn_b[layer], tm=tm)
    return x2.reshape(bsz, s, d)
```

```python
import functools
import math

import jax
import jax.numpy as jnp
from jax import lax
from jax.experimental import pallas as pl
from jax.experimental.pallas import tpu as pltpu

F32 = jnp.float32
BF16 = jnp.bfloat16

HEAD_DIM = 64
LANES = 128
SUBLANES = 8
SSD_GROUPS = 2
SSD_STATE = 128
SSD_CONV = 4
SSD_CHUNK = 128
RWKV_DECAY_LORA = 64
RWKV_ICLR_LORA = 64
RWKV_GATE_LORA = 128
RWKV_CHUNK = 64
RWKV_PREP_ROWS = 256
MOBA_BLOCK = 256
MOBA_TOPK = 3
MOBA_GROUP = 4
MOBA_LAG_LIMIT = 64.0
MOBA_PAIRS_PER_STEP = 2
XATTN_HEADS = 4
DEPTH = 2
DEEPNORM_ALPHA = (2 * DEPTH) ** 0.25
LN_EPS = 1e-5
RMS_EPS = 1e-5
RWKV_LNX_EPS = 64e-5
NEG_BIG = -1e30
LOG2_E = math.log2(math.e)
VMEM_LIMIT = 60 * 1024 * 1024


def _params(*sem):
    return pltpu.CompilerParams(dimension_semantics=sem, vmem_limit_bytes=VMEM_LIMIT)


def _dot(a, b):
    return jnp.dot(a.astype(BF16), b.astype(BF16), preferred_element_type=F32)


def _dot_nt(a, b):
    return lax.dot_general(a.astype(BF16), b.astype(BF16), (((1,), (1,)), ((), ())),
                           preferred_element_type=F32)


def _dot_tn(a, b):
    return lax.dot_general(a.astype(BF16), b.astype(BF16), (((0,), (0,)), ((), ())),
                           preferred_element_type=F32)


def _split3(x):
    hi = x.astype(BF16)
    r1 = x - hi.astype(F32)
    mid = r1.astype(BF16)
    lo = (r1 - mid.astype(F32)).astype(BF16)
    return hi, mid, lo


def _dot_exact_lhs(m, x):
    hi, mid, lo = _split3(x)
    m = m.astype(BF16)
    return (jnp.dot(m, hi, preferred_element_type=F32) + jnp.dot(m, mid, preferred_element_type=F32)
            + jnp.dot(m, lo, preferred_element_type=F32))


def _dot_exact_rhs(x, m):
    hi, mid, lo = _split3(x)
    m = m.astype(BF16)
    return (jnp.dot(hi, m, preferred_element_type=F32) + jnp.dot(mid, m, preferred_element_type=F32)
            + jnp.dot(lo, m, preferred_element_type=F32))


def _sigmoid(x):
    return 1.0 / (1.0 + jnp.exp(-x))


def _softplus(x):
    return jnp.maximum(x, 0.0) + jnp.log1p(jnp.exp(-jnp.abs(x)))


def _layer_norm(v, g, b):
    mu = jnp.mean(v, axis=-1, keepdims=True)
    c = v - mu
    var = jnp.mean(c * c, axis=-1, keepdims=True)
    return c * lax.rsqrt(var + LN_EPS) * g + b


def _col_chunks(n, cap=1536):
    width = _col_tile(n, cap)
    return [slice(j, j + width) for j in range(0, n, width)]


def _mm_kernel(n_out, x_ref, *refs):
    x = x_ref[...].astype(BF16)
    for w_ref, o_ref in zip(refs[:n_out], refs[n_out:]):
        for sl in _col_chunks(w_ref.shape[1]):
            o_ref[:, sl] = jnp.dot(x, w_ref[:, sl], preferred_element_type=F32).astype(o_ref.dtype)


def _matmul(x, ws, *, tm, out_dtype=F32):
    m, k = x.shape
    assert m % tm == 0
    return pl.pallas_call(
        functools.partial(_mm_kernel, len(ws)),
        grid=(m // tm,),
        in_specs=[pl.BlockSpec((tm, k), lambda i: (i, 0))] + [pl.BlockSpec(w.shape, lambda i: (0, 0)) for w in ws],
        out_specs=[pl.BlockSpec((tm, w.shape[1]), lambda i: (i, 0)) for w in ws],
        out_shape=[jax.ShapeDtypeStruct((m, w.shape[1]), out_dtype) for w in ws],
        compiler_params=_params("parallel"),
        name="matmul",
    )(x, *ws)


def _mm_res_ln_kernel(n_in, *refs):
    hs = refs[:n_in]
    ws = refs[n_in:2 * n_in]
    res_ref, g_ref, b_ref, o_ref = refs[2 * n_in:]
    acc = _dot(hs[0][...], ws[0][...])
    for h_ref, w_ref in zip(hs[1:], ws[1:]):
        acc = acc + _dot(h_ref[...], w_ref[...])
    o_ref[...] = _layer_norm(DEEPNORM_ALPHA * res_ref[...] + acc, g_ref[...], b_ref[...])


def _matmul_residual_ln(hs, w, res, g, b, *, tm):
    m, d = res.shape
    n_in = len(hs)
    kw = w.shape[0] // n_in
    assert all(h.shape[1] == kw for h in hs)
    in_specs = ([pl.BlockSpec((tm, kw), lambda i: (i, 0)) for _ in hs]
                + [pl.BlockSpec((kw, d), lambda i, j=j: (j, 0)) for j in range(n_in)]
                + [pl.BlockSpec((tm, d), lambda i: (i, 0)),
                   pl.BlockSpec((1, d), lambda i: (0, 0)),
                   pl.BlockSpec((1, d), lambda i: (0, 0))])
    return pl.pallas_call(
        functools.partial(_mm_res_ln_kernel, n_in),
        grid=(m // tm,),
        in_specs=in_specs,
        out_specs=pl.BlockSpec((tm, d), lambda i: (i, 0)),
        out_shape=jax.ShapeDtypeStruct((m, d), F32),
        compiler_params=_params("parallel"),
        name="matmul_residual_ln",
    )(*hs, *([w] * n_in), res, g.reshape(1, d), b.reshape(1, d))


def _swiglu_kernel(x_ref, w13_ref, o_ref):
    x = x_ref[...].astype(BF16)
    n = o_ref.shape[1]
    for sl in _col_chunks(n):
        gate = jnp.dot(x, w13_ref[:, sl], preferred_element_type=F32)
        up = jnp.dot(x, w13_ref[:, slice(n + sl.start, n + sl.stop)], preferred_element_type=F32)
        o_ref[:, sl] = (gate * _sigmoid(gate) * up).astype(o_ref.dtype)


def _swiglu_up(x, w13, *, tm):
    m, k = x.shape
    n = w13.shape[1] // 2
    return pl.pallas_call(
        _swiglu_kernel,
        grid=(m // tm,),
        in_specs=[pl.BlockSpec((tm, k), lambda i: (i, 0)),
                  pl.BlockSpec((k, 2 * n), lambda i: (0, 0))],
        out_specs=pl.BlockSpec((tm, n), lambda i: (i, 0)),
        out_shape=jax.ShapeDtypeStruct((m, n), BF16),
        compiler_params=_params("parallel"),
        name="swiglu_up",
    )(x, w13)


def _xattn_kernel(x_ref, kv_ref, wq_ref, wo_ref, g_ref, b_ref, o_ref):
    x = x_ref[0]
    d = x.shape[-1]
    hd = d // XATTN_HEADS
    q = _dot(x, wq_ref[...])
    kv = kv_ref[0]
    heads = range(XATTN_HEADS)
    ss = [_dot_nt(q[:, h * hd:(h + 1) * hd], kv[:, h * hd:(h + 1) * hd]) * (hd ** -0.5) for h in heads]
    ps = []
    for s in ss:
        p = jnp.exp(s - jnp.max(s, axis=-1, keepdims=True))
        ps.append(p / jnp.sum(p, axis=-1, keepdims=True))
    o = jnp.concatenate([_dot(ps[h], kv[:, d + h * hd:d + (h + 1) * hd]) for h in heads], axis=-1)
    xa = _dot(o, wo_ref[...])
    o_ref[0] = _layer_norm(DEEPNORM_ALPHA * x + xa, g_ref[...], b_ref[...])


def _cross_attention_ln(x, kv, wq, wo, g, b, *, tm):
    bsz, s, d = x.shape
    m = kv.shape[1]
    return pl.pallas_call(
        _xattn_kernel,
        grid=(bsz, s // tm),
        in_specs=[pl.BlockSpec((1, tm, d), lambda bi, i: (bi, i, 0)),
                  pl.BlockSpec((1, m, 2 * d), lambda bi, i: (bi, 0, 0)),
                  pl.BlockSpec((d, d), lambda bi, i: (0, 0)),
                  pl.BlockSpec((d, d), lambda bi, i: (0, 0)),
                  pl.BlockSpec((1, d), lambda bi, i: (0, 0)),
                  pl.BlockSpec((1, d), lambda bi, i: (0, 0))],
        out_specs=pl.BlockSpec((1, tm, d), lambda bi, i: (bi, i, 0)),
        out_shape=jax.ShapeDtypeStruct((bsz, s, d), F32),
        compiler_params=_params("parallel", "parallel"),
        name="cross_attention_ln",
    )(x, kv, wq, wo, g.reshape(1, d), b.reshape(1, d))


def _ssd_chunk(c, z_ref, xbc_ref, xbcp_ref, dt_ref, dtt_ref, cw_ref, cb_ref, dtb_ref, dtbt_ref,
               aneg_ref, anegt_ref, dskip_ref, ng_ref, y_ref, state_ref, ext_ref):
    q = SSD_CHUNK
    width = z_ref.shape[-1]
    n_pairs = width // LANES

    ext_ref[0:SUBLANES, :] = jnp.where(c > 0, xbcp_ref[0], 0.0)
    ext_ref[SUBLANES:SUBLANES + q, :] = xbc_ref[0]
    xcs = []
    for sl in _col_chunks(ext_ref.shape[1], 256):
        conv = cb_ref[:, sl] + cw_ref[SSD_CONV - 1:SSD_CONV, sl] * ext_ref[SUBLANES:SUBLANES + q, sl]
        for k in range(SSD_CONV - 1):
            off = SUBLANES - (SSD_CONV - 1) + k
            conv = conv + cw_ref[k:k + 1, sl] * ext_ref[off:off + q, sl]
        xcs.append(conv * _sigmoid(conv))
        yield
    xc = jnp.concatenate(xcs, axis=1)
    xs = xc[:, :width]
    gn = SSD_GROUPS * SSD_STATE
    bm = xc[:, width:width + gn]
    cm = xc[:, width + gn:width + 2 * gn]

    dt = _softplus(dt_ref[0] + dtb_ref[...])
    a = dt * aneg_ref[...]
    dtt = _softplus(dtt_ref[0] + dtbt_ref[...])
    at = dtt * anegt_ref[...]
    row = lax.broadcasted_iota(jnp.int32, (q, q), 0)
    col = lax.broadcasted_iota(jnp.int32, (q, q), 1)
    causal = row >= col
    tri = jnp.where(causal, 1.0, 0.0)
    a_cum = _dot_exact_lhs(tri, a)
    a_cumt = _dot_exact_rhs(at, jnp.where(row <= col, 1.0, 0.0))
    yield

    lane = lax.broadcasted_iota(jnp.int32, (1, LANES), 1)
    lane_lo = lane < HEAD_DIM
    rowp = lax.broadcasted_iota(jnp.int32, (LANES, 1), 0)
    pairs_per_group = n_pairs // SSD_GROUPS
    ys = []
    for p in range(n_pairs):
        g = p // pairs_per_group
        h0, h1 = 2 * p, 2 * p + 1
        bg = bm[:, g * SSD_STATE:(g + 1) * SSD_STATE]
        cg = cm[:, g * SSD_STATE:(g + 1) * SSD_STATE]
        cb = _dot_nt(cg, bg)
        xs_p = xs[:, p * LANES:(p + 1) * LANES]
        dt_p = jnp.where(lane_lo, dt[:, h0:h0 + 1], dt[:, h1:h1 + 1])
        acum_p = jnp.where(lane_lo, a_cum[:, h0:h0 + 1], a_cum[:, h1:h1 + 1])
        xdt = xs_p * dt_p
        ms = []
        for h in (h0, h1):
            seg = a_cum[:, h:h + 1] - a_cumt[h:h + 1, :]
            ms.append(cb * jnp.exp(jnp.where(causal, seg, NEG_BIG)))
        m2 = jnp.concatenate(ms, axis=1)
        x2 = jnp.concatenate([jnp.where(lane_lo, xdt, 0.0), jnp.where(lane_lo, 0.0, xdt)], axis=0)
        y_diag = _dot(m2, x2)
        prev = state_ref[p]
        y_off = _dot_nt(cg, prev) * jnp.exp(acum_p)
        a_last = jnp.where(lane_lo, a_cum[q - 1:q, h0:h0 + 1], a_cum[q - 1:q, h1:h1 + 1])
        xdw = xdt * jnp.exp(a_last - acum_p)
        st = _dot_tn(xdw, bg)
        cd = jnp.where(rowp < HEAD_DIM, jnp.exp(a_cumt[h0:h0 + 1, q - 1:q]),
                       jnp.exp(a_cumt[h1:h1 + 1, q - 1:q]))
        state_ref[p] = prev * cd + st
        d_p = dskip_ref[:, p * LANES:(p + 1) * LANES]
        ys.append(y_diag + y_off + d_p * xs_p)
        yield
    y = jnp.concatenate(ys, axis=1)
    z = z_ref[0]
    y = y * (z * _sigmoid(z))
    gw = width // SSD_GROUPS
    outs = []
    for g in range(SSD_GROUPS):
        yg = y[:, g * gw:(g + 1) * gw]
        outs.append(yg * lax.rsqrt(jnp.mean(yg * yg, axis=-1, keepdims=True) + RMS_EPS))
    y_ref[0] = (jnp.concatenate(outs, axis=1) * ng_ref[...]).astype(y_ref.dtype)


def _ssd_operands(z, xbc, dt_pad, conv_w, conv_b, dt_bias, a_log, d_skip, norm_g):
    bsz, l, width = z.shape
    heads = width // HEAD_DIM
    xw = xbc.shape[-1]
    q = SSD_CHUNK
    dtt = jnp.swapaxes(dt_pad[:, :, :heads], 1, 2)
    pad = LANES - heads
    dtb = jnp.pad(dt_bias, (0, pad)).reshape(1, LANES)
    a_neg = -jnp.exp(a_log.astype(F32))
    aneg = jnp.pad(a_neg, (0, pad)).reshape(1, LANES)
    dskip = jnp.repeat(d_skip, HEAD_DIM).reshape(1, width)
    row = lambda n: pl.BlockSpec((1, n), lambda bi, c: (0, 0))
    specs = [pl.BlockSpec((1, q, width), lambda bi, c: (bi, c, 0)),
             pl.BlockSpec((1, q, xw), lambda bi, c: (bi, c, 0)),
             pl.BlockSpec((1, SUBLANES, xw), lambda bi, c: (bi, jnp.maximum(c * (q // SUBLANES) - 1, 0), 0)),
             pl.BlockSpec((1, q, LANES), lambda bi, c: (bi, c, 0)),
             pl.BlockSpec((1, heads, q), lambda bi, c: (bi, 0, c)),
             pl.BlockSpec((SSD_CONV, xw), lambda bi, c: (0, 0)),
             row(xw), row(LANES),
             pl.BlockSpec((heads, 1), lambda bi, c: (0, 0)),
             row(LANES),
             pl.BlockSpec((heads, 1), lambda bi, c: (0, 0)),
             row(width), row(width)]
    args = [z, xbc, xbc, dt_pad, dtt, conv_w, conv_b.reshape(1, xw), dtb, dt_bias.reshape(heads, 1),
            aneg, a_neg.reshape(heads, 1), dskip, norm_g.reshape(1, width)]
    scratch = [pltpu.VMEM((width // LANES, LANES, SSD_STATE), F32), pltpu.VMEM((q + SUBLANES, xw), F32)]
    return args, specs, scratch


def _rwkv_prep_kernel(n_extra, x_ref, xp_ref, win_ref, mu_ref, w0_ref, a0_ref, kk_ref, ka_ref, w2_ref, a2_ref,
                      g2_ref, e_ref, et_ref, *refs):
    extra_w = refs[:n_extra]
    r_ref, lw_ref, k_ref, v_ref, an_ref, bn_ref, g_ref = refs[n_extra:n_extra + 7]
    extra_o = refs[n_extra + 7:2 * n_extra + 7]
    ext_ref = refs[-1]
    tm = x_ref.shape[1]
    width = r_ref.shape[-1]
    i = pl.program_id(1)
    xe = jnp.concatenate([jnp.where(i > 0, xp_ref[0], 0.0), x_ref[0]], axis=0).astype(BF16)
    win = ext_ref.shape[1]
    lora = RWKV_DECAY_LORA + RWKV_ICLR_LORA

    def project(c0, c1):
        ext_ref[:, c0:c1] = jnp.dot(xe, win_ref[:, c0:c1], preferred_element_type=F32)

    sub = min(tm, RWKV_PREP_ROWS)
    rows = slice(0, sub)

    def shifted_mix(c0, c1):
        rw = ext_ref[SUBLANES + rows.start:SUBLANES + rows.stop, c0:c1]
        prev = ext_ref[SUBLANES - 1 + rows.start:SUBLANES - 1 + rows.stop, c0:c1]
        return rw + (prev - rw) * mu_ref[:, c0:c1]

    def projections():
        half = width // 2
        extras = [(w_ref, o_ref, sl) for w_ref, o_ref in zip(extra_w, extra_o)
                  for sl in _col_chunks(w_ref.shape[1], 256)]
        own = [(3 * width, win), (width, width + half), (width + half, 2 * width), (0, half), (half, width),
               (2 * width, 2 * width + half), (2 * width + half, 3 * width)]
        for n, cols in enumerate(own):
            project(*cols)
            yield
            if n >= 2 and extras:
                w_ref, o_ref, sl = extras.pop(0)
                o_ref[0, :, sl] = jnp.dot(xe[SUBLANES:], w_ref[:, sl], preferred_element_type=F32)
                yield
        for w_ref, o_ref, sl in extras:
            o_ref[0, :, sl] = jnp.dot(xe[SUBLANES:], w_ref[:, sl], preferred_element_type=F32)
            yield

    pieces = projections()

    def mxu(n=1):
        for _ in range(n):
            next(pieces, None)

    for r0 in range(0, tm, sub):
        rows = slice(r0, r0 + sub)
        mxu(3)
        s_lo = shifted_mix(3 * width, win)
        lo = s_lo[:, :lora]
        wv = w0_ref[...] + _dot(jnp.tanh(lo), w2_ref[...])
        av = _sigmoid(a0_ref[...] + _dot(lo, a2_ref[...]))
        g_ref[0, rows, :] = _dot(_sigmoid(s_lo[:, lora:]), g2_ref[...]).astype(g_ref.dtype)
        mxu(1)
        for sl in _col_chunks(width, 512):
            lw_ref[0, rows, sl] = -jnp.exp(-_softplus(-wv[:, sl]) - 0.5)
            mxu(1)
        k = shifted_mix(width, 2 * width)
        kkr = k * kk_ref[...]
        mxu(1)
        ss = _dot_exact_rhs(kkr * kkr, e_ref[...])
        k_ref[0, rows, :] = (k * (1.0 + (av - 1.0) * ka_ref[...])).astype(k_ref.dtype)
        mxu(2)
        inv = lax.rsqrt(jnp.maximum(ss, 1e-24))
        kk = kkr * _dot_exact_rhs(inv, et_ref[...])
        mxu(1)
        an_ref[0, rows, :] = (-kk).astype(an_ref.dtype)
        bn_ref[0, rows, :] = (kk * av).astype(bn_ref.dtype)
        mxu(2)
        r_ref[0, rows, :] = shifted_mix(0, width).astype(r_ref.dtype)
        mxu(1)
        v_ref[0, rows, :] = shifted_mix(2 * width, 3 * width).astype(v_ref.dtype)
    for _ in pieces:
        pass


def _rwkv_chunks(rows_list, fill, r_ref, lw_ref, k_ref, v_ref, an_ref, bn_ref, g_ref, rk_ref, lg_ref, lb_ref,
                 y_ref, state_ref):
    c = RWKV_CHUNK
    c2 = 2 * c
    width = r_ref.shape[-1]
    n_pairs = width // LANES

    row = lax.broadcasted_iota(jnp.int32, (c, c), 0)
    col = lax.broadcasted_iota(jnp.int32, (c, c), 1)
    tri = jnp.where(row >= col, 1.0, 0.0)
    roww = lax.broadcasted_iota(jnp.int32, (c, c2), 0)
    colw = lax.broadcasted_iota(jnp.int32, (c, c2), 1) % c
    strict = jnp.where(roww > colw, 1.0, 0.0)
    incl = jnp.where(roww >= colw, 1.0, 0.0)
    eye = jnp.where(roww == colw, 1.0, 0.0)
    same_head = jnp.where((lax.broadcasted_iota(jnp.int32, (LANES, LANES), 0) // HEAD_DIM)
                          == (lax.broadcasted_iota(jnp.int32, (LANES, LANES), 1) // HEAD_DIM), 1.0, 0.0)
    lane = lax.broadcasted_iota(jnp.int32, (1, LANES), 1)
    m0 = jnp.where(lane < HEAD_DIM, 1.0, 0.0)
    m1 = 1.0 - m0

    lane_lo2 = lax.broadcasted_iota(jnp.int32, (c, LANES), 1) < HEAD_DIM

    def stack(x):
        xb = x.astype(BF16)
        zero = jnp.zeros_like(xb)
        return jnp.concatenate([jnp.where(lane_lo2, xb, zero), jnp.where(lane_lo2, zero, xb)], axis=0)

    sls = [slice(p * LANES, (p + 1) * LANES) for p in range(n_pairs)]
    units = [(rows, sl) for rows in rows_list for sl in sls]
    lhs2s, rhs4s, vs, v_stks, bks, g_lasts = [], [], [], [], [], []
    for rows, sl in units:
        r = r_ref[0, rows, sl].astype(F32)
        lw = lw_ref[0, rows, sl]
        k = k_ref[0, rows, sl].astype(F32)
        b = bn_ref[0, rows, sl].astype(F32)
        v = v_ref[0, rows, sl].astype(F32)
        cum = _dot_exact_lhs(tri, lw)
        cum_last = cum[c - 1:c, :]
        e_neg = jnp.exp(-cum)
        e_tail = jnp.exp(cum_last - cum)
        at = an_ref[0, rows, sl].astype(F32) * jnp.exp(cum - lw)
        rt = r * jnp.exp(cum)
        lhs2s.append(jnp.concatenate([at, rt], axis=0).astype(BF16))
        rhs4s.append(jnp.concatenate([stack(b * e_neg), stack(k * e_neg)], axis=0))
        vs.append(v.astype(BF16))
        v_stks.append(stack(v))
        bks.append(jnp.concatenate([b * e_tail, k * e_tail], axis=0).astype(BF16))
        g_lasts.append(jnp.exp(cum_last))
    gms = [_dot_nt(l, rh) for l, rh in zip(lhs2s, rhs4s)]
    pws = [gm[:c, :c2] * strict for gm in gms]
    a_aks = [(gm[:c, c2:] * strict).astype(BF16) for gm in gms]
    a_rbks = [jnp.concatenate([gm[c:, :c2] * incl, gm[c:, c2:] * incl], axis=1).astype(BF16) for gm in gms]
    akvs = [_dot(a_ak, v_stk) for a_ak, v_stk in zip(a_aks, v_stks)]
    fill()
    tinvs = [eye + pw for pw in pws]
    pw_stks = [stack(pw) for pw in pws]
    for _ in range(int(math.log2(c)) - 1):
        pws = [_dot(pw, pw_stk) for pw, pw_stk in zip(pws, pw_stks)]
        pw_stks = [stack(pw) for pw in pws]
        tinvs = [tinv + _dot(tinv, pw_stk) for tinv, pw_stk in zip(tinvs, pw_stks)]
        fill()
    lane_lo = lane < HEAD_DIM

    def head_sum(x):
        lo = jnp.sum(x * m0, axis=-1, keepdims=True)
        hi = jnp.sum(x * m1, axis=-1, keepdims=True)
        return jnp.where(lane_lo, lo, hi)

    for si, rows in enumerate(rows_list):
        u0 = si * n_pairs
        sts = [state_ref[p] for p in range(n_pairs)]
        ahrhs = [_dot_nt(lhs2s[u0 + p], sts[p]) for p in range(n_pairs)]
        fill()
        us = [_dot(tinvs[u0 + p], stack(ahrhs[p][:c] + akvs[u0 + p])) for p in range(n_pairs)]
        fill()
        for p in range(n_pairs):
            uv = jnp.concatenate([us[p].astype(BF16), vs[u0 + p]], axis=0)
            state_ref[p] = sts[p] * g_lasts[u0 + p] + same_head * _dot_tn(uv, bks[u0 + p])
        fill()
        ys = [ahrhs[p][c:] + _dot(a_rbks[u0 + p], jnp.concatenate([stack(us[p]), v_stks[u0 + p]], axis=0))
              for p in range(n_pairs)]
        bonus = [head_sum(r_ref[0, rows, sl].astype(F32) * k_ref[0, rows, sl].astype(F32) * rk_ref[:, sl])
                 for sl in sls]
        ycs = [y - head_sum(y) * (1.0 / HEAD_DIM) for y in ys]
        yvs = [head_sum(yc * yc) * (1.0 / HEAD_DIM) for yc in ycs]
        for p, sl in enumerate(sls):
            yn = ycs[p] * lax.rsqrt(yvs[p] + RWKV_LNX_EPS) * lg_ref[:, sl] + lb_ref[:, sl]
            out = (yn + bonus[p] * v_ref[0, rows, sl].astype(F32)) * g_ref[0, rows, sl].astype(F32)
            y_ref[0, rows, sl] = out.astype(y_ref.dtype)


def _rwkv_operands(x, w_in, extra_ws, mu, w0, w2, a0, a2, g2, k_k, k_a, r_k, lnx_g, lnx_b, *, tm):
    bsz, l, d = x.shape
    n_extra = len(extra_ws)
    win = w_in.shape[1]
    width = w0.shape[0]
    w2p = jnp.concatenate([w2, jnp.zeros((RWKV_ICLR_LORA, width), F32)], axis=0).astype(BF16)
    a2p = jnp.concatenate([jnp.zeros((RWKV_DECAY_LORA, width), F32), a2], axis=0).astype(BF16)
    head_of = jnp.arange(width) // HEAD_DIM
    e = (head_of[:, None] == jnp.arange(LANES)[None, :]).astype(BF16)
    et = e.T
    vec = lambda x: x.reshape(1, -1)
    row = lambda n: pl.BlockSpec((1, n), lambda bi, i: (0, 0))
    full = lambda a: pl.BlockSpec(a.shape, lambda bi, i: (0, 0))
    single = lambda a: pl.BlockSpec(a.shape, lambda bi, i: (0, 0), pipeline_mode=pl.Buffered(1))
    tile = lambda w: pl.BlockSpec((1, tm, w), lambda bi, i: (bi, i, 0))
    sds = lambda w, dt=F32: jax.ShapeDtypeStruct((bsz, l, w), dt)
    g2b = g2.astype(BF16)
    outs = pl.pallas_call(
        functools.partial(_rwkv_prep_kernel, n_extra),
        grid=(bsz, l // tm),
        in_specs=[pl.BlockSpec((1, tm, d), lambda bi, i: (bi, i, 0)),
                  pl.BlockSpec((1, SUBLANES, d), lambda bi, i: (bi, jnp.maximum(i * (tm // SUBLANES) - 1, 0), 0)),
                  single(w_in), row(win), row(width), row(width), row(width), row(width),
                  full(w2p), full(a2p), full(g2b), full(e), full(et)] + [single(w) for w in extra_ws],
        out_specs=[tile(width)] * 7 + [tile(w.shape[1]) for w in extra_ws],
        out_shape=([sds(width, BF16), sds(width)] + [sds(width, BF16)] * 5
                   + [sds(w.shape[1]) for w in extra_ws]),
        scratch_shapes=[pltpu.VMEM((tm + SUBLANES, win), F32)],
        compiler_params=_params("parallel", "parallel"),
        name="rwkv_prep",
    )(x, x, w_in, vec(mu), vec(w0), vec(a0), vec(k_k), vec(k_a), w2p, a2p, g2b, e, et, *extra_ws)
    r, lw, k, v, an, bn, g = outs[:7]
    ctile = pl.BlockSpec((1, SSD_CHUNK, width), lambda bi, ci: (bi, ci, 0))
    crow = pl.BlockSpec((1, width), lambda bi, ci: (0, 0))
    args = [r, lw, k, v, an, bn, g, vec(r_k), vec(lnx_g), vec(lnx_b)]
    scratch = [pltpu.VMEM((width // LANES, LANES, LANES), F32)]
    return outs[7:], (args, [ctile] * 7 + [crow] * 3, scratch)


def _mixers_kernel(n_ssd, n_rwkv, *refs):
    ssd_in = refs[:n_ssd]
    rwkv_in = refs[n_ssd:n_ssd + n_rwkv]
    y_ssd_ref, y_rwkv_ref, ssd_state, ssd_ext, rwkv_state = refs[n_ssd + n_rwkv:]
    c = pl.program_id(1)

    @pl.when(c == 0)
    def _():
        ssd_state[...] = jnp.zeros_like(ssd_state)
        rwkv_state[...] = jnp.zeros_like(rwkv_state)

    ssd_steps = _ssd_chunk(c, *ssd_in, y_ssd_ref, ssd_state, ssd_ext)
    fill = lambda: next(ssd_steps, None)
    rows_list = [slice(sub * RWKV_CHUNK, (sub + 1) * RWKV_CHUNK) for sub in range(SSD_CHUNK // RWKV_CHUNK)]
    _rwkv_chunks(rows_list, fill, *rwkv_in, y_rwkv_ref, rwkv_state)
    for _ in ssd_steps:
        pass


def _ssd_rwkv_mixers(ssd_ops, rwkv_ops, bsz, l, ssd_width, rwkv_width):
    ssd_args, ssd_specs, ssd_scratch = ssd_ops
    rwkv_args, rwkv_specs, rwkv_scratch = rwkv_ops
    q = SSD_CHUNK
    out_spec = lambda w: pl.BlockSpec((1, q, w), lambda bi, c: (bi, c, 0))
    return pl.pallas_call(
        functools.partial(_mixers_kernel, len(ssd_args), len(rwkv_args)),
        grid=(bsz, l // q),
        in_specs=ssd_specs + rwkv_specs,
        out_specs=[out_spec(ssd_width), out_spec(rwkv_width)],
        out_shape=[jax.ShapeDtypeStruct((bsz, l, ssd_width), BF16), jax.ShapeDtypeStruct((bsz, l, rwkv_width), BF16)],
        scratch_shapes=ssd_scratch + rwkv_scratch,
        compiler_params=_params("parallel", "arbitrary"),
        name="ssd_rwkv_mixers",
    )(*ssd_args, *rwkv_args)


def _moba_kernel(q_ref, qn_ref, k_ref, v_ref, o_ref, kb_ref, vt_ref, kmean_ref, qo_ref, qp_ref):
    blk = MOBA_BLOCK
    half = HEAD_DIM
    nb = k_ref.shape[0] // blk
    nbp = kmean_ref.shape[0]
    npair = q_ref.shape[1] // LANES
    heads = range(2 * npair)
    group = math.gcd(nb, MOBA_GROUP)
    qi = pl.program_id(2)

    @pl.when(qi == 0)
    def _():
        lane = lax.broadcasted_iota(jnp.int32, (blk, LANES), 1)
        rowp = lax.broadcasted_iota(jnp.int32, (LANES, blk), 0)
        if nbp > nb:
            kmean_ref[...] = jnp.zeros_like(kmean_ref)
        for n in range(nb):
            rows = slice(n * blk, (n + 1) * blk)
            kmean_ref[n:n + 1, :] = jnp.mean(k_ref[rows, :], axis=0, keepdims=True)
            for pp in range(npair):
                cols = slice(pp * LANES, (pp + 1) * LANES)
                kn = k_ref[rows, cols]
                kb_ref[2 * pp, n] = jnp.where(lane < half, kn, jnp.where(lane == half + n, 1.0, 0.0)).astype(BF16)
                kb_ref[2 * pp + 1, n] = jnp.where(lane >= half, kn, jnp.where(lane == n, 1.0, 0.0)).astype(BF16)
                vtn = v_ref[rows, cols].T
                vt_ref[2 * pp, n] = jnp.where(rowp < half, vtn, jnp.where(rowp == half, 1.0, 0.0)).astype(BF16)
                vt_ref[2 * pp + 1, n] = jnp.where(rowp >= half, vtn, jnp.where(rowp == 0, 1.0, 0.0)).astype(BF16)

    rown = lax.broadcasted_iota(jnp.int32, (nbp, blk), 0)
    lane_k = lax.broadcasted_iota(jnp.int32, (nbp, LANES), 1)
    zeros = jnp.zeros((half, blk), F32)

    def prepare_queries(src_ref, tile):
        qts = [src_ref[:, pp * LANES:(pp + 1) * LANES].T for pp in range(npair)]
        gates, qhs = [], []
        for hh in heads:
            pp, h = divmod(hh, 2)
            km = kmean_ref[:, pp * LANES:(pp + 1) * LANES]
            kmh = jnp.where((lane_k < half) if h == 0 else (lane_k >= half), km, 0.0)
            gates.append(jnp.dot(kmh, qts[pp], preferred_element_type=F32, precision=lax.Precision.HIGHEST))
            qhs.append(qts[pp][h * half:(h + 1) * half] * (half ** -0.5 * LOG2_E))
        for hh in heads:
            gate = jnp.where(rown < tile, gates[hh], -jnp.inf)
            bias = jnp.full((nbp, blk), NEG_BIG, F32)
            for _ in range(MOBA_TOPK):
                mx = jnp.max(gate, axis=0, keepdims=True)
                first = jnp.min(jnp.where(gate == mx, rown, nbp), axis=0, keepdims=True)
                pick = (rown == first) & (mx > -jnp.inf)
                bias = jnp.where(pick, 0.0, bias)
                gate = jnp.where(pick, -jnp.inf, gate)
            aug = jnp.concatenate([bias, jnp.zeros((half - nbp, blk), F32)], axis=0)
            qo_ref[hh] = jnp.concatenate([qhs[hh], zeros] if hh % 2 == 0 else [zeros, qhs[hh]],
                                         axis=0).astype(BF16)
            qp_ref[hh] = jnp.concatenate([qhs[hh], aug] if hh % 2 == 0 else [aug, qhs[hh]], axis=0).astype(BF16)

    @pl.when(qi == 0)
    def _():
        prepare_queries(q_ref, 0)

    q_own = [qo_ref[h] for h in heads]
    q_past = [qp_ref[h] for h in heads]
    s_own = [jnp.dot(kb_ref[h, qi], q_own[h], preferred_element_type=F32) for h in heads]
    prepare_queries(qn_ref, qi + 1)

    causal = (lax.broadcasted_iota(jnp.int32, (blk, blk), 0) <= lax.broadcasted_iota(jnp.int32, (blk, blk), 1))
    ms, ps = [], []
    for h in heads:
        s = jnp.where(causal, s_own[h], NEG_BIG)
        ms.append(jnp.max(s, axis=0, keepdims=True))
        ps.append(jnp.exp2(s - ms[h]).astype(BF16))
    carry = []
    for h in heads:
        carry += [ms[h], jnp.dot(vt_ref[h, qi], ps[h], preferred_element_type=F32)]

    def scores(gi, h):
        return [jnp.dot(kb_ref[h, gi * group + g], q_past[h], preferred_element_type=F32) for g in range(group)]

    def values(gi, h):
        return jnp.concatenate([vt_ref[h, gi * group + g] for g in range(group)], axis=1)

    def body_lagged(gi, carry):
        excess = carry[-1]
        out = []
        sss = [scores(gi, h) for h in heads]
        pcats, gmaxs = [], []
        for h in heads:
            m_run = carry[2 * h]
            gmax = None
            ps = []
            for s in sss[h]:
                cm = jnp.max(s, axis=0, keepdims=True)
                gmax = cm if gmax is None else jnp.maximum(gmax, cm)
                ps.append(jnp.exp2(s - m_run).astype(BF16))
            pcats.append(jnp.concatenate(ps, axis=0))
            gmaxs.append(gmax)
        for h in heads:
            m_run, acc = carry[2 * h], carry[2 * h + 1]
            m_new = jnp.maximum(m_run, gmaxs[h])
            excess = jnp.maximum(excess, gmaxs[h] - m_run)
            acc = jnp.exp2(m_run - m_new) * (acc + jnp.dot(values(gi, h), pcats[h], preferred_element_type=F32))
            out += [m_new, acc]
        return tuple(out) + (excess,)

    def body_exact_max(gi, carry):
        sss = [scores(gi, h) for h in heads]
        m_news = []
        for h in heads:
            m_new = carry[2 * h]
            for s in sss[h]:
                m_new = jnp.maximum(m_new, jnp.max(s, axis=0, keepdims=True))
            m_news.append(m_new)
        pcats = [jnp.concatenate([jnp.exp2(s - m_news[h]).astype(BF16) for s in sss[h]], axis=0)
                 for h in heads]
        out = []
        for h in heads:
            alpha = jnp.exp2(carry[2 * h] - m_news[h])
            out += [m_news[h], alpha * carry[2 * h + 1]
                    + jnp.dot(values(gi, h), pcats[h], preferred_element_type=F32)]
        return tuple(out)

    rowq = lax.broadcasted_iota(jnp.int32, (LANES, blk), 0)

    def write_out(final):
        for pp in range(npair):
            acc0, acc1 = final[4 * pp + 1], final[4 * pp + 3]
            out_t = jnp.where(rowq < half, acc0 / acc0[half:half + 1], acc1 / acc1[0:1])
            o_ref[:, pp * LANES:(pp + 1) * LANES] = out_t.T.astype(o_ref.dtype)

    n_groups = (qi + group - 1) // group
    final = lax.fori_loop(0, n_groups, body_lagged, tuple(carry) + (jnp.full((1, blk), NEG_BIG, F32),))
    write_out(final)

    @pl.when(jnp.max(final[-1]) > MOBA_LAG_LIMIT)
    def _():
        write_out(lax.fori_loop(0, n_groups, body_exact_max, tuple(carry)))


def _moba_attention(qkv, bsz, s, heads):
    blk = MOBA_BLOCK
    assert s % blk == 0 and (heads * HEAD_DIM) % LANES == 0
    nb = s // blk
    assert nb <= HEAD_DIM
    nbp = -(-nb // SUBLANES) * SUBLANES
    pairs = heads * HEAD_DIM // LANES
    pps = math.gcd(pairs, MOBA_PAIRS_PER_STEP)
    cw = pps * LANES
    steps = pairs // pps
    return pl.pallas_call(
        _moba_kernel,
        grid=(bsz, steps, nb),
        in_specs=[pl.BlockSpec((blk, cw), lambda b, p, i: (b * nb + i, p)),
                  pl.BlockSpec((blk, cw), lambda b, p, i: (b * nb + jnp.minimum(i + 1, nb - 1), p)),
                  pl.BlockSpec((s, cw), lambda b, p, i: (b, steps + p)),
                  pl.BlockSpec((s, cw), lambda b, p, i: (b, 2 * steps + p))],
        out_specs=pl.BlockSpec((blk, cw), lambda b, p, i: (b * nb + i, p)),
        out_shape=jax.ShapeDtypeStruct((bsz * s, heads * HEAD_DIM), BF16),
        scratch_shapes=[pltpu.VMEM((2 * pps, nb, blk, LANES), BF16), pltpu.VMEM((2 * pps, nb, LANES, blk), BF16),
                        pltpu.VMEM((nbp, cw), F32),
                        pltpu.VMEM((2 * pps, LANES, blk), BF16), pltpu.VMEM((2 * pps, LANES, blk), BF16)],
        compiler_params=_params("parallel", "parallel", "arbitrary"),
        name="moba_attention",
    )(qkv, qkv, qkv, qkv)


def _row_tile(m):
    for t in (512, 256, 128, 64, 32, 16, 8):
        if m % t == 0:
            return t
    raise ValueError(f"row count {m} is not a multiple of 8")


def _col_tile(n, cap=2048):
    best = None
    for t in range(LANES, min(n, cap) + 1, LANES):
        if n % t == 0:
            best = t
    if best is None:
        raise ValueError(f"column count {n} is not a multiple of {LANES}")
    return best


def kernel(x, mem, even_w_in, ssd_conv_w, ssd_conv_b, ssd_dt_bias, ssd_a_log, ssd_d, ssd_norm_g, rwkv_mu, rwkv_w0, rwkv_w2, rwkv_a0, rwkv_a2, rwkv_g2, rwkv_k_k, rwkv_k_a, rwkv_r_k, rwkv_lnx_g, rwkv_lnx_b, even_w_out, odd_w_qkv, odd_w_out, ln_mix_g, ln_mix_b, xa_wq, xa_wkv, xa_wo, ln_xa_g, ln_xa_b, ffn_w13, ffn_w2, ln_ffn_g, ln_ffn_b):
    bsz, s, d = x.shape
    m = bsz * s
    tm = _row_tile(s)
    ssd_width = ssd_norm_g.shape[-1]
    ssd_heads = ssd_dt_bias.shape[-1]
    ssd_xbc = ssd_conv_b.shape[-1]
    ssd_in = ssd_width + ssd_xbc + ssd_heads
    rwkv_width = rwkv_w0.shape[-1]
    mem2 = mem.reshape(bsz * mem.shape[1], d)
    x2 = x.reshape(m, d)
    for layer in range(DEPTH):
        j = layer // 2
        if layer % 2 == 0:
            w_in = even_w_in[j].astype(BF16)
            w_z = w_in[:, :ssd_width]
            w_xbc = w_in[:, ssd_width:ssd_width + ssd_xbc]
            w_dt = jnp.pad(w_in[:, ssd_width + ssd_xbc:ssd_in], ((0, 0), (0, LANES - ssd_heads)))
            w_rw = w_in[:, ssd_in:]
            (z, xbc, dt_pad), rwkv_ops = _rwkv_operands(
                x2.reshape(bsz, s, d), w_rw, [w_z, w_xbc, w_dt], rwkv_mu[j], rwkv_w0[j], rwkv_w2[j],
                rwkv_a0[j], rwkv_a2[j], rwkv_g2[j], rwkv_k_k[j], rwkv_k_a[j], rwkv_r_k[j],
                rwkv_lnx_g[j], rwkv_lnx_b[j], tm=tm)
            ssd_ops = _ssd_operands(z, xbc, dt_pad, ssd_conv_w[j], ssd_conv_b[j], ssd_dt_bias[j],
                                    ssd_a_log[j], ssd_d[j], ssd_norm_g[j])
            y_ssd, y_rwkv = _ssd_rwkv_mixers(ssd_ops, rwkv_ops, bsz, s, ssd_width, rwkv_width)
            x2 = _matmul_residual_ln([y_ssd.reshape(m, -1), y_rwkv.reshape(m, -1)],
                                     even_w_out[j].astype(BF16), x2,
                                     ln_mix_g[layer], ln_mix_b[layer], tm=tm)
        else:
            heads = d // HEAD_DIM
            qkv, = _matmul(x2, [odd_w_qkv[j].astype(BF16)], tm=tm)
            attn = _moba_attention(qkv, bsz, s, heads)
            x2 = _matmul_residual_ln([attn], odd_w_out[j].astype(BF16), x2,
                                     ln_mix_g[layer], ln_mix_b[layer], tm=tm)
        kv, = _matmul(mem2, [xa_wkv[layer].astype(BF16)], tm=_row_tile(mem2.shape[0]))
        x3 = _cross_attention_ln(x2.reshape(bsz, s, d), kv.reshape(bsz, -1, 2 * d),
                                 xa_wq[layer].astype(BF16), xa_wo[layer].astype(BF16),
                                 ln_xa_g[layer], ln_xa_b[layer], tm=tm)
        x2 = x3.reshape(m, d)
        h = _swiglu_up(x2, ffn_w13[layer].astype(BF16), tm=tm)
        x2 = _matmul_residual_ln([h], ffn_w2[layer].astype(BF16), x2,
                                 ln_ffn_g[layer], ln_ffn_b[layer], tm=tm)
    return x2.reshape(bsz, s, d)
```

```python
import functools
import math

import jax
import jax.numpy as jnp
from jax import lax
from jax.experimental import pallas as pl
from jax.experimental.pallas import tpu as pltpu

F32 = jnp.float32
BF16 = jnp.bfloat16

HEAD_DIM = 64
LANES = 128
SUBLANES = 8
SSD_GROUPS = 2
SSD_STATE = 128
SSD_CONV = 4
SSD_CHUNK = 128
RWKV_DECAY_LORA = 64
RWKV_ICLR_LORA = 64
RWKV_GATE_LORA = 128
RWKV_CHUNK = 64
RWKV_PREP_ROWS = 256
MOBA_BLOCK = 256
MOBA_TOPK = 3
MOBA_GROUP = 4
MOBA_LAG_LIMIT = 64.0
MOBA_PAIRS_PER_STEP = 2
XATTN_HEADS = 4
DEPTH = 2
DEEPNORM_ALPHA = (2 * DEPTH) ** 0.25
LN_EPS = 1e-5
RMS_EPS = 1e-5
RWKV_LNX_EPS = 64e-5
NEG_BIG = -1e30
LOG2_E = math.log2(math.e)
VMEM_LIMIT = 60 * 1024 * 1024


def _params(*sem):
    return pltpu.CompilerParams(dimension_semantics=sem, vmem_limit_bytes=VMEM_LIMIT)


def _dot(a, b):
    return jnp.dot(a.astype(BF16), b.astype(BF16), preferred_element_type=F32)


def _dot_nt(a, b):
    return lax.dot_general(a.astype(BF16), b.astype(BF16), (((1,), (1,)), ((), ())),
                           preferred_element_type=F32)


def _dot_tn(a, b):
    return lax.dot_general(a.astype(BF16), b.astype(BF16), (((0,), (0,)), ((), ())),
                           preferred_element_type=F32)


def _split3(x):
    hi = x.astype(BF16)
    r1 = x - hi.astype(F32)
    mid = r1.astype(BF16)
    lo = (r1 - mid.astype(F32)).astype(BF16)
    return hi, mid, lo


def _dot_exact_lhs(m, x):
    hi, mid, lo = _split3(x)
    m = m.astype(BF16)
    return (jnp.dot(m, hi, preferred_element_type=F32) + jnp.dot(m, mid, preferred_element_type=F32)
            + jnp.dot(m, lo, preferred_element_type=F32))


def _dot_exact_rhs(x, m):
    hi, mid, lo = _split3(x)
    m = m.astype(BF16)
    return (jnp.dot(hi, m, preferred_element_type=F32) + jnp.dot(mid, m, preferred_element_type=F32)
            + jnp.dot(lo, m, preferred_element_type=F32))


def _sigmoid(x):
    return 1.0 / (1.0 + jnp.exp(-x))


def _softplus(x):
    return jnp.maximum(x, 0.0) + jnp.log1p(jnp.exp(-jnp.abs(x)))


def _layer_norm(v, g, b):
    mu = jnp.mean(v, axis=-1, keepdims=True)
    c = v - mu
    var = jnp.mean(c * c, axis=-1, keepdims=True)
    return c * lax.rsqrt(var + LN_EPS) * g + b


def _col_chunks(n, cap=1536):
    width = _col_tile(n, cap)
    return [slice(j, j + width) for j in range(0, n, width)]


def _mm_kernel(n_out, x_ref, *refs):
    x = x_ref[...].astype(BF16)
    for w_ref, o_ref in zip(refs[:n_out], refs[n_out:]):
        for sl in _col_chunks(w_ref.shape[1]):
            o_ref[:, sl] = jnp.dot(x, w_ref[:, sl], preferred_element_type=F32).astype(o_ref.dtype)


def _matmul(x, ws, *, tm, out_dtype=F32):
    m, k = x.shape
    assert m % tm == 0
    return pl.pallas_call(
        functools.partial(_mm_kernel, len(ws)),
        grid=(m // tm,),
        in_specs=[pl.BlockSpec((tm, k), lambda i: (i, 0))] + [pl.BlockSpec(w.shape, lambda i: (0, 0)) for w in ws],
        out_specs=[pl.BlockSpec((tm, w.shape[1]), lambda i: (i, 0)) for w in ws],
        out_shape=[jax.ShapeDtypeStruct((m, w.shape[1]), out_dtype) for w in ws],
        compiler_params=_params("parallel"),
        name="matmul",
    )(x, *ws)


def _mm_res_ln_kernel(n_in, *refs):
    hs = refs[:n_in]
    ws = refs[n_in:2 * n_in]
    res_ref, g_ref, b_ref, o_ref = refs[2 * n_in:]
    acc = _dot(hs[0][...], ws[0][...])
    for h_ref, w_ref in zip(hs[1:], ws[1:]):
        acc = acc + _dot(h_ref[...], w_ref[...])
    o_ref[...] = _layer_norm(DEEPNORM_ALPHA * res_ref[...] + acc, g_ref[...], b_ref[...])


def _matmul_residual_ln(hs, w, res, g, b, *, tm):
    m, d = res.shape
    n_in = len(hs)
    kw = w.shape[0] // n_in
    assert all(h.shape[1] == kw for h in hs)
    in_specs = ([pl.BlockSpec((tm, kw), lambda i: (i, 0)) for _ in hs]
                + [pl.BlockSpec((kw, d), lambda i, j=j: (j, 0)) for j in range(n_in)]
                + [pl.BlockSpec((tm, d), lambda i: (i, 0)),
                   pl.BlockSpec((1, d), lambda i: (0, 0)),
                   pl.BlockSpec((1, d), lambda i: (0, 0))])
    return pl.pallas_call(
        functools.partial(_mm_res_ln_kernel, n_in),
        grid=(m // tm,),
        in_specs=in_specs,
        out_specs=pl.BlockSpec((tm, d), lambda i: (i, 0)),
        out_shape=jax.ShapeDtypeStruct((m, d), F32),
        compiler_params=_params("parallel"),
        name="matmul_residual_ln",
    )(*hs, *([w] * n_in), res, g.reshape(1, d), b.reshape(1, d))


def _swiglu_kernel(x_ref, w13_ref, o_ref):
    x = x_ref[...].astype(BF16)
    n = o_ref.shape[1]
    for sl in _col_chunks(n):
        gate = jnp.dot(x, w13_ref[:, sl], preferred_element_type=F32)
        up = jnp.dot(x, w13_ref[:, slice(n + sl.start, n + sl.stop)], preferred_element_type=F32)
        o_ref[:, sl] = (gate * _sigmoid(gate) * up).astype(o_ref.dtype)


def _swiglu_up(x, w13, *, tm):
    m, k = x.shape
    n = w13.shape[1] // 2
    return pl.pallas_call(
        _swiglu_kernel,
        grid=(m // tm,),
        in_specs=[pl.BlockSpec((tm, k), lambda i: (i, 0)),
                  pl.BlockSpec((k, 2 * n), lambda i: (0, 0))],
        out_specs=pl.BlockSpec((tm, n), lambda i: (i, 0)),
        out_shape=jax.ShapeDtypeStruct((m, n), BF16),
        compiler_params=_params("parallel"),
        name="swiglu_up",
    )(x, w13)


def _xattn_kernel(x_ref, kv_ref, wq_ref, wo_ref, g_ref, b_ref, o_ref):
    x = x_ref[0]
    d = x.shape[-1]
    hd = d // XATTN_HEADS
    q = _dot(x, wq_ref[...])
    kv = kv_ref[0]
    heads = range(XATTN_HEADS)
    ss = [_dot_nt(q[:, h * hd:(h + 1) * hd], kv[:, h * hd:(h + 1) * hd]) * (hd ** -0.5) for h in heads]
    ps = []
    for s in ss:
        p = jnp.exp(s - jnp.max(s, axis=-1, keepdims=True))
        ps.append(p / jnp.sum(p, axis=-1, keepdims=True))
    o = jnp.concatenate([_dot(ps[h], kv[:, d + h * hd:d + (h + 1) * hd]) for h in heads], axis=-1)
    xa = _dot(o, wo_ref[...])
    o_ref[0] = _layer_norm(DEEPNORM_ALPHA * x + xa, g_ref[...], b_ref[...])


def _cross_attention_ln(x, kv, wq, wo, g, b, *, tm):
    bsz, s, d = x.shape
    m = kv.shape[1]
    return pl.pallas_call(
        _xattn_kernel,
        grid=(bsz, s // tm),
        in_specs=[pl.BlockSpec((1, tm, d), lambda bi, i: (bi, i, 0)),
                  pl.BlockSpec((1, m, 2 * d), lambda bi, i: (bi, 0, 0)),
                  pl.BlockSpec((d, d), lambda bi, i: (0, 0)),
                  pl.BlockSpec((d, d), lambda bi, i: (0, 0)),
                  pl.BlockSpec((1, d), lambda bi, i: (0, 0)),
                  pl.BlockSpec((1, d), lambda bi, i: (0, 0))],
        out_specs=pl.BlockSpec((1, tm, d), lambda bi, i: (bi, i, 0)),
        out_shape=jax.ShapeDtypeStruct((bsz, s, d), F32),
        compiler_params=_params("parallel", "parallel"),
        name="cross_attention_ln",
    )(x, kv, wq, wo, g.reshape(1, d), b.reshape(1, d))


def _ssd_chunk(c, z_ref, xbc_ref, xbcp_ref, dt_ref, dtt_ref, cw_ref, cb_ref, dtb_ref, dtbt_ref,
               aneg_ref, anegt_ref, dskip_ref, ng_ref, y_ref, state_ref, ext_ref):
    q = SSD_CHUNK
    width = z_ref.shape[-1]
    n_pairs = width // LANES

    ext_ref[0:SUBLANES, :] = jnp.where(c > 0, xbcp_ref[0], 0.0)
    ext_ref[SUBLANES:SUBLANES + q, :] = xbc_ref[0]
    xcs = []
    for sl in _col_chunks(ext_ref.shape[1], 256):
        conv = cb_ref[:, sl] + cw_ref[SSD_CONV - 1:SSD_CONV, sl] * ext_ref[SUBLANES:SUBLANES + q, sl]
        for k in range(SSD_CONV - 1):
            off = SUBLANES - (SSD_CONV - 1) + k
            conv = conv + cw_ref[k:k + 1, sl] * ext_ref[off:off + q, sl]
        xcs.append(conv * _sigmoid(conv))
        yield
    xc = jnp.concatenate(xcs, axis=1)
    xs = xc[:, :width]
    gn = SSD_GROUPS * SSD_STATE
    bm = xc[:, width:width + gn]
    cm = xc[:, width + gn:width + 2 * gn]

    dt = _softplus(dt_ref[0] + dtb_ref[...])
    a = dt * aneg_ref[...]
    dtt = _softplus(dtt_ref[0] + dtbt_ref[...])
    at = dtt * anegt_ref[...]
    row = lax.broadcasted_iota(jnp.int32, (q, q), 0)
    col = lax.broadcasted_iota(jnp.int32, (q, q), 1)
    causal = row >= col
    tri = jnp.where(causal, 1.0, 0.0)
    a_cum = _dot_exact_lhs(tri, a)
    a_cumt = _dot_exact_rhs(at, jnp.where(row <= col, 1.0, 0.0))
    yield

    lane = lax.broadcasted_iota(jnp.int32, (1, LANES), 1)
    lane_lo = lane < HEAD_DIM
    rowp = lax.broadcasted_iota(jnp.int32, (LANES, 1), 0)
    pairs_per_group = n_pairs // SSD_GROUPS
    ys = []
    for p in range(n_pairs):
        g = p // pairs_per_group
        h0, h1 = 2 * p, 2 * p + 1
        bg = bm[:, g * SSD_STATE:(g + 1) * SSD_STATE]
        cg = cm[:, g * SSD_STATE:(g + 1) * SSD_STATE]
        cb = _dot_nt(cg, bg)
        xs_p = xs[:, p * LANES:(p + 1) * LANES]
        dt_p = jnp.where(lane_lo, dt[:, h0:h0 + 1], dt[:, h1:h1 + 1])
        acum_p = jnp.where(lane_lo, a_cum[:, h0:h0 + 1], a_cum[:, h1:h1 + 1])
        xdt = xs_p * dt_p
        ms = []
        for h in (h0, h1):
            seg = a_cum[:, h:h + 1] - a_cumt[h:h + 1, :]
            ms.append(cb * jnp.exp(jnp.where(causal, seg, NEG_BIG)))
        m2 = jnp.concatenate(ms, axis=1).astype(BF16)
        x2 = jnp.concatenate([jnp.where(lane_lo, xdt, 0.0), jnp.where(lane_lo, 0.0, xdt)], axis=0).astype(BF16)
        a_last = jnp.where(lane_lo, a_cum[q - 1:q, h0:h0 + 1], a_cum[q - 1:q, h1:h1 + 1])
        xdw = (xdt * jnp.exp(a_last - acum_p)).astype(BF16)
        yield
        y_diag = _dot(m2, x2)
        prev = state_ref[p]
        y_off = _dot_nt(cg, prev) * jnp.exp(acum_p)
        st = _dot_tn(xdw, bg)
        cd = jnp.where(rowp < HEAD_DIM, jnp.exp(a_cumt[h0:h0 + 1, q - 1:q]),
                       jnp.exp(a_cumt[h1:h1 + 1, q - 1:q]))
        state_ref[p] = prev * cd + st
        d_p = dskip_ref[:, p * LANES:(p + 1) * LANES]
        ys.append(y_diag + y_off + d_p * xs_p)
        yield
    y = jnp.concatenate(ys, axis=1)
    z = z_ref[0]
    y = y * (z * _sigmoid(z))
    gw = width // SSD_GROUPS
    outs = []
    for g in range(SSD_GROUPS):
        yg = y[:, g * gw:(g + 1) * gw]
        outs.append(yg * lax.rsqrt(jnp.mean(yg * yg, axis=-1, keepdims=True) + RMS_EPS))
    y_ref[0] = (jnp.concatenate(outs, axis=1) * ng_ref[...]).astype(y_ref.dtype)


def _ssd_operands(z, xbc, dt_pad, conv_w, conv_b, dt_bias, a_log, d_skip, norm_g):
    bsz, l, width = z.shape
    heads = width // HEAD_DIM
    xw = xbc.shape[-1]
    q = SSD_CHUNK
    dtt = jnp.swapaxes(dt_pad[:, :, :heads], 1, 2)
    pad = LANES - heads
    dtb = jnp.pad(dt_bias, (0, pad)).reshape(1, LANES)
    a_neg = -jnp.exp(a_log.astype(F32))
    aneg = jnp.pad(a_neg, (0, pad)).reshape(1, LANES)
    dskip = jnp.repeat(d_skip, HEAD_DIM).reshape(1, width)
    row = lambda n: pl.BlockSpec((1, n), lambda bi, c: (0, 0))
    specs = [pl.BlockSpec((1, q, width), lambda bi, c: (bi, c, 0)),
             pl.BlockSpec((1, q, xw), lambda bi, c: (bi, c, 0)),
             pl.BlockSpec((1, SUBLANES, xw), lambda bi, c: (bi, jnp.maximum(c * (q // SUBLANES) - 1, 0), 0)),
             pl.BlockSpec((1, q, LANES), lambda bi, c: (bi, c, 0)),
             pl.BlockSpec((1, heads, q), lambda bi, c: (bi, 0, c)),
             pl.BlockSpec((SSD_CONV, xw), lambda bi, c: (0, 0)),
             row(xw), row(LANES),
             pl.BlockSpec((heads, 1), lambda bi, c: (0, 0)),
             row(LANES),
             pl.BlockSpec((heads, 1), lambda bi, c: (0, 0)),
             row(width), row(width)]
    args = [z, xbc, xbc, dt_pad, dtt, conv_w, conv_b.reshape(1, xw), dtb, dt_bias.reshape(heads, 1),
            aneg, a_neg.reshape(heads, 1), dskip, norm_g.reshape(1, width)]
    scratch = [pltpu.VMEM((width // LANES, LANES, SSD_STATE), F32), pltpu.VMEM((q + SUBLANES, xw), F32)]
    return args, specs, scratch


def _rwkv_prep_kernel(n_extra, x_ref, xp_ref, win_ref, mu_ref, w0_ref, a0_ref, kk_ref, ka_ref, w2_ref, a2_ref,
                      g2_ref, e_ref, et_ref, *refs):
    extra_w = refs[:n_extra]
    r_ref, lw_ref, k_ref, v_ref, an_ref, bn_ref, g_ref = refs[n_extra:n_extra + 7]
    extra_o = refs[n_extra + 7:2 * n_extra + 7]
    ext_ref = refs[-1]
    tm = x_ref.shape[1]
    width = r_ref.shape[-1]
    i = pl.program_id(1)
    xe = jnp.concatenate([jnp.where(i > 0, xp_ref[0], 0.0), x_ref[0]], axis=0).astype(BF16)
    win = ext_ref.shape[1]
    lora = RWKV_DECAY_LORA + RWKV_ICLR_LORA

    def project(c0, c1):
        ext_ref[:, c0:c1] = jnp.dot(xe, win_ref[:, c0:c1], preferred_element_type=F32)

    sub = min(tm, RWKV_PREP_ROWS)
    rows = slice(0, sub)

    def shifted_mix(c0, c1):
        rw = ext_ref[SUBLANES + rows.start:SUBLANES + rows.stop, c0:c1]
        prev = ext_ref[SUBLANES - 1 + rows.start:SUBLANES - 1 + rows.stop, c0:c1]
        return rw + (prev - rw) * mu_ref[:, c0:c1]

    def projections():
        half = width // 2
        extras = [(w_ref, o_ref, sl) for w_ref, o_ref in zip(extra_w, extra_o)
                  for sl in _col_chunks(w_ref.shape[1], 256)]
        own = [(3 * width, win), (width, width + half), (width + half, 2 * width), (0, half), (half, width),
               (2 * width, 2 * width + half), (2 * width + half, 3 * width)]
        for n, cols in enumerate(own):
            project(*cols)
            yield
            if n >= 2 and extras:
                w_ref, o_ref, sl = extras.pop(0)
                o_ref[0, :, sl] = jnp.dot(xe[SUBLANES:], w_ref[:, sl], preferred_element_type=F32)
                yield
        for w_ref, o_ref, sl in extras:
            o_ref[0, :, sl] = jnp.dot(xe[SUBLANES:], w_ref[:, sl], preferred_element_type=F32)
            yield

    pieces = projections()

    def mxu(n=1):
        for _ in range(n):
            next(pieces, None)

    for r0 in range(0, tm, sub):
        rows = slice(r0, r0 + sub)
        mxu(3)
        s_lo = shifted_mix(3 * width, win)
        lo = s_lo[:, :lora]
        wv = w0_ref[...] + _dot(jnp.tanh(lo), w2_ref[...])
        av = _sigmoid(a0_ref[...] + _dot(lo, a2_ref[...]))
        g_ref[0, rows, :] = _dot(_sigmoid(s_lo[:, lora:]), g2_ref[...]).astype(g_ref.dtype)
        mxu(1)
        for sl in _col_chunks(width, 512):
            lw_ref[0, rows, sl] = -jnp.exp(-_softplus(-wv[:, sl]) - 0.5)
            mxu(1)
        k = shifted_mix(width, 2 * width)
        kkr = k * kk_ref[...]
        mxu(1)
        ss = _dot_exact_rhs(kkr * kkr, e_ref[...])
        k_ref[0, rows, :] = (k * (1.0 + (av - 1.0) * ka_ref[...])).astype(k_ref.dtype)
        mxu(2)
        inv = lax.rsqrt(jnp.maximum(ss, 1e-24))
        kk = kkr * _dot_exact_rhs(inv, et_ref[...])
        mxu(1)
        an_ref[0, rows, :] = (-kk).astype(an_ref.dtype)
        bn_ref[0, rows, :] = (kk * av).astype(bn_ref.dtype)
        mxu(2)
        r_ref[0, rows, :] = shifted_mix(0, width).astype(r_ref.dtype)
        mxu(1)
        v_ref[0, rows, :] = shifted_mix(2 * width, 3 * width).astype(v_ref.dtype)
    for _ in pieces:
        pass


def _rwkv_chunks(rows_list, fill, r_ref, lw_ref, k_ref, v_ref, an_ref, bn_ref, g_ref, rk_ref, lg_ref, lb_ref,
                 y_ref, state_ref):
    c = RWKV_CHUNK
    c2 = 2 * c
    width = r_ref.shape[-1]
    n_pairs = width // LANES

    row = lax.broadcasted_iota(jnp.int32, (c, c), 0)
    col = lax.broadcasted_iota(jnp.int32, (c, c), 1)
    tri = jnp.where(row >= col, 1.0, 0.0)
    roww = lax.broadcasted_iota(jnp.int32, (c, c2), 0)
    colw = lax.broadcasted_iota(jnp.int32, (c, c2), 1) % c
    strict = jnp.where(roww > colw, 1.0, 0.0)
    incl = jnp.where(roww >= colw, 1.0, 0.0)
    eye = jnp.where(roww == colw, 1.0, 0.0)
    same_head = jnp.where((lax.broadcasted_iota(jnp.int32, (LANES, LANES), 0) // HEAD_DIM)
                          == (lax.broadcasted_iota(jnp.int32, (LANES, LANES), 1) // HEAD_DIM), 1.0, 0.0)
    lane = lax.broadcasted_iota(jnp.int32, (1, LANES), 1)
    m0 = jnp.where(lane < HEAD_DIM, 1.0, 0.0)
    m1 = 1.0 - m0

    lane_lo2 = lax.broadcasted_iota(jnp.int32, (c, LANES), 1) < HEAD_DIM

    def stack(x):
        xb = x.astype(BF16)
        zero = jnp.zeros_like(xb)
        return jnp.concatenate([jnp.where(lane_lo2, xb, zero), jnp.where(lane_lo2, zero, xb)], axis=0)

    sls = [slice(p * LANES, (p + 1) * LANES) for p in range(n_pairs)]
    units = [(rows, sl) for rows in rows_list for sl in sls]
    lhs2s, rhs4s, vs, v_stks, bks, g_lasts = [], [], [], [], [], []
    for rows, sl in units:
        r = r_ref[0, rows, sl].astype(F32)
        lw = lw_ref[0, rows, sl]
        k = k_ref[0, rows, sl].astype(F32)
        b = bn_ref[0, rows, sl].astype(F32)
        v = v_ref[0, rows, sl].astype(F32)
        cum = _dot_exact_lhs(tri, lw)
        cum_last = cum[c - 1:c, :]
        e_neg = jnp.exp(-cum)
        e_tail = jnp.exp(cum_last - cum)
        at = an_ref[0, rows, sl].astype(F32) * jnp.exp(cum - lw)
        rt = r * jnp.exp(cum)
        lhs2s.append(jnp.concatenate([at, rt], axis=0).astype(BF16))
        rhs4s.append(jnp.concatenate([stack(b * e_neg), stack(k * e_neg)], axis=0))
        vs.append(v.astype(BF16))
        v_stks.append(stack(v))
        bks.append(jnp.concatenate([b * e_tail, k * e_tail], axis=0).astype(BF16))
        g_lasts.append(jnp.exp(cum_last))
    gms = [_dot_nt(l, rh) for l, rh in zip(lhs2s, rhs4s)]
    pws = [gm[:c, :c2] * strict for gm in gms]
    a_aks = [(gm[:c, c2:] * strict).astype(BF16) for gm in gms]
    a_rbks = [jnp.concatenate([gm[c:, :c2] * incl, gm[c:, c2:] * incl], axis=1).astype(BF16) for gm in gms]
    akvs = [_dot(a_ak, v_stk) for a_ak, v_stk in zip(a_aks, v_stks)]
    fill()
    tinvs = [eye + pw for pw in pws]
    pw_stks = [stack(pw) for pw in pws]
    for _ in range(int(math.log2(c)) - 1):
        pws = [_dot(pw, pw_stk) for pw, pw_stk in zip(pws, pw_stks)]
        pw_stks = [stack(pw) for pw in pws]
        tinvs = [tinv + _dot(tinv, pw_stk) for tinv, pw_stk in zip(tinvs, pw_stks)]
        fill()
    lane_lo = lane < HEAD_DIM

    def head_sum(x):
        lo = jnp.sum(x * m0, axis=-1, keepdims=True)
        hi = jnp.sum(x * m1, axis=-1, keepdims=True)
        return jnp.where(lane_lo, lo, hi)

    for si, rows in enumerate(rows_list):
        u0 = si * n_pairs
        sts = [state_ref[p] for p in range(n_pairs)]
        ahrhs = [_dot_nt(lhs2s[u0 + p], sts[p]) for p in range(n_pairs)]
        fill()
        us = [_dot(tinvs[u0 + p], stack(ahrhs[p][:c] + akvs[u0 + p])) for p in range(n_pairs)]
        fill()
        for p in range(n_pairs):
            uv = jnp.concatenate([us[p].astype(BF16), vs[u0 + p]], axis=0)
            state_ref[p] = sts[p] * g_lasts[u0 + p] + same_head * _dot_tn(uv, bks[u0 + p])
        fill()
        ys = [ahrhs[p][c:] + _dot(a_rbks[u0 + p], jnp.concatenate([stack(us[p]), v_stks[u0 + p]], axis=0))
              for p in range(n_pairs)]
        bonus = [head_sum(r_ref[0, rows, sl].astype(F32) * k_ref[0, rows, sl].astype(F32) * rk_ref[:, sl])
                 for sl in sls]
        ycs = [y - head_sum(y) * (1.0 / HEAD_DIM) for y in ys]
        yvs = [head_sum(yc * yc) * (1.0 / HEAD_DIM) for yc in ycs]
        for p, sl in enumerate(sls):
            yn = ycs[p] * lax.rsqrt(yvs[p] + RWKV_LNX_EPS) * lg_ref[:, sl] + lb_ref[:, sl]
            out = (yn + bonus[p] * v_ref[0, rows, sl].astype(F32)) * g_ref[0, rows, sl].astype(F32)
            y_ref[0, rows, sl] = out.astype(y_ref.dtype)


def _rwkv_operands(x, w_in, extra_ws, mu, w0, w2, a0, a2, g2, k_k, k_a, r_k, lnx_g, lnx_b, *, tm):
    bsz, l, d = x.shape
    n_extra = len(extra_ws)
    win = w_in.shape[1]
    width = w0.shape[0]
    w2p = jnp.concatenate([w2, jnp.zeros((RWKV_ICLR_LORA, width), F32)], axis=0).astype(BF16)
    a2p = jnp.concatenate([jnp.zeros((RWKV_DECAY_LORA, width), F32), a2], axis=0).astype(BF16)
    head_of = jnp.arange(width) // HEAD_DIM
    e = (head_of[:, None] == jnp.arange(LANES)[None, :]).astype(BF16)
    et = e.T
    vec = lambda x: x.reshape(1, -1)
    row = lambda n: pl.BlockSpec((1, n), lambda bi, i: (0, 0))
    full = lambda a: pl.BlockSpec(a.shape, lambda bi, i: (0, 0))
    single = lambda a: pl.BlockSpec(a.shape, lambda bi, i: (0, 0), pipeline_mode=pl.Buffered(1))
    tile = lambda w: pl.BlockSpec((1, tm, w), lambda bi, i: (bi, i, 0))
    sds = lambda w, dt=F32: jax.ShapeDtypeStruct((bsz, l, w), dt)
    g2b = g2.astype(BF16)
    outs = pl.pallas_call(
        functools.partial(_rwkv_prep_kernel, n_extra),
        grid=(bsz, l // tm),
        in_specs=[pl.BlockSpec((1, tm, d), lambda bi, i: (bi, i, 0)),
                  pl.BlockSpec((1, SUBLANES, d), lambda bi, i: (bi, jnp.maximum(i * (tm // SUBLANES) - 1, 0), 0)),
                  single(w_in), row(win), row(width), row(width), row(width), row(width),
                  full(w2p), full(a2p), full(g2b), full(e), full(et)] + [single(w) for w in extra_ws],
        out_specs=[tile(width)] * 7 + [tile(w.shape[1]) for w in extra_ws],
        out_shape=([sds(width, BF16), sds(width)] + [sds(width, BF16)] * 5
                   + [sds(w.shape[1]) for w in extra_ws]),
        scratch_shapes=[pltpu.VMEM((tm + SUBLANES, win), F32)],
        compiler_params=_params("parallel", "parallel"),
        name="rwkv_prep",
    )(x, x, w_in, vec(mu), vec(w0), vec(a0), vec(k_k), vec(k_a), w2p, a2p, g2b, e, et, *extra_ws)
    r, lw, k, v, an, bn, g = outs[:7]
    ctile = pl.BlockSpec((1, SSD_CHUNK, width), lambda bi, ci: (bi, ci, 0))
    crow = pl.BlockSpec((1, width), lambda bi, ci: (0, 0))
    args = [r, lw, k, v, an, bn, g, vec(r_k), vec(lnx_g), vec(lnx_b)]
    scratch = [pltpu.VMEM((width // LANES, LANES, LANES), F32)]
    return outs[7:], (args, [ctile] * 7 + [crow] * 3, scratch)


def _mixers_kernel(n_ssd, n_rwkv, *refs):
    ssd_in = refs[:n_ssd]
    rwkv_in = refs[n_ssd:n_ssd + n_rwkv]
    y_ssd_ref, y_rwkv_ref, ssd_state, ssd_ext, rwkv_state = refs[n_ssd + n_rwkv:]
    c = pl.program_id(1)

    @pl.when(c == 0)
    def _():
        ssd_state[...] = jnp.zeros_like(ssd_state)
        rwkv_state[...] = jnp.zeros_like(rwkv_state)

    ssd_steps = _ssd_chunk(c, *ssd_in, y_ssd_ref, ssd_state, ssd_ext)
    fill = lambda: (next(ssd_steps, None), next(ssd_steps, None))
    rows_list = [slice(sub * RWKV_CHUNK, (sub + 1) * RWKV_CHUNK) for sub in range(SSD_CHUNK // RWKV_CHUNK)]
    _rwkv_chunks(rows_list, fill, *rwkv_in, y_rwkv_ref, rwkv_state)
    for _ in ssd_steps:
        pass


def _ssd_rwkv_mixers(ssd_ops, rwkv_ops, bsz, l, ssd_width, rwkv_width):
    ssd_args, ssd_specs, ssd_scratch = ssd_ops
    rwkv_args, rwkv_specs, rwkv_scratch = rwkv_ops
    q = SSD_CHUNK
    out_spec = lambda w: pl.BlockSpec((1, q, w), lambda bi, c: (bi, c, 0))
    return pl.pallas_call(
        functools.partial(_mixers_kernel, len(ssd_args), len(rwkv_args)),
        grid=(bsz, l // q),
        in_specs=ssd_specs + rwkv_specs,
        out_specs=[out_spec(ssd_width), out_spec(rwkv_width)],
        out_shape=[jax.ShapeDtypeStruct((bsz, l, ssd_width), BF16), jax.ShapeDtypeStruct((bsz, l, rwkv_width), BF16)],
        scratch_shapes=ssd_scratch + rwkv_scratch,
        compiler_params=_params("parallel", "arbitrary"),
        name="ssd_rwkv_mixers",
    )(*ssd_args, *rwkv_args)


def _moba_kernel(q_ref, qn_ref, k_ref, v_ref, o_ref, kb_ref, vt_ref, kmean_ref, qo_ref, qp_ref):
    blk = MOBA_BLOCK
    half = HEAD_DIM
    nb = k_ref.shape[0] // blk
    nbp = kmean_ref.shape[0]
    npair = q_ref.shape[1] // LANES
    heads = range(2 * npair)
    group = math.gcd(nb, MOBA_GROUP)
    qi = pl.program_id(2)

    @pl.when(qi == 0)
    def _():
        lane = lax.broadcasted_iota(jnp.int32, (blk, LANES), 1)
        rowp = lax.broadcasted_iota(jnp.int32, (LANES, blk), 0)
        if nbp > nb:
            kmean_ref[...] = jnp.zeros_like(kmean_ref)
        for n in range(nb):
            rows = slice(n * blk, (n + 1) * blk)
            kmean_ref[n:n + 1, :] = jnp.mean(k_ref[rows, :], axis=0, keepdims=True)
            for pp in range(npair):
                cols = slice(pp * LANES, (pp + 1) * LANES)
                kn = k_ref[rows, cols]
                kb_ref[2 * pp, n] = jnp.where(lane < half, kn, jnp.where(lane == half + n, 1.0, 0.0)).astype(BF16)
                kb_ref[2 * pp + 1, n] = jnp.where(lane >= half, kn, jnp.where(lane == n, 1.0, 0.0)).astype(BF16)
                vtn = v_ref[rows, cols].T
                vt_ref[2 * pp, n] = jnp.where(rowp < half, vtn, jnp.where(rowp == half, 1.0, 0.0)).astype(BF16)
                vt_ref[2 * pp + 1, n] = jnp.where(rowp >= half, vtn, jnp.where(rowp == 0, 1.0, 0.0)).astype(BF16)

    rown = lax.broadcasted_iota(jnp.int32, (nbp, blk), 0)
    lane_k = lax.broadcasted_iota(jnp.int32, (nbp, LANES), 1)
    zeros = jnp.zeros((half, blk), F32)

    def prepare_queries(src_ref, tile):
        qts = [src_ref[:, pp * LANES:(pp + 1) * LANES].T for pp in range(npair)]
        gates, qhs = [], []
        for hh in heads:
            pp, h = divmod(hh, 2)
            km = kmean_ref[:, pp * LANES:(pp + 1) * LANES]
            kmh = jnp.where((lane_k < half) if h == 0 else (lane_k >= half), km, 0.0)
            gates.append(jnp.dot(kmh, qts[pp], preferred_element_type=F32, precision=lax.Precision.HIGHEST))
            qhs.append(qts[pp][h * half:(h + 1) * half] * (half ** -0.5 * LOG2_E))
        for hh in heads:
            gate = jnp.where(rown < tile, gates[hh], -jnp.inf)
            bias = jnp.full((nbp, blk), NEG_BIG, F32)
            for _ in range(MOBA_TOPK):
                mx = jnp.max(gate, axis=0, keepdims=True)
                first = jnp.min(jnp.where(gate == mx, rown, nbp), axis=0, keepdims=True)
                pick = (rown == first) & (mx > -jnp.inf)
                bias = jnp.where(pick, 0.0, bias)
                gate = jnp.where(pick, -jnp.inf, gate)
            aug = jnp.concatenate([bias, jnp.zeros((half - nbp, blk), F32)], axis=0)
            qo_ref[hh] = jnp.concatenate([qhs[hh], zeros] if hh % 2 == 0 else [zeros, qhs[hh]],
                                         axis=0).astype(BF16)
            qp_ref[hh] = jnp.concatenate([qhs[hh], aug] if hh % 2 == 0 else [aug, qhs[hh]], axis=0).astype(BF16)

    @pl.when(qi == 0)
    def _():
        prepare_queries(q_ref, 0)

    q_own = [qo_ref[h] for h in heads]
    q_past = [qp_ref[h] for h in heads]
    s_own = [jnp.dot(kb_ref[h, qi], q_own[h], preferred_element_type=F32) for h in heads]
    prepare_queries(qn_ref, qi + 1)

    causal = (lax.broadcasted_iota(jnp.int32, (blk, blk), 0) <= lax.broadcasted_iota(jnp.int32, (blk, blk), 1))
    ms, ps = [], []
    for h in heads:
        s = jnp.where(causal, s_own[h], NEG_BIG)
        ms.append(jnp.max(s, axis=0, keepdims=True))
        ps.append(jnp.exp2(s - ms[h]).astype(BF16))
    carry = []
    for h in heads:
        carry += [ms[h], jnp.dot(vt_ref[h, qi], ps[h], preferred_element_type=F32)]

    def scores(gi, h):
        return [jnp.dot(kb_ref[h, gi * group + g], q_past[h], preferred_element_type=F32) for g in range(group)]

    def values(gi, h):
        return jnp.concatenate([vt_ref[h, gi * group + g] for g in range(group)], axis=1)

    def body_lagged(gi, carry):
        excess = carry[-1]
        out = []
        sss = [scores(gi, h) for h in heads]
        pcats, gmaxs = [], []
        for h in heads:
            m_run = carry[2 * h]
            gmax = None
            ps = []
            for s in sss[h]:
                cm = jnp.max(s, axis=0, keepdims=True)
                gmax = cm if gmax is None else jnp.maximum(gmax, cm)
                ps.append(jnp.exp2(s - m_run).astype(BF16))
            pcats.append(jnp.concatenate(ps, axis=0))
            gmaxs.append(gmax)
        for h in heads:
            m_run, acc = carry[2 * h], carry[2 * h + 1]
            m_new = jnp.maximum(m_run, gmaxs[h])
            excess = jnp.maximum(excess, gmaxs[h] - m_run)
            acc = jnp.exp2(m_run - m_new) * (acc + jnp.dot(values(gi, h), pcats[h], preferred_element_type=F32))
            out += [m_new, acc]
        return tuple(out) + (excess,)

    def body_exact_max(gi, carry):
        sss = [scores(gi, h) for h in heads]
        m_news = []
        for h in heads:
            m_new = carry[2 * h]
            for s in sss[h]:
                m_new = jnp.maximum(m_new, jnp.max(s, axis=0, keepdims=True))
            m_news.append(m_new)
        pcats = [jnp.concatenate([jnp.exp2(s - m_news[h]).astype(BF16) for s in sss[h]], axis=0)
                 for h in heads]
        out = []
        for h in heads:
            alpha = jnp.exp2(carry[2 * h] - m_news[h])
            out += [m_news[h], alpha * carry[2 * h + 1]
                    + jnp.dot(values(gi, h), pcats[h], preferred_element_type=F32)]
        return tuple(out)

    rowq = lax.broadcasted_iota(jnp.int32, (LANES, blk), 0)

    def write_out(final):
        for pp in range(npair):
            acc0, acc1 = final[4 * pp + 1], final[4 * pp + 3]
            out_t = jnp.where(rowq < half, acc0 / acc0[half:half + 1], acc1 / acc1[0:1])
            o_ref[:, pp * LANES:(pp + 1) * LANES] = out_t.T.astype(o_ref.dtype)

    n_groups = (qi + group - 1) // group
    final = lax.fori_loop(0, n_groups, body_lagged, tuple(carry) + (jnp.full((1, blk), NEG_BIG, F32),))
    write_out(final)

    @pl.when(jnp.max(final[-1]) > MOBA_LAG_LIMIT)
    def _():
        write_out(lax.fori_loop(0, n_groups, body_exact_max, tuple(carry)))


def _moba_attention(qkv, bsz, s, heads):
    blk = MOBA_BLOCK
    assert s % blk == 0 and (heads * HEAD_DIM) % LANES == 0
    nb = s // blk
    assert nb <= HEAD_DIM
    nbp = -(-nb // SUBLANES) * SUBLANES
    pairs = heads * HEAD_DIM // LANES
    pps = math.gcd(pairs, MOBA_PAIRS_PER_STEP)
    cw = pps * LANES
    steps = pairs // pps
    return pl.pallas_call(
        _moba_kernel,
        grid=(bsz, steps, nb),
        in_specs=[pl.BlockSpec((blk, cw), lambda b, p, i: (b * nb + i, p)),
                  pl.BlockSpec((blk, cw), lambda b, p, i: (b * nb + jnp.minimum(i + 1, nb - 1), p)),
                  pl.BlockSpec((s, cw), lambda b, p, i: (b, steps + p)),
                  pl.BlockSpec((s, cw), lambda b, p, i: (b, 2 * steps + p))],
        out_specs=pl.BlockSpec((blk, cw), lambda b, p, i: (b * nb + i, p)),
        out_shape=jax.ShapeDtypeStruct((bsz * s, heads * HEAD_DIM), BF16),
        scratch_shapes=[pltpu.VMEM((2 * pps, nb, blk, LANES), BF16), pltpu.VMEM((2 * pps, nb, LANES, blk), BF16),
                        pltpu.VMEM((nbp, cw), F32),
                        pltpu.VMEM((2 * pps, LANES, blk), BF16), pltpu.VMEM((2 * pps, LANES, blk), BF16)],
        compiler_params=_params("parallel", "parallel", "arbitrary"),
        name="moba_attention",
    )(qkv, qkv, qkv, qkv)


def _row_tile(m):
    for t in (512, 256, 128, 64, 32, 16, 8):
        if m % t == 0:
            return t
    raise ValueError(f"row count {m} is not a multiple of 8")


def _col_tile(n, cap=2048):
    best = None
    for t in range(LANES, min(n, cap) + 1, LANES):
        if n % t == 0:
            best = t
    if best is None:
        raise ValueError(f"column count {n} is not a multiple of {LANES}")
    return best


def kernel(x, mem, even_w_in, ssd_conv_w, ssd_conv_b, ssd_dt_bias, ssd_a_log, ssd_d, ssd_norm_g, rwkv_mu, rwkv_w0, rwkv_w2, rwkv_a0, rwkv_a2, rwkv_g2, rwkv_k_k, rwkv_k_a, rwkv_r_k, rwkv_lnx_g, rwkv_lnx_b, even_w_out, odd_w_qkv, odd_w_out, ln_mix_g, ln_mix_b, xa_wq, xa_wkv, xa_wo, ln_xa_g, ln_xa_b, ffn_w13, ffn_w2, ln_ffn_g, ln_ffn_b):
    bsz, s, d = x.shape
    m = bsz * s
    tm = _row_tile(s)
    ssd_width = ssd_norm_g.shape[-1]
    ssd_heads = ssd_dt_bias.shape[-1]
    ssd_xbc = ssd_conv_b.shape[-1]
    ssd_in = ssd_width + ssd_xbc + ssd_heads
    rwkv_width = rwkv_w0.shape[-1]
    mem2 = mem.reshape(bsz * mem.shape[1], d)
    x2 = x.reshape(m, d)
    for layer in range(DEPTH):
        j = layer // 2
        if layer % 2 == 0:
            w_in = even_w_in[j].astype(BF16)
            w_z = w_in[:, :ssd_width]
            w_xbc = w_in[:, ssd_width:ssd_width + ssd_xbc]
            w_dt = jnp.pad(w_in[:, ssd_width + ssd_xbc:ssd_in], ((0, 0), (0, LANES - ssd_heads)))
            w_rw = w_in[:, ssd_in:]
            (z, xbc, dt_pad), rwkv_ops = _rwkv_operands(
                x2.reshape(bsz, s, d), w_rw, [w_z, w_xbc, w_dt], rwkv_mu[j], rwkv_w0[j], rwkv_w2[j],
                rwkv_a0[j], rwkv_a2[j], rwkv_g2[j], rwkv_k_k[j], rwkv_k_a[j], rwkv_r_k[j],
                rwkv_lnx_g[j], rwkv_lnx_b[j], tm=tm)
            ssd_ops = _ssd_operands(z, xbc, dt_pad, ssd_conv_w[j], ssd_conv_b[j], ssd_dt_bias[j],
                                    ssd_a_log[j], ssd_d[j], ssd_norm_g[j])
            y_ssd, y_rwkv = _ssd_rwkv_mixers(ssd_ops, rwkv_ops, bsz, s, ssd_width, rwkv_width)
            x2 = _matmul_residual_ln([y_ssd.reshape(m, -1), y_rwkv.reshape(m, -1)],
                                     even_w_out[j].astype(BF16), x2,
                                     ln_mix_g[layer], ln_mix_b[layer], tm=tm)
        else:
            heads = d // HEAD_DIM
            qkv, = _matmul(x2, [odd_w_qkv[j].astype(BF16)], tm=tm)
            attn = _moba_attention(qkv, bsz, s, heads)
            x2 = _matmul_residual_ln([attn], odd_w_out[j].astype(BF16), x2,
                                     ln_mix_g[layer], ln_mix_b[layer], tm=tm)
        kv, = _matmul(mem2, [xa_wkv[layer].astype(BF16)], tm=_row_tile(mem2.shape[0]))
        x3 = _cross_attention_ln(x2.reshape(bsz, s, d), kv.reshape(bsz, -1, 2 * d),
                                 xa_wq[layer].astype(BF16), xa_wo[layer].astype(BF16),
                                 ln_xa_g[layer], ln_xa_b[layer], tm=tm)
        x2 = x3.reshape(m, d)
        h = _swiglu_up(x2, ffn_w13[layer].astype(BF16), tm=tm)
        x2 = _matmul_residual_ln([h], ffn_w2[layer].astype(BF16), x2,
                                 ln_ffn_g[layer], ln_ffn_b[layer], tm=tm)
    return x2.reshape(bsz, s, d)
```

```python
import functools
import math

import jax
import jax.numpy as jnp
from jax import lax
from jax.experimental import pallas as pl
from jax.experimental.pallas import tpu as pltpu

F32 = jnp.float32
BF16 = jnp.bfloat16

HEAD_DIM = 64
LANES = 128
SUBLANES = 8
SSD_GROUPS = 2
SSD_STATE = 128
SSD_CONV = 4
SSD_CHUNK = 128
RWKV_DECAY_LORA = 64
RWKV_ICLR_LORA = 64
RWKV_GATE_LORA = 128
RWKV_CHUNK = 64
RWKV_PREP_ROWS = 256
MOBA_BLOCK = 256
MOBA_TOPK = 3
MOBA_GROUP = 4
MOBA_LAG_LIMIT = 64.0
MOBA_PAIRS_PER_STEP = 2
XATTN_HEADS = 4
SWIGLU_COLS = 256
DEPTH = 2
DEEPNORM_ALPHA = (2 * DEPTH) ** 0.25
LN_EPS = 1e-5
RMS_EPS = 1e-5
RWKV_LNX_EPS = 64e-5
NEG_BIG = -1e30
LOG2_E = math.log2(math.e)
VMEM_LIMIT = 60 * 1024 * 1024


def _params(*sem):
    return pltpu.CompilerParams(dimension_semantics=sem, vmem_limit_bytes=VMEM_LIMIT)


def _dot(a, b):
    return jnp.dot(a.astype(BF16), b.astype(BF16), preferred_element_type=F32)


def _dot_nt(a, b):
    return lax.dot_general(a.astype(BF16), b.astype(BF16), (((1,), (1,)), ((), ())),
                           preferred_element_type=F32)


def _dot_tn(a, b):
    return lax.dot_general(a.astype(BF16), b.astype(BF16), (((0,), (0,)), ((), ())),
                           preferred_element_type=F32)


def _split3(x):
    hi = x.astype(BF16)
    r1 = x - hi.astype(F32)
    mid = r1.astype(BF16)
    lo = (r1 - mid.astype(F32)).astype(BF16)
    return hi, mid, lo


def _dot_exact_lhs(m, x):
    hi, mid, lo = _split3(x)
    m = m.astype(BF16)
    return (jnp.dot(m, hi, preferred_element_type=F32) + jnp.dot(m, mid, preferred_element_type=F32)
            + jnp.dot(m, lo, preferred_element_type=F32))


def _dot_exact_rhs(x, m):
    hi, mid, lo = _split3(x)
    m = m.astype(BF16)
    return (jnp.dot(hi, m, preferred_element_type=F32) + jnp.dot(mid, m, preferred_element_type=F32)
            + jnp.dot(lo, m, preferred_element_type=F32))


def _sigmoid(x):
    return 1.0 / (1.0 + jnp.exp(-x))


def _softplus(x):
    return jnp.maximum(x, 0.0) + jnp.log1p(jnp.exp(-jnp.abs(x)))


def _layer_norm(v, g, b):
    mu = jnp.mean(v, axis=-1, keepdims=True)
    c = v - mu
    var = jnp.mean(c * c, axis=-1, keepdims=True)
    return c * lax.rsqrt(var + LN_EPS) * g + b


def _col_chunks(n, cap=1536):
    width = _col_tile(n, cap)
    return [slice(j, j + width) for j in range(0, n, width)]


def _mm_kernel(n_out, x_ref, *refs):
    x = x_ref[...].astype(BF16)
    for w_ref, o_ref in zip(refs[:n_out], refs[n_out:]):
        for sl in _col_chunks(w_ref.shape[1]):
            o_ref[:, sl] = jnp.dot(x, w_ref[:, sl], preferred_element_type=F32).astype(o_ref.dtype)


def _matmul(x, ws, *, tm, out_dtype=F32):
    m, k = x.shape
    assert m % tm == 0
    return pl.pallas_call(
        functools.partial(_mm_kernel, len(ws)),
        grid=(m // tm,),
        in_specs=[pl.BlockSpec((tm, k), lambda i: (i, 0))] + [pl.BlockSpec(w.shape, lambda i: (0, 0)) for w in ws],
        out_specs=[pl.BlockSpec((tm, w.shape[1]), lambda i: (i, 0)) for w in ws],
        out_shape=[jax.ShapeDtypeStruct((m, w.shape[1]), out_dtype) for w in ws],
        compiler_params=_params("parallel"),
        name="matmul",
    )(x, *ws)


def _mm_res_ln_kernel(n_in, *refs):
    hs = refs[:n_in]
    ws = refs[n_in:2 * n_in]
    res_ref, g_ref, b_ref, o_ref = refs[2 * n_in:]
    acc = _dot(hs[0][...], ws[0][...])
    for h_ref, w_ref in zip(hs[1:], ws[1:]):
        acc = acc + _dot(h_ref[...], w_ref[...])
    o_ref[...] = _layer_norm(DEEPNORM_ALPHA * res_ref[...] + acc, g_ref[...], b_ref[...])


def _matmul_residual_ln(hs, w, res, g, b, *, tm):
    m, d = res.shape
    n_in = len(hs)
    kw = w.shape[0] // n_in
    assert all(h.shape[1] == kw for h in hs)
    in_specs = ([pl.BlockSpec((tm, kw), lambda i: (i, 0)) for _ in hs]
                + [pl.BlockSpec((kw, d), lambda i, j=j: (j, 0)) for j in range(n_in)]
                + [pl.BlockSpec((tm, d), lambda i: (i, 0)),
                   pl.BlockSpec((1, d), lambda i: (0, 0)),
                   pl.BlockSpec((1, d), lambda i: (0, 0))])
    return pl.pallas_call(
        functools.partial(_mm_res_ln_kernel, n_in),
        grid=(m // tm,),
        in_specs=in_specs,
        out_specs=pl.BlockSpec((tm, d), lambda i: (i, 0)),
        out_shape=jax.ShapeDtypeStruct((m, d), F32),
        compiler_params=_params("parallel"),
        name="matmul_residual_ln",
    )(*hs, *([w] * n_in), res, g.reshape(1, d), b.reshape(1, d))


def _swiglu_kernel(x_ref, w13_ref, o_ref):
    x = x_ref[...].astype(BF16)
    n = o_ref.shape[1]
    pending = None
    for sl in _col_chunks(n, SWIGLU_COLS) + [None]:
        if sl is not None:
            gate = jnp.dot(x, w13_ref[:, sl], preferred_element_type=F32)
            up = jnp.dot(x, w13_ref[:, slice(n + sl.start, n + sl.stop)], preferred_element_type=F32)
        if pending is not None:
            psl, pgate, pup = pending
            o_ref[:, psl] = (pgate * _sigmoid(pgate) * pup).astype(o_ref.dtype)
        pending = (sl, gate, up) if sl is not None else None


def _swiglu_up(x, w13, *, tm):
    m, k = x.shape
    n = w13.shape[1] // 2
    return pl.pallas_call(
        _swiglu_kernel,
        grid=(m // tm,),
        in_specs=[pl.BlockSpec((tm, k), lambda i: (i, 0)),
                  pl.BlockSpec((k, 2 * n), lambda i: (0, 0))],
        out_specs=pl.BlockSpec((tm, n), lambda i: (i, 0)),
        out_shape=jax.ShapeDtypeStruct((m, n), BF16),
        compiler_params=_params("parallel"),
        name="swiglu_up",
    )(x, w13)


def _xattn_kernel(x_ref, kv_ref, wq_ref, wo_ref, g_ref, b_ref, o_ref):
    x = x_ref[0]
    d = x.shape[-1]
    hd = d // XATTN_HEADS
    q = _dot(x, wq_ref[...])
    kv = kv_ref[0]
    heads = range(XATTN_HEADS)
    ss = [_dot_nt(q[:, h * hd:(h + 1) * hd], kv[:, h * hd:(h + 1) * hd]) * (hd ** -0.5) for h in heads]
    ps = []
    for s in ss:
        p = jnp.exp(s - jnp.max(s, axis=-1, keepdims=True))
        ps.append(p / jnp.sum(p, axis=-1, keepdims=True))
    o = jnp.concatenate([_dot(ps[h], kv[:, d + h * hd:d + (h + 1) * hd]) for h in heads], axis=-1)
    xa = _dot(o, wo_ref[...])
    o_ref[0] = _layer_norm(DEEPNORM_ALPHA * x + xa, g_ref[...], b_ref[...])


def _cross_attention_ln(x, kv, wq, wo, g, b, *, tm):
    bsz, s, d = x.shape
    m = kv.shape[1]
    return pl.pallas_call(
        _xattn_kernel,
        grid=(bsz, s // tm),
        in_specs=[pl.BlockSpec((1, tm, d), lambda bi, i: (bi, i, 0)),
                  pl.BlockSpec((1, m, 2 * d), lambda bi, i: (bi, 0, 0)),
                  pl.BlockSpec((d, d), lambda bi, i: (0, 0)),
                  pl.BlockSpec((d, d), lambda bi, i: (0, 0)),
                  pl.BlockSpec((1, d), lambda bi, i: (0, 0)),
                  pl.BlockSpec((1, d), lambda bi, i: (0, 0))],
        out_specs=pl.BlockSpec((1, tm, d), lambda bi, i: (bi, i, 0)),
        out_shape=jax.ShapeDtypeStruct((bsz, s, d), F32),
        compiler_params=_params("parallel", "parallel"),
        name="cross_attention_ln",
    )(x, kv, wq, wo, g.reshape(1, d), b.reshape(1, d))


def _ssd_chunk(c, z_ref, xbc_ref, xbcp_ref, dt_ref, dtt_ref, cw_ref, cb_ref, dtb_ref, dtbt_ref,
               aneg_ref, anegt_ref, dskip_ref, ng_ref, y_ref, state_ref, ext_ref):
    q = SSD_CHUNK
    width = z_ref.shape[-1]
    n_pairs = width // LANES

    ext_ref[0:SUBLANES, :] = jnp.where(c > 0, xbcp_ref[0], 0.0)
    ext_ref[SUBLANES:SUBLANES + q, :] = xbc_ref[0]
    xcs = []
    for sl in _col_chunks(ext_ref.shape[1], 256):
        conv = cb_ref[:, sl] + cw_ref[SSD_CONV - 1:SSD_CONV, sl] * ext_ref[SUBLANES:SUBLANES + q, sl]
        for k in range(SSD_CONV - 1):
            off = SUBLANES - (SSD_CONV - 1) + k
            conv = conv + cw_ref[k:k + 1, sl] * ext_ref[off:off + q, sl]
        xcs.append(conv * _sigmoid(conv))
        yield
    xc = jnp.concatenate(xcs, axis=1)
    xs = xc[:, :width]
    gn = SSD_GROUPS * SSD_STATE
    bm = xc[:, width:width + gn]
    cm = xc[:, width + gn:width + 2 * gn]

    dt = _softplus(dt_ref[0] + dtb_ref[...])
    a = dt * aneg_ref[...]
    dtt = _softplus(dtt_ref[0] + dtbt_ref[...])
    at = dtt * anegt_ref[...]
    row = lax.broadcasted_iota(jnp.int32, (q, q), 0)
    col = lax.broadcasted_iota(jnp.int32, (q, q), 1)
    causal = row >= col
    tri = jnp.where(causal, 1.0, 0.0)
    a_cum = _dot_exact_lhs(tri, a)
    a_cumt = _dot_exact_rhs(at, jnp.where(row <= col, 1.0, 0.0))
    yield

    lane = lax.broadcasted_iota(jnp.int32, (1, LANES), 1)
    lane_lo = lane < HEAD_DIM
    rowp = lax.broadcasted_iota(jnp.int32, (LANES, 1), 0)
    pairs_per_group = n_pairs // SSD_GROUPS
    ys = []
    for p in range(n_pairs):
        g = p // pairs_per_group
        h0, h1 = 2 * p, 2 * p + 1
        bg = bm[:, g * SSD_STATE:(g + 1) * SSD_STATE]
        cg = cm[:, g * SSD_STATE:(g + 1) * SSD_STATE]
        cb = _dot_nt(cg, bg)
        xs_p = xs[:, p * LANES:(p + 1) * LANES]
        dt_p = jnp.where(lane_lo, dt[:, h0:h0 + 1], dt[:, h1:h1 + 1])
        acum_p = jnp.where(lane_lo, a_cum[:, h0:h0 + 1], a_cum[:, h1:h1 + 1])
        xdt = xs_p * dt_p
        ms = []
        for h in (h0, h1):
            seg = a_cum[:, h:h + 1] - a_cumt[h:h + 1, :]
            ms.append(cb * jnp.exp(jnp.where(causal, seg, NEG_BIG)))
        m2 = jnp.concatenate(ms, axis=1).astype(BF16)
        x2 = jnp.concatenate([jnp.where(lane_lo, xdt, 0.0), jnp.where(lane_lo, 0.0, xdt)], axis=0).astype(BF16)
        a_last = jnp.where(lane_lo, a_cum[q - 1:q, h0:h0 + 1], a_cum[q - 1:q, h1:h1 + 1])
        xdw = (xdt * jnp.exp(a_last - acum_p)).astype(BF16)
        yield
        y_diag = _dot(m2, x2)
        prev = state_ref[p]
        y_off = _dot_nt(cg, prev) * jnp.exp(acum_p)
        st = _dot_tn(xdw, bg)
        cd = jnp.where(rowp < HEAD_DIM, jnp.exp(a_cumt[h0:h0 + 1, q - 1:q]),
                       jnp.exp(a_cumt[h1:h1 + 1, q - 1:q]))
        state_ref[p] = prev * cd + st
        d_p = dskip_ref[:, p * LANES:(p + 1) * LANES]
        ys.append(y_diag + y_off + d_p * xs_p)
        yield
    y = jnp.concatenate(ys, axis=1)
    z = z_ref[0]
    y = y * (z * _sigmoid(z))
    gw = width // SSD_GROUPS
    outs = []
    for g in range(SSD_GROUPS):
        yg = y[:, g * gw:(g + 1) * gw]
        outs.append(yg * lax.rsqrt(jnp.mean(yg * yg, axis=-1, keepdims=True) + RMS_EPS))
    y_ref[0] = (jnp.concatenate(outs, axis=1) * ng_ref[...]).astype(y_ref.dtype)


def _ssd_operands(z, xbc, dt_pad, conv_w, conv_b, dt_bias, a_log, d_skip, norm_g):
    bsz, l, width = z.shape
    heads = width // HEAD_DIM
    xw = xbc.shape[-1]
    q = SSD_CHUNK
    dtt = jnp.swapaxes(dt_pad[:, :, :heads], 1, 2)
    pad = LANES - heads
    dtb = jnp.pad(dt_bias, (0, pad)).reshape(1, LANES)
    a_neg = -jnp.exp(a_log.astype(F32))
    aneg = jnp.pad(a_neg, (0, pad)).reshape(1, LANES)
    dskip = jnp.repeat(d_skip, HEAD_DIM).reshape(1, width)
    row = lambda n: pl.BlockSpec((1, n), lambda bi, c: (0, 0))
    specs = [pl.BlockSpec((1, q, width), lambda bi, c: (bi, c, 0)),
             pl.BlockSpec((1, q, xw), lambda bi, c: (bi, c, 0)),
             pl.BlockSpec((1, SUBLANES, xw), lambda bi, c: (bi, jnp.maximum(c * (q // SUBLANES) - 1, 0), 0)),
             pl.BlockSpec((1, q, LANES), lambda bi, c: (bi, c, 0)),
             pl.BlockSpec((1, heads, q), lambda bi, c: (bi, 0, c)),
             pl.BlockSpec((SSD_CONV, xw), lambda bi, c: (0, 0)),
             row(xw), row(LANES),
             pl.BlockSpec((heads, 1), lambda bi, c: (0, 0)),
             row(LANES),
             pl.BlockSpec((heads, 1), lambda bi, c: (0, 0)),
             row(width), row(width)]
    args = [z, xbc, xbc, dt_pad, dtt, conv_w, conv_b.reshape(1, xw), dtb, dt_bias.reshape(heads, 1),
            aneg, a_neg.reshape(heads, 1), dskip, norm_g.reshape(1, width)]
    scratch = [pltpu.VMEM((width // LANES, LANES, SSD_STATE), F32), pltpu.VMEM((q + SUBLANES, xw), F32)]
    return args, specs, scratch


def _rwkv_prep_kernel(n_extra, x_ref, xp_ref, win_ref, mu_ref, w0_ref, a0_ref, kk_ref, ka_ref, w2_ref, a2_ref,
                      g2_ref, e_ref, et_ref, *refs):
    extra_w = refs[:n_extra]
    r_ref, lw_ref, k_ref, v_ref, an_ref, bn_ref, g_ref = refs[n_extra:n_extra + 7]
    extra_o = refs[n_extra + 7:2 * n_extra + 7]
    ext_ref = refs[-1]
    tm = x_ref.shape[1]
    width = r_ref.shape[-1]
    i = pl.program_id(1)
    xe = jnp.concatenate([jnp.where(i > 0, xp_ref[0], 0.0), x_ref[0]], axis=0).astype(BF16)
    win = ext_ref.shape[1]
    lora = RWKV_DECAY_LORA + RWKV_ICLR_LORA

    def project(c0, c1):
        ext_ref[:, c0:c1] = jnp.dot(xe, win_ref[:, c0:c1], preferred_element_type=F32)

    sub = min(tm, RWKV_PREP_ROWS)
    rows = slice(0, sub)

    def shifted_mix(c0, c1):
        rw = ext_ref[SUBLANES + rows.start:SUBLANES + rows.stop, c0:c1]
        prev = ext_ref[SUBLANES - 1 + rows.start:SUBLANES - 1 + rows.stop, c0:c1]
        return rw + (prev - rw) * mu_ref[:, c0:c1]

    def projections():
        half = width // 2
        extras = [(w_ref, o_ref, sl) for w_ref, o_ref in zip(extra_w, extra_o)
                  for sl in _col_chunks(w_ref.shape[1], 256)]
        own = [(3 * width, win), (width, width + half), (width + half, 2 * width), (0, half), (half, width),
               (2 * width, 2 * width + half), (2 * width + half, 3 * width)]
        for n, cols in enumerate(own):
            project(*cols)
            yield
            if n >= 2 and extras:
                w_ref, o_ref, sl = extras.pop(0)
                o_ref[0, :, sl] = jnp.dot(xe[SUBLANES:], w_ref[:, sl], preferred_element_type=F32)
                yield
        for w_ref, o_ref, sl in extras:
            o_ref[0, :, sl] = jnp.dot(xe[SUBLANES:], w_ref[:, sl], preferred_element_type=F32)
            yield

    pieces = projections()

    def mxu(n=1):
        for _ in range(n):
            next(pieces, None)

    for r0 in range(0, tm, sub):
        rows = slice(r0, r0 + sub)
        mxu(3)
        s_lo = shifted_mix(3 * width, win)
        lo = s_lo[:, :lora]
        wv = w0_ref[...] + _dot(jnp.tanh(lo), w2_ref[...])
        av = _sigmoid(a0_ref[...] + _dot(lo, a2_ref[...]))
        g_ref[0, rows, :] = _dot(_sigmoid(s_lo[:, lora:]), g2_ref[...]).astype(g_ref.dtype)
        mxu(1)
        for sl in _col_chunks(width, 512):
            lw_ref[0, rows, sl] = -jnp.exp(-_softplus(-wv[:, sl]) - 0.5)
            mxu(1)
        k = shifted_mix(width, 2 * width)
        kkr = k * kk_ref[...]
        mxu(1)
        ss = _dot_exact_rhs(kkr * kkr, e_ref[...])
        k_ref[0, rows, :] = (k * (1.0 + (av - 1.0) * ka_ref[...])).astype(k_ref.dtype)
        mxu(2)
        inv = lax.rsqrt(jnp.maximum(ss, 1e-24))
        kk = kkr * _dot_exact_rhs(inv, et_ref[...])
        mxu(1)
        an_ref[0, rows, :] = (-kk).astype(an_ref.dtype)
        bn_ref[0, rows, :] = (kk * av).astype(bn_ref.dtype)
        mxu(2)
        r_ref[0, rows, :] = shifted_mix(0, width).astype(r_ref.dtype)
        mxu(1)
        v_ref[0, rows, :] = shifted_mix(2 * width, 3 * width).astype(v_ref.dtype)
    for _ in pieces:
        pass


def _rwkv_chunks(rows_list, fill, r_ref, lw_ref, k_ref, v_ref, an_ref, bn_ref, g_ref, rk_ref, lg_ref, lb_ref,
                 y_ref, state_ref):
    c = RWKV_CHUNK
    c2 = 2 * c
    width = r_ref.shape[-1]
    n_pairs = width // LANES

    row = lax.broadcasted_iota(jnp.int32, (c, c), 0)
    col = lax.broadcasted_iota(jnp.int32, (c, c), 1)
    tri = jnp.where(row >= col, 1.0, 0.0)
    roww = lax.broadcasted_iota(jnp.int32, (c, c2), 0)
    colw = lax.broadcasted_iota(jnp.int32, (c, c2), 1) % c
    strict = jnp.where(roww > colw, 1.0, 0.0)
    incl = jnp.where(roww >= colw, 1.0, 0.0)
    eye = jnp.where(roww == colw, 1.0, 0.0)
    same_head = jnp.where((lax.broadcasted_iota(jnp.int32, (LANES, LANES), 0) // HEAD_DIM)
                          == (lax.broadcasted_iota(jnp.int32, (LANES, LANES), 1) // HEAD_DIM), 1.0, 0.0)
    lane = lax.broadcasted_iota(jnp.int32, (1, LANES), 1)
    m0 = jnp.where(lane < HEAD_DIM, 1.0, 0.0)
    m1 = 1.0 - m0

    lane_lo2 = lax.broadcasted_iota(jnp.int32, (c, LANES), 1) < HEAD_DIM

    def stack(x):
        xb = x.astype(BF16)
        zero = jnp.zeros_like(xb)
        return jnp.concatenate([jnp.where(lane_lo2, xb, zero), jnp.where(lane_lo2, zero, xb)], axis=0)

    sls = [slice(p * LANES, (p + 1) * LANES) for p in range(n_pairs)]
    units = [(rows, sl) for rows in rows_list for sl in sls]
    lhs2s, rhs4s, vs, v_stks, bks, g_lasts = [], [], [], [], [], []
    for rows, sl in units:
        r = r_ref[0, rows, sl].astype(F32)
        lw = lw_ref[0, rows, sl]
        k = k_ref[0, rows, sl].astype(F32)
        b = bn_ref[0, rows, sl].astype(F32)
        v = v_ref[0, rows, sl].astype(F32)
        cum = _dot_exact_lhs(tri, lw)
        cum_last = cum[c - 1:c, :]
        e_neg = jnp.exp(-cum)
        e_tail = jnp.exp(cum_last - cum)
        at = an_ref[0, rows, sl].astype(F32) * jnp.exp(cum - lw)
        rt = r * jnp.exp(cum)
        lhs2s.append(jnp.concatenate([at, rt], axis=0).astype(BF16))
        rhs4s.append(jnp.concatenate([stack(b * e_neg), stack(k * e_neg)], axis=0))
        vs.append(v.astype(BF16))
        v_stks.append(stack(v))
        bks.append(jnp.concatenate([b * e_tail, k * e_tail], axis=0).astype(BF16))
        g_lasts.append(jnp.exp(cum_last))
    gms = [_dot_nt(l, rh) for l, rh in zip(lhs2s, rhs4s)]
    pws = [gm[:c, :c2] * strict for gm in gms]
    a_aks = [(gm[:c, c2:] * strict).astype(BF16) for gm in gms]
    a_rbks = [jnp.concatenate([gm[c:, :c2] * incl, gm[c:, c2:] * incl], axis=1).astype(BF16) for gm in gms]
    akvs = [_dot(a_ak, v_stk) for a_ak, v_stk in zip(a_aks, v_stks)]
    fill()
    tinvs = [eye + pw for pw in pws]
    pw_stks = [stack(pw) for pw in pws]
    for _ in range(int(math.log2(c)) - 1):
        pws = [_dot(pw, pw_stk) for pw, pw_stk in zip(pws, pw_stks)]
        pw_stks = [stack(pw) for pw in pws]
        tinvs = [tinv + _dot(tinv, pw_stk) for tinv, pw_stk in zip(tinvs, pw_stks)]
        fill()
    lane_lo = lane < HEAD_DIM

    def head_sum(x):
        lo = jnp.sum(x * m0, axis=-1, keepdims=True)
        hi = jnp.sum(x * m1, axis=-1, keepdims=True)
        return jnp.where(lane_lo, lo, hi)

    for si, rows in enumerate(rows_list):
        u0 = si * n_pairs
        sts = [state_ref[p] for p in range(n_pairs)]
        ahrhs = [_dot_nt(lhs2s[u0 + p], sts[p]) for p in range(n_pairs)]
        fill()
        us = [_dot(tinvs[u0 + p], stack(ahrhs[p][:c] + akvs[u0 + p])) for p in range(n_pairs)]
        fill()
        for p in range(n_pairs):
            uv = jnp.concatenate([us[p].astype(BF16), vs[u0 + p]], axis=0)
            state_ref[p] = sts[p] * g_lasts[u0 + p] + same_head * _dot_tn(uv, bks[u0 + p])
        fill()
        ys = [ahrhs[p][c:] + _dot(a_rbks[u0 + p], jnp.concatenate([stack(us[p]), v_stks[u0 + p]], axis=0))
              for p in range(n_pairs)]
        bonus = [head_sum(r_ref[0, rows, sl].astype(F32) * k_ref[0, rows, sl].astype(F32) * rk_ref[:, sl])
                 for sl in sls]
        ycs = [y - head_sum(y) * (1.0 / HEAD_DIM) for y in ys]
        yvs = [head_sum(yc * yc) * (1.0 / HEAD_DIM) for yc in ycs]
        for p, sl in enumerate(sls):
            yn = ycs[p] * lax.rsqrt(yvs[p] + RWKV_LNX_EPS) * lg_ref[:, sl] + lb_ref[:, sl]
            out = (yn + bonus[p] * v_ref[0, rows, sl].astype(F32)) * g_ref[0, rows, sl].astype(F32)
            y_ref[0, rows, sl] = out.astype(y_ref.dtype)


def _rwkv_operands(x, w_in, extra_ws, mu, w0, w2, a0, a2, g2, k_k, k_a, r_k, lnx_g, lnx_b, *, tm):
    bsz, l, d = x.shape
    n_extra = len(extra_ws)
    win = w_in.shape[1]
    width = w0.shape[0]
    w2p = jnp.concatenate([w2, jnp.zeros((RWKV_ICLR_LORA, width), F32)], axis=0).astype(BF16)
    a2p = jnp.concatenate([jnp.zeros((RWKV_DECAY_LORA, width), F32), a2], axis=0).astype(BF16)
    head_of = jnp.arange(width) // HEAD_DIM
    e = (head_of[:, None] == jnp.arange(LANES)[None, :]).astype(BF16)
    et = e.T
    vec = lambda x: x.reshape(1, -1)
    row = lambda n: pl.BlockSpec((1, n), lambda bi, i: (0, 0))
    full = lambda a: pl.BlockSpec(a.shape, lambda bi, i: (0, 0))
    single = lambda a: pl.BlockSpec(a.shape, lambda bi, i: (0, 0), pipeline_mode=pl.Buffered(1))
    tile = lambda w: pl.BlockSpec((1, tm, w), lambda bi, i: (bi, i, 0))
    sds = lambda w, dt=F32: jax.ShapeDtypeStruct((bsz, l, w), dt)
    g2b = g2.astype(BF16)
    outs = pl.pallas_call(
        functools.partial(_rwkv_prep_kernel, n_extra),
        grid=(bsz, l // tm),
        in_specs=[pl.BlockSpec((1, tm, d), lambda bi, i: (bi, i, 0)),
                  pl.BlockSpec((1, SUBLANES, d), lambda bi, i: (bi, jnp.maximum(i * (tm // SUBLANES) - 1, 0), 0)),
                  single(w_in), row(win), row(width), row(width), row(width), row(width),
                  full(w2p), full(a2p), full(g2b), full(e), full(et)] + [single(w) for w in extra_ws],
        out_specs=[tile(width)] * 7 + [tile(w.shape[1]) for w in extra_ws],
        out_shape=([sds(width, BF16), sds(width)] + [sds(width, BF16)] * 5
                   + [sds(w.shape[1]) for w in extra_ws]),
        scratch_shapes=[pltpu.VMEM((tm + SUBLANES, win), F32)],
        compiler_params=_params("parallel", "parallel"),
        name="rwkv_prep",
    )(x, x, w_in, vec(mu), vec(w0), vec(a0), vec(k_k), vec(k_a), w2p, a2p, g2b, e, et, *extra_ws)
    r, lw, k, v, an, bn, g = outs[:7]
    ctile = pl.BlockSpec((1, SSD_CHUNK, width), lambda bi, ci: (bi, ci, 0))
    crow = pl.BlockSpec((1, width), lambda bi, ci: (0, 0))
    args = [r, lw, k, v, an, bn, g, vec(r_k), vec(lnx_g), vec(lnx_b)]
    scratch = [pltpu.VMEM((width // LANES, LANES, LANES), F32)]
    return outs[7:], (args, [ctile] * 7 + [crow] * 3, scratch)


def _mixers_kernel(n_ssd, n_rwkv, *refs):
    ssd_in = refs[:n_ssd]
    rwkv_in = refs[n_ssd:n_ssd + n_rwkv]
    y_ssd_ref, y_rwkv_ref, ssd_state, ssd_ext, rwkv_state = refs[n_ssd + n_rwkv:]
    c = pl.program_id(1)

    @pl.when(c == 0)
    def _():
        ssd_state[...] = jnp.zeros_like(ssd_state)
        rwkv_state[...] = jnp.zeros_like(rwkv_state)

    ssd_steps = _ssd_chunk(c, *ssd_in, y_ssd_ref, ssd_state, ssd_ext)
    fill = lambda: (next(ssd_steps, None), next(ssd_steps, None))
    rows_list = [slice(sub * RWKV_CHUNK, (sub + 1) * RWKV_CHUNK) for sub in range(SSD_CHUNK // RWKV_CHUNK)]
    _rwkv_chunks(rows_list, fill, *rwkv_in, y_rwkv_ref, rwkv_state)
    for _ in ssd_steps:
        pass


def _ssd_rwkv_mixers(ssd_ops, rwkv_ops, bsz, l, ssd_width, rwkv_width):
    ssd_args, ssd_specs, ssd_scratch = ssd_ops
    rwkv_args, rwkv_specs, rwkv_scratch = rwkv_ops
    q = SSD_CHUNK
    out_spec = lambda w: pl.BlockSpec((1, q, w), lambda bi, c: (bi, c, 0))
    return pl.pallas_call(
        functools.partial(_mixers_kernel, len(ssd_args), len(rwkv_args)),
        grid=(bsz, l // q),
        in_specs=ssd_specs + rwkv_specs,
        out_specs=[out_spec(ssd_width), out_spec(rwkv_width)],
        out_shape=[jax.ShapeDtypeStruct((bsz, l, ssd_width), BF16), jax.ShapeDtypeStruct((bsz, l, rwkv_width), BF16)],
        scratch_shapes=ssd_scratch + rwkv_scratch,
        compiler_params=_params("parallel", "arbitrary"),
        name="ssd_rwkv_mixers",
    )(*ssd_args, *rwkv_args)


def _moba_kernel(q_ref, qn_ref, k_ref, v_ref, o_ref, kb_ref, vt_ref, kmean_ref, qo_ref, qp_ref):
    blk = MOBA_BLOCK
    half = HEAD_DIM
    nb = k_ref.shape[0] // blk
    nbp = kmean_ref.shape[0]
    npair = q_ref.shape[1] // LANES
    heads = range(2 * npair)
    group = math.gcd(nb, MOBA_GROUP)
    qi = pl.program_id(2)

    @pl.when(qi == 0)
    def _():
        lane = lax.broadcasted_iota(jnp.int32, (blk, LANES), 1)
        rowp = lax.broadcasted_iota(jnp.int32, (LANES, blk), 0)
        if nbp > nb:
            kmean_ref[...] = jnp.zeros_like(kmean_ref)
        for n in range(nb):
            rows = slice(n * blk, (n + 1) * blk)
            kmean_ref[n:n + 1, :] = jnp.mean(k_ref[rows, :], axis=0, keepdims=True)
            for pp in range(npair):
                cols = slice(pp * LANES, (pp + 1) * LANES)
                kn = k_ref[rows, cols]
                kb_ref[2 * pp, n] = jnp.where(lane < half, kn, jnp.where(lane == half + n, 1.0, 0.0)).astype(BF16)
                kb_ref[2 * pp + 1, n] = jnp.where(lane >= half, kn, jnp.where(lane == n, 1.0, 0.0)).astype(BF16)
                vtn = v_ref[rows, cols].T
                vt_ref[2 * pp, n] = jnp.where(rowp < half, vtn, jnp.where(rowp == half, 1.0, 0.0)).astype(BF16)
                vt_ref[2 * pp + 1, n] = jnp.where(rowp >= half, vtn, jnp.where(rowp == 0, 1.0, 0.0)).astype(BF16)

    rown = lax.broadcasted_iota(jnp.int32, (nbp, blk), 0)
    lane_k = lax.broadcasted_iota(jnp.int32, (nbp, LANES), 1)
    zeros = jnp.zeros((half, blk), F32)

    def prepare_queries(src_ref, tile):
        qts = [src_ref[:, pp * LANES:(pp + 1) * LANES].T for pp in range(npair)]
        gates, qhs = [], []
        for hh in heads:
            pp, h = divmod(hh, 2)
            km = kmean_ref[:, pp * LANES:(pp + 1) * LANES]
            kmh = jnp.where((lane_k < half) if h == 0 else (lane_k >= half), km, 0.0)
            gates.append(jnp.dot(kmh, qts[pp], preferred_element_type=F32, precision=lax.Precision.HIGHEST))
            qhs.append(qts[pp][h * half:(h + 1) * half] * (half ** -0.5 * LOG2_E))
        for hh in heads:
            gate = jnp.where(rown < tile, gates[hh], -jnp.inf)
            bias = jnp.full((nbp, blk), NEG_BIG, F32)
            for _ in range(MOBA_TOPK):
                mx = jnp.max(gate, axis=0, keepdims=True)
                first = jnp.min(jnp.where(gate == mx, rown, nbp), axis=0, keepdims=True)
                pick = (rown == first) & (mx > -jnp.inf)
                bias = jnp.where(pick, 0.0, bias)
                gate = jnp.where(pick, -jnp.inf, gate)
            aug = jnp.concatenate([bias, jnp.zeros((half - nbp, blk), F32)], axis=0)
            qo_ref[hh] = jnp.concatenate([qhs[hh], zeros] if hh % 2 == 0 else [zeros, qhs[hh]],
                                         axis=0).astype(BF16)
            qp_ref[hh] = jnp.concatenate([qhs[hh], aug] if hh % 2 == 0 else [aug, qhs[hh]], axis=0).astype(BF16)

    @pl.when(qi == 0)
    def _():
        prepare_queries(q_ref, 0)

    q_own = [qo_ref[h] for h in heads]
    q_past = [qp_ref[h] for h in heads]
    s_own = [jnp.dot(kb_ref[h, qi], q_own[h], preferred_element_type=F32) for h in heads]
    prepare_queries(qn_ref, qi + 1)

    causal = (lax.broadcasted_iota(jnp.int32, (blk, blk), 0) <= lax.broadcasted_iota(jnp.int32, (blk, blk), 1))
    ms, ps = [], []
    for h in heads:
        s = jnp.where(causal, s_own[h], NEG_BIG)
        ms.append(jnp.max(s, axis=0, keepdims=True))
        ps.append(jnp.exp2(s - ms[h]).astype(BF16))
    carry = []
    for h in heads:
        carry += [ms[h], jnp.dot(vt_ref[h, qi], ps[h], preferred_element_type=F32)]

    def scores(gi, h):
        return [jnp.dot(kb_ref[h, gi * group + g], q_past[h], preferred_element_type=F32) for g in range(group)]

    def values(gi, h):
        return jnp.concatenate([vt_ref[h, gi * group + g] for g in range(group)], axis=1)

    def body_lagged(gi, carry):
        excess = carry[-1]
        out = []
        sss = [scores(gi, h) for h in heads]
        pcats, gmaxs = [], []
        for h in heads:
            m_run = carry[2 * h]
            gmax = None
            ps = []
            for s in sss[h]:
                cm = jnp.max(s, axis=0, keepdims=True)
                gmax = cm if gmax is None else jnp.maximum(gmax, cm)
                ps.append(jnp.exp2(s - m_run).astype(BF16))
            pcats.append(jnp.concatenate(ps, axis=0))
            gmaxs.append(gmax)
        for h in heads:
            m_run, acc = carry[2 * h], carry[2 * h + 1]
            m_new = jnp.maximum(m_run, gmaxs[h])
            excess = jnp.maximum(excess, gmaxs[h] - m_run)
            acc = jnp.exp2(m_run - m_new) * (acc + jnp.dot(values(gi, h), pcats[h], preferred_element_type=F32))
            out += [m_new, acc]
        return tuple(out) + (excess,)

    def body_exact_max(gi, carry):
        sss = [scores(gi, h) for h in heads]
        m_news = []
        for h in heads:
            m_new = carry[2 * h]
            for s in sss[h]:
                m_new = jnp.maximum(m_new, jnp.max(s, axis=0, keepdims=True))
            m_news.append(m_new)
        pcats = [jnp.concatenate([jnp.exp2(s - m_news[h]).astype(BF16) for s in sss[h]], axis=0)
                 for h in heads]
        out = []
        for h in heads:
            alpha = jnp.exp2(carry[2 * h] - m_news[h])
            out += [m_news[h], alpha * carry[2 * h + 1]
                    + jnp.dot(values(gi, h), pcats[h], preferred_element_type=F32)]
        return tuple(out)

    rowq = lax.broadcasted_iota(jnp.int32, (LANES, blk), 0)

    def write_out(final):
        for pp in range(npair):
            acc0, acc1 = final[4 * pp + 1], final[4 * pp + 3]
            out_t = jnp.where(rowq < half, acc0 / acc0[half:half + 1], acc1 / acc1[0:1])
            o_ref[:, pp * LANES:(pp + 1) * LANES] = out_t.T.astype(o_ref.dtype)

    n_groups = (qi + group - 1) // group
    final = lax.fori_loop(0, n_groups, body_lagged, tuple(carry) + (jnp.full((1, blk), NEG_BIG, F32),))
    write_out(final)

    @pl.when(jnp.max(final[-1]) > MOBA_LAG_LIMIT)
    def _():
        write_out(lax.fori_loop(0, n_groups, body_exact_max, tuple(carry)))


def _moba_attention(qkv, bsz, s, heads):
    blk = MOBA_BLOCK
    assert s % blk == 0 and (heads * HEAD_DIM) % LANES == 0
    nb = s // blk
    assert nb <= HEAD_DIM
    nbp = -(-nb // SUBLANES) * SUBLANES
    pairs = heads * HEAD_DIM // LANES
    pps = math.gcd(pairs, MOBA_PAIRS_PER_STEP)
    cw = pps * LANES
    steps = pairs // pps
    return pl.pallas_call(
        _moba_kernel,
        grid=(bsz, steps, nb),
        in_specs=[pl.BlockSpec((blk, cw), lambda b, p, i: (b * nb + i, p)),
                  pl.BlockSpec((blk, cw), lambda b, p, i: (b * nb + jnp.minimum(i + 1, nb - 1), p)),
                  pl.BlockSpec((s, cw), lambda b, p, i: (b, steps + p)),
                  pl.BlockSpec((s, cw), lambda b, p, i: (b, 2 * steps + p))],
        out_specs=pl.BlockSpec((blk, cw), lambda b, p, i: (b * nb + i, p)),
        out_shape=jax.ShapeDtypeStruct((bsz * s, heads * HEAD_DIM), BF16),
        scratch_shapes=[pltpu.VMEM((2 * pps, nb, blk, LANES), BF16), pltpu.VMEM((2 * pps, nb, LANES, blk), BF16),
                        pltpu.VMEM((nbp, cw), F32),
                        pltpu.VMEM((2 * pps, LANES, blk), BF16), pltpu.VMEM((2 * pps, LANES, blk), BF16)],
        compiler_params=_params("parallel", "parallel", "arbitrary"),
        name="moba_attention",
    )(qkv, qkv, qkv, qkv)


def _row_tile(m):
    for t in (512, 256, 128, 64, 32, 16, 8):
        if m % t == 0:
            return t
    raise ValueError(f"row count {m} is not a multiple of 8")


def _col_tile(n, cap=2048):
    best = None
    for t in range(LANES, min(n, cap) + 1, LANES):
        if n % t == 0:
            best = t
    if best is None:
        raise ValueError(f"column count {n} is not a multiple of {LANES}")
    return best


def kernel(x, mem, even_w_in, ssd_conv_w, ssd_conv_b, ssd_dt_bias, ssd_a_log, ssd_d, ssd_norm_g, rwkv_mu, rwkv_w0, rwkv_w2, rwkv_a0, rwkv_a2, rwkv_g2, rwkv_k_k, rwkv_k_a, rwkv_r_k, rwkv_lnx_g, rwkv_lnx_b, even_w_out, odd_w_qkv, odd_w_out, ln_mix_g, ln_mix_b, xa_wq, xa_wkv, xa_wo, ln_xa_g, ln_xa_b, ffn_w13, ffn_w2, ln_ffn_g, ln_ffn_b):
    bsz, s, d = x.shape
    m = bsz * s
    tm = _row_tile(s)
    ssd_width = ssd_norm_g.shape[-1]
    ssd_heads = ssd_dt_bias.shape[-1]
    ssd_xbc = ssd_conv_b.shape[-1]
    ssd_in = ssd_width + ssd_xbc + ssd_heads
    rwkv_width = rwkv_w0.shape[-1]
    mem2 = mem.reshape(bsz * mem.shape[1], d)
    x2 = x.reshape(m, d)
    for layer in range(DEPTH):
        j = layer // 2
        if layer % 2 == 0:
            w_in = even_w_in[j].astype(BF16)
            w_z = w_in[:, :ssd_width]
            w_xbc = w_in[:, ssd_width:ssd_width + ssd_xbc]
            w_dt = jnp.pad(w_in[:, ssd_width + ssd_xbc:ssd_in], ((0, 0), (0, LANES - ssd_heads)))
            w_rw = w_in[:, ssd_in:]
            (z, xbc, dt_pad), rwkv_ops = _rwkv_operands(
                x2.reshape(bsz, s, d), w_rw, [w_z, w_xbc, w_dt], rwkv_mu[j], rwkv_w0[j], rwkv_w2[j],
                rwkv_a0[j], rwkv_a2[j], rwkv_g2[j], rwkv_k_k[j], rwkv_k_a[j], rwkv_r_k[j],
                rwkv_lnx_g[j], rwkv_lnx_b[j], tm=tm)
            ssd_ops = _ssd_operands(z, xbc, dt_pad, ssd_conv_w[j], ssd_conv_b[j], ssd_dt_bias[j],
                                    ssd_a_log[j], ssd_d[j], ssd_norm_g[j])
            y_ssd, y_rwkv = _ssd_rwkv_mixers(ssd_ops, rwkv_ops, bsz, s, ssd_width, rwkv_width)
            x2 = _matmul_residual_ln([y_ssd.reshape(m, -1), y_rwkv.reshape(m, -1)],
                                     even_w_out[j].astype(BF16), x2,
                                     ln_mix_g[layer], ln_mix_b[layer], tm=tm)
        else:
            heads = d // HEAD_DIM
            qkv, = _matmul(x2, [odd_w_qkv[j].astype(BF16)], tm=tm)
            attn = _moba_attention(qkv, bsz, s, heads)
            x2 = _matmul_residual_ln([attn], odd_w_out[j].astype(BF16), x2,
                                     ln_mix_g[layer], ln_mix_b[layer], tm=tm)
        kv, = _matmul(mem2, [xa_wkv[layer].astype(BF16)], tm=_row_tile(mem2.shape[0]))
        x3 = _cross_attention_ln(x2.reshape(bsz, s, d), kv.reshape(bsz, -1, 2 * d),
                                 xa_wq[layer].astype(BF16), xa_wo[layer].astype(BF16),
                                 ln_xa_g[layer], ln_xa_b[layer], tm=tm)
        x2 = x3.reshape(m, d)
        h = _swiglu_up(x2, ffn_w13[layer].astype(BF16), tm=tm)
        x2 = _matmul_residual_ln([h], ffn_w2[layer].astype(BF16), x2,
                                 ln_ffn_g[layer], ln_ffn_b[layer], tm=tm)
    return x2.reshape(bsz, s, d)
```

```python
import functools
import math

import jax
import jax.numpy as jnp
from jax import lax
from jax.experimental import pallas as pl
from jax.experimental.pallas import tpu as pltpu

F32 = jnp.float32
BF16 = jnp.bfloat16

HEAD_DIM = 64
LANES = 128
SUBLANES = 8
SSD_GROUPS = 2
SSD_STATE = 128
SSD_CONV = 4
SSD_CHUNK = 128
RWKV_DECAY_LORA = 64
RWKV_ICLR_LORA = 64
RWKV_GATE_LORA = 128
RWKV_CHUNK = 64
RWKV_PREP_ROWS = 256
MOBA_BLOCK = 256
MOBA_TOPK = 3
MOBA_GROUP = 4
MOBA_LAG_LIMIT = 64.0
MOBA_PAIRS_PER_STEP = 2
XATTN_HEADS = 4
SWIGLU_COLS = 256
DEPTH = 2
DEEPNORM_ALPHA = (2 * DEPTH) ** 0.25
LN_EPS = 1e-5
RMS_EPS = 1e-5
RWKV_LNX_EPS = 64e-5
NEG_BIG = -1e30
LOG2_E = math.log2(math.e)
VMEM_LIMIT = 60 * 1024 * 1024


def _params(*sem):
    return pltpu.CompilerParams(dimension_semantics=sem, vmem_limit_bytes=VMEM_LIMIT)


def _dot(a, b):
    return jnp.dot(a.astype(BF16), b.astype(BF16), preferred_element_type=F32)


def _dot_nt(a, b):
    return lax.dot_general(a.astype(BF16), b.astype(BF16), (((1,), (1,)), ((), ())),
                           preferred_element_type=F32)


def _dot_tn(a, b):
    return lax.dot_general(a.astype(BF16), b.astype(BF16), (((0,), (0,)), ((), ())),
                           preferred_element_type=F32)


def _split3(x):
    hi = x.astype(BF16)
    r1 = x - hi.astype(F32)
    mid = r1.astype(BF16)
    lo = (r1 - mid.astype(F32)).astype(BF16)
    return hi, mid, lo


def _dot_exact_lhs(m, x):
    hi, mid, lo = _split3(x)
    m = m.astype(BF16)
    return (jnp.dot(m, hi, preferred_element_type=F32) + jnp.dot(m, mid, preferred_element_type=F32)
            + jnp.dot(m, lo, preferred_element_type=F32))


def _dot_exact_rhs(x, m):
    hi, mid, lo = _split3(x)
    m = m.astype(BF16)
    return (jnp.dot(hi, m, preferred_element_type=F32) + jnp.dot(mid, m, preferred_element_type=F32)
            + jnp.dot(lo, m, preferred_element_type=F32))


def _sigmoid(x):
    return 1.0 / (1.0 + jnp.exp(-x))


def _softplus(x):
    return jnp.maximum(x, 0.0) + jnp.log1p(jnp.exp(-jnp.abs(x)))


def _layer_norm(v, g, b):
    mu = jnp.mean(v, axis=-1, keepdims=True)
    c = v - mu
    var = jnp.mean(c * c, axis=-1, keepdims=True)
    return c * lax.rsqrt(var + LN_EPS) * g + b


def _col_chunks(n, cap=1536):
    width = _col_tile(n, cap)
    return [slice(j, j + width) for j in range(0, n, width)]


def _mm_kernel(n_out, x_ref, *refs):
    x = x_ref[...].astype(BF16)
    for w_ref, o_ref in zip(refs[:n_out], refs[n_out:]):
        for sl in _col_chunks(w_ref.shape[1]):
            o_ref[:, sl] = jnp.dot(x, w_ref[:, sl], preferred_element_type=F32).astype(o_ref.dtype)


def _matmul(x, ws, *, tm, out_dtype=F32):
    m, k = x.shape
    assert m % tm == 0
    return pl.pallas_call(
        functools.partial(_mm_kernel, len(ws)),
        grid=(m // tm,),
        in_specs=[pl.BlockSpec((tm, k), lambda i: (i, 0))] + [pl.BlockSpec(w.shape, lambda i: (0, 0)) for w in ws],
        out_specs=[pl.BlockSpec((tm, w.shape[1]), lambda i: (i, 0)) for w in ws],
        out_shape=[jax.ShapeDtypeStruct((m, w.shape[1]), out_dtype) for w in ws],
        compiler_params=_params("parallel"),
        name="matmul",
    )(x, *ws)


def _mm_res_ln_kernel(n_in, *refs):
    hs = refs[:n_in]
    ws = refs[n_in:2 * n_in]
    res_ref, g_ref, b_ref, o_ref = refs[2 * n_in:]
    tm = o_ref.shape[0]
    groups = [slice(0, tm // 2), slice(tm // 2, tm)]
    accs = []
    for rows in groups:
        acc = _dot(hs[0][rows, :], ws[0][...])
        for h_ref, w_ref in zip(hs[1:], ws[1:]):
            acc = acc + _dot(h_ref[rows, :], w_ref[...])
        accs.append(acc)
    for rows, acc in zip(groups, accs):
        o_ref[rows, :] = _layer_norm(DEEPNORM_ALPHA * res_ref[rows, :] + acc, g_ref[...], b_ref[...])


def _matmul_residual_ln(hs, w, res, g, b, *, tm):
    m, d = res.shape
    n_in = len(hs)
    kw = w.shape[0] // n_in
    assert all(h.shape[1] == kw for h in hs)
    in_specs = ([pl.BlockSpec((tm, kw), lambda i: (i, 0)) for _ in hs]
                + [pl.BlockSpec((kw, d), lambda i, j=j: (j, 0)) for j in range(n_in)]
                + [pl.BlockSpec((tm, d), lambda i: (i, 0)),
                   pl.BlockSpec((1, d), lambda i: (0, 0)),
                   pl.BlockSpec((1, d), lambda i: (0, 0))])
    return pl.pallas_call(
        functools.partial(_mm_res_ln_kernel, n_in),
        grid=(m // tm,),
        in_specs=in_specs,
        out_specs=pl.BlockSpec((tm, d), lambda i: (i, 0)),
        out_shape=jax.ShapeDtypeStruct((m, d), F32),
        compiler_params=_params("parallel"),
        name="matmul_residual_ln",
    )(*hs, *([w] * n_in), res, g.reshape(1, d), b.reshape(1, d))


def _swiglu_kernel(x_ref, w13_ref, o_ref):
    x = x_ref[...].astype(BF16)
    n = o_ref.shape[1]
    pending = None
    for sl in _col_chunks(n, SWIGLU_COLS) + [None]:
        if sl is not None:
            gate = jnp.dot(x, w13_ref[:, sl], preferred_element_type=F32)
            up = jnp.dot(x, w13_ref[:, slice(n + sl.start, n + sl.stop)], preferred_element_type=F32)
        if pending is not None:
            psl, pgate, pup = pending
            o_ref[:, psl] = (pgate * _sigmoid(pgate) * pup).astype(o_ref.dtype)
        pending = (sl, gate, up) if sl is not None else None


def _swiglu_up(x, w13, *, tm):
    m, k = x.shape
    n = w13.shape[1] // 2
    return pl.pallas_call(
        _swiglu_kernel,
        grid=(m // tm,),
        in_specs=[pl.BlockSpec((tm, k), lambda i: (i, 0)),
                  pl.BlockSpec((k, 2 * n), lambda i: (0, 0))],
        out_specs=pl.BlockSpec((tm, n), lambda i: (i, 0)),
        out_shape=jax.ShapeDtypeStruct((m, n), BF16),
        compiler_params=_params("parallel"),
        name="swiglu_up",
    )(x, w13)


def _xattn_kernel(x_ref, kv_ref, wq_ref, wo_ref, g_ref, b_ref, o_ref):
    x = x_ref[0]
    d = x.shape[-1]
    hd = d // XATTN_HEADS
    q = _dot(x, wq_ref[...])
    kv = kv_ref[0]
    heads = range(XATTN_HEADS)
    ss = [_dot_nt(q[:, h * hd:(h + 1) * hd], kv[:, h * hd:(h + 1) * hd]) * (hd ** -0.5) for h in heads]
    ps = []
    for s in ss:
        p = jnp.exp(s - jnp.max(s, axis=-1, keepdims=True))
        ps.append(p / jnp.sum(p, axis=-1, keepdims=True))
    o = jnp.concatenate([_dot(ps[h], kv[:, d + h * hd:d + (h + 1) * hd]) for h in heads], axis=-1)
    xa = _dot(o, wo_ref[...])
    o_ref[0] = _layer_norm(DEEPNORM_ALPHA * x + xa, g_ref[...], b_ref[...])


def _cross_attention_ln(x, kv, wq, wo, g, b, *, tm):
    bsz, s, d = x.shape
    m = kv.shape[1]
    return pl.pallas_call(
        _xattn_kernel,
        grid=(bsz, s // tm),
        in_specs=[pl.BlockSpec((1, tm, d), lambda bi, i: (bi, i, 0)),
                  pl.BlockSpec((1, m, 2 * d), lambda bi, i: (bi, 0, 0)),
                  pl.BlockSpec((d, d), lambda bi, i: (0, 0)),
                  pl.BlockSpec((d, d), lambda bi, i: (0, 0)),
                  pl.BlockSpec((1, d), lambda bi, i: (0, 0)),
                  pl.BlockSpec((1, d), lambda bi, i: (0, 0))],
        out_specs=pl.BlockSpec((1, tm, d), lambda bi, i: (bi, i, 0)),
        out_shape=jax.ShapeDtypeStruct((bsz, s, d), F32),
        compiler_params=_params("parallel", "parallel"),
        name="cross_attention_ln",
    )(x, kv, wq, wo, g.reshape(1, d), b.reshape(1, d))


def _ssd_chunk(c, z_ref, xbc_ref, xbcp_ref, dt_ref, dtt_ref, cw_ref, cb_ref, dtb_ref, dtbt_ref,
               aneg_ref, anegt_ref, dskip_ref, ng_ref, y_ref, state_ref, ext_ref):
    q = SSD_CHUNK
    width = z_ref.shape[-1]
    n_pairs = width // LANES

    ext_ref[0:SUBLANES, :] = jnp.where(c > 0, xbcp_ref[0], 0.0)
    ext_ref[SUBLANES:SUBLANES + q, :] = xbc_ref[0]
    xcs = []
    for sl in _col_chunks(ext_ref.shape[1], 256):
        conv = cb_ref[:, sl] + cw_ref[SSD_CONV - 1:SSD_CONV, sl] * ext_ref[SUBLANES:SUBLANES + q, sl]
        for k in range(SSD_CONV - 1):
            off = SUBLANES - (SSD_CONV - 1) + k
            conv = conv + cw_ref[k:k + 1, sl] * ext_ref[off:off + q, sl]
        xcs.append(conv * _sigmoid(conv))
        yield
    xc = jnp.concatenate(xcs, axis=1)
    xs = xc[:, :width]
    gn = SSD_GROUPS * SSD_STATE
    bm = xc[:, width:width + gn]
    cm = xc[:, width + gn:width + 2 * gn]

    dt = _softplus(dt_ref[0] + dtb_ref[...])
    a = dt * aneg_ref[...]
    dtt = _softplus(dtt_ref[0] + dtbt_ref[...])
    at = dtt * anegt_ref[...]
    row = lax.broadcasted_iota(jnp.int32, (q, q), 0)
    col = lax.broadcasted_iota(jnp.int32, (q, q), 1)
    causal = row >= col
    tri = jnp.where(causal, 1.0, 0.0)
    a_cum = _dot_exact_lhs(tri, a)
    a_cumt = _dot_exact_rhs(at, jnp.where(row <= col, 1.0, 0.0))
    yield

    lane = lax.broadcasted_iota(jnp.int32, (1, LANES), 1)
    lane_lo = lane < HEAD_DIM
    rowp = lax.broadcasted_iota(jnp.int32, (LANES, 1), 0)
    pairs_per_group = n_pairs // SSD_GROUPS
    ys = []
    for p in range(n_pairs):
        g = p // pairs_per_group
        h0, h1 = 2 * p, 2 * p + 1
        bg = bm[:, g * SSD_STATE:(g + 1) * SSD_STATE]
        cg = cm[:, g * SSD_STATE:(g + 1) * SSD_STATE]
        cb = _dot_nt(cg, bg)
        xs_p = xs[:, p * LANES:(p + 1) * LANES]
        dt_p = jnp.where(lane_lo, dt[:, h0:h0 + 1], dt[:, h1:h1 + 1])
        acum_p = jnp.where(lane_lo, a_cum[:, h0:h0 + 1], a_cum[:, h1:h1 + 1])
        xdt = xs_p * dt_p
        ms = []
        for h in (h0, h1):
            seg = a_cum[:, h:h + 1] - a_cumt[h:h + 1, :]
            ms.append(cb * jnp.exp(jnp.where(causal, seg, NEG_BIG)))
        m2 = jnp.concatenate(ms, axis=1).astype(BF16)
        x2 = jnp.concatenate([jnp.where(lane_lo, xdt, 0.0), jnp.where(lane_lo, 0.0, xdt)], axis=0).astype(BF16)
        a_last = jnp.where(lane_lo, a_cum[q - 1:q, h0:h0 + 1], a_cum[q - 1:q, h1:h1 + 1])
        xdw = (xdt * jnp.exp(a_last - acum_p)).astype(BF16)
        yield
        y_diag = _dot(m2, x2)
        prev = state_ref[p]
        y_off = _dot_nt(cg, prev) * jnp.exp(acum_p)
        st = _dot_tn(xdw, bg)
        cd = jnp.where(rowp < HEAD_DIM, jnp.exp(a_cumt[h0:h0 + 1, q - 1:q]),
                       jnp.exp(a_cumt[h1:h1 + 1, q - 1:q]))
        state_ref[p] = prev * cd + st
        d_p = dskip_ref[:, p * LANES:(p + 1) * LANES]
        ys.append(y_diag + y_off + d_p * xs_p)
        yield
    y = jnp.concatenate(ys, axis=1)
    z = z_ref[0]
    y = y * (z * _sigmoid(z))
    gw = width // SSD_GROUPS
    outs = []
    for g in range(SSD_GROUPS):
        yg = y[:, g * gw:(g + 1) * gw]
        outs.append(yg * lax.rsqrt(jnp.mean(yg * yg, axis=-1, keepdims=True) + RMS_EPS))
    y_ref[0] = (jnp.concatenate(outs, axis=1) * ng_ref[...]).astype(y_ref.dtype)


def _ssd_operands(z, xbc, dt_pad, conv_w, conv_b, dt_bias, a_log, d_skip, norm_g):
    bsz, l, width = z.shape
    heads = width // HEAD_DIM
    xw = xbc.shape[-1]
    q = SSD_CHUNK
    dtt = jnp.swapaxes(dt_pad[:, :, :heads], 1, 2)
    pad = LANES - heads
    dtb = jnp.pad(dt_bias, (0, pad)).reshape(1, LANES)
    a_neg = -jnp.exp(a_log.astype(F32))
    aneg = jnp.pad(a_neg, (0, pad)).reshape(1, LANES)
    dskip = jnp.repeat(d_skip, HEAD_DIM).reshape(1, width)
    row = lambda n: pl.BlockSpec((1, n), lambda bi, c: (0, 0))
    specs = [pl.BlockSpec((1, q, width), lambda bi, c: (bi, c, 0)),
             pl.BlockSpec((1, q, xw), lambda bi, c: (bi, c, 0)),
             pl.BlockSpec((1, SUBLANES, xw), lambda bi, c: (bi, jnp.maximum(c * (q // SUBLANES) - 1, 0), 0)),
             pl.BlockSpec((1, q, LANES), lambda bi, c: (bi, c, 0)),
             pl.BlockSpec((1, heads, q), lambda bi, c: (bi, 0, c)),
             pl.BlockSpec((SSD_CONV, xw), lambda bi, c: (0, 0)),
             row(xw), row(LANES),
             pl.BlockSpec((heads, 1), lambda bi, c: (0, 0)),
             row(LANES),
             pl.BlockSpec((heads, 1), lambda bi, c: (0, 0)),
             row(width), row(width)]
    args = [z, xbc, xbc, dt_pad, dtt, conv_w, conv_b.reshape(1, xw), dtb, dt_bias.reshape(heads, 1),
            aneg, a_neg.reshape(heads, 1), dskip, norm_g.reshape(1, width)]
    scratch = [pltpu.VMEM((width // LANES, LANES, SSD_STATE), F32), pltpu.VMEM((q + SUBLANES, xw), F32)]
    return args, specs, scratch


def _rwkv_prep_kernel(n_extra, x_ref, xp_ref, win_ref, mu_ref, w0_ref, a0_ref, kk_ref, ka_ref, w2_ref, a2_ref,
                      g2_ref, e_ref, et_ref, *refs):
    extra_w = refs[:n_extra]
    r_ref, lw_ref, k_ref, v_ref, an_ref, bn_ref, g_ref = refs[n_extra:n_extra + 7]
    extra_o = refs[n_extra + 7:2 * n_extra + 7]
    ext_ref = refs[-1]
    tm = x_ref.shape[1]
    width = r_ref.shape[-1]
    i = pl.program_id(1)
    xe = jnp.concatenate([jnp.where(i > 0, xp_ref[0], 0.0), x_ref[0]], axis=0).astype(BF16)
    win = ext_ref.shape[1]
    lora = RWKV_DECAY_LORA + RWKV_ICLR_LORA

    def project(c0, c1):
        ext_ref[:, c0:c1] = jnp.dot(xe, win_ref[:, c0:c1], preferred_element_type=F32)

    sub = min(tm, RWKV_PREP_ROWS)
    rows = slice(0, sub)

    def shifted_mix(c0, c1):
        rw = ext_ref[SUBLANES + rows.start:SUBLANES + rows.stop, c0:c1]
        prev = ext_ref[SUBLANES - 1 + rows.start:SUBLANES - 1 + rows.stop, c0:c1]
        return rw + (prev - rw) * mu_ref[:, c0:c1]

    def projections():
        half = width // 2
        extras = [(w_ref, o_ref, sl) for w_ref, o_ref in zip(extra_w, extra_o)
                  for sl in _col_chunks(w_ref.shape[1], 256)]
        own = [(3 * width, win), (width, width + half), (width + half, 2 * width), (0, half), (half, width),
               (2 * width, 2 * width + half), (2 * width + half, 3 * width)]
        for n, cols in enumerate(own):
            project(*cols)
            yield
            if n >= 2 and extras:
                w_ref, o_ref, sl = extras.pop(0)
                o_ref[0, :, sl] = jnp.dot(xe[SUBLANES:], w_ref[:, sl], preferred_element_type=F32)
                yield
        for w_ref, o_ref, sl in extras:
            o_ref[0, :, sl] = jnp.dot(xe[SUBLANES:], w_ref[:, sl], preferred_element_type=F32)
            yield

    pieces = projections()

    def mxu(n=1):
        for _ in range(n):
            next(pieces, None)

    for r0 in range(0, tm, sub):
        rows = slice(r0, r0 + sub)
        mxu(3)
        s_lo = shifted_mix(3 * width, win)
        lo = s_lo[:, :lora]
        wv = w0_ref[...] + _dot(jnp.tanh(lo), w2_ref[...])
        av = _sigmoid(a0_ref[...] + _dot(lo, a2_ref[...]))
        g_ref[0, rows, :] = _dot(_sigmoid(s_lo[:, lora:]), g2_ref[...]).astype(g_ref.dtype)
        mxu(1)
        for sl in _col_chunks(width, 512):
            lw_ref[0, rows, sl] = -jnp.exp(-_softplus(-wv[:, sl]) - 0.5)
            mxu(1)
        k = shifted_mix(width, 2 * width)
        kkr = k * kk_ref[...]
        mxu(1)
        ss = _dot_exact_rhs(kkr * kkr, e_ref[...])
        k_ref[0, rows, :] = (k * (1.0 + (av - 1.0) * ka_ref[...])).astype(k_ref.dtype)
        mxu(2)
        inv = lax.rsqrt(jnp.maximum(ss, 1e-24))
        kk = kkr * _dot_exact_rhs(inv, et_ref[...])
        mxu(1)
        an_ref[0, rows, :] = (-kk).astype(an_ref.dtype)
        bn_ref[0, rows, :] = (kk * av).astype(bn_ref.dtype)
        mxu(2)
        r_ref[0, rows, :] = shifted_mix(0, width).astype(r_ref.dtype)
        mxu(1)
        v_ref[0, rows, :] = shifted_mix(2 * width, 3 * width).astype(v_ref.dtype)
    for _ in pieces:
        pass


def _rwkv_chunks(rows_list, fill, r_ref, lw_ref, k_ref, v_ref, an_ref, bn_ref, g_ref, rk_ref, lg_ref, lb_ref,
                 y_ref, state_ref):
    c = RWKV_CHUNK
    c2 = 2 * c
    width = r_ref.shape[-1]
    n_pairs = width // LANES

    row = lax.broadcasted_iota(jnp.int32, (c, c), 0)
    col = lax.broadcasted_iota(jnp.int32, (c, c), 1)
    tri = jnp.where(row >= col, 1.0, 0.0)
    roww = lax.broadcasted_iota(jnp.int32, (c, c2), 0)
    colw = lax.broadcasted_iota(jnp.int32, (c, c2), 1) % c
    strict = jnp.where(roww > colw, 1.0, 0.0)
    incl = jnp.where(roww >= colw, 1.0, 0.0)
    eye = jnp.where(roww == colw, 1.0, 0.0)
    same_head = jnp.where((lax.broadcasted_iota(jnp.int32, (LANES, LANES), 0) // HEAD_DIM)
                          == (lax.broadcasted_iota(jnp.int32, (LANES, LANES), 1) // HEAD_DIM), 1.0, 0.0)
    lane = lax.broadcasted_iota(jnp.int32, (1, LANES), 1)
    m0 = jnp.where(lane < HEAD_DIM, 1.0, 0.0)
    m1 = 1.0 - m0

    lane_lo2 = lax.broadcasted_iota(jnp.int32, (c, LANES), 1) < HEAD_DIM

    def stack(x):
        xb = x.astype(BF16)
        zero = jnp.zeros_like(xb)
        return jnp.concatenate([jnp.where(lane_lo2, xb, zero), jnp.where(lane_lo2, zero, xb)], axis=0)

    sls = [slice(p * LANES, (p + 1) * LANES) for p in range(n_pairs)]
    units = [(rows, sl) for rows in rows_list for sl in sls]
    lhs2s, rhs4s, vs, v_stks, bks, g_lasts = [], [], [], [], [], []
    for rows, sl in units:
        r = r_ref[0, rows, sl].astype(F32)
        lw = lw_ref[0, rows, sl]
        k = k_ref[0, rows, sl].astype(F32)
        b = bn_ref[0, rows, sl].astype(F32)
        v = v_ref[0, rows, sl].astype(F32)
        cum = _dot_exact_lhs(tri, lw)
        cum_last = cum[c - 1:c, :]
        e_neg = jnp.exp(-cum)
        e_tail = jnp.exp(cum_last - cum)
        at = an_ref[0, rows, sl].astype(F32) * jnp.exp(cum - lw)
        rt = r * jnp.exp(cum)
        lhs2s.append(jnp.concatenate([at, rt], axis=0).astype(BF16))
        rhs4s.append(jnp.concatenate([stack(b * e_neg), stack(k * e_neg)], axis=0))
        vs.append(v.astype(BF16))
        v_stks.append(stack(v))
        bks.append(jnp.concatenate([b * e_tail, k * e_tail], axis=0).astype(BF16))
        g_lasts.append(jnp.exp(cum_last))
    gms = [_dot_nt(l, rh) for l, rh in zip(lhs2s, rhs4s)]
    pws = [gm[:c, :c2] * strict for gm in gms]
    a_aks = [(gm[:c, c2:] * strict).astype(BF16) for gm in gms]
    a_rbks = [jnp.concatenate([gm[c:, :c2] * incl, gm[c:, c2:] * incl], axis=1).astype(BF16) for gm in gms]
    akvs = [_dot(a_ak, v_stk) for a_ak, v_stk in zip(a_aks, v_stks)]
    fill()
    tinvs = [eye + pw for pw in pws]
    pw_stks = [stack(pw) for pw in pws]
    for _ in range(int(math.log2(c)) - 1):
        pws = [_dot(pw, pw_stk) for pw, pw_stk in zip(pws, pw_stks)]
        pw_stks = [stack(pw) for pw in pws]
        tinvs = [tinv + _dot(tinv, pw_stk) for tinv, pw_stk in zip(tinvs, pw_stks)]
        fill()
    lane_lo = lane < HEAD_DIM

    def head_sum(x):
        lo = jnp.sum(x * m0, axis=-1, keepdims=True)
        hi = jnp.sum(x * m1, axis=-1, keepdims=True)
        return jnp.where(lane_lo, lo, hi)

    for si, rows in enumerate(rows_list):
        u0 = si * n_pairs
        sts = [state_ref[p] for p in range(n_pairs)]
        ahrhs = [_dot_nt(lhs2s[u0 + p], sts[p]) for p in range(n_pairs)]
        fill()
        us = [_dot(tinvs[u0 + p], stack(ahrhs[p][:c] + akvs[u0 + p])) for p in range(n_pairs)]
        fill()
        for p in range(n_pairs):
            uv = jnp.concatenate([us[p].astype(BF16), vs[u0 + p]], axis=0)
            state_ref[p] = sts[p] * g_lasts[u0 + p] + same_head * _dot_tn(uv, bks[u0 + p])
        fill()
        ys = [ahrhs[p][c:] + _dot(a_rbks[u0 + p], jnp.concatenate([stack(us[p]), v_stks[u0 + p]], axis=0))
              for p in range(n_pairs)]
        bonus = [head_sum(r_ref[0, rows, sl].astype(F32) * k_ref[0, rows, sl].astype(F32) * rk_ref[:, sl])
                 for sl in sls]
        ycs = [y - head_sum(y) * (1.0 / HEAD_DIM) for y in ys]
        yvs = [head_sum(yc * yc) * (1.0 / HEAD_DIM) for yc in ycs]
        for p, sl in enumerate(sls):
            yn = ycs[p] * lax.rsqrt(yvs[p] + RWKV_LNX_EPS) * lg_ref[:, sl] + lb_ref[:, sl]
            out = (yn + bonus[p] * v_ref[0, rows, sl].astype(F32)) * g_ref[0, rows, sl].astype(F32)
            y_ref[0, rows, sl] = out.astype(y_ref.dtype)


def _rwkv_operands(x, w_in, extra_ws, mu, w0, w2, a0, a2, g2, k_k, k_a, r_k, lnx_g, lnx_b, *, tm):
    bsz, l, d = x.shape
    n_extra = len(extra_ws)
    win = w_in.shape[1]
    width = w0.shape[0]
    w2p = jnp.concatenate([w2, jnp.zeros((RWKV_ICLR_LORA, width), F32)], axis=0).astype(BF16)
    a2p = jnp.concatenate([jnp.zeros((RWKV_DECAY_LORA, width), F32), a2], axis=0).astype(BF16)
    head_of = jnp.arange(width) // HEAD_DIM
    e = (head_of[:, None] == jnp.arange(LANES)[None, :]).astype(BF16)
    et = e.T
    vec = lambda x: x.reshape(1, -1)
    row = lambda n: pl.BlockSpec((1, n), lambda bi, i: (0, 0))
    full = lambda a: pl.BlockSpec(a.shape, lambda bi, i: (0, 0))
    single = lambda a: pl.BlockSpec(a.shape, lambda bi, i: (0, 0), pipeline_mode=pl.Buffered(1))
    tile = lambda w: pl.BlockSpec((1, tm, w), lambda bi, i: (bi, i, 0))
    sds = lambda w, dt=F32: jax.ShapeDtypeStruct((bsz, l, w), dt)
    g2b = g2.astype(BF16)
    outs = pl.pallas_call(
        functools.partial(_rwkv_prep_kernel, n_extra),
        grid=(bsz, l // tm),
        in_specs=[pl.BlockSpec((1, tm, d), lambda bi, i: (bi, i, 0)),
                  pl.BlockSpec((1, SUBLANES, d), lambda bi, i: (bi, jnp.maximum(i * (tm // SUBLANES) - 1, 0), 0)),
                  single(w_in), row(win), row(width), row(width), row(width), row(width),
                  full(w2p), full(a2p), full(g2b), full(e), full(et)] + [single(w) for w in extra_ws],
        out_specs=[tile(width)] * 7 + [tile(w.shape[1]) for w in extra_ws],
        out_shape=([sds(width, BF16), sds(width)] + [sds(width, BF16)] * 5
                   + [sds(w.shape[1]) for w in extra_ws]),
        scratch_shapes=[pltpu.VMEM((tm + SUBLANES, win), F32)],
        compiler_params=_params("parallel", "parallel"),
        name="rwkv_prep",
    )(x, x, w_in, vec(mu), vec(w0), vec(a0), vec(k_k), vec(k_a), w2p, a2p, g2b, e, et, *extra_ws)
    r, lw, k, v, an, bn, g = outs[:7]
    ctile = pl.BlockSpec((1, SSD_CHUNK, width), lambda bi, ci: (bi, ci, 0))
    crow = pl.BlockSpec((1, width), lambda bi, ci: (0, 0))
    args = [r, lw, k, v, an, bn, g, vec(r_k), vec(lnx_g), vec(lnx_b)]
    scratch = [pltpu.VMEM((width // LANES, LANES, LANES), F32)]
    return outs[7:], (args, [ctile] * 7 + [crow] * 3, scratch)


def _mixers_kernel(n_ssd, n_rwkv, *refs):
    ssd_in = refs[:n_ssd]
    rwkv_in = refs[n_ssd:n_ssd + n_rwkv]
    y_ssd_ref, y_rwkv_ref, ssd_state, ssd_ext, rwkv_state = refs[n_ssd + n_rwkv:]
    c = pl.program_id(1)

    @pl.when(c == 0)
    def _():
        ssd_state[...] = jnp.zeros_like(ssd_state)
        rwkv_state[...] = jnp.zeros_like(rwkv_state)

    ssd_steps = _ssd_chunk(c, *ssd_in, y_ssd_ref, ssd_state, ssd_ext)
    fill = lambda: (next(ssd_steps, None), next(ssd_steps, None))
    rows_list = [slice(sub * RWKV_CHUNK, (sub + 1) * RWKV_CHUNK) for sub in range(SSD_CHUNK // RWKV_CHUNK)]
    _rwkv_chunks(rows_list, fill, *rwkv_in, y_rwkv_ref, rwkv_state)
    for _ in ssd_steps:
        pass


def _ssd_rwkv_mixers(ssd_ops, rwkv_ops, bsz, l, ssd_width, rwkv_width):
    ssd_args, ssd_specs, ssd_scratch = ssd_ops
    rwkv_args, rwkv_specs, rwkv_scratch = rwkv_ops
    q = SSD_CHUNK
    out_spec = lambda w: pl.BlockSpec((1, q, w), lambda bi, c: (bi, c, 0))
    return pl.pallas_call(
        functools.partial(_mixers_kernel, len(ssd_args), len(rwkv_args)),
        grid=(bsz, l // q),
        in_specs=ssd_specs + rwkv_specs,
        out_specs=[out_spec(ssd_width), out_spec(rwkv_width)],
        out_shape=[jax.ShapeDtypeStruct((bsz, l, ssd_width), BF16), jax.ShapeDtypeStruct((bsz, l, rwkv_width), BF16)],
        scratch_shapes=ssd_scratch + rwkv_scratch,
        compiler_params=_params("parallel", "arbitrary"),
        name="ssd_rwkv_mixers",
    )(*ssd_args, *rwkv_args)


def _moba_kernel(q_ref, qn_ref, k_ref, v_ref, o_ref, kb_ref, vt_ref, kmean_ref, qo_ref, qp_ref):
    blk = MOBA_BLOCK
    half = HEAD_DIM
    nb = k_ref.shape[0] // blk
    nbp = kmean_ref.shape[0]
    npair = q_ref.shape[1] // LANES
    heads = range(2 * npair)
    group = math.gcd(nb, MOBA_GROUP)
    qi = pl.program_id(2)

    @pl.when(qi == 0)
    def _():
        lane = lax.broadcasted_iota(jnp.int32, (blk, LANES), 1)
        rowp = lax.broadcasted_iota(jnp.int32, (LANES, blk), 0)
        if nbp > nb:
            kmean_ref[...] = jnp.zeros_like(kmean_ref)
        for n in range(nb):
            rows = slice(n * blk, (n + 1) * blk)
            kmean_ref[n:n + 1, :] = jnp.mean(k_ref[rows, :], axis=0, keepdims=True)
            for pp in range(npair):
                cols = slice(pp * LANES, (pp + 1) * LANES)
                kn = k_ref[rows, cols]
                kb_ref[2 * pp, n] = jnp.where(lane < half, kn, jnp.where(lane == half + n, 1.0, 0.0)).astype(BF16)
                kb_ref[2 * pp + 1, n] = jnp.where(lane >= half, kn, jnp.where(lane == n, 1.0, 0.0)).astype(BF16)
                vtn = v_ref[rows, cols].T
                vt_ref[2 * pp, n] = jnp.where(rowp < half, vtn, jnp.where(rowp == half, 1.0, 0.0)).astype(BF16)
                vt_ref[2 * pp + 1, n] = jnp.where(rowp >= half, vtn, jnp.where(rowp == 0, 1.0, 0.0)).astype(BF16)

    rown = lax.broadcasted_iota(jnp.int32, (nbp, blk), 0)
    lane_k = lax.broadcasted_iota(jnp.int32, (nbp, LANES), 1)
    zeros = jnp.zeros((half, blk), F32)

    def prepare_queries(src_ref, tile):
        qts = [src_ref[:, pp * LANES:(pp + 1) * LANES].T for pp in range(npair)]
        gates, qhs = [], []
        for hh in heads:
            pp, h = divmod(hh, 2)
            km = kmean_ref[:, pp * LANES:(pp + 1) * LANES]
            kmh = jnp.where((lane_k < half) if h == 0 else (lane_k >= half), km, 0.0)
            gates.append(jnp.dot(kmh, qts[pp], preferred_element_type=F32, precision=lax.Precision.HIGHEST))
            qhs.append(qts[pp][h * half:(h + 1) * half] * (half ** -0.5 * LOG2_E))
        for hh in heads:
            gate = jnp.where(rown < tile, gates[hh], -jnp.inf)
            bias = jnp.full((nbp, blk), NEG_BIG, F32)
            for _ in range(MOBA_TOPK):
                mx = jnp.max(gate, axis=0, keepdims=True)
                first = jnp.min(jnp.where(gate == mx, rown, nbp), axis=0, keepdims=True)
                pick = (rown == first) & (mx > -jnp.inf)
                bias = jnp.where(pick, 0.0, bias)
                gate = jnp.where(pick, -jnp.inf, gate)
            aug = jnp.concatenate([bias, jnp.zeros((half - nbp, blk), F32)], axis=0)
            qo_ref[hh] = jnp.concatenate([qhs[hh], zeros] if hh % 2 == 0 else [zeros, qhs[hh]],
                                         axis=0).astype(BF16)
            qp_ref[hh] = jnp.concatenate([qhs[hh], aug] if hh % 2 == 0 else [aug, qhs[hh]], axis=0).astype(BF16)

    @pl.when(qi == 0)
    def _():
        prepare_queries(q_ref, 0)

    q_own = [qo_ref[h] for h in heads]
    q_past = [qp_ref[h] for h in heads]
    s_own = [jnp.dot(kb_ref[h, qi], q_own[h], preferred_element_type=F32) for h in heads]
    prepare_queries(qn_ref, qi + 1)

    causal = (lax.broadcasted_iota(jnp.int32, (blk, blk), 0) <= lax.broadcasted_iota(jnp.int32, (blk, blk), 1))
    ms, ps = [], []
    for h in heads:
        s = jnp.where(causal, s_own[h], NEG_BIG)
        ms.append(jnp.max(s, axis=0, keepdims=True))
        ps.append(jnp.exp2(s - ms[h]).astype(BF16))
    carry = []
    for h in heads:
        carry += [ms[h], jnp.dot(vt_ref[h, qi], ps[h], preferred_element_type=F32)]

    def scores(gi, h):
        return [jnp.dot(kb_ref[h, gi * group + g], q_past[h], preferred_element_type=F32) for g in range(group)]

    def values(gi, h):
        return jnp.concatenate([vt_ref[h, gi * group + g] for g in range(group)], axis=1)

    def body_lagged(gi, carry):
        excess = carry[-1]
        out = []
        sss = [scores(gi, h) for h in heads]
        pcats, gmaxs = [], []
        for h in heads:
            m_run = carry[2 * h]
            gmax = None
            ps = []
            for s in sss[h]:
                cm = jnp.max(s, axis=0, keepdims=True)
                gmax = cm if gmax is None else jnp.maximum(gmax, cm)
                ps.append(jnp.exp2(s - m_run).astype(BF16))
            pcats.append(jnp.concatenate(ps, axis=0))
            gmaxs.append(gmax)
        for h in heads:
            m_run, acc = carry[2 * h], carry[2 * h + 1]
            m_new = jnp.maximum(m_run, gmaxs[h])
            excess = jnp.maximum(excess, gmaxs[h] - m_run)
            acc = jnp.exp2(m_run - m_new) * (acc + jnp.dot(values(gi, h), pcats[h], preferred_element_type=F32))
            out += [m_new, acc]
        return tuple(out) + (excess,)

    def body_exact_max(gi, carry):
        sss = [scores(gi, h) for h in heads]
        m_news = []
        for h in heads:
            m_new = carry[2 * h]
            for s in sss[h]:
                m_new = jnp.maximum(m_new, jnp.max(s, axis=0, keepdims=True))
            m_news.append(m_new)
        pcats = [jnp.concatenate([jnp.exp2(s - m_news[h]).astype(BF16) for s in sss[h]], axis=0)
                 for h in heads]
        out = []
        for h in heads:
            alpha = jnp.exp2(carry[2 * h] - m_news[h])
            out += [m_news[h], alpha * carry[2 * h + 1]
                    + jnp.dot(values(gi, h), pcats[h], preferred_element_type=F32)]
        return tuple(out)

    rowq = lax.broadcasted_iota(jnp.int32, (LANES, blk), 0)

    def write_out(final):
        for pp in range(npair):
            acc0, acc1 = final[4 * pp + 1], final[4 * pp + 3]
            out_t = jnp.where(rowq < half, acc0 / acc0[half:half + 1], acc1 / acc1[0:1])
            o_ref[:, pp * LANES:(pp + 1) * LANES] = out_t.T.astype(o_ref.dtype)

    n_groups = (qi + group - 1) // group
    final = lax.fori_loop(0, n_groups, body_lagged, tuple(carry) + (jnp.full((1, blk), NEG_BIG, F32),))
    write_out(final)

    @pl.when(jnp.max(final[-1]) > MOBA_LAG_LIMIT)
    def _():
        write_out(lax.fori_loop(0, n_groups, body_exact_max, tuple(carry)))


def _moba_attention(qkv, bsz, s, heads):
    blk = MOBA_BLOCK
    assert s % blk == 0 and (heads * HEAD_DIM) % LANES == 0
    nb = s // blk
    assert nb <= HEAD_DIM
    nbp = -(-nb // SUBLANES) * SUBLANES
    pairs = heads * HEAD_DIM // LANES
    pps = math.gcd(pairs, MOBA_PAIRS_PER_STEP)
    cw = pps * LANES
    steps = pairs // pps
    return pl.pallas_call(
        _moba_kernel,
        grid=(bsz, steps, nb),
        in_specs=[pl.BlockSpec((blk, cw), lambda b, p, i: (b * nb + i, p)),
                  pl.BlockSpec((blk, cw), lambda b, p, i: (b * nb + jnp.minimum(i + 1, nb - 1), p)),
                  pl.BlockSpec((s, cw), lambda b, p, i: (b, steps + p)),
                  pl.BlockSpec((s, cw), lambda b, p, i: (b, 2 * steps + p))],
        out_specs=pl.BlockSpec((blk, cw), lambda b, p, i: (b * nb + i, p)),
        out_shape=jax.ShapeDtypeStruct((bsz * s, heads * HEAD_DIM), BF16),
        scratch_shapes=[pltpu.VMEM((2 * pps, nb, blk, LANES), BF16), pltpu.VMEM((2 * pps, nb, LANES, blk), BF16),
                        pltpu.VMEM((nbp, cw), F32),
                        pltpu.VMEM((2 * pps, LANES, blk), BF16), pltpu.VMEM((2 * pps, LANES, blk), BF16)],
        compiler_params=_params("parallel", "parallel", "arbitrary"),
        name="moba_attention",
    )(qkv, qkv, qkv, qkv)


def _row_tile(m):
    for t in (512, 256, 128, 64, 32, 16, 8):
        if m % t == 0:
            return t
    raise ValueError(f"row count {m} is not a multiple of 8")


def _col_tile(n, cap=2048):
    best = None
    for t in range(LANES, min(n, cap) + 1, LANES):
        if n % t == 0:
            best = t
    if best is None:
        raise ValueError(f"column count {n} is not a multiple of {LANES}")
    return best


def kernel(x, mem, even_w_in, ssd_conv_w, ssd_conv_b, ssd_dt_bias, ssd_a_log, ssd_d, ssd_norm_g, rwkv_mu, rwkv_w0, rwkv_w2, rwkv_a0, rwkv_a2, rwkv_g2, rwkv_k_k, rwkv_k_a, rwkv_r_k, rwkv_lnx_g, rwkv_lnx_b, even_w_out, odd_w_qkv, odd_w_out, ln_mix_g, ln_mix_b, xa_wq, xa_wkv, xa_wo, ln_xa_g, ln_xa_b, ffn_w13, ffn_w2, ln_ffn_g, ln_ffn_b):
    bsz, s, d = x.shape
    m = bsz * s
    tm = _row_tile(s)
    ssd_width = ssd_norm_g.shape[-1]
    ssd_heads = ssd_dt_bias.shape[-1]
    ssd_xbc = ssd_conv_b.shape[-1]
    ssd_in = ssd_width + ssd_xbc + ssd_heads
    rwkv_width = rwkv_w0.shape[-1]
    mem2 = mem.reshape(bsz * mem.shape[1], d)
    x2 = x.reshape(m, d)
    for layer in range(DEPTH):
        j = layer // 2
        if layer % 2 == 0:
            w_in = even_w_in[j].astype(BF16)
            w_z = w_in[:, :ssd_width]
            w_xbc = w_in[:, ssd_width:ssd_width + ssd_xbc]
            w_dt = jnp.pad(w_in[:, ssd_width + ssd_xbc:ssd_in], ((0, 0), (0, LANES - ssd_heads)))
            w_rw = w_in[:, ssd_in:]
            (z, xbc, dt_pad), rwkv_ops = _rwkv_operands(
                x2.reshape(bsz, s, d), w_rw, [w_z, w_xbc, w_dt], rwkv_mu[j], rwkv_w0[j], rwkv_w2[j],
                rwkv_a0[j], rwkv_a2[j], rwkv_g2[j], rwkv_k_k[j], rwkv_k_a[j], rwkv_r_k[j],
                rwkv_lnx_g[j], rwkv_lnx_b[j], tm=tm)
            ssd_ops = _ssd_operands(z, xbc, dt_pad, ssd_conv_w[j], ssd_conv_b[j], ssd_dt_bias[j],
                                    ssd_a_log[j], ssd_d[j], ssd_norm_g[j])
            y_ssd, y_rwkv = _ssd_rwkv_mixers(ssd_ops, rwkv_ops, bsz, s, ssd_width, rwkv_width)
            x2 = _matmul_residual_ln([y_ssd.reshape(m, -1), y_rwkv.reshape(m, -1)],
                                     even_w_out[j].astype(BF16), x2,
                                     ln_mix_g[layer], ln_mix_b[layer], tm=tm)
        else:
            heads = d // HEAD_DIM
            qkv, = _matmul(x2, [odd_w_qkv[j].astype(BF16)], tm=tm)
            attn = _moba_attention(qkv, bsz, s, heads)
            x2 = _matmul_residual_ln([attn], odd_w_out[j].astype(BF16), x2,
                                     ln_mix_g[layer], ln_mix_b[layer], tm=tm)
        kv, = _matmul(mem2, [xa_wkv[layer].astype(BF16)], tm=_row_tile(mem2.shape[0]))
        x3 = _cross_attention_ln(x2.reshape(bsz, s, d), kv.reshape(bsz, -1, 2 * d),
                                 xa_wq[layer].astype(BF16), xa_wo[layer].astype(BF16),
                                 ln_xa_g[layer], ln_xa_b[layer], tm=tm)
        x2 = x3.reshape(m, d)
        h = _swiglu_up(x2, ffn_w13[layer].astype(BF16), tm=tm)
        x2 = _matmul_residual_ln([h], ffn_w2[layer].astype(BF16), x2,
                                 ln_ffn_g[layer], ln_ffn_b[layer], tm=tm)
    return x2.reshape(bsz, s, d)
```

```python
import functools
import math

import jax
import jax.numpy as jnp
from jax import lax
from jax.experimental import pallas as pl
from jax.experimental.pallas import tpu as pltpu

F32 = jnp.float32
BF16 = jnp.bfloat16

HEAD_DIM = 64
LANES = 128
SUBLANES = 8
SSD_GROUPS = 2
SSD_STATE = 128
SSD_CONV = 4
SSD_CHUNK = 128
RWKV_DECAY_LORA = 64
RWKV_ICLR_LORA = 64
RWKV_GATE_LORA = 128
RWKV_CHUNK = 64
RWKV_PREP_ROWS = 256
MOBA_BLOCK = 256
MOBA_TOPK = 3
MOBA_GROUP = 4
MOBA_LAG_LIMIT = 64.0
MOBA_PAIRS_PER_STEP = 2
XATTN_HEADS = 4
SWIGLU_COLS = 256
DEPTH = 2
DEEPNORM_ALPHA = (2 * DEPTH) ** 0.25
LN_EPS = 1e-5
RMS_EPS = 1e-5
RWKV_LNX_EPS = 64e-5
NEG_BIG = -1e30
LOG2_E = math.log2(math.e)
VMEM_LIMIT = 60 * 1024 * 1024


def _params(*sem):
    return pltpu.CompilerParams(dimension_semantics=sem, vmem_limit_bytes=VMEM_LIMIT)


def _dot(a, b):
    return jnp.dot(a.astype(BF16), b.astype(BF16), preferred_element_type=F32)


def _dot_nt(a, b):
    return lax.dot_general(a.astype(BF16), b.astype(BF16), (((1,), (1,)), ((), ())),
                           preferred_element_type=F32)


def _dot_tn(a, b):
    return lax.dot_general(a.astype(BF16), b.astype(BF16), (((0,), (0,)), ((), ())),
                           preferred_element_type=F32)


def _split3(x):
    hi = x.astype(BF16)
    r1 = x - hi.astype(F32)
    mid = r1.astype(BF16)
    lo = (r1 - mid.astype(F32)).astype(BF16)
    return hi, mid, lo


def _dot_exact_lhs(m, x):
    hi, mid, lo = _split3(x)
    m = m.astype(BF16)
    return (jnp.dot(m, hi, preferred_element_type=F32) + jnp.dot(m, mid, preferred_element_type=F32)
            + jnp.dot(m, lo, preferred_element_type=F32))


def _dot_exact_rhs(x, m):
    hi, mid, lo = _split3(x)
    m = m.astype(BF16)
    return (jnp.dot(hi, m, preferred_element_type=F32) + jnp.dot(mid, m, preferred_element_type=F32)
            + jnp.dot(lo, m, preferred_element_type=F32))


def _sigmoid(x):
    return 1.0 / (1.0 + jnp.exp(-x))


def _softplus(x):
    return jnp.maximum(x, 0.0) + jnp.log1p(jnp.exp(-jnp.abs(x)))


def _layer_norm(v, g, b):
    mu = jnp.mean(v, axis=-1, keepdims=True)
    c = v - mu
    var = jnp.mean(c * c, axis=-1, keepdims=True)
    return c * lax.rsqrt(var + LN_EPS) * g + b


def _col_chunks(n, cap=1536):
    width = _col_tile(n, cap)
    return [slice(j, j + width) for j in range(0, n, width)]


def _mm_kernel(n_out, x_ref, *refs):
    x = x_ref[...].astype(BF16)
    for w_ref, o_ref in zip(refs[:n_out], refs[n_out:]):
        for sl in _col_chunks(w_ref.shape[1]):
            o_ref[:, sl] = jnp.dot(x, w_ref[:, sl], preferred_element_type=F32).astype(o_ref.dtype)


def _matmul(x, ws, *, tm, out_dtype=F32):
    m, k = x.shape
    assert m % tm == 0
    return pl.pallas_call(
        functools.partial(_mm_kernel, len(ws)),
        grid=(m // tm,),
        in_specs=[pl.BlockSpec((tm, k), lambda i: (i, 0))] + [pl.BlockSpec(w.shape, lambda i: (0, 0)) for w in ws],
        out_specs=[pl.BlockSpec((tm, w.shape[1]), lambda i: (i, 0)) for w in ws],
        out_shape=[jax.ShapeDtypeStruct((m, w.shape[1]), out_dtype) for w in ws],
        compiler_params=_params("parallel"),
        name="matmul",
    )(x, *ws)


def _mm_res_ln_kernel(n_in, *refs):
    hs = refs[:n_in]
    ws = refs[n_in:2 * n_in]
    res_ref, g_ref, b_ref, o_ref = refs[2 * n_in:]
    tm = o_ref.shape[0]
    groups = [slice(0, tm // 2), slice(tm // 2, tm)]
    accs = []
    for rows in groups:
        acc = _dot(hs[0][rows, :], ws[0][...])
        for h_ref, w_ref in zip(hs[1:], ws[1:]):
            acc = acc + _dot(h_ref[rows, :], w_ref[...])
        accs.append(acc)
    for rows, acc in zip(groups, accs):
        o_ref[rows, :] = _layer_norm(DEEPNORM_ALPHA * res_ref[rows, :] + acc, g_ref[...], b_ref[...])


def _matmul_residual_ln(hs, w, res, g, b, *, tm):
    m, d = res.shape
    n_in = len(hs)
    kw = w.shape[0] // n_in
    assert all(h.shape[1] == kw for h in hs)
    in_specs = ([pl.BlockSpec((tm, kw), lambda i: (i, 0)) for _ in hs]
                + [pl.BlockSpec((kw, d), lambda i, j=j: (j, 0)) for j in range(n_in)]
                + [pl.BlockSpec((tm, d), lambda i: (i, 0)),
                   pl.BlockSpec((1, d), lambda i: (0, 0)),
                   pl.BlockSpec((1, d), lambda i: (0, 0))])
    return pl.pallas_call(
        functools.partial(_mm_res_ln_kernel, n_in),
        grid=(m // tm,),
        in_specs=in_specs,
        out_specs=pl.BlockSpec((tm, d), lambda i: (i, 0)),
        out_shape=jax.ShapeDtypeStruct((m, d), F32),
        compiler_params=_params("parallel"),
        name="matmul_residual_ln",
    )(*hs, *([w] * n_in), res, g.reshape(1, d), b.reshape(1, d))


def _swiglu_kernel(x_ref, w13_ref, o_ref):
    x = x_ref[...].astype(BF16)
    n = o_ref.shape[1]
    pending = None
    for sl in _col_chunks(n, SWIGLU_COLS) + [None]:
        if sl is not None:
            gate = jnp.dot(x, w13_ref[:, sl], preferred_element_type=F32)
            up = jnp.dot(x, w13_ref[:, slice(n + sl.start, n + sl.stop)], preferred_element_type=F32)
        if pending is not None:
            psl, pgate, pup = pending
            o_ref[:, psl] = (pgate * _sigmoid(pgate) * pup).astype(o_ref.dtype)
        pending = (sl, gate, up) if sl is not None else None


def _swiglu_up(x, w13, *, tm):
    m, k = x.shape
    n = w13.shape[1] // 2
    return pl.pallas_call(
        _swiglu_kernel,
        grid=(m // tm,),
        in_specs=[pl.BlockSpec((tm, k), lambda i: (i, 0)),
                  pl.BlockSpec((k, 2 * n), lambda i: (0, 0))],
        out_specs=pl.BlockSpec((tm, n), lambda i: (i, 0)),
        out_shape=jax.ShapeDtypeStruct((m, n), BF16),
        compiler_params=_params("parallel"),
        name="swiglu_up",
    )(x, w13)


def _xattn_kernel(x_ref, mem_ref, wkv_ref, wq_ref, wo_ref, g_ref, b_ref, o_ref, kv_ref):
    x = x_ref[0]
    d = x.shape[-1]
    hd = d // XATTN_HEADS

    @pl.when(pl.program_id(1) == 0)
    def _():
        mem = mem_ref[0].astype(BF16)
        for sl in _col_chunks(kv_ref.shape[1]):
            kv_ref[:, sl] = jnp.dot(mem, wkv_ref[:, sl], preferred_element_type=F32)

    q = _dot(x, wq_ref[...])
    kv = kv_ref[...]
    heads = range(XATTN_HEADS)
    ss = [_dot_nt(q[:, h * hd:(h + 1) * hd], kv[:, h * hd:(h + 1) * hd]) * (hd ** -0.5) for h in heads]
    ps = []
    for s in ss:
        p = jnp.exp(s - jnp.max(s, axis=-1, keepdims=True))
        ps.append(p / jnp.sum(p, axis=-1, keepdims=True))
    o = jnp.concatenate([_dot(ps[h], kv[:, d + h * hd:d + (h + 1) * hd]) for h in heads], axis=-1)
    xa = _dot(o, wo_ref[...])
    o_ref[0] = _layer_norm(DEEPNORM_ALPHA * x + xa, g_ref[...], b_ref[...])


def _cross_attention_ln(x, mem, wkv, wq, wo, g, b, *, tm):
    bsz, s, d = x.shape
    m = mem.shape[1]
    return pl.pallas_call(
        _xattn_kernel,
        grid=(bsz, s // tm),
        in_specs=[pl.BlockSpec((1, tm, d), lambda bi, i: (bi, i, 0)),
                  pl.BlockSpec((1, m, d), lambda bi, i: (bi, 0, 0)),
                  pl.BlockSpec((d, 2 * d), lambda bi, i: (0, 0)),
                  pl.BlockSpec((d, d), lambda bi, i: (0, 0)),
                  pl.BlockSpec((d, d), lambda bi, i: (0, 0)),
                  pl.BlockSpec((1, d), lambda bi, i: (0, 0)),
                  pl.BlockSpec((1, d), lambda bi, i: (0, 0))],
        out_specs=pl.BlockSpec((1, tm, d), lambda bi, i: (bi, i, 0)),
        out_shape=jax.ShapeDtypeStruct((bsz, s, d), F32),
        scratch_shapes=[pltpu.VMEM((m, 2 * d), F32)],
        compiler_params=_params("parallel", "arbitrary"),
        name="cross_attention_ln",
    )(x, mem, wkv, wq, wo, g.reshape(1, d), b.reshape(1, d))


def _ssd_chunk(c, z_ref, xbc_ref, xbcp_ref, dt_ref, dtt_ref, cw_ref, cb_ref, dtb_ref, dtbt_ref,
               aneg_ref, anegt_ref, dskip_ref, ng_ref, y_ref, state_ref, ext_ref):
    q = SSD_CHUNK
    width = z_ref.shape[-1]
    n_pairs = width // LANES

    ext_ref[0:SUBLANES, :] = jnp.where(c > 0, xbcp_ref[0], 0.0)
    ext_ref[SUBLANES:SUBLANES + q, :] = xbc_ref[0]
    xcs = []
    for sl in _col_chunks(ext_ref.shape[1], 256):
        conv = cb_ref[:, sl] + cw_ref[SSD_CONV - 1:SSD_CONV, sl] * ext_ref[SUBLANES:SUBLANES + q, sl]
        for k in range(SSD_CONV - 1):
            off = SUBLANES - (SSD_CONV - 1) + k
            conv = conv + cw_ref[k:k + 1, sl] * ext_ref[off:off + q, sl]
        xcs.append(conv * _sigmoid(conv))
        yield
    xc = jnp.concatenate(xcs, axis=1)
    xs = xc[:, :width]
    gn = SSD_GROUPS * SSD_STATE
    bm = xc[:, width:width + gn]
    cm = xc[:, width + gn:width + 2 * gn]

    dt = _softplus(dt_ref[0] + dtb_ref[...])
    a = dt * aneg_ref[...]
    dtt = _softplus(dtt_ref[0] + dtbt_ref[...])
    at = dtt * anegt_ref[...]
    row = lax.broadcasted_iota(jnp.int32, (q, q), 0)
    col = lax.broadcasted_iota(jnp.int32, (q, q), 1)
    causal = row >= col
    tri = jnp.where(causal, 1.0, 0.0)
    a_cum = _dot_exact_lhs(tri, a)
    a_cumt = _dot_exact_rhs(at, jnp.where(row <= col, 1.0, 0.0))
    yield

    lane = lax.broadcasted_iota(jnp.int32, (1, LANES), 1)
    lane_lo = lane < HEAD_DIM
    rowp = lax.broadcasted_iota(jnp.int32, (LANES, 1), 0)
    pairs_per_group = n_pairs // SSD_GROUPS
    ys = []
    for p in range(n_pairs):
        g = p // pairs_per_group
        h0, h1 = 2 * p, 2 * p + 1
        bg = bm[:, g * SSD_STATE:(g + 1) * SSD_STATE]
        cg = cm[:, g * SSD_STATE:(g + 1) * SSD_STATE]
        cb = _dot_nt(cg, bg)
        xs_p = xs[:, p * LANES:(p + 1) * LANES]
        dt_p = jnp.where(lane_lo, dt[:, h0:h0 + 1], dt[:, h1:h1 + 1])
        acum_p = jnp.where(lane_lo, a_cum[:, h0:h0 + 1], a_cum[:, h1:h1 + 1])
        xdt = xs_p * dt_p
        ms = []
        for h in (h0, h1):
            seg = a_cum[:, h:h + 1] - a_cumt[h:h + 1, :]
            ms.append(cb * jnp.exp(jnp.where(causal, seg, NEG_BIG)))
        m2 = jnp.concatenate(ms, axis=1).astype(BF16)
        x2 = jnp.concatenate([jnp.where(lane_lo, xdt, 0.0), jnp.where(lane_lo, 0.0, xdt)], axis=0).astype(BF16)
        a_last = jnp.where(lane_lo, a_cum[q - 1:q, h0:h0 + 1], a_cum[q - 1:q, h1:h1 + 1])
        xdw = (xdt * jnp.exp(a_last - acum_p)).astype(BF16)
        yield
        y_diag = _dot(m2, x2)
        prev = state_ref[p]
        y_off = _dot_nt(cg, prev) * jnp.exp(acum_p)
        st = _dot_tn(xdw, bg)
        cd = jnp.where(rowp < HEAD_DIM, jnp.exp(a_cumt[h0:h0 + 1, q - 1:q]),
                       jnp.exp(a_cumt[h1:h1 + 1, q - 1:q]))
        state_ref[p] = prev * cd + st
        d_p = dskip_ref[:, p * LANES:(p + 1) * LANES]
        ys.append(y_diag + y_off + d_p * xs_p)
        yield
    y = jnp.concatenate(ys, axis=1)
    z = z_ref[0]
    y = y * (z * _sigmoid(z))
    gw = width // SSD_GROUPS
    outs = []
    for g in range(SSD_GROUPS):
        yg = y[:, g * gw:(g + 1) * gw]
        outs.append(yg * lax.rsqrt(jnp.mean(yg * yg, axis=-1, keepdims=True) + RMS_EPS))
    y_ref[0] = (jnp.concatenate(outs, axis=1) * ng_ref[...]).astype(y_ref.dtype)


def _ssd_operands(z, xbc, dt_pad, conv_w, conv_b, dt_bias, a_log, d_skip, norm_g):
    bsz, l, width = z.shape
    heads = width // HEAD_DIM
    xw = xbc.shape[-1]
    q = SSD_CHUNK
    dtt = jnp.swapaxes(dt_pad[:, :, :heads], 1, 2)
    pad = LANES - heads
    dtb = jnp.pad(dt_bias, (0, pad)).reshape(1, LANES)
    a_neg = -jnp.exp(a_log.astype(F32))
    aneg = jnp.pad(a_neg, (0, pad)).reshape(1, LANES)
    dskip = jnp.repeat(d_skip, HEAD_DIM).reshape(1, width)
    row = lambda n: pl.BlockSpec((1, n), lambda bi, c: (0, 0))
    specs = [pl.BlockSpec((1, q, width), lambda bi, c: (bi, c, 0)),
             pl.BlockSpec((1, q, xw), lambda bi, c: (bi, c, 0)),
             pl.BlockSpec((1, SUBLANES, xw), lambda bi, c: (bi, jnp.maximum(c * (q // SUBLANES) - 1, 0), 0)),
             pl.BlockSpec((1, q, LANES), lambda bi, c: (bi, c, 0)),
             pl.BlockSpec((1, heads, q), lambda bi, c: (bi, 0, c)),
             pl.BlockSpec((SSD_CONV, xw), lambda bi, c: (0, 0)),
             row(xw), row(LANES),
             pl.BlockSpec((heads, 1), lambda bi, c: (0, 0)),
             row(LANES),
             pl.BlockSpec((heads, 1), lambda bi, c: (0, 0)),
             row(width), row(width)]
    args = [z, xbc, xbc, dt_pad, dtt, conv_w, conv_b.reshape(1, xw), dtb, dt_bias.reshape(heads, 1),
            aneg, a_neg.reshape(heads, 1), dskip, norm_g.reshape(1, width)]
    scratch = [pltpu.VMEM((width // LANES, LANES, SSD_STATE), F32), pltpu.VMEM((q + SUBLANES, xw), F32)]
    return args, specs, scratch


def _rwkv_prep_kernel(n_extra, x_ref, xp_ref, win_ref, mu_ref, w0_ref, a0_ref, kk_ref, ka_ref, w2_ref, a2_ref,
                      g2_ref, e_ref, et_ref, *refs):
    extra_w = refs[:n_extra]
    r_ref, lw_ref, k_ref, v_ref, an_ref, bn_ref, g_ref = refs[n_extra:n_extra + 7]
    extra_o = refs[n_extra + 7:2 * n_extra + 7]
    ext_ref = refs[-1]
    tm = x_ref.shape[1]
    width = r_ref.shape[-1]
    i = pl.program_id(1)
    xe = jnp.concatenate([jnp.where(i > 0, xp_ref[0], 0.0), x_ref[0]], axis=0).astype(BF16)
    win = ext_ref.shape[1]
    lora = RWKV_DECAY_LORA + RWKV_ICLR_LORA

    def project(c0, c1):
        ext_ref[:, c0:c1] = jnp.dot(xe, win_ref[:, c0:c1], preferred_element_type=F32)

    sub = min(tm, RWKV_PREP_ROWS)
    rows = slice(0, sub)

    def shifted_mix(c0, c1):
        rw = ext_ref[SUBLANES + rows.start:SUBLANES + rows.stop, c0:c1]
        prev = ext_ref[SUBLANES - 1 + rows.start:SUBLANES - 1 + rows.stop, c0:c1]
        return rw + (prev - rw) * mu_ref[:, c0:c1]

    def projections():
        half = width // 2
        extras = [(w_ref, o_ref, sl) for w_ref, o_ref in zip(extra_w, extra_o)
                  for sl in _col_chunks(w_ref.shape[1], 256)]
        own = [(3 * width, win), (width, width + half), (width + half, 2 * width), (0, half), (half, width),
               (2 * width, 2 * width + half), (2 * width + half, 3 * width)]
        for n, cols in enumerate(own):
            project(*cols)
            yield
            if n >= 2 and extras:
                w_ref, o_ref, sl = extras.pop(0)
                o_ref[0, :, sl] = jnp.dot(xe[SUBLANES:], w_ref[:, sl], preferred_element_type=F32)
                yield
        for w_ref, o_ref, sl in extras:
            o_ref[0, :, sl] = jnp.dot(xe[SUBLANES:], w_ref[:, sl], preferred_element_type=F32)
            yield

    pieces = projections()

    def mxu(n=1):
        for _ in range(n):
            next(pieces, None)

    for r0 in range(0, tm, sub):
        rows = slice(r0, r0 + sub)
        mxu(3)
        s_lo = shifted_mix(3 * width, win)
        lo = s_lo[:, :lora]
        wv = w0_ref[...] + _dot(jnp.tanh(lo), w2_ref[...])
        av = _sigmoid(a0_ref[...] + _dot(lo, a2_ref[...]))
        g_ref[0, rows, :] = _dot(_sigmoid(s_lo[:, lora:]), g2_ref[...]).astype(g_ref.dtype)
        mxu(1)
        for sl in _col_chunks(width, 512):
            lw_ref[0, rows, sl] = -jnp.exp(-_softplus(-wv[:, sl]) - 0.5)
            mxu(1)
        k = shifted_mix(width, 2 * width)
        kkr = k * kk_ref[...]
        mxu(1)
        ss = _dot_exact_rhs(kkr * kkr, e_ref[...])
        k_ref[0, rows, :] = (k * (1.0 + (av - 1.0) * ka_ref[...])).astype(k_ref.dtype)
        mxu(2)
        inv = lax.rsqrt(jnp.maximum(ss, 1e-24))
        kk = kkr * _dot_exact_rhs(inv, et_ref[...])
        mxu(1)
        an_ref[0, rows, :] = (-kk).astype(an_ref.dtype)
        bn_ref[0, rows, :] = (kk * av).astype(bn_ref.dtype)
        mxu(2)
        r_ref[0, rows, :] = shifted_mix(0, width).astype(r_ref.dtype)
        mxu(1)
        v_ref[0, rows, :] = shifted_mix(2 * width, 3 * width).astype(v_ref.dtype)
    for _ in pieces:
        pass


def _rwkv_chunks(rows_list, fill, r_ref, lw_ref, k_ref, v_ref, an_ref, bn_ref, g_ref, rk_ref, lg_ref, lb_ref,
                 y_ref, state_ref):
    c = RWKV_CHUNK
    c2 = 2 * c
    width = r_ref.shape[-1]
    n_pairs = width // LANES

    row = lax.broadcasted_iota(jnp.int32, (c, c), 0)
    col = lax.broadcasted_iota(jnp.int32, (c, c), 1)
    tri = jnp.where(row >= col, 1.0, 0.0)
    roww = lax.broadcasted_iota(jnp.int32, (c, c2), 0)
    colw = lax.broadcasted_iota(jnp.int32, (c, c2), 1) % c
    strict = jnp.where(roww > colw, 1.0, 0.0)
    incl = jnp.where(roww >= colw, 1.0, 0.0)
    eye = jnp.where(roww == colw, 1.0, 0.0)
    same_head = jnp.where((lax.broadcasted_iota(jnp.int32, (LANES, LANES), 0) // HEAD_DIM)
                          == (lax.broadcasted_iota(jnp.int32, (LANES, LANES), 1) // HEAD_DIM), 1.0, 0.0)
    lane = lax.broadcasted_iota(jnp.int32, (1, LANES), 1)
    m0 = jnp.where(lane < HEAD_DIM, 1.0, 0.0)
    m1 = 1.0 - m0

    lane_lo2 = lax.broadcasted_iota(jnp.int32, (c, LANES), 1) < HEAD_DIM

    def stack(x):
        xb = x.astype(BF16)
        zero = jnp.zeros_like(xb)
        return jnp.concatenate([jnp.where(lane_lo2, xb, zero), jnp.where(lane_lo2, zero, xb)], axis=0)

    sls = [slice(p * LANES, (p + 1) * LANES) for p in range(n_pairs)]
    units = [(rows, sl) for rows in rows_list for sl in sls]
    lhs2s, rhs4s, vs, v_stks, bks, g_lasts = [], [], [], [], [], []
    for rows, sl in units:
        r = r_ref[0, rows, sl].astype(F32)
        lw = lw_ref[0, rows, sl]
        k = k_ref[0, rows, sl].astype(F32)
        b = bn_ref[0, rows, sl].astype(F32)
        v = v_ref[0, rows, sl].astype(F32)
        cum = _dot_exact_lhs(tri, lw)
        cum_last = cum[c - 1:c, :]
        e_neg = jnp.exp(-cum)
        e_tail = jnp.exp(cum_last - cum)
        at = an_ref[0, rows, sl].astype(F32) * jnp.exp(cum - lw)
        rt = r * jnp.exp(cum)
        lhs2s.append(jnp.concatenate([at, rt], axis=0).astype(BF16))
        rhs4s.append(jnp.concatenate([stack(b * e_neg), stack(k * e_neg)], axis=0))
        vs.append(v.astype(BF16))
        v_stks.append(stack(v))
        bks.append(jnp.concatenate([b * e_tail, k * e_tail], axis=0).astype(BF16))
        g_lasts.append(jnp.exp(cum_last))
    gms = [_dot_nt(l, rh) for l, rh in zip(lhs2s, rhs4s)]
    pws = [gm[:c, :c2] * strict for gm in gms]
    a_aks = [(gm[:c, c2:] * strict).astype(BF16) for gm in gms]
    a_rbks = [jnp.concatenate([gm[c:, :c2] * incl, gm[c:, c2:] * incl], axis=1).astype(BF16) for gm in gms]
    akvs = [_dot(a_ak, v_stk) for a_ak, v_stk in zip(a_aks, v_stks)]
    fill()
    tinvs = [eye + pw for pw in pws]
    pw_stks = [stack(pw) for pw in pws]
    for _ in range(int(math.log2(c)) - 1):
        pws = [_dot(pw, pw_stk) for pw, pw_stk in zip(pws, pw_stks)]
        pw_stks = [stack(pw) for pw in pws]
        tinvs = [tinv + _dot(tinv, pw_stk) for tinv, pw_stk in zip(tinvs, pw_stks)]
        fill()
    lane_lo = lane < HEAD_DIM

    def head_sum(x):
        lo = jnp.sum(x * m0, axis=-1, keepdims=True)
        hi = jnp.sum(x * m1, axis=-1, keepdims=True)
        return jnp.where(lane_lo, lo, hi)

    for si, rows in enumerate(rows_list):
        u0 = si * n_pairs
        sts = [state_ref[p] for p in range(n_pairs)]
        ahrhs = [_dot_nt(lhs2s[u0 + p], sts[p]) for p in range(n_pairs)]
        fill()
        us = [_dot(tinvs[u0 + p], stack(ahrhs[p][:c] + akvs[u0 + p])) for p in range(n_pairs)]
        fill()
        for p in range(n_pairs):
            uv = jnp.concatenate([us[p].astype(BF16), vs[u0 + p]], axis=0)
            state_ref[p] = sts[p] * g_lasts[u0 + p] + same_head * _dot_tn(uv, bks[u0 + p])
        fill()
        ys = [ahrhs[p][c:] + _dot(a_rbks[u0 + p], jnp.concatenate([stack(us[p]), v_stks[u0 + p]], axis=0))
              for p in range(n_pairs)]
        bonus = [head_sum(r_ref[0, rows, sl].astype(F32) * k_ref[0, rows, sl].astype(F32) * rk_ref[:, sl])
                 for sl in sls]
        ycs = [y - head_sum(y) * (1.0 / HEAD_DIM) for y in ys]
        yvs = [head_sum(yc * yc) * (1.0 / HEAD_DIM) for yc in ycs]
        for p, sl in enumerate(sls):
            yn = ycs[p] * lax.rsqrt(yvs[p] + RWKV_LNX_EPS) * lg_ref[:, sl] + lb_ref[:, sl]
            out = (yn + bonus[p] * v_ref[0, rows, sl].astype(F32)) * g_ref[0, rows, sl].astype(F32)
            y_ref[0, rows, sl] = out.astype(y_ref.dtype)


def _rwkv_operands(x, w_in, extra_ws, mu, w0, w2, a0, a2, g2, k_k, k_a, r_k, lnx_g, lnx_b, *, tm):
    bsz, l, d = x.shape
    n_extra = len(extra_ws)
    win = w_in.shape[1]
    width = w0.shape[0]
    w2p = jnp.concatenate([w2, jnp.zeros((RWKV_ICLR_LORA, width), F32)], axis=0).astype(BF16)
    a2p = jnp.concatenate([jnp.zeros((RWKV_DECAY_LORA, width), F32), a2], axis=0).astype(BF16)
    head_of = jnp.arange(width) // HEAD_DIM
    e = (head_of[:, None] == jnp.arange(LANES)[None, :]).astype(BF16)
    et = e.T
    vec = lambda x: x.reshape(1, -1)
    row = lambda n: pl.BlockSpec((1, n), lambda bi, i: (0, 0))
    full = lambda a: pl.BlockSpec(a.shape, lambda bi, i: (0, 0))
    single = lambda a: pl.BlockSpec(a.shape, lambda bi, i: (0, 0), pipeline_mode=pl.Buffered(1))
    tile = lambda w: pl.BlockSpec((1, tm, w), lambda bi, i: (bi, i, 0))
    sds = lambda w, dt=F32: jax.ShapeDtypeStruct((bsz, l, w), dt)
    g2b = g2.astype(BF16)
    outs = pl.pallas_call(
        functools.partial(_rwkv_prep_kernel, n_extra),
        grid=(bsz, l // tm),
        in_specs=[pl.BlockSpec((1, tm, d), lambda bi, i: (bi, i, 0)),
                  pl.BlockSpec((1, SUBLANES, d), lambda bi, i: (bi, jnp.maximum(i * (tm // SUBLANES) - 1, 0), 0)),
                  single(w_in), row(win), row(width), row(width), row(width), row(width),
                  full(w2p), full(a2p), full(g2b), full(e), full(et)] + [single(w) for w in extra_ws],
        out_specs=[tile(width)] * 7 + [tile(w.shape[1]) for w in extra_ws],
        out_shape=([sds(width, BF16), sds(width)] + [sds(width, BF16)] * 5
                   + [sds(w.shape[1]) for w in extra_ws]),
        scratch_shapes=[pltpu.VMEM((tm + SUBLANES, win), F32)],
        compiler_params=_params("parallel", "parallel"),
        name="rwkv_prep",
    )(x, x, w_in, vec(mu), vec(w0), vec(a0), vec(k_k), vec(k_a), w2p, a2p, g2b, e, et, *extra_ws)
    r, lw, k, v, an, bn, g = outs[:7]
    ctile = pl.BlockSpec((1, SSD_CHUNK, width), lambda bi, ci: (bi, ci, 0))
    crow = pl.BlockSpec((1, width), lambda bi, ci: (0, 0))
    args = [r, lw, k, v, an, bn, g, vec(r_k), vec(lnx_g), vec(lnx_b)]
    scratch = [pltpu.VMEM((width // LANES, LANES, LANES), F32)]
    return outs[7:], (args, [ctile] * 7 + [crow] * 3, scratch)


def _mixers_kernel(n_ssd, n_rwkv, *refs):
    ssd_in = refs[:n_ssd]
    rwkv_in = refs[n_ssd:n_ssd + n_rwkv]
    y_ssd_ref, y_rwkv_ref, ssd_state, ssd_ext, rwkv_state = refs[n_ssd + n_rwkv:]
    c = pl.program_id(1)

    @pl.when(c == 0)
    def _():
        ssd_state[...] = jnp.zeros_like(ssd_state)
        rwkv_state[...] = jnp.zeros_like(rwkv_state)

    ssd_steps = _ssd_chunk(c, *ssd_in, y_ssd_ref, ssd_state, ssd_ext)
    fill = lambda: (next(ssd_steps, None), next(ssd_steps, None))
    rows_list = [slice(sub * RWKV_CHUNK, (sub + 1) * RWKV_CHUNK) for sub in range(SSD_CHUNK // RWKV_CHUNK)]
    _rwkv_chunks(rows_list, fill, *rwkv_in, y_rwkv_ref, rwkv_state)
    for _ in ssd_steps:
        pass


def _ssd_rwkv_mixers(ssd_ops, rwkv_ops, bsz, l, ssd_width, rwkv_width):
    ssd_args, ssd_specs, ssd_scratch = ssd_ops
    rwkv_args, rwkv_specs, rwkv_scratch = rwkv_ops
    q = SSD_CHUNK
    out_spec = lambda w: pl.BlockSpec((1, q, w), lambda bi, c: (bi, c, 0))
    return pl.pallas_call(
        functools.partial(_mixers_kernel, len(ssd_args), len(rwkv_args)),
        grid=(bsz, l // q),
        in_specs=ssd_specs + rwkv_specs,
        out_specs=[out_spec(ssd_width), out_spec(rwkv_width)],
        out_shape=[jax.ShapeDtypeStruct((bsz, l, ssd_width), BF16), jax.ShapeDtypeStruct((bsz, l, rwkv_width), BF16)],
        scratch_shapes=ssd_scratch + rwkv_scratch,
        compiler_params=_params("parallel", "arbitrary"),
        name="ssd_rwkv_mixers",
    )(*ssd_args, *rwkv_args)


def _moba_kernel(q_ref, qn_ref, k_ref, v_ref, o_ref, kb_ref, vt_ref, kmean_ref, qo_ref, qp_ref):
    blk = MOBA_BLOCK
    half = HEAD_DIM
    nb = k_ref.shape[0] // blk
    nbp = kmean_ref.shape[0]
    npair = q_ref.shape[1] // LANES
    heads = range(2 * npair)
    group = math.gcd(nb, MOBA_GROUP)
    qi = pl.program_id(2)

    @pl.when(qi == 0)
    def _():
        lane = lax.broadcasted_iota(jnp.int32, (blk, LANES), 1)
        rowp = lax.broadcasted_iota(jnp.int32, (LANES, blk), 0)
        if nbp > nb:
            kmean_ref[...] = jnp.zeros_like(kmean_ref)
        for n in range(nb):
            rows = slice(n * blk, (n + 1) * blk)
            kmean_ref[n:n + 1, :] = jnp.mean(k_ref[rows, :], axis=0, keepdims=True)
            for pp in range(npair):
                cols = slice(pp * LANES, (pp + 1) * LANES)
                kn = k_ref[rows, cols]
                kb_ref[2 * pp, n] = jnp.where(lane < half, kn, jnp.where(lane == half + n, 1.0, 0.0)).astype(BF16)
                kb_ref[2 * pp + 1, n] = jnp.where(lane >= half, kn, jnp.where(lane == n, 1.0, 0.0)).astype(BF16)
                vtn = v_ref[rows, cols].T
                vt_ref[2 * pp, n] = jnp.where(rowp < half, vtn, jnp.where(rowp == half, 1.0, 0.0)).astype(BF16)
                vt_ref[2 * pp + 1, n] = jnp.where(rowp >= half, vtn, jnp.where(rowp == 0, 1.0, 0.0)).astype(BF16)

    rown = lax.broadcasted_iota(jnp.int32, (nbp, blk), 0)
    lane_k = lax.broadcasted_iota(jnp.int32, (nbp, LANES), 1)
    zeros = jnp.zeros((half, blk), F32)

    def prepare_queries(src_ref, tile):
        qts = [src_ref[:, pp * LANES:(pp + 1) * LANES].T for pp in range(npair)]
        gates, qhs = [], []
        for hh in heads:
            pp, h = divmod(hh, 2)
            km = kmean_ref[:, pp * LANES:(pp + 1) * LANES]
            kmh = jnp.where((lane_k < half) if h == 0 else (lane_k >= half), km, 0.0)
            gates.append(jnp.dot(kmh, qts[pp], preferred_element_type=F32, precision=lax.Precision.HIGHEST))
            qhs.append(qts[pp][h * half:(h + 1) * half] * (half ** -0.5 * LOG2_E))
        for hh in heads:
            gate = jnp.where(rown < tile, gates[hh], -jnp.inf)
            bias = jnp.full((nbp, blk), NEG_BIG, F32)
            for _ in range(MOBA_TOPK):
                mx = jnp.max(gate, axis=0, keepdims=True)
                first = jnp.min(jnp.where(gate == mx, rown, nbp), axis=0, keepdims=True)
                pick = (rown == first) & (mx > -jnp.inf)
                bias = jnp.where(pick, 0.0, bias)
                gate = jnp.where(pick, -jnp.inf, gate)
            aug = jnp.concatenate([bias, jnp.zeros((half - nbp, blk), F32)], axis=0)
            qo_ref[hh] = jnp.concatenate([qhs[hh], zeros] if hh % 2 == 0 else [zeros, qhs[hh]],
                                         axis=0).astype(BF16)
            qp_ref[hh] = jnp.concatenate([qhs[hh], aug] if hh % 2 == 0 else [aug, qhs[hh]], axis=0).astype(BF16)

    @pl.when(qi == 0)
    def _():
        prepare_queries(q_ref, 0)

    q_own = [qo_ref[h] for h in heads]
    q_past = [qp_ref[h] for h in heads]
    s_own = [jnp.dot(kb_ref[h, qi], q_own[h], preferred_element_type=F32) for h in heads]
    prepare_queries(qn_ref, qi + 1)

    causal = (lax.broadcasted_iota(jnp.int32, (blk, blk), 0) <= lax.broadcasted_iota(jnp.int32, (blk, blk), 1))
    ms, ps = [], []
    for h in heads:
        s = jnp.where(causal, s_own[h], NEG_BIG)
        ms.append(jnp.max(s, axis=0, keepdims=True))
        ps.append(jnp.exp2(s - ms[h]).astype(BF16))
    carry = []
    for h in heads:
        carry += [ms[h], jnp.dot(vt_ref[h, qi], ps[h], preferred_element_type=F32)]

    def scores(gi, h):
        return [jnp.dot(kb_ref[h, gi * group + g], q_past[h], preferred_element_type=F32) for g in range(group)]

    def values(gi, h):
        return jnp.concatenate([vt_ref[h, gi * group + g] for g in range(group)], axis=1)

    def body_lagged(gi, carry):
        excess = carry[-1]
        out = []
        sss = [scores(gi, h) for h in heads]
        pcats, gmaxs = [], []
        for h in heads:
            m_run = carry[2 * h]
            gmax = None
            ps = []
            for s in sss[h]:
                cm = jnp.max(s, axis=0, keepdims=True)
                gmax = cm if gmax is None else jnp.maximum(gmax, cm)
                ps.append(jnp.exp2(s - m_run).astype(BF16))
            pcats.append(jnp.concatenate(ps, axis=0))
            gmaxs.append(gmax)
        for h in heads:
            m_run, acc = carry[2 * h], carry[2 * h + 1]
            m_new = jnp.maximum(m_run, gmaxs[h])
            excess = jnp.maximum(excess, gmaxs[h] - m_run)
            acc = jnp.exp2(m_run - m_new) * (acc + jnp.dot(values(gi, h), pcats[h], preferred_element_type=F32))
            out += [m_new, acc]
        return tuple(out) + (excess,)

    def body_exact_max(gi, carry):
        sss = [scores(gi, h) for h in heads]
        m_news = []
        for h in heads:
            m_new = carry[2 * h]
            for s in sss[h]:
                m_new = jnp.maximum(m_new, jnp.max(s, axis=0, keepdims=True))
            m_news.append(m_new)
        pcats = [jnp.concatenate([jnp.exp2(s - m_news[h]).astype(BF16) for s in sss[h]], axis=0)
                 for h in heads]
        out = []
        for h in heads:
            alpha = jnp.exp2(carry[2 * h] - m_news[h])
            out += [m_news[h], alpha * carry[2 * h + 1]
                    + jnp.dot(values(gi, h), pcats[h], preferred_element_type=F32)]
        return tuple(out)

    rowq = lax.broadcasted_iota(jnp.int32, (LANES, blk), 0)

    def write_out(final):
        for pp in range(npair):
            acc0, acc1 = final[4 * pp + 1], final[4 * pp + 3]
            out_t = jnp.where(rowq < half, acc0 / acc0[half:half + 1], acc1 / acc1[0:1])
            o_ref[:, pp * LANES:(pp + 1) * LANES] = out_t.T.astype(o_ref.dtype)

    n_groups = (qi + group - 1) // group
    final = lax.fori_loop(0, n_groups, body_lagged, tuple(carry) + (jnp.full((1, blk), NEG_BIG, F32),))
    write_out(final)

    @pl.when(jnp.max(final[-1]) > MOBA_LAG_LIMIT)
    def _():
        write_out(lax.fori_loop(0, n_groups, body_exact_max, tuple(carry)))


def _moba_attention(qkv, bsz, s, heads):
    blk = MOBA_BLOCK
    assert s % blk == 0 and (heads * HEAD_DIM) % LANES == 0
    nb = s // blk
    assert nb <= HEAD_DIM
    nbp = -(-nb // SUBLANES) * SUBLANES
    pairs = heads * HEAD_DIM // LANES
    pps = math.gcd(pairs, MOBA_PAIRS_PER_STEP)
    cw = pps * LANES
    steps = pairs // pps
    return pl.pallas_call(
        _moba_kernel,
        grid=(bsz, steps, nb),
        in_specs=[pl.BlockSpec((blk, cw), lambda b, p, i: (b * nb + i, p)),
                  pl.BlockSpec((blk, cw), lambda b, p, i: (b * nb + jnp.minimum(i + 1, nb - 1), p)),
                  pl.BlockSpec((s, cw), lambda b, p, i: (b, steps + p)),
                  pl.BlockSpec((s, cw), lambda b, p, i: (b, 2 * steps + p))],
        out_specs=pl.BlockSpec((blk, cw), lambda b, p, i: (b * nb + i, p)),
        out_shape=jax.ShapeDtypeStruct((bsz * s, heads * HEAD_DIM), BF16),
        scratch_shapes=[pltpu.VMEM((2 * pps, nb, blk, LANES), BF16), pltpu.VMEM((2 * pps, nb, LANES, blk), BF16),
                        pltpu.VMEM((nbp, cw), F32),
                        pltpu.VMEM((2 * pps, LANES, blk), BF16), pltpu.VMEM((2 * pps, LANES, blk), BF16)],
        compiler_params=_params("parallel", "parallel", "arbitrary"),
        name="moba_attention",
    )(qkv, qkv, qkv, qkv)


def _row_tile(m):
    for t in (512, 256, 128, 64, 32, 16, 8):
        if m % t == 0:
            return t
    raise ValueError(f"row count {m} is not a multiple of 8")


def _col_tile(n, cap=2048):
    best = None
    for t in range(LANES, min(n, cap) + 1, LANES):
        if n % t == 0:
            best = t
    if best is None:
        raise ValueError(f"column count {n} is not a multiple of {LANES}")
    return best


def kernel(x, mem, even_w_in, ssd_conv_w, ssd_conv_b, ssd_dt_bias, ssd_a_log, ssd_d, ssd_norm_g, rwkv_mu, rwkv_w0, rwkv_w2, rwkv_a0, rwkv_a2, rwkv_g2, rwkv_k_k, rwkv_k_a, rwkv_r_k, rwkv_lnx_g, rwkv_lnx_b, even_w_out, odd_w_qkv, odd_w_out, ln_mix_g, ln_mix_b, xa_wq, xa_wkv, xa_wo, ln_xa_g, ln_xa_b, ffn_w13, ffn_w2, ln_ffn_g, ln_ffn_b):
    bsz, s, d = x.shape
    m = bsz * s
    tm = _row_tile(s)
    ssd_width = ssd_norm_g.shape[-1]
    ssd_heads = ssd_dt_bias.shape[-1]
    ssd_xbc = ssd_conv_b.shape[-1]
    ssd_in = ssd_width + ssd_xbc + ssd_heads
    rwkv_width = rwkv_w0.shape[-1]
    mem2 = mem.reshape(bsz * mem.shape[1], d)
    x2 = x.reshape(m, d)
    for layer in range(DEPTH):
        j = layer // 2
        if layer % 2 == 0:
            w_in = even_w_in[j].astype(BF16)
            w_z = w_in[:, :ssd_width]
            w_xbc = w_in[:, ssd_width:ssd_width + ssd_xbc]
            w_dt = jnp.pad(w_in[:, ssd_width + ssd_xbc:ssd_in], ((0, 0), (0, LANES - ssd_heads)))
            w_rw = w_in[:, ssd_in:]
            (z, xbc, dt_pad), rwkv_ops = _rwkv_operands(
                x2.reshape(bsz, s, d), w_rw, [w_z, w_xbc, w_dt], rwkv_mu[j], rwkv_w0[j], rwkv_w2[j],
                rwkv_a0[j], rwkv_a2[j], rwkv_g2[j], rwkv_k_k[j], rwkv_k_a[j], rwkv_r_k[j],
                rwkv_lnx_g[j], rwkv_lnx_b[j], tm=tm)
            ssd_ops = _ssd_operands(z, xbc, dt_pad, ssd_conv_w[j], ssd_conv_b[j], ssd_dt_bias[j],
                                    ssd_a_log[j], ssd_d[j], ssd_norm_g[j])
            y_ssd, y_rwkv = _ssd_rwkv_mixers(ssd_ops, rwkv_ops, bsz, s, ssd_width, rwkv_width)
            x2 = _matmul_residual_ln([y_ssd.reshape(m, -1), y_rwkv.reshape(m, -1)],
                                     even_w_out[j].astype(BF16), x2,
                                     ln_mix_g[layer], ln_mix_b[layer], tm=tm)
        else:
            heads = d // HEAD_DIM
            qkv, = _matmul(x2, [odd_w_qkv[j].astype(BF16)], tm=tm)
            attn = _moba_attention(qkv, bsz, s, heads)
            x2 = _matmul_residual_ln([attn], odd_w_out[j].astype(BF16), x2,
                                     ln_mix_g[layer], ln_mix_b[layer], tm=tm)
        x3 = _cross_attention_ln(x2.reshape(bsz, s, d), mem, xa_wkv[layer].astype(BF16),
                                 xa_wq[layer].astype(BF16), xa_wo[layer].astype(BF16),
                                 ln_xa_g[layer], ln_xa_b[layer], tm=tm)
        x2 = x3.reshape(m, d)
        h = _swiglu_up(x2, ffn_w13[layer].astype(BF16), tm=tm)
        x2 = _matmul_residual_ln([h], ffn_w2[layer].astype(BF16), x2,
                                 ln_ffn_g[layer], ln_ffn_b[layer], tm=tm)
    return x2.reshape(bsz, s, d)
```
